```python
import math, functools
import jax, jax.numpy as jnp
from jax import lax
import numpy as np

D_MODEL = 1024
BATCH = 4
SEQ = 8192
DEPTH = 2
DEC_BATCH = 16
DEC_SEQ = 32
PAST_LEN = 4096

CHUNK = 64
H_A = 8
DK_A = 128
DV_A = 128
QK_A = H_A * DK_A
V_A = H_A * DV_A
C_QKV = 2 * QK_A + V_A
CONV_W = 4
H_B = 8
NOPE = 128
ROPE = 64
V_B = 128
Q_RANK = 384
KV_RANK = 256
ROPE_THETA = 10000.0
ATTN_SCALE = (NOPE + ROPE) ** -0.5
Q_BLOCK = 128
D_FF = -(-8 * D_MODEL // (3 * 256)) * 256
ALPHA = (2 * DEPTH) ** 0.25
BETA_INIT = (8 * DEPTH) ** -0.25
EPS = 1e-6
_SIZES = (C_QKV, V_A, H_A, H_A, Q_RANK, KV_RANK, ROPE, D_MODEL, D_MODEL)
SPLIT_POINTS = tuple(sum(_SIZES[:i + 1]) for i in range(len(_SIZES) - 1))
N_IN = sum(_SIZES)

kernel_name = "gdn_mla_parallel_deepnorm_stream_step"


def layer_norm(x, g, b):
    xf = x.astype(jnp.float32)
    mu = jnp.mean(xf, -1, keepdims=True)
    var = jnp.mean(jnp.square(xf - mu), -1, keepdims=True)
    return ((xf - mu) * lax.rsqrt(var + EPS) * g + b).astype(x.dtype)


def rms_norm(x, g):
    xf = x.astype(jnp.float32)
    return (xf * lax.rsqrt(jnp.mean(jnp.square(xf), -1, keepdims=True) + EPS) * g).astype(x.dtype)


def l2_norm(x):
    xf = x.astype(jnp.float32)
    return xf * lax.rsqrt(jnp.sum(jnp.square(xf), -1, keepdims=True) + EPS)


def rope(x, pos):
    half = x.shape[-1] // 2
    inv = ROPE_THETA ** (-jnp.arange(half, dtype=jnp.float32) / half)
    ang = pos.astype(jnp.float32)[:, None] * inv[None, :]
    cos = jnp.cos(ang)[None, :, None, :]
    sin = jnp.sin(ang)[None, :, None, :]
    x1 = x[..., :half].astype(jnp.float32)
    x2 = x[..., half:].astype(jnp.float32)
    return jnp.concatenate([x1 * cos - x2 * sin, x1 * sin + x2 * cos], -1).astype(x.dtype)


def short_conv(x_new, conv_state, w):
    L = x_new.shape[1]
    xcat = jnp.concatenate([conv_state.astype(x_new.dtype), x_new], axis=1)
    y = xcat[:, 0:L] * w[0]
    for j in range(1, CONV_W):
        y = y + xcat[:, j:j + L] * w[j]
    return jax.nn.silu(y), xcat[:, -(CONV_W - 1):]


def gated_delta_rule(q, k, v, g, beta, S0):
    B, L, H, dk = q.shape
    dv = v.shape[-1]
    C = min(CHUNK, L)
    N = L // C

    def blk(t):
        t = t.reshape((B, N, C, H) + t.shape[3:])
        return jnp.moveaxis(t, 3, 1)

    q, k, v = (blk(t.astype(jnp.float32)) for t in (q, k, v))
    g, beta = blk(g), blk(beta)
    G = jnp.cumsum(g, axis=-1)
    diff = G[..., :, None] - G[..., None, :]
    idx = jnp.arange(C)
    strict = idx[:, None] > idx[None, :]
    incl = idx[:, None] >= idx[None, :]
    dec_strict = jnp.exp(jnp.where(strict, diff, -jnp.inf))
    dec_incl = jnp.exp(jnp.where(incl, diff, -jnp.inf))
    kk = jnp.einsum('bhnid,bhnjd->bhnij', k, k)
    A = beta[..., None] * kk * dec_strict
    rhs = jnp.concatenate([v * beta[..., None], k * (beta * jnp.exp(G))[..., None]], -1)
    sol = lax.linalg.triangular_solve(A, rhs, left_side=True, lower=True, unit_diagonal=True)
    u, w = sol[..., :dv], sol[..., dv:]
    qk = jnp.einsum('bhnid,bhnjd->bhnij', q, k) * dec_incl
    qg = q * jnp.exp(G)[..., None]
    kd = k * jnp.exp(G[..., -1:] - G)[..., None]
    gl = jnp.exp(G[..., -1])
    xs = tuple(jnp.moveaxis(t, 2, 0) for t in (u, w, qg, qk, kd, gl))

    def step(S, xs_n):
        u_n, w_n, qg_n, qk_n, kd_n, gl_n = xs_n
        v_new = u_n - jnp.einsum('bhck,bhkv->bhcv', w_n, S)
        o_n = jnp.einsum('bhck,bhkv->bhcv', qg_n, S) + jnp.einsum('bhij,bhjv->bhiv', qk_n, v_new)
        S = S * gl_n[..., None, None] + jnp.einsum('bhck,bhcv->bhkv', kd_n, v_new)
        return S, o_n

    S, o = lax.scan(step, S0.astype(jnp.float32), xs)
    o = jnp.transpose(o, (1, 0, 3, 2, 4)).reshape(B, L, H, dv)
    return o, S


def chunk_causal_attention(q, k, v, q_pos, k_pos):
    B, L, H, Dq = q.shape
    QB = Q_BLOCK if L % Q_BLOCK == 0 else L
    nb = L // QB
    qb = jnp.moveaxis(q.reshape(B, nb, QB, H, Dq), 1, 0)
    pb = q_pos.reshape(nb, QB)
    kc = k_pos // CHUNK

    def one(args):
        qi, pi = args
        s = jnp.einsum('bqhd,bkd->bhqk', qi, k).astype(jnp.float32) * ATTN_SCALE
        s = jnp.where(kc[None, :] <= (pi // CHUNK)[:, None], s, -jnp.inf)
        p = jax.nn.softmax(s, axis=-1).astype(v.dtype)
        return jnp.einsum('bhqk,bkc->bqhc', p, v)

    o = lax.map(one, (qb, pb))
    return jnp.moveaxis(o, 0, 1).reshape(B, L, H, v.shape[-1])


def trunk_layer(x, conv_state, S0, ckv_past, kr_past, w_in, conv_w, a_log, dt_bias, gdn_norm_g, w_oa,
                q_norm_g, w_uq, kv_norm_g, w_ukv, w_ob, w_out, ln1_g, ln1_b, w_gu, w_down, ln2_g, ln2_b):
    B, L, _ = x.shape
    P = ckv_past.shape[1]
    q_pos = P + jnp.arange(L)
    k_pos = jnp.arange(P + L)
    h = x @ w_in
    qkv_raw, z, a_raw, b_raw, c_q, c_kv, k_r, g_a, g_b = jnp.split(h, SPLIT_POINTS, axis=-1)

    qkv, conv_new = short_conv(qkv_raw, conv_state, conv_w)
    q, k, v = jnp.split(qkv, [QK_A, 2 * QK_A], axis=-1)
    q = l2_norm(q.reshape(B, L, H_A, DK_A)) * (DK_A ** -0.5)
    k = l2_norm(k.reshape(B, L, H_A, DK_A))
    v = v.reshape(B, L, H_A, DV_A)
    beta = jax.nn.sigmoid(b_raw.astype(jnp.float32))
    g = -jnp.exp(a_log.astype(jnp.float32)) * jax.nn.softplus(
        a_raw.astype(jnp.float32) + dt_bias.astype(jnp.float32))
    o_a, S_new = gated_delta_rule(q, k, v, g, beta, S0)
    zf = z.astype(jnp.float32).reshape(B, L, H_A, DV_A)
    o_a = (rms_norm(o_a, gdn_norm_g) * jax.nn.silu(zf)).astype(x.dtype).reshape(B, L, V_A)
    y_a = o_a @ w_oa

    qf = (rms_norm(c_q, q_norm_g) @ w_uq).reshape(B, L, H_B, NOPE + ROPE)
    q_nope, q_rope = qf[..., :NOPE], rope(qf[..., NOPE:], q_pos)
    ckv_new = rms_norm(c_kv, kv_norm_g)
    kr_new = rope(k_r[:, :, None, :], q_pos)[:, :, 0]
    w_ukv_h = w_ukv.reshape(KV_RANK, H_B, NOPE + V_B)
    w_uk, w_uv = w_ukv_h[..., :NOPE], w_ukv_h[..., NOPE:]
    q_lat = jnp.einsum('blhn,chn->blhc', q_nope, w_uk)
    keys_lat = jnp.concatenate([ckv_past.astype(x.dtype), ckv_new], axis=1)
    keys_r = jnp.concatenate([kr_past.astype(x.dtype), kr_new], axis=1)
    o_lat = chunk_causal_attention(jnp.concatenate([q_lat, q_rope], -1),
                                   jnp.concatenate([keys_lat, keys_r], -1), keys_lat, q_pos, k_pos)
    o_b = jnp.einsum('blhc,chv->blhv', o_lat, w_uv).reshape(B, L, H_B * V_B)
    y_b = o_b @ w_ob

    m = jax.nn.sigmoid(g_a) * y_a + jax.nn.sigmoid(g_b) * y_b
    x = layer_norm(ALPHA * x + m @ w_out, ln1_g, ln1_b)
    f1, f3 = jnp.split(x @ w_gu, 2, axis=-1)
    x = layer_norm(ALPHA * x + (jax.nn.silu(f1) * f3) @ w_down, ln2_g, ln2_b)
    return x, conv_new, S_new.astype(S0.dtype), ckv_new, kr_new


def setup_inputs(seed: int = 0) -> dict:
    key = jax.random.key(seed)
    ks = iter(jax.random.split(key, 32))

    def nrm(shape, scale):
        return jax.random.normal(next(ks), shape, jnp.float32) * scale

    dt = jax.random.uniform(next(ks), (DEPTH, H_A), jnp.float32, 0.001, 0.1)
    return {
        "x_prompt": nrm((BATCH, SEQ, D_MODEL), 1.0),
        "x_sample": nrm((DEC_BATCH, DEC_SEQ, D_MODEL), 1.0),
        "state_conv": nrm((DEPTH, DEC_BATCH, CONV_W - 1, C_QKV), 1.0),
        "state_gdn": nrm((DEPTH, DEC_BATCH, H_A, DK_A, DV_A), 0.05),
        "cache_ckv": nrm((DEPTH, DEC_BATCH, PAST_LEN, KV_RANK), 1.0),
        "cache_krope": nrm((DEPTH, DEC_BATCH, PAST_LEN, ROPE), 1.0),
        "w_in": nrm((DEPTH, D_MODEL, N_IN), D_MODEL ** -0.5),
        "conv_w": nrm((DEPTH, CONV_W, C_QKV), CONV_W ** -0.5),
        "a_log": jnp.log(jax.random.uniform(next(ks), (DEPTH, H_A), jnp.float32, 1.0, 16.0)),
        "dt_bias": jnp.log(jnp.expm1(dt)),
        "gdn_norm_g": 1.0 + nrm((DEPTH, DV_A), 0.02),
        "w_oa": nrm((DEPTH, V_A, D_MODEL), BETA_INIT * V_A ** -0.5),
        "q_norm_g": 1.0 + nrm((DEPTH, Q_RANK), 0.02),
        "w_uq": nrm((DEPTH, Q_RANK, H_B * (NOPE + ROPE)), Q_RANK ** -0.5),
        "kv_norm_g": 1.0 + nrm((DEPTH, KV_RANK), 0.02),
        "w_ukv": nrm((DEPTH, KV_RANK, H_B * (NOPE + V_B)), KV_RANK ** -0.5),
        "w_ob": nrm((DEPTH, H_B * V_B, D_MODEL), BETA_INIT * (H_B * V_B) ** -0.5),
        "w_out": nrm((DEPTH, D_MODEL, D_MODEL), BETA_INIT * D_MODEL ** -0.5),
        "ln1_g": 1.0 + nrm((DEPTH, D_MODEL), 0.02),
        "ln1_b": nrm((DEPTH, D_MODEL), 0.02),
        "w_gu": nrm((DEPTH, D_MODEL, 2 * D_FF), D_MODEL ** -0.5),
        "w_down": nrm((DEPTH, D_FF, D_MODEL), BETA_INIT * D_FF ** -0.5),
        "ln2_g": 1.0 + nrm((DEPTH, D_MODEL), 0.02),
        "ln2_b": nrm((DEPTH, D_MODEL), 0.02),
    }


def reference(x_prompt, x_sample, state_conv, state_gdn, cache_ckv, cache_krope, w_in, conv_w, a_log,
              dt_bias, gdn_norm_g, w_oa, q_norm_g, w_uq, kv_norm_g, w_ukv, w_ob, w_out, ln1_g, ln1_b,
              w_gu, w_down, ln2_g, ln2_b):
    Bp = x_prompt.shape[0]
    dtp = x_prompt.dtype
    zero_conv = jnp.zeros((Bp, CONV_W - 1, C_QKV), dtp)
    zero_S = jnp.zeros((Bp, H_A, DK_A, DV_A), dtp)
    empty_ckv = jnp.zeros((Bp, 0, KV_RANK), dtp)
    empty_kr = jnp.zeros((Bp, 0, ROPE), dtp)
    yp, ys = x_prompt, x_sample
    pc, pg, pk, pr, sc, sg, sk, sr = [], [], [], [], [], [], [], []
    for l in range(DEPTH):
        wl = (w_in[l], conv_w[l], a_log[l], dt_bias[l], gdn_norm_g[l], w_oa[l], q_norm_g[l], w_uq[l],
              kv_norm_g[l], w_ukv[l], w_ob[l], w_out[l], ln1_g[l], ln1_b[l], w_gu[l], w_down[l],
              ln2_g[l], ln2_b[l])
        yp, c1, g1, k1, r1 = trunk_layer(yp, zero_conv, zero_S, empty_ckv, empty_kr, *wl)
        ys, c2, g2, k2, r2 = trunk_layer(ys, state_conv[l], state_gdn[l], cache_ckv[l], cache_krope[l], *wl)
        pc.append(c1); pg.append(g1); pk.append(k1); pr.append(r1)
        sc.append(c2); sg.append(g2); sk.append(k2); sr.append(r2)
    return (yp, ys, jnp.stack(pc), jnp.stack(pg), jnp.stack(pk), jnp.stack(pr),
            jnp.stack(sc), jnp.stack(sg), jnp.stack(sk), jnp.stack(sr))
```

```python
import functools

import numpy as np
import jax
import jax.numpy as jnp
from jax import lax
from jax.experimental import pallas as pl
from jax.experimental.pallas import tpu as pltpu

F32 = jnp.float32
BF16 = jnp.bfloat16

D_MODEL = 1024
DEPTH = 2
CHUNK = 64
H_A = 8
DK_A = 128
DV_A = 128
QK_A = H_A * DK_A
V_A = H_A * DV_A
C_QKV = 2 * QK_A + V_A
CONV_W = 4
H_B = 8
NOPE = 128
ROPE = 64
V_B = 128
Q_RANK = 384
KV_RANK = 256
ROPE_THETA = 10000.0
ATTN_SCALE = (NOPE + ROPE) ** -0.5
D_FF = -(-8 * D_MODEL // (3 * 256)) * 256
ALPHA = (2 * DEPTH) ** 0.25
EPS = 1e-6
_SIZES = (C_QKV, V_A, H_A, H_A, Q_RANK, KV_RANK, ROPE, D_MODEL, D_MODEL)
_OFFS = tuple(int(v) for v in np.cumsum((0,) + _SIZES))

LANES = 128
VMEM_LIMIT = 56 * 1024 * 1024

COL_QKV = 0
COL_Z = COL_QKV + C_QKV
COL_GA = COL_Z + V_A
COL_GB = COL_GA + D_MODEL
COL_CKV = COL_GB + D_MODEL
COL_KRA = COL_CKV + KV_RANK
COL_CQ = COL_KRA + LANES
COL_KRB = COL_CQ + Q_RANK
COL_AB = COL_KRB + LANES
N_PROJ = COL_AB + LANES
KC_W = KV_RANK + LANES

GDN_CHUNK = 128


def _cparams(sem):
    return pltpu.CompilerParams(dimension_semantics=sem, vmem_limit_bytes=VMEM_LIMIT)


def _sigmoid(x):
    return jax.nn.sigmoid(x)


def _silu(x):
    return x * jax.nn.sigmoid(x)


def _mm(a, b):
    return jnp.dot(a.astype(BF16), b.astype(BF16), preferred_element_type=F32)


def _mm_nt(a, b):
    return lax.dot_general(a.astype(BF16), b.astype(BF16), (((1,), (1,)), ((), ())),
                           preferred_element_type=F32)


def _mm_tn(a, b):
    return lax.dot_general(a.astype(BF16), b.astype(BF16), (((0,), (0,)), ((), ())),
                           preferred_element_type=F32)


def _proj_body(x_ref, w_ref, o_ref, xb_ref):
    @pl.when(pl.program_id(1) == 0)
    def _():
        xb_ref[...] = x_ref[...].astype(BF16)

    o_ref[...] = jnp.dot(xb_ref[...], w_ref[...], preferred_element_type=F32)


def _proj_in(x, w, tm, tn):
    t, k = x.shape
    n = w.shape[1]
    return pl.pallas_call(
        _proj_body,
        grid=(t // tm, n // tn),
        in_specs=[pl.BlockSpec((tm, k), lambda i, j: (i, 0)),
                  pl.BlockSpec((k, tn), lambda i, j: (0, j))],
        out_specs=pl.BlockSpec((tm, tn), lambda i, j: (i, j)),
        out_shape=jax.ShapeDtypeStruct((t, n), F32),
        scratch_shapes=[pltpu.VMEM((tm, k), BF16)],
        compiler_params=_cparams(("parallel", "arbitrary")),
        name="proj_in",
    )(x, w)


def _gates_body(ab_ref, al_ref, dt_ref, o_ref, *, tm, l_pad, l_valid):
    x = ab_ref[...]
    lane = lax.broadcasted_iota(jnp.int32, x.shape, 1)
    xa = x + dt_ref[...]
    sp = jnp.maximum(xa, 0.0) + jnp.log1p(jnp.exp(-jnp.abs(xa)))
    g = -jnp.exp(al_ref[...]) * sp
    y = jnp.where(lane < H_A, g, _sigmoid(x))
    yt = y.T[0:2 * H_A, :]
    if l_valid < l_pad:
        col = lax.broadcasted_iota(jnp.int32, yt.shape, 1) + pl.program_id(0) * tm
        yt = jnp.where(col % l_pad < l_valid, yt, 0.0)
    r = lax.broadcasted_iota(jnp.int32, (GDN_CHUNK, GDN_CHUNK), 0)
    c = lax.broadcasted_iota(jnp.int32, (GDN_CHUNK, GDN_CHUNK), 1)
    tri = jnp.where(r <= c, 1.0, 0.0).astype(F32)
    for s in range(tm // GDN_CHUNK):
        sl = slice(s * GDN_CHUNK, (s + 1) * GDN_CHUNK)
        o_ref[0:H_A, sl] = jnp.dot(yt[0:H_A, sl], tri, precision=lax.Precision.HIGHEST,
                                   preferred_element_type=F32)
    o_ref[H_A:2 * H_A, :] = yt[H_A:2 * H_A, :]


def _gates(h1, al_lane, dt_lane, tm, l_pad, l_valid):
    t = h1.shape[0]
    body = functools.partial(_gates_body, tm=tm, l_pad=l_pad, l_valid=l_valid)
    return pl.pallas_call(
        body,
        grid=(t // tm,),
        in_specs=[pl.BlockSpec((tm, LANES), lambda i: (i, COL_AB // LANES)),
                  pl.BlockSpec((1, LANES), lambda i: (0, 0)),
                  pl.BlockSpec((1, LANES), lambda i: (0, 0))],
        out_specs=pl.BlockSpec((2 * H_A, tm), lambda i: (0, i)),
        out_shape=jax.ShapeDtypeStruct((2 * H_A, t), F32),
        compiler_params=_cparams(("parallel",)),
        name="gates",
    )(h1, al_lane, dt_lane)


def _gdn_body(q_ref, k_ref, v_ref, z_ref, gc_ref, bt_ref, cwq_ref, cwk_ref, cwv_ref,
              csq_ref, csk_ref, csv_ref, s0_ref, gn_ref, o_ref, so_ref,
              xq_ref, xk_ref, xv_ref, s_ref, *, tb, nt):
    t = pl.program_id(2)
    hist = CONV_W - 1
    base = 8

    @pl.when(t == 0)
    def _():
        s_ref[...] = s0_ref[0, 0]
        xq_ref[base - hist:base, :] = csq_ref[0]
        xk_ref[base - hist:base, :] = csk_ref[0]
        xv_ref[base - hist:base, :] = csv_ref[0]

    def conv(raw_ref, xs_ref, cw_ref):
        xs_ref[base:base + tb, :] = raw_ref[...]
        w = cw_ref[...]
        y = xs_ref[base - hist:base - hist + tb, :] * w[0:1, :]
        for j in range(1, CONV_W):
            y = y + xs_ref[base - hist + j:base - hist + j + tb, :] * w[j:j + 1, :]
        xs_ref[base - hist:base, :] = xs_ref[base + tb - hist:base + tb, :]
        return _silu(y)

    q = conv(q_ref, xq_ref, cwq_ref)
    k = conv(k_ref, xk_ref, cwk_ref)
    v = conv(v_ref, xv_ref, cwv_ref)
    q = q * lax.rsqrt(jnp.sum(q * q, -1, keepdims=True) + EPS) * (DK_A ** -0.5)
    k = k * lax.rsqrt(jnp.sum(k * k, -1, keepdims=True) + EPS)

    cc = GDN_CHUNK
    ii = lax.broadcasted_iota(jnp.int32, (cc, cc), 0)
    jj = lax.broadcasted_iota(jnp.int32, (cc, cc), 1)
    strict = ii > jj
    incl = ii >= jj
    eye = jnp.where(ii == jj, 1.0, 0.0).astype(F32)
    s_state = s_ref[...]
    for c in range(tb // cc):
        r = slice(c * cc, (c + 1) * cc)
        qc, kc, vc = q[r], k[r], v[r]
        m_row = jnp.broadcast_to(gc_ref[0, :, r], (cc, cc))
        m_col = m_row.T
        b_col = jnp.broadcast_to(bt_ref[0, :, r], (cc, cc)).T
        diff = m_col - m_row
        dec_strict = jnp.exp(jnp.where(strict, diff, -jnp.inf))
        dec_incl = jnp.exp(jnp.where(incl, diff, -jnp.inf))
        e_g = jnp.exp(m_col)
        g_last = m_col[cc - 1:cc, :]
        kq = _mm_nt(jnp.concatenate([kc, qc], axis=0), kc)
        a_mat = b_col * kq[0:cc] * dec_strict
        qk = kq[cc:2 * cc] * dec_incl
        x_inv = eye - jnp.where((ii >> 1) == (jj >> 1), a_mat, 0.0)
        sft = 1
        while (1 << sft) < cc:
            off = ((ii >> (sft + 1)) == (jj >> (sft + 1))) & ((ii >> sft) != (jj >> sft))
            l_mat = jnp.where(off, a_mat, 0.0)
            x_inv = x_inv - _mm(x_inv, _mm(l_mat, x_inv))
            sft += 1
        rhs = jnp.concatenate([vc * b_col, kc * (b_col * e_g)], axis=1)
        uw = _mm(x_inv, rhs)
        u, w = uw[:, 0:DV_A], uw[:, DV_A:]
        ws = _mm(jnp.concatenate([w, qc * e_g], axis=0), s_state)
        v_new = u - ws[0:cc]
        o = ws[cc:2 * cc] + _mm(qk, v_new)
        kd = kc * jnp.exp(g_last - m_col)
        s_state = s_state * jnp.exp(g_last) + _mm_tn(kd, v_new)
        o = o * lax.rsqrt(jnp.mean(o * o, -1, keepdims=True) + EPS) * gn_ref[...]
        o_ref[r, :] = (o * _silu(z_ref[r, :])).astype(o_ref.dtype)
    s_ref[...] = s_state

    @pl.when(t == nt - 1)
    def _():
        so_ref[0, 0] = s_state


def _gdn(h1, gates3, conv_w, conv_state, s0, gn, bsz, seq, tb):
    nt = seq // tb
    t_rows = bsz * seq
    body = functools.partial(_gdn_body, tb=tb, nt=nt)
    nb = C_QKV // LANES // 3

    def rows(off):
        return pl.BlockSpec((tb, LANES), lambda b, h, t: (b * nt + t, off + h))

    def cw(off):
        return pl.BlockSpec((CONV_W, LANES), lambda b, h, t: (0, off + h))

    def cs(off):
        return pl.BlockSpec((1, CONV_W - 1, LANES), lambda b, h, t: (b, 0, off + h))

    return pl.pallas_call(
        body,
        grid=(bsz, H_A, nt),
        in_specs=[rows(0), rows(nb), rows(2 * nb), rows(COL_Z // LANES),
                  pl.BlockSpec((1, 1, tb), lambda b, h, t: (h, 0, b * nt + t)),
                  pl.BlockSpec((1, 1, tb), lambda b, h, t: (H_A + h, 0, b * nt + t)),
                  cw(0), cw(nb), cw(2 * nb), cs(0), cs(nb), cs(2 * nb),
                  pl.BlockSpec((1, 1, DK_A, DV_A), lambda b, h, t: (b, h, 0, 0)),
                  pl.BlockSpec((1, DV_A), lambda b, h, t: (0, 0))],
        out_specs=[pl.BlockSpec((tb, LANES), lambda b, h, t: (b * nt + t, h)),
                   pl.BlockSpec((1, 1, DK_A, DV_A), lambda b, h, t: (b, h, 0, 0))],
        out_shape=[jax.ShapeDtypeStruct((t_rows, V_A), BF16),
                   jax.ShapeDtypeStruct((bsz, H_A, DK_A, DV_A), F32)],
        scratch_shapes=[pltpu.VMEM((tb + 8, LANES), F32)] * 3 + [pltpu.VMEM((DK_A, DV_A), F32)],
        compiler_params=_cparams(("parallel", "parallel", "arbitrary")),
        name="gdn",
    )(h1, h1, h1, h1, gates3, gates3, conv_w, conv_w, conv_w,
      conv_state, conv_state, conv_state, s0, gn)


def _mla_pre_body(cq_ref, ckv_ref, kra_ref, krb_ref, cos_ref, sin_ref, qg_ref, wuq_ref, wuk_ref,
                  kvg_ref, q_ref, kc_ref, ckvo_ref, kro_ref):
    cq = cq_ref[...]
    cqn = cq * lax.rsqrt(jnp.mean(cq * cq, -1, keepdims=True) + EPS) * qg_ref[...]
    qf = jnp.dot(cqn.astype(BF16), wuq_ref[...], preferred_element_type=F32)
    cos_t = cos_ref[...]
    sin_t = sin_ref[...]
    n_nope = H_B * NOPE
    n_rope = H_B * ROPE
    qr = (qf[:, n_nope:n_nope + n_rope] * cos_t + qf[:, n_nope + n_rope:] * sin_t) * ATTN_SCALE
    lane = lax.broadcasted_iota(jnp.int32, (cq.shape[0], LANES), 1)
    for h in range(H_B):
        ql = jnp.dot(qf[:, h * NOPE:(h + 1) * NOPE].astype(BF16), wuk_ref[h],
                     preferred_element_type=F32) * ATTN_SCALE
        blk = qr[:, (h // 2) * LANES:(h // 2 + 1) * LANES]
        keep = (lane < ROPE) if h % 2 == 0 else (lane >= ROPE)
        q_ref[0, h, :, 0:KV_RANK] = ql.astype(BF16)
        q_ref[0, h, :, KV_RANK:KC_W] = jnp.where(keep, blk, 0.0).astype(BF16)
    ckv = ckv_ref[...]
    ckvn = ckv * lax.rsqrt(jnp.mean(ckv * ckv, -1, keepdims=True) + EPS) * kvg_ref[...]
    kr2 = kra_ref[...] * cos_t[:, 0:LANES] + krb_ref[...] * sin_t[:, 0:LANES]
    ckvo_ref[...] = ckvn
    kro_ref[...] = kr2[:, 0:ROPE]
    kc_ref[:, 0:KV_RANK] = ckvn.astype(BF16)
    kc_ref[:, KV_RANK:KC_W] = kr2.astype(BF16)


def _mla_pre(h1, cos_t, sin_t, qg, wuq, wuk, kvg, bsz, seq, tm):
    t_rows = bsz * seq
    npb = seq // tm
    const2 = lambda i: (0, 0)
    return pl.pallas_call(
        _mla_pre_body,
        grid=(t_rows // tm,),
        in_specs=[pl.BlockSpec((tm, Q_RANK), lambda i: (i, COL_CQ // Q_RANK)),
                  pl.BlockSpec((tm, KV_RANK), lambda i: (i, COL_CKV // KV_RANK)),
                  pl.BlockSpec((tm, LANES), lambda i: (i, COL_KRA // LANES)),
                  pl.BlockSpec((tm, LANES), lambda i: (i, COL_KRB // LANES)),
                  pl.BlockSpec((tm, H_B * ROPE), lambda i: (i % npb, 0)),
                  pl.BlockSpec((tm, H_B * ROPE), lambda i: (i % npb, 0)),
                  pl.BlockSpec((1, Q_RANK), const2),
                  pl.BlockSpec(wuq.shape, const2),
                  pl.BlockSpec(wuk.shape, lambda i: (0, 0, 0)),
                  pl.BlockSpec((1, KV_RANK), const2)],
        out_specs=[pl.BlockSpec((1, H_B, tm, KC_W), lambda i: (i // npb, 0, i % npb, 0)),
                   pl.BlockSpec((tm, KC_W), lambda i: (i, 0)),
                   pl.BlockSpec((tm, KV_RANK), lambda i: (i, 0)),
                   pl.BlockSpec((tm, ROPE), lambda i: (i, 0))],
        out_shape=[jax.ShapeDtypeStruct((bsz, H_B, seq, KC_W), BF16),
                   jax.ShapeDtypeStruct((t_rows, KC_W), BF16),
                   jax.ShapeDtypeStruct((t_rows, KV_RANK), F32),
                   jax.ShapeDtypeStruct((t_rows, ROPE), F32)],
        compiler_params=_cparams(("parallel",)),
        name="mla_pre",
    )(h1, h1, h1, h1, cos_t, sin_t, qg, wuq, wuk, kvg)


def _attn_body(qi_ref, kj_ref, fl_ref, q_ref, k_ref, wuv_ref, o_ref, m_ref, l_ref, acc_ref,
               *, tq, tk, past, lk):
    s_idx = pl.program_id(1)
    flags = fl_ref[s_idx]
    rows = H_B * tq

    @pl.when((flags & 1) != 0)
    def _():
        m_ref[...] = jnp.full(m_ref.shape, -jnp.inf, F32)
        l_ref[...] = jnp.zeros(l_ref.shape, F32)
        acc_ref[...] = jnp.zeros(acc_ref.shape, F32)

    def update(masked):
        q = q_ref[0].reshape(rows, KC_W)
        kt = k_ref[0]
        s = lax.dot_general(q, kt, (((1,), (1,)), ((), ())), preferred_element_type=F32)
        if masked:
            qpos = past + qi_ref[s_idx] * tq + lax.broadcasted_iota(jnp.int32, (tq, tk), 0)
            kpos = kj_ref[s_idx] * tk + lax.broadcasted_iota(jnp.int32, (tq, tk), 1)
            ok = ((kpos // CHUNK) <= (qpos // CHUNK)) & (kpos < lk)
            bias = jnp.where(ok, 0.0, -jnp.inf).astype(F32)
            s = (s.reshape(H_B, tq, tk) + bias[None]).reshape(rows, tk)
        m_prev = m_ref[...]
        m_new = jnp.maximum(m_prev, jnp.max(s, -1, keepdims=True))
        p = jnp.exp(s - m_new)
        alpha = jnp.exp(m_prev - m_new)
        l_ref[...] = alpha * l_ref[...] + jnp.sum(p, -1, keepdims=True)
        acc_ref[...] = alpha * acc_ref[...] + jnp.dot(p.astype(BF16), kt[:, 0:KV_RANK],
                                                      preferred_element_type=F32)
        m_ref[...] = m_new

    @pl.when((flags & 4) != 0)
    def _():
        update(True)

    @pl.when((flags & 4) == 0)
    def _():
        update(False)

    @pl.when((flags & 2) != 0)
    def _():
        o = acc_ref[...] / l_ref[...]
        for h in range(H_B):
            oh = o[h * tq:(h + 1) * tq, :].astype(BF16)
            o_ref[0, :, h * V_B:(h + 1) * V_B] = jnp.dot(
                oh, wuv_ref[h], preferred_element_type=F32).astype(o_ref.dtype)


def _attn_steps(seq, past, lk, tq, tk):
    qi, kj, fl = [], [], []
    for i in range(seq // tq):
        p0 = past + i * tq
        p1 = p0 + tq - 1
        last_vis = min(lk - 1, (p1 // CHUNK) * CHUNK + CHUNK - 1)
        all_vis = min(lk - 1, (p0 // CHUNK) * CHUNK + CHUNK - 1)
        jmax = last_vis // tk
        for j in range(jmax + 1):
            f = (1 if j == 0 else 0) | (2 if j == jmax else 0)
            if (j + 1) * tk - 1 > all_vis:
                f |= 4
            qi.append(i), kj.append(j), fl.append(f)
    return (np.asarray(qi, np.int32), np.asarray(kj, np.int32), np.asarray(fl, np.int32))


def _attn(q, kc, wuv, past, lk, tq, tk):
    bsz, _, seq, _ = q.shape
    qi, kj, fl = _attn_steps(seq, past, lk, tq, tk)
    body = functools.partial(_attn_body, tq=tq, tk=tk, past=past, lk=lk)
    grid_spec = pltpu.PrefetchScalarGridSpec(
        num_scalar_prefetch=3,
        grid=(bsz, len(qi)),
        in_specs=[pl.BlockSpec((1, H_B, tq, KC_W), lambda b, s, qi, kj, fl: (b, 0, qi[s], 0)),
                  pl.BlockSpec((1, tk, KC_W), lambda b, s, qi, kj, fl: (b, kj[s], 0)),
                  pl.BlockSpec(wuv.shape, lambda b, s, qi, kj, fl: (0, 0, 0))],
        out_specs=pl.BlockSpec((1, tq, H_B * V_B), lambda b, s, qi, kj, fl: (b, qi[s], 0)),
        scratch_shapes=[pltpu.VMEM((H_B * tq, 1), F32), pltpu.VMEM((H_B * tq, 1), F32),
                        pltpu.VMEM((H_B * tq, KV_RANK), F32)],
    )
    return pl.pallas_call(
        body,
        grid_spec=grid_spec,
        out_shape=jax.ShapeDtypeStruct((bsz, seq, H_B * V_B), BF16),
        compiler_params=_cparams(("parallel", "arbitrary")),
        name="attn",
    )(jnp.asarray(qi), jnp.asarray(kj), jnp.asarray(fl), q, kc, wuv)


def _layer_norm(r, g, b):
    mu = jnp.mean(r, -1, keepdims=True)
    d = r - mu
    var = jnp.mean(d * d, -1, keepdims=True)
    return d * lax.rsqrt(var + EPS) * g + b


def _merge_body(oa_ref, ob_ref, ga_ref, gb_ref, x_ref, woa_ref, wob_ref, wout_ref, g_ref, b_ref, o_ref):
    ya = jnp.dot(oa_ref[...], woa_ref[...], preferred_element_type=F32)
    yb = jnp.dot(ob_ref[...], wob_ref[...], preferred_element_type=F32)
    m = _sigmoid(ga_ref[...]) * ya + _sigmoid(gb_ref[...]) * yb
    r = ALPHA * x_ref[...] + jnp.dot(m.astype(BF16), wout_ref[...], preferred_element_type=F32)
    o_ref[...] = _layer_norm(r, g_ref[...], b_ref[...])


def _merge(oa, ob, h1, x, woa, wob, wout, g, b, tm):
    t = x.shape[0]
    row = lambda i: (i, 0)
    const = lambda i: (0, 0)
    wspec = pl.BlockSpec((D_MODEL, D_MODEL), const)
    return pl.pallas_call(
        _merge_body,
        grid=(t // tm,),
        in_specs=[pl.BlockSpec((tm, V_A), row), pl.BlockSpec((tm, H_B * V_B), row),
                  pl.BlockSpec((tm, D_MODEL), lambda i: (i, COL_GA // D_MODEL)),
                  pl.BlockSpec((tm, D_MODEL), lambda i: (i, COL_GB // D_MODEL)),
                  pl.BlockSpec((tm, D_MODEL), row), wspec, wspec, wspec,
                  pl.BlockSpec((1, D_MODEL), const), pl.BlockSpec((1, D_MODEL), const)],
        out_specs=pl.BlockSpec((tm, D_MODEL), row),
        out_shape=jax.ShapeDtypeStruct((t, D_MODEL), F32),
        compiler_params=_cparams(("parallel",)),
        name="merge",
    )(oa, ob, h1, h1, x, woa, wob, wout, g, b)


def _ffn_body(x_ref, wg_ref, wu_ref, wd_ref, g_ref, b_ref, o_ref, h_ref, *, tf):
    x = x_ref[...]
    xb = x.astype(BF16)
    for j in range(D_FF // tf):
        sl = slice(j * tf, (j + 1) * tf)
        f1 = jnp.dot(xb, wg_ref[:, sl], preferred_element_type=F32)
        f3 = jnp.dot(xb, wu_ref[:, sl], preferred_element_type=F32)
        h_ref[:, sl] = (_silu(f1) * f3).astype(BF16)
    y = jnp.dot(h_ref[...], wd_ref[...], preferred_element_type=F32)
    o_ref[...] = _layer_norm(ALPHA * x + y, g_ref[...], b_ref[...])


def _ffn(x, wg, wu, wd, g, b, tm, tf):
    t = x.shape[0]
    row = lambda i: (i, 0)
    const = lambda i: (0, 0)
    single = pl.Buffered(1)
    return pl.pallas_call(
        functools.partial(_ffn_body, tf=tf),
        grid=(t // tm,),
        in_specs=[pl.BlockSpec((tm, D_MODEL), row),
                  pl.BlockSpec((D_MODEL, D_FF), const, pipeline_mode=single),
                  pl.BlockSpec((D_MODEL, D_FF), const, pipeline_mode=single),
                  pl.BlockSpec((D_FF, D_MODEL), const, pipeline_mode=single),
                  pl.BlockSpec((1, D_MODEL), const), pl.BlockSpec((1, D_MODEL), const)],
        out_specs=pl.BlockSpec((tm, D_MODEL), row),
        out_shape=jax.ShapeDtypeStruct((t, D_MODEL), F32),
        scratch_shapes=[pltpu.VMEM((tm, D_FF), BF16)],
        compiler_params=_cparams(("parallel",)),
        name="ffn",
    )(x, wg, wu, wd, g, b)


def _prep_layer_weights(w_in, conv_w, a_log, dt_bias, gdn_norm_g, w_oa, q_norm_g, w_uq, kv_norm_g,
                        w_ukv, w_ob, w_out, ln1_g, ln1_b, w_gu, w_down, ln2_g, ln2_b):
    seg = lambda i: w_in[:, _OFFS[i]:_OFFS[i + 1]]
    qkv, z, a, b, c_q, c_kv, k_r, g_a, g_b = (seg(i) for i in range(9))
    half = ROPE // 2
    k_r_rot = jnp.concatenate([-k_r[:, half:], k_r[:, :half]], axis=1)
    pad = jnp.zeros((D_MODEL, LANES - 2 * H_A), w_in.dtype)
    w_proj = jnp.concatenate([qkv, z, g_a, g_b, c_kv, k_r, k_r, c_q, k_r_rot, k_r_rot, a, b, pad],
                             axis=1).astype(BF16)
    lane_pad = jnp.zeros((LANES - H_A,), F32)
    al_lane = jnp.concatenate([a_log.astype(F32), lane_pad]).reshape(1, LANES)
    dt_lane = jnp.concatenate([dt_bias.astype(F32), lane_pad]).reshape(1, LANES)
    uq = w_uq.reshape(Q_RANK, H_B, NOPE + ROPE)
    uq_nope = uq[:, :, :NOPE].reshape(Q_RANK, H_B * NOPE)
    uq_rope = uq[:, :, NOPE:]
    uq_rot = jnp.concatenate([-uq_rope[:, :, half:], uq_rope[:, :, :half]], axis=2)
    w_uq_ext = jnp.concatenate([uq_nope, uq_rope.reshape(Q_RANK, H_B * ROPE),
                                uq_rot.reshape(Q_RANK, H_B * ROPE)], axis=1).astype(BF16)
    ukv = w_ukv.reshape(KV_RANK, H_B, NOPE + V_B)
    w_uk_t = jnp.transpose(ukv[:, :, :NOPE], (1, 2, 0)).astype(BF16)
    w_uv = jnp.transpose(ukv[:, :, NOPE:], (1, 0, 2)).astype(BF16)
    return dict(
        w_proj=w_proj, conv_w=conv_w.astype(F32), al_lane=al_lane, dt_lane=dt_lane,
        gn=gdn_norm_g.reshape(1, DV_A).astype(F32), w_oa=w_oa.astype(BF16),
        qg=q_norm_g.reshape(1, Q_RANK).astype(F32), w_uq=w_uq_ext, w_uk_t=w_uk_t, w_uv=w_uv,
        kvg=kv_norm_g.reshape(1, KV_RANK).astype(F32), w_ob=w_ob.astype(BF16),
        w_out=w_out.astype(BF16), ln1_g=ln1_g.reshape(1, D_MODEL), ln1_b=ln1_b.reshape(1, D_MODEL),
        w_g=w_gu[:, :D_FF].astype(BF16), w_u=w_gu[:, D_FF:].astype(BF16), w_down=w_down.astype(BF16),
        ln2_g=ln2_g.reshape(1, D_MODEL), ln2_b=ln2_b.reshape(1, D_MODEL))


def _rope_tables(past, seq):
    half = ROPE // 2
    inv = ROPE_THETA ** (-jnp.arange(half, dtype=F32) / half)
    ang = (past + jnp.arange(seq)).astype(F32)[:, None] * inv[None, :]
    cos = jnp.tile(jnp.cos(ang), (1, 2 * H_B))
    sin = jnp.tile(jnp.sin(ang), (1, 2 * H_B))
    return cos, sin


def _tiles(seq, l_valid):
    prompt = seq >= 1024
    return dict(
        tm_proj=1024, tn_proj=1024,
        tm_gates=1024,
        tb_gdn=256 if prompt else GDN_CHUNK,
        tm_mla=512 if prompt else seq,
        tq=256 if prompt else seq,
        tk=512 if prompt else 1408,
        tm_merge=256, tm_ffn=256, tf=D_FF // 2)


def _trunk_layer(x, conv_state, s0, kc_past, wl, bsz, seq, l_valid, past):
    ts = _tiles(seq, l_valid)
    t_rows = bsz * seq
    h1 = _proj_in(x, wl["w_proj"], min(ts["tm_proj"], t_rows), ts["tn_proj"])
    gates = _gates(h1, wl["al_lane"], wl["dt_lane"], min(ts["tm_gates"], t_rows), seq, l_valid)
    o_a, s_new = _gdn(h1, gates.reshape(2 * H_A, 1, t_rows), wl["conv_w"], conv_state, s0, wl["gn"],
                      bsz, seq, ts["tb_gdn"])
    cos_t, sin_t = _rope_tables(past, seq)
    q, kc_new, ckv_new, kr_new = _mla_pre(h1, cos_t, sin_t, wl["qg"], wl["w_uq"], wl["w_uk_t"],
                                          wl["kvg"], bsz, seq, ts["tm_mla"])
    kc_new = kc_new.reshape(bsz, seq, KC_W)[:, :l_valid]
    kc = kc_new if kc_past is None else jnp.concatenate([kc_past, kc_new], axis=1)
    lk = kc.shape[1]
    lk_pad = -(-lk // ts["tk"]) * ts["tk"]
    if lk_pad != lk:
        kc = jnp.pad(kc, ((0, 0), (0, lk_pad - lk), (0, 0)))
    o_b = _attn(q, kc, wl["w_uv"], past, lk, ts["tq"], ts["tk"]).reshape(t_rows, H_B * V_B)
    x1 = _merge(o_a, o_b, h1, x, wl["w_oa"], wl["w_ob"], wl["w_out"], wl["ln1_g"], wl["ln1_b"],
                ts["tm_merge"])
    x2 = _ffn(x1, wl["w_g"], wl["w_u"], wl["w_down"], wl["ln2_g"], wl["ln2_b"], ts["tm_ffn"], ts["tf"])
    h3 = h1.reshape(bsz, seq, N_PROJ)
    conv_new = h3[:, l_valid - (CONV_W - 1):l_valid, COL_QKV:COL_QKV + C_QKV]
    ckv_new = ckv_new.reshape(bsz, seq, KV_RANK)[:, :l_valid]
    kr_new = kr_new.reshape(bsz, seq, ROPE)[:, :l_valid]
    return x2, conv_new, s_new, ckv_new, kr_new


def kernel(x_prompt, x_sample, state_conv, state_gdn, cache_ckv, cache_krope, w_in, conv_w, a_log, dt_bias, gdn_norm_g, w_oa, q_norm_g, w_uq, kv_norm_g, w_ukv, w_ob, w_out, ln1_g, ln1_b, w_gu, w_down, ln2_g, ln2_b):
    bp, lp, _ = x_prompt.shape
    bs, ls, _ = x_sample.shape
    past = cache_ckv.shape[2]
    ls_pad = -(-ls // GDN_CHUNK) * GDN_CHUNK
    yp = x_prompt.reshape(bp * lp, D_MODEL)
    ys = jnp.pad(x_sample, ((0, 0), (0, ls_pad - ls), (0, 0))).reshape(bs * ls_pad, D_MODEL)
    zero_conv = jnp.zeros((bp, CONV_W - 1, C_QKV), F32)
    zero_s = jnp.zeros((bp, H_A, DK_A, DV_A), F32)
    outs_p, outs_s = [], []
    for l in range(w_in.shape[0]):
        wl = _prep_layer_weights(w_in[l], conv_w[l], a_log[l], dt_bias[l], gdn_norm_g[l], w_oa[l],
                                 q_norm_g[l], w_uq[l], kv_norm_g[l], w_ukv[l], w_ob[l], w_out[l],
                                 ln1_g[l], ln1_b[l], w_gu[l], w_down[l], ln2_g[l], ln2_b[l])
        yp, *rest_p = _trunk_layer(yp, zero_conv, zero_s, None, wl, bp, lp, lp, 0)
        kc_past = jnp.concatenate([cache_ckv[l], cache_krope[l], cache_krope[l]], axis=-1).astype(BF16)
        ys, *rest_s = _trunk_layer(ys, state_conv[l], state_gdn[l], kc_past, wl, bs, ls_pad, ls, past)
        outs_p.append(rest_p)
        outs_s.append(rest_s)
    stack = lambda outs, i: jnp.stack([o[i] for o in outs])
    y_prompt = yp.reshape(bp, lp, D_MODEL)
    y_sample = ys.reshape(bs, ls_pad, D_MODEL)[:, :ls]
    return (y_prompt, y_sample,
            stack(outs_p, 0), stack(outs_p, 1), stack(outs_p, 2), stack(outs_p, 3),
            stack(outs_s, 0), stack(outs_s, 1), stack(outs_s, 2), stack(outs_s, 3))
```

```python
import functools

import numpy as np
import jax
import jax.numpy as jnp
from jax import lax
from jax.experimental import pallas as pl
from jax.experimental.pallas import tpu as pltpu

F32 = jnp.float32
BF16 = jnp.bfloat16

D_MODEL = 1024
DEPTH = 2
CHUNK = 64
H_A = 8
DK_A = 128
DV_A = 128
QK_A = H_A * DK_A
V_A = H_A * DV_A
C_QKV = 2 * QK_A + V_A
CONV_W = 4
H_B = 8
NOPE = 128
ROPE = 64
V_B = 128
Q_RANK = 384
KV_RANK = 256
ROPE_THETA = 10000.0
ATTN_SCALE = (NOPE + ROPE) ** -0.5
Q_SCALE = ATTN_SCALE * float(np.log2(np.e))
D_FF = -(-8 * D_MODEL // (3 * 256)) * 256
ALPHA = (2 * DEPTH) ** 0.25
EPS = 1e-6
_SIZES = (C_QKV, V_A, H_A, H_A, Q_RANK, KV_RANK, ROPE, D_MODEL, D_MODEL)
_OFFS = tuple(int(v) for v in np.cumsum((0,) + _SIZES))

LANES = 128
VMEM_LIMIT = 56 * 1024 * 1024

COL_QKV = 0
COL_Z = COL_QKV + C_QKV
COL_GA = COL_Z + V_A
COL_GB = COL_GA + D_MODEL
COL_CKV = COL_GB + D_MODEL
COL_KRA = COL_CKV + KV_RANK
COL_CQ = COL_KRA + LANES
COL_KRB = COL_CQ + Q_RANK
COL_AB = COL_KRB + LANES
N_PROJ = COL_AB + LANES
KC_W = KV_RANK + LANES

GDN_CHUNK = 128


def _cparams(sem):
    return pltpu.CompilerParams(dimension_semantics=sem, vmem_limit_bytes=VMEM_LIMIT)


def _sigmoid(x):
    return jax.nn.sigmoid(x)


def _silu(x):
    return x * jax.nn.sigmoid(x)


def _mm(a, b):
    return jnp.dot(a.astype(BF16), b.astype(BF16), preferred_element_type=F32)


def _mm_nt(a, b):
    return lax.dot_general(a.astype(BF16), b.astype(BF16), (((1,), (1,)), ((), ())),
                           preferred_element_type=F32)


def _mm_tn(a, b):
    return lax.dot_general(a.astype(BF16), b.astype(BF16), (((0,), (0,)), ((), ())),
                           preferred_element_type=F32)


def _proj_body(x_ref, w_ref, o_ref, xb_ref):
    @pl.when(pl.program_id(1) == 0)
    def _():
        xb_ref[...] = x_ref[...].astype(BF16)

    o_ref[...] = jnp.dot(xb_ref[...], w_ref[...], preferred_element_type=F32)


def _proj_in(x, w, tm, tn):
    t, k = x.shape
    n = w.shape[1]
    return pl.pallas_call(
        _proj_body,
        grid=(t // tm, n // tn),
        in_specs=[pl.BlockSpec((tm, k), lambda i, j: (i, 0)),
                  pl.BlockSpec((k, tn), lambda i, j: (0, j))],
        out_specs=pl.BlockSpec((tm, tn), lambda i, j: (i, j)),
        out_shape=jax.ShapeDtypeStruct((t, n), F32),
        scratch_shapes=[pltpu.VMEM((tm, k), BF16)],
        compiler_params=_cparams(("parallel", "arbitrary")),
        name="proj_in",
    )(x, w)


def _gates_body(ab_ref, al_ref, dt_ref, o_ref, *, tm, l_pad, l_valid):
    x = ab_ref[...]
    lane = lax.broadcasted_iota(jnp.int32, x.shape, 1)
    xa = x + dt_ref[...]
    sp = jnp.maximum(xa, 0.0) + jnp.log1p(jnp.exp(-jnp.abs(xa)))
    g = -jnp.exp(al_ref[...]) * sp
    y = jnp.where(lane < H_A, g, _sigmoid(x))
    yt = y.T[0:2 * H_A, :]
    if l_valid < l_pad:
        col = lax.broadcasted_iota(jnp.int32, yt.shape, 1) + pl.program_id(0) * tm
        yt = jnp.where(col % l_pad < l_valid, yt, 0.0)
    r = lax.broadcasted_iota(jnp.int32, (GDN_CHUNK, GDN_CHUNK), 0)
    c = lax.broadcasted_iota(jnp.int32, (GDN_CHUNK, GDN_CHUNK), 1)
    tri = jnp.where(r <= c, 1.0, 0.0).astype(F32)
    for s in range(tm // GDN_CHUNK):
        sl = slice(s * GDN_CHUNK, (s + 1) * GDN_CHUNK)
        o_ref[0:H_A, sl] = jnp.dot(yt[0:H_A, sl], tri, precision=lax.Precision.HIGHEST,
                                   preferred_element_type=F32)
    o_ref[H_A:2 * H_A, :] = yt[H_A:2 * H_A, :]


def _gates(h1, al_lane, dt_lane, tm, l_pad, l_valid):
    t = h1.shape[0]
    body = functools.partial(_gates_body, tm=tm, l_pad=l_pad, l_valid=l_valid)
    return pl.pallas_call(
        body,
        grid=(t // tm,),
        in_specs=[pl.BlockSpec((tm, LANES), lambda i: (i, COL_AB // LANES)),
                  pl.BlockSpec((1, LANES), lambda i: (0, 0)),
                  pl.BlockSpec((1, LANES), lambda i: (0, 0))],
        out_specs=pl.BlockSpec((2 * H_A, tm), lambda i: (0, i)),
        out_shape=jax.ShapeDtypeStruct((2 * H_A, t), F32),
        compiler_params=_cparams(("parallel",)),
        name="gates",
    )(h1, al_lane, dt_lane)


def _gdn_body(q_ref, k_ref, v_ref, z_ref, gc_ref, bt_ref, cwq_ref, cwk_ref, cwv_ref,
              csq_ref, csk_ref, csv_ref, s0_ref, gn_ref, o_ref, so_ref,
              xq_ref, xk_ref, xv_ref, s_ref, *, tb, nt, hb):
    t = pl.program_id(2)
    hist = CONV_W - 1
    base = 8

    @pl.when(t == 0)
    def _():
        s_ref[...] = s0_ref[0]
        xq_ref[base - hist:base, :] = csq_ref[0]
        xk_ref[base - hist:base, :] = csk_ref[0]
        xv_ref[base - hist:base, :] = csv_ref[0]

    xq_ref[base:base + tb, :] = q_ref[...]
    xk_ref[base:base + tb, :] = k_ref[...]
    xv_ref[base:base + tb, :] = v_ref[...]

    def conv(xs_ref, cw_ref, ln):
        y = xs_ref[base - hist:base - hist + tb, ln] * cw_ref[0:1, ln]
        for j in range(1, CONV_W):
            y = y + xs_ref[base - hist + j:base - hist + j + tb, ln] * cw_ref[j:j + 1, ln]
        return _silu(y)

    cc = GDN_CHUNK
    ii = lax.broadcasted_iota(jnp.int32, (cc, cc), 0)
    jj = lax.broadcasted_iota(jnp.int32, (cc, cc), 1)
    strict = ii > jj
    incl = ii >= jj
    eye = jnp.where(ii == jj, 1.0, 0.0).astype(F32)
    heads = range(hb)
    lanes = [slice(hd * LANES, (hd + 1) * LANES) for hd in heads]
    units = [(hd, c) for hd in heads for c in range(tb // cc)]
    rows = {u: slice(u[1] * cc, (u[1] + 1) * cc) for u in units}
    q = [conv(xq_ref, cwq_ref, ln) for ln in lanes]
    k = [conv(xk_ref, cwk_ref, ln) for ln in lanes]
    v = [conv(xv_ref, cwv_ref, ln) for ln in lanes]
    q = [x * lax.rsqrt(jnp.sum(x * x, -1, keepdims=True) + EPS) * (DK_A ** -0.5) for x in q]
    k = [x * lax.rsqrt(jnp.sum(x * x, -1, keepdims=True) + EPS) for x in k]
    qc = {u: q[u[0]][rows[u]] for u in units}
    kc = {u: k[u[0]][rows[u]] for u in units}
    vc = {u: v[u[0]][rows[u]] for u in units}
    m_row = {u: jnp.broadcast_to(gc_ref[u[0], :, rows[u]], (cc, cc)) for u in units}
    m_col = {u: m_row[u].T for u in units}
    b_col = {u: jnp.broadcast_to(bt_ref[u[0], :, rows[u]], (cc, cc)).T for u in units}
    kq = {u: _mm_nt(jnp.concatenate([kc[u], qc[u]], axis=0), kc[u]) for u in units}
    diff = {u: m_col[u] - m_row[u] for u in units}
    a_mat = {u: b_col[u] * kq[u][0:cc] * jnp.exp(jnp.where(strict, diff[u], -jnp.inf)) for u in units}
    qk = {u: kq[u][cc:2 * cc] * jnp.exp(jnp.where(incl, diff[u], -jnp.inf)) for u in units}
    e_g = {u: jnp.exp(m_col[u]) for u in units}
    x_inv = {u: eye - jnp.where((ii >> 1) == (jj >> 1), a_mat[u], 0.0) for u in units}
    sft = 1
    while (1 << sft) < cc:
        off = ((ii >> (sft + 1)) == (jj >> (sft + 1))) & ((ii >> sft) != (jj >> sft))
        lx = {u: _mm(jnp.where(off, a_mat[u], 0.0), x_inv[u]) for u in units}
        x_inv = {u: x_inv[u] - _mm(x_inv[u], lx[u]) for u in units}
        sft += 1
    uw = {u: _mm(x_inv[u], jnp.concatenate([vc[u] * b_col[u], kc[u] * (b_col[u] * e_g[u])], axis=1))
          for u in units}
    wq = {u: jnp.concatenate([uw[u][:, DV_A:], qc[u] * e_g[u]], axis=0) for u in units}
    g_last = {u: m_col[u][cc - 1:cc, :] for u in units}
    kd = {u: kc[u] * jnp.exp(g_last[u] - m_col[u]) for u in units}
    s_state = [s_ref[hd] for hd in heads]
    for c in range(tb // cc):
        ws = [_mm(wq[(hd, c)], s_state[hd]) for hd in heads]
        v_new = [uw[(hd, c)][:, 0:DV_A] - ws[hd][0:cc] for hd in heads]
        s_state = [s_state[hd] * jnp.exp(g_last[(hd, c)]) + _mm_tn(kd[(hd, c)], v_new[hd]) for hd in heads]
        o = [ws[hd][cc:2 * cc] + _mm(qk[(hd, c)], v_new[hd]) for hd in heads]
        o = [x * lax.rsqrt(jnp.mean(x * x, -1, keepdims=True) + EPS) * gn_ref[...] for x in o]
        for hd in heads:
            r = rows[(hd, c)]
            o_ref[r, lanes[hd]] = (o[hd] * _silu(z_ref[r, lanes[hd]])).astype(o_ref.dtype)
    for hd in heads:
        s_ref[hd] = s_state[hd]

    xq_ref[base - hist:base, :] = xq_ref[base + tb - hist:base + tb, :]
    xk_ref[base - hist:base, :] = xk_ref[base + tb - hist:base + tb, :]
    xv_ref[base - hist:base, :] = xv_ref[base + tb - hist:base + tb, :]

    @pl.when(t == nt - 1)
    def _():
        so_ref[0] = s_ref[...]


def _gdn(h1, gates3, conv_w, conv_state, s0, gn, bsz, seq, tb, hb):
    nt = seq // tb
    t_rows = bsz * seq
    body = functools.partial(_gdn_body, tb=tb, nt=nt, hb=hb)
    width = hb * LANES
    ng = H_A // hb

    def rows(seg):
        return pl.BlockSpec((tb, width), lambda b, g, t: (b * nt + t, seg * ng + g))

    def cw(seg):
        return pl.BlockSpec((CONV_W, width), lambda b, g, t: (0, seg * ng + g))

    def cs(seg):
        return pl.BlockSpec((1, CONV_W - 1, width), lambda b, g, t: (b, 0, seg * ng + g))

    return pl.pallas_call(
        body,
        grid=(bsz, ng, nt),
        in_specs=[rows(0), rows(1), rows(2), rows(COL_Z // QK_A),
                  pl.BlockSpec((hb, 1, tb), lambda b, g, t: (g, 0, b * nt + t)),
                  pl.BlockSpec((hb, 1, tb), lambda b, g, t: (ng + g, 0, b * nt + t)),
                  cw(0), cw(1), cw(2), cs(0), cs(1), cs(2),
                  pl.BlockSpec((1, hb, DK_A, DV_A), lambda b, g, t: (b, g, 0, 0)),
                  pl.BlockSpec((1, DV_A), lambda b, g, t: (0, 0))],
        out_specs=[pl.BlockSpec((tb, width), lambda b, g, t: (b * nt + t, g)),
                   pl.BlockSpec((1, hb, DK_A, DV_A), lambda b, g, t: (b, g, 0, 0))],
        out_shape=[jax.ShapeDtypeStruct((t_rows, V_A), BF16),
                   jax.ShapeDtypeStruct((bsz, H_A, DK_A, DV_A), F32)],
        scratch_shapes=[pltpu.VMEM((tb + 8, width), F32)] * 3 + [pltpu.VMEM((hb, DK_A, DV_A), F32)],
        compiler_params=_cparams(("parallel", "parallel", "arbitrary")),
        name="gdn",
    )(h1, h1, h1, h1, gates3, gates3, conv_w, conv_w, conv_w,
      conv_state, conv_state, conv_state, s0, gn)


def _mla_pre_body(cq_ref, ckv_ref, kra_ref, krb_ref, cos_ref, sin_ref, qg_ref, wuq_ref, wuk_ref,
                  kvg_ref, q_ref, kc_ref, vt_ref, ckvo_ref, kro_ref):
    cq = cq_ref[...]
    cqn = cq * lax.rsqrt(jnp.mean(cq * cq, -1, keepdims=True) + EPS) * qg_ref[...]
    qf = jnp.dot(cqn.astype(BF16), wuq_ref[...], preferred_element_type=F32)
    cos_t = cos_ref[...]
    sin_t = sin_ref[...]
    n_nope = H_B * NOPE
    n_rope = H_B * ROPE
    qr = (qf[:, n_nope:n_nope + n_rope] * cos_t + qf[:, n_nope + n_rope:] * sin_t) * Q_SCALE
    lane = lax.broadcasted_iota(jnp.int32, (cq.shape[0], LANES), 1)
    for h in range(H_B):
        ql = jnp.dot(qf[:, h * NOPE:(h + 1) * NOPE].astype(BF16), wuk_ref[h],
                     preferred_element_type=F32) * Q_SCALE
        blk = qr[:, (h // 2) * LANES:(h // 2 + 1) * LANES]
        keep = (lane < ROPE) if h % 2 == 0 else (lane >= ROPE)
        q_ref[0, h, :, 0:KV_RANK] = ql.astype(BF16)
        q_ref[0, h, :, KV_RANK:KC_W] = jnp.where(keep, blk, 0.0).astype(BF16)
    ckv = ckv_ref[...]
    ckvn = ckv * lax.rsqrt(jnp.mean(ckv * ckv, -1, keepdims=True) + EPS) * kvg_ref[...]
    kr2 = kra_ref[...] * cos_t[:, 0:LANES] + krb_ref[...] * sin_t[:, 0:LANES]
    ckvo_ref[...] = ckvn
    kro_ref[...] = kr2[:, 0:ROPE]
    kc_ref[:, 0:KV_RANK] = ckvn.astype(BF16)
    kc_ref[:, KV_RANK:KC_W] = kr2.astype(BF16)
    vt_ref[0] = ckvn.T.astype(BF16)


def _mla_pre(h1, cos_t, sin_t, qg, wuq, wuk, kvg, bsz, seq, tm):
    t_rows = bsz * seq
    npb = seq // tm
    const2 = lambda i: (0, 0)
    return pl.pallas_call(
        _mla_pre_body,
        grid=(t_rows // tm,),
        in_specs=[pl.BlockSpec((tm, Q_RANK), lambda i: (i, COL_CQ // Q_RANK)),
                  pl.BlockSpec((tm, KV_RANK), lambda i: (i, COL_CKV // KV_RANK)),
                  pl.BlockSpec((tm, LANES), lambda i: (i, COL_KRA // LANES)),
                  pl.BlockSpec((tm, LANES), lambda i: (i, COL_KRB // LANES)),
                  pl.BlockSpec((tm, H_B * ROPE), lambda i: (i % npb, 0)),
                  pl.BlockSpec((tm, H_B * ROPE), lambda i: (i % npb, 0)),
                  pl.BlockSpec((1, Q_RANK), const2),
                  pl.BlockSpec(wuq.shape, const2),
                  pl.BlockSpec(wuk.shape, lambda i: (0, 0, 0)),
                  pl.BlockSpec((1, KV_RANK), const2)],
        out_specs=[pl.BlockSpec((1, H_B, tm, KC_W), lambda i: (i // npb, 0, i % npb, 0)),
                   pl.BlockSpec((tm, KC_W), lambda i: (i, 0)),
                   pl.BlockSpec((1, KV_RANK, tm), lambda i: (i // npb, 0, i % npb)),
                   pl.BlockSpec((tm, KV_RANK), lambda i: (i, 0)),
                   pl.BlockSpec((tm, ROPE), lambda i: (i, 0))],
        out_shape=[jax.ShapeDtypeStruct((bsz, H_B, seq, KC_W), BF16),
                   jax.ShapeDtypeStruct((t_rows, KC_W), BF16),
                   jax.ShapeDtypeStruct((bsz, KV_RANK, seq), BF16),
                   jax.ShapeDtypeStruct((t_rows, KV_RANK), F32),
                   jax.ShapeDtypeStruct((t_rows, ROPE), F32)],
        compiler_params=_cparams(("parallel",)),
        name="mla_pre",
    )(h1, h1, h1, h1, cos_t, sin_t, qg, wuq, wuk, kvg)


def _attn_body(qi_ref, kj_ref, fl_ref, q_ref, k_ref, vt_ref, wuv_ref, o_ref, m_ref, l_ref, acc_ref,
               *, tq, tk, past, lk):
    s_idx = pl.program_id(1)
    flags = fl_ref[s_idx]

    @pl.when((flags & 1) != 0)
    def _():
        m_ref[...] = jnp.full(m_ref.shape, -jnp.inf, F32)
        l_ref[...] = jnp.zeros(l_ref.shape, F32)
        acc_ref[...] = jnp.zeros(acc_ref.shape, F32)

    def scores(h, bias):
        s = lax.dot_general(k_ref[0], q_ref[0, h], (((1,), (1,)), ((), ())),
                            preferred_element_type=F32)
        return s if bias is None else s + bias

    def update(masked):
        bias = None
        if masked:
            shift = CHUNK.bit_length() - 1
            kpos = kj_ref[s_idx] * tk + lax.broadcasted_iota(jnp.int32, (tk, tq), 0)
            qpos = past + qi_ref[s_idx] * tq + lax.broadcasted_iota(jnp.int32, (tk, tq), 1)
            ok = ((kpos >> shift) <= (qpos >> shift)) & (kpos < lk)
            bias = jnp.where(ok, 0.0, -jnp.inf).astype(F32)
        s_next = scores(0, bias)
        for h in range(H_B):
            s = s_next
            if h + 1 < H_B:
                s_next = scores(h + 1, bias)
            m_prev = m_ref[h]
            m_new = jnp.maximum(m_prev, jnp.max(s, 0, keepdims=True))
            p = jnp.exp2(s - m_new)
            alpha = jnp.exp2(m_prev - m_new)
            l_ref[h] = alpha * l_ref[h] + jnp.sum(p, 0, keepdims=True)
            pv = jnp.dot(vt_ref[0], p.astype(BF16), preferred_element_type=F32)
            acc_ref[h] = alpha * acc_ref[h] + pv
            m_ref[h] = m_new

    @pl.when((flags & 4) != 0)
    def _():
        update(True)

    @pl.when((flags & 4) == 0)
    def _():
        update(False)

    @pl.when((flags & 2) != 0)
    def _():
        for h in range(H_B):
            o_t = (acc_ref[h] * (1.0 / l_ref[h])).astype(BF16)
            ob_t = jnp.dot(wuv_ref[h], o_t, preferred_element_type=F32)
            o_ref[0, :, h * V_B:(h + 1) * V_B] = ob_t.T.astype(o_ref.dtype)


def _attn_steps(seq, past, lk, tq, tk):
    qi, kj, fl = [], [], []
    for i in range(seq // tq):
        p0 = past + i * tq
        p1 = p0 + tq - 1
        last_vis = min(lk - 1, (p1 // CHUNK) * CHUNK + CHUNK - 1)
        all_vis = min(lk - 1, (p0 // CHUNK) * CHUNK + CHUNK - 1)
        jmax = last_vis // tk
        for j in range(jmax + 1):
            f = (1 if j == 0 else 0) | (2 if j == jmax else 0)
            if (j + 1) * tk - 1 > all_vis:
                f |= 4
            qi.append(i), kj.append(j), fl.append(f)
    return (np.asarray(qi, np.int32), np.asarray(kj, np.int32), np.asarray(fl, np.int32))


def _attn(q, kc, vt, wuv_t, past, lk, tq, tk):
    bsz, _, seq, _ = q.shape
    qi, kj, fl = _attn_steps(seq, past, lk, tq, tk)
    body = functools.partial(_attn_body, tq=tq, tk=tk, past=past, lk=lk)
    grid_spec = pltpu.PrefetchScalarGridSpec(
        num_scalar_prefetch=3,
        grid=(bsz, len(qi)),
        in_specs=[pl.BlockSpec((1, H_B, tq, KC_W), lambda b, s, qi, kj, fl: (b, 0, qi[s], 0)),
                  pl.BlockSpec((1, tk, KC_W), lambda b, s, qi, kj, fl: (b, kj[s], 0)),
                  pl.BlockSpec((1, KV_RANK, tk), lambda b, s, qi, kj, fl: (b, 0, kj[s])),
                  pl.BlockSpec(wuv_t.shape, lambda b, s, qi, kj, fl: (0, 0, 0))],
        out_specs=pl.BlockSpec((1, tq, H_B * V_B), lambda b, s, qi, kj, fl: (b, qi[s], 0)),
        scratch_shapes=[pltpu.VMEM((H_B, 1, tq), F32), pltpu.VMEM((H_B, 1, tq), F32),
                        pltpu.VMEM((H_B, KV_RANK, tq), F32)],
    )
    return pl.pallas_call(
        body,
        grid_spec=grid_spec,
        out_shape=jax.ShapeDtypeStruct((bsz, seq, H_B * V_B), BF16),
        compiler_params=_cparams(("parallel", "arbitrary")),
        name="attn",
    )(jnp.asarray(qi), jnp.asarray(kj), jnp.asarray(fl), q, kc, vt, wuv_t)


def _layer_norm(r, g, b):
    mu = jnp.mean(r, -1, keepdims=True)
    d = r - mu
    var = jnp.mean(d * d, -1, keepdims=True)
    return d * lax.rsqrt(var + EPS) * g + b


def _merge_body(oa_ref, ob_ref, ga_ref, gb_ref, x_ref, woa_ref, wob_ref, wout_ref, g_ref, b_ref, o_ref):
    ya = jnp.dot(oa_ref[...], woa_ref[...], preferred_element_type=F32)
    yb = jnp.dot(ob_ref[...], wob_ref[...], preferred_element_type=F32)
    m = _sigmoid(ga_ref[...]) * ya + _sigmoid(gb_ref[...]) * yb
    r = ALPHA * x_ref[...] + jnp.dot(m.astype(BF16), wout_ref[...], preferred_element_type=F32)
    o_ref[...] = _layer_norm(r, g_ref[...], b_ref[...])


def _merge(oa, ob, h1, x, woa, wob, wout, g, b, tm):
    t = x.shape[0]
    row = lambda i: (i, 0)
    const = lambda i: (0, 0)
    wspec = pl.BlockSpec((D_MODEL, D_MODEL), const)
    return pl.pallas_call(
        _merge_body,
        grid=(t // tm,),
        in_specs=[pl.BlockSpec((tm, V_A), row), pl.BlockSpec((tm, H_B * V_B), row),
                  pl.BlockSpec((tm, D_MODEL), lambda i: (i, COL_GA // D_MODEL)),
                  pl.BlockSpec((tm, D_MODEL), lambda i: (i, COL_GB // D_MODEL)),
                  pl.BlockSpec((tm, D_MODEL), row), wspec, wspec, wspec,
                  pl.BlockSpec((1, D_MODEL), const), pl.BlockSpec((1, D_MODEL), const)],
        out_specs=pl.BlockSpec((tm, D_MODEL), row),
        out_shape=jax.ShapeDtypeStruct((t, D_MODEL), F32),
        compiler_params=_cparams(("parallel",)),
        name="merge",
    )(oa, ob, h1, h1, x, woa, wob, wout, g, b)


def _ffn_body(x_ref, wg_ref, wu_ref, wd_ref, g_ref, b_ref, o_ref, h_ref, *, tf):
    x = x_ref[...]
    xb = x.astype(BF16)
    for j in range(D_FF // tf):
        sl = slice(j * tf, (j + 1) * tf)
        f1 = jnp.dot(xb, wg_ref[:, sl], preferred_element_type=F32)
        f3 = jnp.dot(xb, wu_ref[:, sl], preferred_element_type=F32)
        h_ref[:, sl] = (_silu(f1) * f3).astype(BF16)
    y = jnp.dot(h_ref[...], wd_ref[...], preferred_element_type=F32)
    o_ref[...] = _layer_norm(ALPHA * x + y, g_ref[...], b_ref[...])


def _ffn(x, wg, wu, wd, g, b, tm, tf):
    t = x.shape[0]
    row = lambda i: (i, 0)
    const = lambda i: (0, 0)
    single = pl.Buffered(1)
    return pl.pallas_call(
        functools.partial(_ffn_body, tf=tf),
        grid=(t // tm,),
        in_specs=[pl.BlockSpec((tm, D_MODEL), row),
                  pl.BlockSpec((D_MODEL, D_FF), const, pipeline_mode=single),
                  pl.BlockSpec((D_MODEL, D_FF), const, pipeline_mode=single),
                  pl.BlockSpec((D_FF, D_MODEL), const, pipeline_mode=single),
                  pl.BlockSpec((1, D_MODEL), const), pl.BlockSpec((1, D_MODEL), const)],
        out_specs=pl.BlockSpec((tm, D_MODEL), row),
        out_shape=jax.ShapeDtypeStruct((t, D_MODEL), F32),
        scratch_shapes=[pltpu.VMEM((tm, D_FF), BF16)],
        compiler_params=_cparams(("parallel",)),
        name="ffn",
    )(x, wg, wu, wd, g, b)


def _prep_layer_weights(w_in, conv_w, a_log, dt_bias, gdn_norm_g, w_oa, q_norm_g, w_uq, kv_norm_g,
                        w_ukv, w_ob, w_out, ln1_g, ln1_b, w_gu, w_down, ln2_g, ln2_b):
    seg = lambda i: w_in[:, _OFFS[i]:_OFFS[i + 1]]
    qkv, z, a, b, c_q, c_kv, k_r, g_a, g_b = (seg(i) for i in range(9))
    half = ROPE // 2
    k_r_rot = jnp.concatenate([-k_r[:, half:], k_r[:, :half]], axis=1)
    pad = jnp.zeros((D_MODEL, LANES - 2 * H_A), w_in.dtype)
    w_proj = jnp.concatenate([qkv, z, g_a, g_b, c_kv, k_r, k_r, c_q, k_r_rot, k_r_rot, a, b, pad],
                             axis=1).astype(BF16)
    lane_pad = jnp.zeros((LANES - H_A,), F32)
    al_lane = jnp.concatenate([a_log.astype(F32), lane_pad]).reshape(1, LANES)
    dt_lane = jnp.concatenate([dt_bias.astype(F32), lane_pad]).reshape(1, LANES)
    uq = w_uq.reshape(Q_RANK, H_B, NOPE + ROPE)
    uq_nope = uq[:, :, :NOPE].reshape(Q_RANK, H_B * NOPE)
    uq_rope = uq[:, :, NOPE:]
    uq_rot = jnp.concatenate([-uq_rope[:, :, half:], uq_rope[:, :, :half]], axis=2)
    w_uq_ext = jnp.concatenate([uq_nope, uq_rope.reshape(Q_RANK, H_B * ROPE),
                                uq_rot.reshape(Q_RANK, H_B * ROPE)], axis=1).astype(BF16)
    ukv = w_ukv.reshape(KV_RANK, H_B, NOPE + V_B)
    w_uk_t = jnp.transpose(ukv[:, :, :NOPE], (1, 2, 0)).astype(BF16)
    w_uv = jnp.transpose(ukv[:, :, NOPE:], (1, 2, 0)).astype(BF16)
    return dict(
        w_proj=w_proj, conv_w=conv_w.astype(F32), al_lane=al_lane, dt_lane=dt_lane,
        gn=gdn_norm_g.reshape(1, DV_A).astype(F32), w_oa=w_oa.astype(BF16),
        qg=q_norm_g.reshape(1, Q_RANK).astype(F32), w_uq=w_uq_ext, w_uk_t=w_uk_t, w_uv=w_uv,
        kvg=kv_norm_g.reshape(1, KV_RANK).astype(F32), w_ob=w_ob.astype(BF16),
        w_out=w_out.astype(BF16), ln1_g=ln1_g.reshape(1, D_MODEL), ln1_b=ln1_b.reshape(1, D_MODEL),
        w_g=w_gu[:, :D_FF].astype(BF16), w_u=w_gu[:, D_FF:].astype(BF16), w_down=w_down.astype(BF16),
        ln2_g=ln2_g.reshape(1, D_MODEL), ln2_b=ln2_b.reshape(1, D_MODEL))


def _rope_tables(past, seq):
    half = ROPE // 2
    inv = ROPE_THETA ** (-jnp.arange(half, dtype=F32) / half)
    ang = (past + jnp.arange(seq)).astype(F32)[:, None] * inv[None, :]
    cos = jnp.tile(jnp.cos(ang), (1, 2 * H_B))
    sin = jnp.tile(jnp.sin(ang), (1, 2 * H_B))
    return cos, sin


def _tiles(seq, l_valid):
    prompt = seq >= 1024
    return dict(
        tm_proj=1024, tn_proj=1024,
        tm_gates=1024,
        tb_gdn=2 * GDN_CHUNK if prompt else GDN_CHUNK, hb_gdn=H_A,
        tm_mla=512 if prompt else seq,
        tq=256 if prompt else seq,
        tk=512 if prompt else 1408,
        tm_merge=256, tm_ffn=256, tf=D_FF // 2)


def _trunk_layer(x, conv_state, s0, kc_past, wl, bsz, seq, l_valid, past):
    ts = _tiles(seq, l_valid)
    t_rows = bsz * seq
    h1 = _proj_in(x, wl["w_proj"], min(ts["tm_proj"], t_rows), ts["tn_proj"])
    gates = _gates(h1, wl["al_lane"], wl["dt_lane"], min(ts["tm_gates"], t_rows), seq, l_valid)
    o_a, s_new = _gdn(h1, gates.reshape(2 * H_A, 1, t_rows), wl["conv_w"], conv_state, s0, wl["gn"],
                      bsz, seq, ts["tb_gdn"], ts["hb_gdn"])
    cos_t, sin_t = _rope_tables(past, seq)
    q, kc, vt, ckv_new, kr_new = _mla_pre(h1, cos_t, sin_t, wl["qg"], wl["w_uq"], wl["w_uk_t"],
                                          wl["kvg"], bsz, seq, ts["tm_mla"])
    kc = kc.reshape(bsz, seq, KC_W)[:, :l_valid]
    vt = vt[:, :, :l_valid]
    if kc_past is not None:
        kc = jnp.concatenate([kc_past, kc], axis=1)
        vt = jnp.concatenate([jnp.swapaxes(kc_past[:, :, :KV_RANK], 1, 2), vt], axis=2)
    lk = kc.shape[1]
    lk_pad = -(-lk // ts["tk"]) * ts["tk"]
    if lk_pad != lk:
        kc = jnp.pad(kc, ((0, 0), (0, lk_pad - lk), (0, 0)))
        vt = jnp.pad(vt, ((0, 0), (0, 0), (0, lk_pad - lk)))
    o_b = _attn(q, kc, vt, wl["w_uv"], past, lk, ts["tq"], ts["tk"]).reshape(t_rows, H_B * V_B)
    x1 = _merge(o_a, o_b, h1, x, wl["w_oa"], wl["w_ob"], wl["w_out"], wl["ln1_g"], wl["ln1_b"],
                ts["tm_merge"])
    x2 = _ffn(x1, wl["w_g"], wl["w_u"], wl["w_down"], wl["ln2_g"], wl["ln2_b"], ts["tm_ffn"], ts["tf"])
    h3 = h1.reshape(bsz, seq, N_PROJ)
    conv_new = h3[:, l_valid - (CONV_W - 1):l_valid, COL_QKV:COL_QKV + C_QKV]
    ckv_new = ckv_new.reshape(bsz, seq, KV_RANK)[:, :l_valid]
    kr_new = kr_new.reshape(bsz, seq, ROPE)[:, :l_valid]
    return x2, conv_new, s_new, ckv_new, kr_new


def kernel(x_prompt, x_sample, state_conv, state_gdn, cache_ckv, cache_krope, w_in, conv_w, a_log, dt_bias, gdn_norm_g, w_oa, q_norm_g, w_uq, kv_norm_g, w_ukv, w_ob, w_out, ln1_g, ln1_b, w_gu, w_down, ln2_g, ln2_b):
    bp, lp, _ = x_prompt.shape
    bs, ls, _ = x_sample.shape
    past = cache_ckv.shape[2]
    ls_pad = -(-ls // GDN_CHUNK) * GDN_CHUNK
    yp = x_prompt.reshape(bp * lp, D_MODEL)
    ys = jnp.pad(x_sample, ((0, 0), (0, ls_pad - ls), (0, 0))).reshape(bs * ls_pad, D_MODEL)
    zero_conv = jnp.zeros((bp, CONV_W - 1, C_QKV), F32)
    zero_s = jnp.zeros((bp, H_A, DK_A, DV_A), F32)
    outs_p, outs_s = [], []
    for l in range(w_in.shape[0]):
        wl = _prep_layer_weights(w_in[l], conv_w[l], a_log[l], dt_bias[l], gdn_norm_g[l], w_oa[l],
                                 q_norm_g[l], w_uq[l], kv_norm_g[l], w_ukv[l], w_ob[l], w_out[l],
                                 ln1_g[l], ln1_b[l], w_gu[l], w_down[l], ln2_g[l], ln2_b[l])
        yp, *rest_p = _trunk_layer(yp, zero_conv, zero_s, None, wl, bp, lp, lp, 0)
        kc_past = jnp.concatenate([cache_ckv[l], cache_krope[l], cache_krope[l]], axis=-1).astype(BF16)
        ys, *rest_s = _trunk_layer(ys, state_conv[l], state_gdn[l], kc_past, wl, bs, ls_pad, ls, past)
        outs_p.append(rest_p)
        outs_s.append(rest_s)
    stack = lambda outs, i: jnp.stack([o[i] for o in outs])
    y_prompt = yp.reshape(bp, lp, D_MODEL)
    y_sample = ys.reshape(bs, ls_pad, D_MODEL)[:, :ls]
    return (y_prompt, y_sample,
            stack(outs_p, 0), stack(outs_p, 1), stack(outs_p, 2), stack(outs_p, 3),
            stack(outs_s, 0), stack(outs_s, 1), stack(outs_s, 2), stack(outs_s, 3))
```

```python
import functools

import numpy as np
import jax
import jax.numpy as jnp
from jax import lax
from jax.experimental import pallas as pl
from jax.experimental.pallas import tpu as pltpu

F32 = jnp.float32
BF16 = jnp.bfloat16

D_MODEL = 1024
DEPTH = 2
CHUNK = 64
H_A = 8
DK_A = 128
DV_A = 128
QK_A = H_A * DK_A
V_A = H_A * DV_A
C_QKV = 2 * QK_A + V_A
CONV_W = 4
H_B = 8
NOPE = 128
ROPE = 64
V_B = 128
Q_RANK = 384
KV_RANK = 256
ROPE_THETA = 10000.0
ATTN_SCALE = (NOPE + ROPE) ** -0.5
Q_SCALE = ATTN_SCALE * float(np.log2(np.e))
D_FF = -(-8 * D_MODEL // (3 * 256)) * 256
ALPHA = (2 * DEPTH) ** 0.25
EPS = 1e-6
_SIZES = (C_QKV, V_A, H_A, H_A, Q_RANK, KV_RANK, ROPE, D_MODEL, D_MODEL)
_OFFS = tuple(int(v) for v in np.cumsum((0,) + _SIZES))

LANES = 128
VMEM_LIMIT = 56 * 1024 * 1024

COL_QKV = 0
COL_Z = COL_QKV + C_QKV
COL_GA = COL_Z + V_A
COL_GB = COL_GA + D_MODEL
COL_CKV = COL_GB + D_MODEL
COL_KRA = COL_CKV + KV_RANK
COL_CQ = COL_KRA + LANES
COL_KRB = COL_CQ + Q_RANK
COL_AB = COL_KRB + LANES
N_PROJ = COL_AB + LANES
KC_W = KV_RANK + LANES

GDN_CHUNK = 128


def _cparams(sem):
    return pltpu.CompilerParams(dimension_semantics=sem, vmem_limit_bytes=VMEM_LIMIT)


def _sigmoid(x):
    return jax.nn.sigmoid(x)


def _silu(x):
    return x * jax.nn.sigmoid(x)


def _mm(a, b):
    return jnp.dot(a.astype(BF16), b.astype(BF16), preferred_element_type=F32)


def _mm_nt(a, b):
    return lax.dot_general(a.astype(BF16), b.astype(BF16), (((1,), (1,)), ((), ())),
                           preferred_element_type=F32)


def _mm_tn(a, b):
    return lax.dot_general(a.astype(BF16), b.astype(BF16), (((0,), (0,)), ((), ())),
                           preferred_element_type=F32)


def _proj_body(x_ref, w_ref, o_ref, xb_ref):
    @pl.when(pl.program_id(1) == 0)
    def _():
        xb_ref[...] = x_ref[...].astype(BF16)

    o_ref[...] = jnp.dot(xb_ref[...], w_ref[...], preferred_element_type=F32)


def _proj_in(x, w, tm, tn):
    t, k = x.shape
    n = w.shape[1]
    return pl.pallas_call(
        _proj_body,
        grid=(t // tm, n // tn),
        in_specs=[pl.BlockSpec((tm, k), lambda i, j: (i, 0)),
                  pl.BlockSpec((k, tn), lambda i, j: (0, j))],
        out_specs=pl.BlockSpec((tm, tn), lambda i, j: (i, j)),
        out_shape=jax.ShapeDtypeStruct((t, n), F32),
        scratch_shapes=[pltpu.VMEM((tm, k), BF16)],
        compiler_params=_cparams(("parallel", "arbitrary")),
        name="proj_in",
    )(x, w)


def _gates_body(ab_ref, al_ref, dt_ref, o_ref, *, tm, l_pad, l_valid):
    x = ab_ref[...]
    lane = lax.broadcasted_iota(jnp.int32, x.shape, 1)
    xa = x + dt_ref[...]
    sp = jnp.maximum(xa, 0.0) + jnp.log1p(jnp.exp(-jnp.abs(xa)))
    g = -jnp.exp(al_ref[...]) * sp
    y = jnp.where(lane < H_A, g, _sigmoid(x))
    yt = y.T[0:2 * H_A, :]
    if l_valid < l_pad:
        col = lax.broadcasted_iota(jnp.int32, yt.shape, 1) + pl.program_id(0) * tm
        yt = jnp.where(col % l_pad < l_valid, yt, 0.0)
    r = lax.broadcasted_iota(jnp.int32, (GDN_CHUNK, GDN_CHUNK), 0)
    c = lax.broadcasted_iota(jnp.int32, (GDN_CHUNK, GDN_CHUNK), 1)
    tri = jnp.where(r <= c, 1.0, 0.0).astype(F32)
    for s in range(tm // GDN_CHUNK):
        sl = slice(s * GDN_CHUNK, (s + 1) * GDN_CHUNK)
        o_ref[0:H_A, sl] = jnp.dot(yt[0:H_A, sl], tri, precision=lax.Precision.HIGHEST,
                                   preferred_element_type=F32)
    o_ref[H_A:2 * H_A, :] = yt[H_A:2 * H_A, :]


def _gates(h1, al_lane, dt_lane, tm, l_pad, l_valid):
    t = h1.shape[0]
    body = functools.partial(_gates_body, tm=tm, l_pad=l_pad, l_valid=l_valid)
    return pl.pallas_call(
        body,
        grid=(t // tm,),
        in_specs=[pl.BlockSpec((tm, LANES), lambda i: (i, COL_AB // LANES)),
                  pl.BlockSpec((1, LANES), lambda i: (0, 0)),
                  pl.BlockSpec((1, LANES), lambda i: (0, 0))],
        out_specs=pl.BlockSpec((2 * H_A, tm), lambda i: (0, i)),
        out_shape=jax.ShapeDtypeStruct((2 * H_A, t), F32),
        compiler_params=_cparams(("parallel",)),
        name="gates",
    )(h1, al_lane, dt_lane)


def _gdn_body(q_ref, k_ref, v_ref, z_ref, gc_ref, bt_ref, cwq_ref, cwk_ref, cwv_ref,
              csq_ref, csk_ref, csv_ref, s0_ref, gn_ref, o_ref, so_ref,
              xq_ref, xk_ref, xv_ref, s_ref, *, tb, nt, hb):
    t = pl.program_id(2)
    hist = CONV_W - 1
    base = 8

    @pl.when(t == 0)
    def _():
        s_ref[...] = s0_ref[0]
        xq_ref[base - hist:base, :] = csq_ref[0]
        xk_ref[base - hist:base, :] = csk_ref[0]
        xv_ref[base - hist:base, :] = csv_ref[0]

    xq_ref[base:base + tb, :] = q_ref[...]
    xk_ref[base:base + tb, :] = k_ref[...]
    xv_ref[base:base + tb, :] = v_ref[...]

    def conv(xs_ref, cw_ref, ln):
        y = xs_ref[base - hist:base - hist + tb, ln] * cw_ref[0:1, ln]
        for j in range(1, CONV_W):
            y = y + xs_ref[base - hist + j:base - hist + j + tb, ln] * cw_ref[j:j + 1, ln]
        return _silu(y)

    cc = GDN_CHUNK
    ii = lax.broadcasted_iota(jnp.int32, (cc, cc), 0)
    jj = lax.broadcasted_iota(jnp.int32, (cc, cc), 1)
    strict = ii > jj
    incl = ii >= jj
    eye = jnp.where(ii == jj, 1.0, 0.0).astype(F32)
    heads = range(hb)
    lanes = [slice(hd * LANES, (hd + 1) * LANES) for hd in heads]
    units = [(hd, c) for hd in heads for c in range(tb // cc)]
    rows = {u: slice(u[1] * cc, (u[1] + 1) * cc) for u in units}
    q = [conv(xq_ref, cwq_ref, ln) for ln in lanes]
    k = [conv(xk_ref, cwk_ref, ln) for ln in lanes]
    v = [conv(xv_ref, cwv_ref, ln) for ln in lanes]
    q = [x * lax.rsqrt(jnp.sum(x * x, -1, keepdims=True) + EPS) * (DK_A ** -0.5) for x in q]
    k = [x * lax.rsqrt(jnp.sum(x * x, -1, keepdims=True) + EPS) for x in k]
    qc = {u: q[u[0]][rows[u]] for u in units}
    kc = {u: k[u[0]][rows[u]] for u in units}
    vc = {u: v[u[0]][rows[u]] for u in units}
    m_row = {u: jnp.broadcast_to(gc_ref[u[0], :, rows[u]], (cc, cc)) for u in units}
    m_col = {u: m_row[u].T for u in units}
    b_col = {u: jnp.broadcast_to(bt_ref[u[0], :, rows[u]], (cc, cc)).T for u in units}
    kq = {u: _mm_nt(jnp.concatenate([kc[u], qc[u]], axis=0), kc[u]) for u in units}
    diff = {u: m_col[u] - m_row[u] for u in units}
    a_mat = {u: b_col[u] * kq[u][0:cc] * jnp.exp(jnp.where(strict, diff[u], -jnp.inf)) for u in units}
    qk = {u: kq[u][cc:2 * cc] * jnp.exp(jnp.where(incl, diff[u], -jnp.inf)) for u in units}
    e_g = {u: jnp.exp(m_col[u]) for u in units}
    x_inv = {u: eye - jnp.where((ii >> 1) == (jj >> 1), a_mat[u], 0.0) for u in units}
    sft = 1
    while (1 << sft) < cc:
        off = ((ii >> (sft + 1)) == (jj >> (sft + 1))) & ((ii >> sft) != (jj >> sft))
        lx = {u: _mm(jnp.where(off, a_mat[u], 0.0), x_inv[u]) for u in units}
        x_inv = {u: x_inv[u] - _mm(x_inv[u], lx[u]) for u in units}
        sft += 1
    uw = {u: _mm(x_inv[u], jnp.concatenate([vc[u] * b_col[u], kc[u] * (b_col[u] * e_g[u])], axis=1))
          for u in units}
    wq = {u: jnp.concatenate([uw[u][:, DV_A:], qc[u] * e_g[u]], axis=0) for u in units}
    g_last = {u: m_col[u][cc - 1:cc, :] for u in units}
    kd = {u: kc[u] * jnp.exp(g_last[u] - m_col[u]) for u in units}
    s_state = [s_ref[hd] for hd in heads]
    for c in range(tb // cc):
        ws = [_mm(wq[(hd, c)], s_state[hd]) for hd in heads]
        v_new = [uw[(hd, c)][:, 0:DV_A] - ws[hd][0:cc] for hd in heads]
        s_state = [s_state[hd] * jnp.exp(g_last[(hd, c)]) + _mm_tn(kd[(hd, c)], v_new[hd]) for hd in heads]
        o = [ws[hd][cc:2 * cc] + _mm(qk[(hd, c)], v_new[hd]) for hd in heads]
        o = [x * lax.rsqrt(jnp.mean(x * x, -1, keepdims=True) + EPS) * gn_ref[...] for x in o]
        for hd in heads:
            r = rows[(hd, c)]
            o_ref[r, lanes[hd]] = (o[hd] * _silu(z_ref[r, lanes[hd]])).astype(o_ref.dtype)
    for hd in heads:
        s_ref[hd] = s_state[hd]

    xq_ref[base - hist:base, :] = xq_ref[base + tb - hist:base + tb, :]
    xk_ref[base - hist:base, :] = xk_ref[base + tb - hist:base + tb, :]
    xv_ref[base - hist:base, :] = xv_ref[base + tb - hist:base + tb, :]

    @pl.when(t == nt - 1)
    def _():
        so_ref[0] = s_ref[...]


def _gdn(h1, gates3, conv_w, conv_state, s0, gn, bsz, seq, tb, hb):
    nt = seq // tb
    t_rows = bsz * seq
    body = functools.partial(_gdn_body, tb=tb, nt=nt, hb=hb)
    width = hb * LANES
    ng = H_A // hb

    def rows(seg):
        return pl.BlockSpec((tb, width), lambda b, g, t: (b * nt + t, seg * ng + g))

    def cw(seg):
        return pl.BlockSpec((CONV_W, width), lambda b, g, t: (0, seg * ng + g))

    def cs(seg):
        return pl.BlockSpec((1, CONV_W - 1, width), lambda b, g, t: (b, 0, seg * ng + g))

    return pl.pallas_call(
        body,
        grid=(bsz, ng, nt),
        in_specs=[rows(0), rows(1), rows(2), rows(COL_Z // QK_A),
                  pl.BlockSpec((hb, 1, tb), lambda b, g, t: (g, 0, b * nt + t)),
                  pl.BlockSpec((hb, 1, tb), lambda b, g, t: (ng + g, 0, b * nt + t)),
                  cw(0), cw(1), cw(2), cs(0), cs(1), cs(2),
                  pl.BlockSpec((1, hb, DK_A, DV_A), lambda b, g, t: (b, g, 0, 0)),
                  pl.BlockSpec((1, DV_A), lambda b, g, t: (0, 0))],
        out_specs=[pl.BlockSpec((tb, width), lambda b, g, t: (b * nt + t, g)),
                   pl.BlockSpec((1, hb, DK_A, DV_A), lambda b, g, t: (b, g, 0, 0))],
        out_shape=[jax.ShapeDtypeStruct((t_rows, V_A), BF16),
                   jax.ShapeDtypeStruct((bsz, H_A, DK_A, DV_A), F32)],
        scratch_shapes=[pltpu.VMEM((tb + 8, width), F32)] * 3 + [pltpu.VMEM((hb, DK_A, DV_A), F32)],
        compiler_params=_cparams(("parallel", "parallel", "arbitrary")),
        name="gdn",
    )(h1, h1, h1, h1, gates3, gates3, conv_w, conv_w, conv_w,
      conv_state, conv_state, conv_state, s0, gn)


def _mla_pre_body(cq_ref, ckv_ref, kra_ref, krb_ref, cos_ref, sin_ref, qg_ref, wuq_ref, wuk_ref,
                  kvg_ref, q_ref, kc_ref, vt_ref, ckvo_ref, kro_ref):
    cq = cq_ref[...]
    cqn = cq * lax.rsqrt(jnp.mean(cq * cq, -1, keepdims=True) + EPS) * qg_ref[...]
    qf = jnp.dot(cqn.astype(BF16), wuq_ref[...], preferred_element_type=F32)
    cos_k = cos_ref[...]
    sin_k = sin_ref[...]
    reps = H_B * ROPE // LANES
    cos_t = jnp.concatenate([cos_k] * reps, axis=1)
    sin_t = jnp.concatenate([sin_k] * reps, axis=1)
    n_nope = H_B * NOPE
    n_rope = H_B * ROPE
    qr = (qf[:, n_nope:n_nope + n_rope] * cos_t + qf[:, n_nope + n_rope:] * sin_t) * Q_SCALE
    lane = lax.broadcasted_iota(jnp.int32, (cq.shape[0], LANES), 1)
    for h in range(H_B):
        ql = jnp.dot(qf[:, h * NOPE:(h + 1) * NOPE].astype(BF16), wuk_ref[h],
                     preferred_element_type=F32) * Q_SCALE
        blk = qr[:, (h // 2) * LANES:(h // 2 + 1) * LANES]
        keep = (lane < ROPE) if h % 2 == 0 else (lane >= ROPE)
        q_ref[0, h, :, 0:KV_RANK] = ql.astype(BF16)
        q_ref[0, h, :, KV_RANK:KC_W] = jnp.where(keep, blk, 0.0).astype(BF16)
    ckv = ckv_ref[...]
    ckvn = ckv * lax.rsqrt(jnp.mean(ckv * ckv, -1, keepdims=True) + EPS) * kvg_ref[...]
    kr2 = kra_ref[...] * cos_k + krb_ref[...] * sin_k
    ckvo_ref[...] = ckvn
    kro_ref[...] = kr2[:, 0:ROPE]
    kc_ref[:, 0:KV_RANK] = ckvn.astype(BF16)
    kc_ref[:, KV_RANK:KC_W] = kr2.astype(BF16)
    vt_ref[0] = ckvn.T.astype(BF16)


def _mla_pre(h1, cos_t, sin_t, qg, wuq, wuk, kvg, bsz, seq, tm):
    t_rows = bsz * seq
    npb = seq // tm
    const2 = lambda i: (0, 0)
    return pl.pallas_call(
        _mla_pre_body,
        grid=(t_rows // tm,),
        in_specs=[pl.BlockSpec((tm, Q_RANK), lambda i: (i, COL_CQ // Q_RANK)),
                  pl.BlockSpec((tm, KV_RANK), lambda i: (i, COL_CKV // KV_RANK)),
                  pl.BlockSpec((tm, LANES), lambda i: (i, COL_KRA // LANES)),
                  pl.BlockSpec((tm, LANES), lambda i: (i, COL_KRB // LANES)),
                  pl.BlockSpec((tm, LANES), lambda i: (i % npb, 0)),
                  pl.BlockSpec((tm, LANES), lambda i: (i % npb, 0)),
                  pl.BlockSpec((1, Q_RANK), const2),
                  pl.BlockSpec(wuq.shape, const2),
                  pl.BlockSpec(wuk.shape, lambda i: (0, 0, 0)),
                  pl.BlockSpec((1, KV_RANK), const2)],
        out_specs=[pl.BlockSpec((1, H_B, tm, KC_W), lambda i: (i // npb, 0, i % npb, 0)),
                   pl.BlockSpec((tm, KC_W), lambda i: (i, 0)),
                   pl.BlockSpec((1, KV_RANK, tm), lambda i: (i // npb, 0, i % npb)),
                   pl.BlockSpec((tm, KV_RANK), lambda i: (i, 0)),
                   pl.BlockSpec((tm, ROPE), lambda i: (i, 0))],
        out_shape=[jax.ShapeDtypeStruct((bsz, H_B, seq, KC_W), BF16),
                   jax.ShapeDtypeStruct((t_rows, KC_W), BF16),
                   jax.ShapeDtypeStruct((bsz, KV_RANK, seq), BF16),
                   jax.ShapeDtypeStruct((t_rows, KV_RANK), F32),
                   jax.ShapeDtypeStruct((t_rows, ROPE), F32)],
        compiler_params=_cparams(("parallel",)),
        name="mla_pre",
    )(h1, h1, h1, h1, cos_t, sin_t, qg, wuq, wuk, kvg)


def _attn_body(qi_ref, kj_ref, fl_ref, q_ref, k_ref, vt_ref, wuv_ref, o_ref, m_ref, l_ref, acc_ref,
               *, tq, tk, past, lk):
    s_idx = pl.program_id(1)
    flags = fl_ref[s_idx]

    @pl.when((flags & 1) != 0)
    def _():
        m_ref[...] = jnp.full(m_ref.shape, -jnp.inf, F32)
        l_ref[...] = jnp.zeros(l_ref.shape, F32)
        acc_ref[...] = jnp.zeros(acc_ref.shape, F32)

    def scores(h, bias):
        s = lax.dot_general(k_ref[0], q_ref[0, h], (((1,), (1,)), ((), ())),
                            preferred_element_type=F32)
        return s if bias is None else s + bias

    def update(masked):
        bias = None
        if masked:
            shift = CHUNK.bit_length() - 1
            kpos = kj_ref[s_idx] * tk + lax.broadcasted_iota(jnp.int32, (tk, tq), 0)
            qpos = past + qi_ref[s_idx] * tq + lax.broadcasted_iota(jnp.int32, (tk, tq), 1)
            ok = ((kpos >> shift) <= (qpos >> shift)) & (kpos < lk)
            bias = jnp.where(ok, 0.0, -jnp.inf).astype(F32)
        s_next = scores(0, bias)
        for h in range(H_B):
            s = s_next
            if h + 1 < H_B:
                s_next = scores(h + 1, bias)
            m_prev = m_ref[h]
            m_new = jnp.maximum(m_prev, jnp.max(s, 0, keepdims=True))
            p = jnp.exp2(s - m_new)
            alpha = jnp.exp2(m_prev - m_new)
            l_ref[h] = alpha * l_ref[h] + jnp.sum(p, 0, keepdims=True)
            pv = jnp.dot(vt_ref[0], p.astype(BF16), preferred_element_type=F32)
            acc_ref[h] = alpha * acc_ref[h] + pv
            m_ref[h] = m_new

    @pl.when((flags & 4) != 0)
    def _():
        update(True)

    @pl.when((flags & 4) == 0)
    def _():
        update(False)

    @pl.when((flags & 2) != 0)
    def _():
        for h in range(H_B):
            o_t = (acc_ref[h] * (1.0 / l_ref[h])).astype(BF16)
            ob_t = jnp.dot(wuv_ref[h], o_t, preferred_element_type=F32)
            o_ref[0, :, h * V_B:(h + 1) * V_B] = ob_t.T.astype(o_ref.dtype)


def _attn_steps(seq, past, lk, tq, tk):
    qi, kj, fl = [], [], []
    for i in range(seq // tq):
        p0 = past + i * tq
        p1 = p0 + tq - 1
        last_vis = min(lk - 1, (p1 // CHUNK) * CHUNK + CHUNK - 1)
        all_vis = min(lk - 1, (p0 // CHUNK) * CHUNK + CHUNK - 1)
        jmax = last_vis // tk
        for j in range(jmax + 1):
            f = (1 if j == 0 else 0) | (2 if j == jmax else 0)
            if (j + 1) * tk - 1 > all_vis:
                f |= 4
            qi.append(i), kj.append(j), fl.append(f)
    return (np.asarray(qi, np.int32), np.asarray(kj, np.int32), np.asarray(fl, np.int32))


def _attn(q, kc, vt, wuv_t, past, lk, tq, tk):
    bsz, _, seq, _ = q.shape
    qi, kj, fl = _attn_steps(seq, past, lk, tq, tk)
    body = functools.partial(_attn_body, tq=tq, tk=tk, past=past, lk=lk)
    grid_spec = pltpu.PrefetchScalarGridSpec(
        num_scalar_prefetch=3,
        grid=(bsz, len(qi)),
        in_specs=[pl.BlockSpec((1, H_B, tq, KC_W), lambda b, s, qi, kj, fl: (b, 0, qi[s], 0)),
                  pl.BlockSpec((1, tk, KC_W), lambda b, s, qi, kj, fl: (b, kj[s], 0)),
                  pl.BlockSpec((1, KV_RANK, tk), lambda b, s, qi, kj, fl: (b, 0, kj[s])),
                  pl.BlockSpec(wuv_t.shape, lambda b, s, qi, kj, fl: (0, 0, 0))],
        out_specs=pl.BlockSpec((1, tq, H_B * V_B), lambda b, s, qi, kj, fl: (b, qi[s], 0)),
        scratch_shapes=[pltpu.VMEM((H_B, 1, tq), F32), pltpu.VMEM((H_B, 1, tq), F32),
                        pltpu.VMEM((H_B, KV_RANK, tq), F32)],
    )
    return pl.pallas_call(
        body,
        grid_spec=grid_spec,
        out_shape=jax.ShapeDtypeStruct((bsz, seq, H_B * V_B), BF16),
        compiler_params=_cparams(("parallel", "arbitrary")),
        name="attn",
    )(jnp.asarray(qi), jnp.asarray(kj), jnp.asarray(fl), q, kc, vt, wuv_t)


def _layer_norm(r, g, b):
    mu = jnp.mean(r, -1, keepdims=True)
    d = r - mu
    var = jnp.mean(d * d, -1, keepdims=True)
    return d * lax.rsqrt(var + EPS) * g + b


def _merge_body(oa_ref, ob_ref, ga_ref, gb_ref, x_ref, woa_ref, wob_ref, wout_ref, g_ref, b_ref, o_ref):
    ya = jnp.dot(oa_ref[...], woa_ref[...], preferred_element_type=F32)
    yb = jnp.dot(ob_ref[...], wob_ref[...], preferred_element_type=F32)
    m = _sigmoid(ga_ref[...]) * ya + _sigmoid(gb_ref[...]) * yb
    r = ALPHA * x_ref[...] + jnp.dot(m.astype(BF16), wout_ref[...], preferred_element_type=F32)
    o_ref[...] = _layer_norm(r, g_ref[...], b_ref[...])


def _merge(oa, ob, h1, x, woa, wob, wout, g, b, tm):
    t = x.shape[0]
    row = lambda i: (i, 0)
    const = lambda i: (0, 0)
    wspec = pl.BlockSpec((D_MODEL, D_MODEL), const)
    return pl.pallas_call(
        _merge_body,
        grid=(t // tm,),
        in_specs=[pl.BlockSpec((tm, V_A), row), pl.BlockSpec((tm, H_B * V_B), row),
                  pl.BlockSpec((tm, D_MODEL), lambda i: (i, COL_GA // D_MODEL)),
                  pl.BlockSpec((tm, D_MODEL), lambda i: (i, COL_GB // D_MODEL)),
                  pl.BlockSpec((tm, D_MODEL), row), wspec, wspec, wspec,
                  pl.BlockSpec((1, D_MODEL), const), pl.BlockSpec((1, D_MODEL), const)],
        out_specs=pl.BlockSpec((tm, D_MODEL), row),
        out_shape=jax.ShapeDtypeStruct((t, D_MODEL), F32),
        compiler_params=_cparams(("parallel",)),
        name="merge",
    )(oa, ob, h1, h1, x, woa, wob, wout, g, b)


def _ffn_body(x_ref, wg_ref, wu_ref, wd_ref, g_ref, b_ref, o_ref, h_ref, *, tf):
    x = x_ref[...]
    xb = x.astype(BF16)
    for j in range(D_FF // tf):
        sl = slice(j * tf, (j + 1) * tf)
        f1 = jnp.dot(xb, wg_ref[:, sl], preferred_element_type=F32)
        f3 = jnp.dot(xb, wu_ref[:, sl], preferred_element_type=F32)
        h_ref[:, sl] = (_silu(f1) * f3).astype(BF16)
    y = jnp.dot(h_ref[...], wd_ref[...], preferred_element_type=F32)
    o_ref[...] = _layer_norm(ALPHA * x + y, g_ref[...], b_ref[...])


def _ffn(x, wg, wu, wd, g, b, tm, tf):
    t = x.shape[0]
    row = lambda i: (i, 0)
    const = lambda i: (0, 0)
    single = pl.Buffered(1)
    return pl.pallas_call(
        functools.partial(_ffn_body, tf=tf),
        grid=(t // tm,),
        in_specs=[pl.BlockSpec((tm, D_MODEL), row),
                  pl.BlockSpec((D_MODEL, D_FF), const, pipeline_mode=single),
                  pl.BlockSpec((D_MODEL, D_FF), const, pipeline_mode=single),
                  pl.BlockSpec((D_FF, D_MODEL), const, pipeline_mode=single),
                  pl.BlockSpec((1, D_MODEL), const), pl.BlockSpec((1, D_MODEL), const)],
        out_specs=pl.BlockSpec((tm, D_MODEL), row),
        out_shape=jax.ShapeDtypeStruct((t, D_MODEL), F32),
        scratch_shapes=[pltpu.VMEM((tm, D_FF), BF16)],
        compiler_params=_cparams(("parallel",)),
        name="ffn",
    )(x, wg, wu, wd, g, b)


def _prep_layer_weights(w_in, conv_w, a_log, dt_bias, gdn_norm_g, w_oa, q_norm_g, w_uq, kv_norm_g,
                        w_ukv, w_ob, w_out, ln1_g, ln1_b, w_gu, w_down, ln2_g, ln2_b):
    seg = lambda i: w_in[:, _OFFS[i]:_OFFS[i + 1]]
    qkv, z, a, b, c_q, c_kv, k_r, g_a, g_b = (seg(i) for i in range(9))
    half = ROPE // 2
    k_r_rot = jnp.concatenate([-k_r[:, half:], k_r[:, :half]], axis=1)
    pad = jnp.zeros((D_MODEL, LANES - 2 * H_A), w_in.dtype)
    w_proj = jnp.concatenate([qkv, z, g_a, g_b, c_kv, k_r, k_r, c_q, k_r_rot, k_r_rot, a, b, pad],
                             axis=1).astype(BF16)
    lane_pad = jnp.zeros((LANES - H_A,), F32)
    al_lane = jnp.concatenate([a_log.astype(F32), lane_pad]).reshape(1, LANES)
    dt_lane = jnp.concatenate([dt_bias.astype(F32), lane_pad]).reshape(1, LANES)
    uq = w_uq.reshape(Q_RANK, H_B, NOPE + ROPE)
    uq_nope = uq[:, :, :NOPE].reshape(Q_RANK, H_B * NOPE)
    uq_rope = uq[:, :, NOPE:]
    uq_rot = jnp.concatenate([-uq_rope[:, :, half:], uq_rope[:, :, :half]], axis=2)
    w_uq_ext = jnp.concatenate([uq_nope, uq_rope.reshape(Q_RANK, H_B * ROPE),
                                uq_rot.reshape(Q_RANK, H_B * ROPE)], axis=1).astype(BF16)
    ukv = w_ukv.reshape(KV_RANK, H_B, NOPE + V_B)
    w_uk_t = jnp.transpose(ukv[:, :, :NOPE], (1, 2, 0)).astype(BF16)
    w_uv = jnp.transpose(ukv[:, :, NOPE:], (1, 2, 0)).astype(BF16)
    return dict(
        w_proj=w_proj, conv_w=conv_w.astype(F32), al_lane=al_lane, dt_lane=dt_lane,
        gn=gdn_norm_g.reshape(1, DV_A).astype(F32), w_oa=w_oa.astype(BF16),
        qg=q_norm_g.reshape(1, Q_RANK).astype(F32), w_uq=w_uq_ext, w_uk_t=w_uk_t, w_uv=w_uv,
        kvg=kv_norm_g.reshape(1, KV_RANK).astype(F32), w_ob=w_ob.astype(BF16),
        w_out=w_out.astype(BF16), ln1_g=ln1_g.reshape(1, D_MODEL), ln1_b=ln1_b.reshape(1, D_MODEL),
        w_g=w_gu[:, :D_FF].astype(BF16), w_u=w_gu[:, D_FF:].astype(BF16), w_down=w_down.astype(BF16),
        ln2_g=ln2_g.reshape(1, D_MODEL), ln2_b=ln2_b.reshape(1, D_MODEL))


def _rope_tables(past, seq):
    half = ROPE // 2
    inv = ROPE_THETA ** (-jnp.arange(half, dtype=F32) / half)
    ang = (past + jnp.arange(seq)).astype(F32)[:, None] * inv[None, :]
    cos = jnp.tile(jnp.cos(ang), (1, LANES // half))
    sin = jnp.tile(jnp.sin(ang), (1, LANES // half))
    return cos, sin


def _tiles(seq, l_valid):
    prompt = seq >= 1024
    return dict(
        tm_proj=1024, tn_proj=1024,
        tm_gates=1024,
        tb_gdn=2 * GDN_CHUNK if prompt else GDN_CHUNK, hb_gdn=H_A,
        tm_mla=512 if prompt else seq,
        tq=512 if prompt else seq,
        tk=512 if prompt else 1408,
        tm_merge=512, tm_ffn=512, tf=D_FF // 2)


def _trunk_layer(x, conv_state, s0, kc_past, wl, bsz, seq, l_valid, past):
    ts = _tiles(seq, l_valid)
    t_rows = bsz * seq
    h1 = _proj_in(x, wl["w_proj"], min(ts["tm_proj"], t_rows), ts["tn_proj"])
    gates = _gates(h1, wl["al_lane"], wl["dt_lane"], min(ts["tm_gates"], t_rows), seq, l_valid)
    o_a, s_new = _gdn(h1, gates.reshape(2 * H_A, 1, t_rows), wl["conv_w"], conv_state, s0, wl["gn"],
                      bsz, seq, ts["tb_gdn"], ts["hb_gdn"])
    cos_t, sin_t = _rope_tables(past, seq)
    q, kc, vt, ckv_new, kr_new = _mla_pre(h1, cos_t, sin_t, wl["qg"], wl["w_uq"], wl["w_uk_t"],
                                          wl["kvg"], bsz, seq, ts["tm_mla"])
    kc = kc.reshape(bsz, seq, KC_W)[:, :l_valid]
    vt = vt[:, :, :l_valid]
    if kc_past is not None:
        kc = jnp.concatenate([kc_past, kc], axis=1)
        vt = jnp.concatenate([jnp.swapaxes(kc_past[:, :, :KV_RANK], 1, 2), vt], axis=2)
    lk = kc.shape[1]
    lk_pad = -(-lk // ts["tk"]) * ts["tk"]
    if lk_pad != lk:
        kc = jnp.pad(kc, ((0, 0), (0, lk_pad - lk), (0, 0)))
        vt = jnp.pad(vt, ((0, 0), (0, 0), (0, lk_pad - lk)))
    o_b = _attn(q, kc, vt, wl["w_uv"], past, lk, ts["tq"], ts["tk"]).reshape(t_rows, H_B * V_B)
    x1 = _merge(o_a, o_b, h1, x, wl["w_oa"], wl["w_ob"], wl["w_out"], wl["ln1_g"], wl["ln1_b"],
                min(ts["tm_merge"], t_rows))
    x2 = _ffn(x1, wl["w_g"], wl["w_u"], wl["w_down"], wl["ln2_g"], wl["ln2_b"],
              min(ts["tm_ffn"], t_rows), ts["tf"])
    h3 = h1.reshape(bsz, seq, N_PROJ)
    conv_new = h3[:, l_valid - (CONV_W - 1):l_valid, COL_QKV:COL_QKV + C_QKV]
    ckv_new = ckv_new.reshape(bsz, seq, KV_RANK)[:, :l_valid]
    kr_new = kr_new.reshape(bsz, seq, ROPE)[:, :l_valid]
    return x2, conv_new, s_new, ckv_new, kr_new


def kernel(x_prompt, x_sample, state_conv, state_gdn, cache_ckv, cache_krope, w_in, conv_w, a_log, dt_bias, gdn_norm_g, w_oa, q_norm_g, w_uq, kv_norm_g, w_ukv, w_ob, w_out, ln1_g, ln1_b, w_gu, w_down, ln2_g, ln2_b):
    bp, lp, _ = x_prompt.shape
    bs, ls, _ = x_sample.shape
    past = cache_ckv.shape[2]
    ls_pad = -(-ls // GDN_CHUNK) * GDN_CHUNK
    yp = x_prompt.reshape(bp * lp, D_MODEL)
    ys = jnp.pad(x_sample, ((0, 0), (0, ls_pad - ls), (0, 0))).reshape(bs * ls_pad, D_MODEL)
    zero_conv = jnp.zeros((bp, CONV_W - 1, C_QKV), F32)
    zero_s = jnp.zeros((bp, H_A, DK_A, DV_A), F32)
    outs_p, outs_s = [], []
    for l in range(w_in.shape[0]):
        wl = _prep_layer_weights(w_in[l], conv_w[l], a_log[l], dt_bias[l], gdn_norm_g[l], w_oa[l],
                                 q_norm_g[l], w_uq[l], kv_norm_g[l], w_ukv[l], w_ob[l], w_out[l],
                                 ln1_g[l], ln1_b[l], w_gu[l], w_down[l], ln2_g[l], ln2_b[l])
        yp, *rest_p = _trunk_layer(yp, zero_conv, zero_s, None, wl, bp, lp, lp, 0)
        kc_past = jnp.concatenate([cache_ckv[l], cache_krope[l], cache_krope[l]], axis=-1).astype(BF16)
        ys, *rest_s = _trunk_layer(ys, state_conv[l], state_gdn[l], kc_past, wl, bs, ls_pad, ls, past)
        outs_p.append(rest_p)
        outs_s.append(rest_s)
    stack = lambda outs, i: jnp.stack([o[i] for o in outs])
    y_prompt = yp.reshape(bp, lp, D_MODEL)
    y_sample = ys.reshape(bs, ls_pad, D_MODEL)[:, :ls]
    return (y_prompt, y_sample,
            stack(outs_p, 0), stack(outs_p, 1), stack(outs_p, 2), stack(outs_p, 3),
            stack(outs_s, 0), stack(outs_s, 1), stack(outs_s, 2), stack(outs_s, 3))
```

```python
import functools

import numpy as np
import jax
import jax.numpy as jnp
from jax import lax
from jax.experimental import pallas as pl
from jax.experimental.pallas import tpu as pltpu

F32 = jnp.float32
BF16 = jnp.bfloat16

D_MODEL = 1024
DEPTH = 2
CHUNK = 64
H_A = 8
DK_A = 128
DV_A = 128
QK_A = H_A * DK_A
V_A = H_A * DV_A
C_QKV = 2 * QK_A + V_A
CONV_W = 4
H_B = 8
NOPE = 128
ROPE = 64
V_B = 128
Q_RANK = 384
KV_RANK = 256
ROPE_THETA = 10000.0
ATTN_SCALE = (NOPE + ROPE) ** -0.5
Q_SCALE = ATTN_SCALE * float(np.log2(np.e))
D_FF = -(-8 * D_MODEL // (3 * 256)) * 256
ALPHA = (2 * DEPTH) ** 0.25
EPS = 1e-6
_SIZES = (C_QKV, V_A, H_A, H_A, Q_RANK, KV_RANK, ROPE, D_MODEL, D_MODEL)
_OFFS = tuple(int(v) for v in np.cumsum((0,) + _SIZES))

LANES = 128
VMEM_LIMIT = 56 * 1024 * 1024

COL_QKV = 0
COL_Z = COL_QKV + C_QKV
COL_GA = COL_Z + V_A
COL_GB = COL_GA + D_MODEL
COL_CKV = COL_GB + D_MODEL
COL_KRA = COL_CKV + KV_RANK
COL_CQ = COL_KRA + LANES
COL_KRB = COL_CQ + Q_RANK
COL_AB = COL_KRB + LANES
N_PROJ = COL_AB + LANES
KC_W = KV_RANK + LANES

GDN_CHUNK = 128


def _cparams(sem):
    return pltpu.CompilerParams(dimension_semantics=sem, vmem_limit_bytes=VMEM_LIMIT)


def _sigmoid(x):
    return jax.nn.sigmoid(x)


def _silu(x):
    return x * jax.nn.sigmoid(x)


def _mm(a, b):
    return jnp.dot(a.astype(BF16), b.astype(BF16), preferred_element_type=F32)


def _mm_nt(a, b):
    return lax.dot_general(a.astype(BF16), b.astype(BF16), (((1,), (1,)), ((), ())),
                           preferred_element_type=F32)


def _mm_tn(a, b):
    return lax.dot_general(a.astype(BF16), b.astype(BF16), (((0,), (0,)), ((), ())),
                           preferred_element_type=F32)


def _proj_body(x_ref, w_ref, o_ref, xb_ref):
    @pl.when(pl.program_id(1) == 0)
    def _():
        xb_ref[...] = x_ref[...].astype(BF16)

    o_ref[...] = jnp.dot(xb_ref[...], w_ref[...], preferred_element_type=F32)


def _proj_in(x, w, tm, tn):
    t, k = x.shape
    n = w.shape[1]
    return pl.pallas_call(
        _proj_body,
        grid=(t // tm, n // tn),
        in_specs=[pl.BlockSpec((tm, k), lambda i, j: (i, 0)),
                  pl.BlockSpec((k, tn), lambda i, j: (0, j))],
        out_specs=pl.BlockSpec((tm, tn), lambda i, j: (i, j)),
        out_shape=jax.ShapeDtypeStruct((t, n), F32),
        scratch_shapes=[pltpu.VMEM((tm, k), BF16)],
        compiler_params=_cparams(("parallel", "arbitrary")),
        name="proj_in",
    )(x, w)


def _gates_body(ab_ref, al_ref, dt_ref, o_ref, *, tm, l_pad, l_valid):
    x = ab_ref[...]
    lane = lax.broadcasted_iota(jnp.int32, x.shape, 1)
    xa = x + dt_ref[...]
    sp = jnp.maximum(xa, 0.0) + jnp.log1p(jnp.exp(-jnp.abs(xa)))
    g = -jnp.exp(al_ref[...]) * sp
    y = jnp.where(lane < H_A, g, _sigmoid(x))
    yt = y.T[0:2 * H_A, :]
    if l_valid < l_pad:
        col = lax.broadcasted_iota(jnp.int32, yt.shape, 1) + pl.program_id(0) * tm
        yt = jnp.where(col % l_pad < l_valid, yt, 0.0)
    r = lax.broadcasted_iota(jnp.int32, (GDN_CHUNK, GDN_CHUNK), 0)
    c = lax.broadcasted_iota(jnp.int32, (GDN_CHUNK, GDN_CHUNK), 1)
    tri = jnp.where(r <= c, 1.0, 0.0).astype(F32)
    for s in range(tm // GDN_CHUNK):
        sl = slice(s * GDN_CHUNK, (s + 1) * GDN_CHUNK)
        o_ref[0:H_A, sl] = jnp.dot(yt[0:H_A, sl], tri, precision=lax.Precision.HIGHEST,
                                   preferred_element_type=F32)
    o_ref[H_A:2 * H_A, :] = yt[H_A:2 * H_A, :]


def _gates(src, col_block, al_lane, dt_lane, tm, l_pad, l_valid):
    t = src.shape[0]
    body = functools.partial(_gates_body, tm=tm, l_pad=l_pad, l_valid=l_valid)
    return pl.pallas_call(
        body,
        grid=(t // tm,),
        in_specs=[pl.BlockSpec((tm, LANES), lambda i: (i, col_block)),
                  pl.BlockSpec((1, LANES), lambda i: (0, 0)),
                  pl.BlockSpec((1, LANES), lambda i: (0, 0))],
        out_specs=pl.BlockSpec((2 * H_A, tm), lambda i: (0, i)),
        out_shape=jax.ShapeDtypeStruct((2 * H_A, t), F32),
        compiler_params=_cparams(("parallel",)),
        name="gates",
    )(src, al_lane, dt_lane)


def _gdn_body(q_ref, k_ref, v_ref, z_ref, gc_ref, bt_ref, cwq_ref, cwk_ref, cwv_ref,
              csq_ref, csk_ref, csv_ref, s0_ref, gn_ref, o_ref, so_ref,
              xq_ref, xk_ref, xv_ref, s_ref, *, tb, tr, nt, hb):
    t = pl.program_id(2)
    hist = CONV_W - 1
    base = 8

    @pl.when(t == 0)
    def _():
        s_ref[...] = s0_ref[0]
        xq_ref[base - hist:base, :] = csq_ref[0]
        xk_ref[base - hist:base, :] = csk_ref[0]
        xv_ref[base - hist:base, :] = csv_ref[0]

    for xs_ref, raw_ref in ((xq_ref, q_ref), (xk_ref, k_ref), (xv_ref, v_ref)):
        xs_ref[base:base + tr, :] = raw_ref[...]
        if tr < tb:
            xs_ref[base + tr:base + tb, :] = jnp.zeros((tb - tr, xs_ref.shape[1]), F32)

    def conv(xs_ref, cw_ref, ln):
        y = xs_ref[base - hist:base - hist + tb, ln] * cw_ref[0:1, ln]
        for j in range(1, CONV_W):
            y = y + xs_ref[base - hist + j:base - hist + j + tb, ln] * cw_ref[j:j + 1, ln]
        return _silu(y)

    cc = GDN_CHUNK
    ii = lax.broadcasted_iota(jnp.int32, (cc, cc), 0)
    jj = lax.broadcasted_iota(jnp.int32, (cc, cc), 1)
    strict = ii > jj
    incl = ii >= jj
    eye = jnp.where(ii == jj, 1.0, 0.0).astype(F32)
    heads = range(hb)
    lanes = [slice(hd * LANES, (hd + 1) * LANES) for hd in heads]
    units = [(hd, c) for hd in heads for c in range(tb // cc)]
    rows = {u: slice(u[1] * cc, (u[1] + 1) * cc) for u in units}
    q = [conv(xq_ref, cwq_ref, ln) for ln in lanes]
    k = [conv(xk_ref, cwk_ref, ln) for ln in lanes]
    v = [conv(xv_ref, cwv_ref, ln) for ln in lanes]
    q = [x * lax.rsqrt(jnp.sum(x * x, -1, keepdims=True) + EPS) * (DK_A ** -0.5) for x in q]
    k = [x * lax.rsqrt(jnp.sum(x * x, -1, keepdims=True) + EPS) for x in k]
    qc = {u: q[u[0]][rows[u]] for u in units}
    kc = {u: k[u[0]][rows[u]] for u in units}
    vc = {u: v[u[0]][rows[u]] for u in units}
    m_row = {u: jnp.broadcast_to(gc_ref[u[0], :, rows[u]], (cc, cc)) for u in units}
    m_col = {u: m_row[u].T for u in units}
    b_col = {u: jnp.broadcast_to(bt_ref[u[0], :, rows[u]], (cc, cc)).T for u in units}
    kq = {u: _mm_nt(jnp.concatenate([kc[u], qc[u]], axis=0), kc[u]) for u in units}
    diff = {u: m_col[u] - m_row[u] for u in units}
    a_mat = {u: b_col[u] * kq[u][0:cc] * jnp.exp(jnp.where(strict, diff[u], -jnp.inf)) for u in units}
    qk = {u: kq[u][cc:2 * cc] * jnp.exp(jnp.where(incl, diff[u], -jnp.inf)) for u in units}
    e_g = {u: jnp.exp(m_col[u]) for u in units}
    x_inv = {u: eye - jnp.where((ii >> 1) == (jj >> 1), a_mat[u], 0.0) for u in units}
    sft = 1
    while (1 << sft) < cc:
        off = ((ii >> (sft + 1)) == (jj >> (sft + 1))) & ((ii >> sft) != (jj >> sft))
        lx = {u: _mm(jnp.where(off, a_mat[u], 0.0), x_inv[u]) for u in units}
        x_inv = {u: x_inv[u] - _mm(x_inv[u], lx[u]) for u in units}
        sft += 1
    uw = {u: _mm(x_inv[u], jnp.concatenate([vc[u] * b_col[u], kc[u] * (b_col[u] * e_g[u])], axis=1))
          for u in units}
    wq = {u: jnp.concatenate([uw[u][:, DV_A:], qc[u] * e_g[u]], axis=0) for u in units}
    g_last = {u: m_col[u][cc - 1:cc, :] for u in units}
    kd = {u: kc[u] * jnp.exp(g_last[u] - m_col[u]) for u in units}
    s_state = [s_ref[hd] for hd in heads]
    for c in range(tb // cc):
        ws = [_mm(wq[(hd, c)], s_state[hd]) for hd in heads]
        v_new = [uw[(hd, c)][:, 0:DV_A] - ws[hd][0:cc] for hd in heads]
        s_state = [s_state[hd] * jnp.exp(g_last[(hd, c)]) + _mm_tn(kd[(hd, c)], v_new[hd]) for hd in heads]
        o = [ws[hd][cc:2 * cc] + _mm(qk[(hd, c)], v_new[hd]) for hd in heads]
        o = [x * lax.rsqrt(jnp.mean(x * x, -1, keepdims=True) + EPS) * gn_ref[...] for x in o]
        nr = min(cc, tr - c * cc)
        r = slice(c * cc, c * cc + nr)
        for hd in heads:
            o_ref[r, lanes[hd]] = (o[hd][0:nr] * _silu(z_ref[r, lanes[hd]])).astype(o_ref.dtype)
    for hd in heads:
        s_ref[hd] = s_state[hd]

    xq_ref[base - hist:base, :] = xq_ref[base + tb - hist:base + tb, :]
    xk_ref[base - hist:base, :] = xk_ref[base + tb - hist:base + tb, :]
    xv_ref[base - hist:base, :] = xv_ref[base + tb - hist:base + tb, :]

    @pl.when(t == nt - 1)
    def _():
        so_ref[0] = s_ref[...]


def _gdn(h1, gates3, conv_w, conv_state, s0, gn, bsz, seq, tb, hb):
    nt = -(-seq // tb)
    tr = min(tb, seq)
    assert seq % tr == 0 and (tr == tb or (nt == 1 and tb == GDN_CHUNK))
    t_rows = bsz * seq
    body = functools.partial(_gdn_body, tb=tb, tr=tr, nt=nt, hb=hb)
    width = hb * LANES
    ng = H_A // hb

    def rows(seg):
        return pl.BlockSpec((tr, width), lambda b, g, t: (b * nt + t, seg * ng + g))

    def cw(seg):
        return pl.BlockSpec((CONV_W, width), lambda b, g, t: (0, seg * ng + g))

    def cs(seg):
        return pl.BlockSpec((1, CONV_W - 1, width), lambda b, g, t: (b, 0, seg * ng + g))

    return pl.pallas_call(
        body,
        grid=(bsz, ng, nt),
        in_specs=[rows(0), rows(1), rows(2), rows(COL_Z // QK_A),
                  pl.BlockSpec((hb, 1, tb), lambda b, g, t: (g, 0, b * nt + t)),
                  pl.BlockSpec((hb, 1, tb), lambda b, g, t: (ng + g, 0, b * nt + t)),
                  cw(0), cw(1), cw(2), cs(0), cs(1), cs(2),
                  pl.BlockSpec((1, hb, DK_A, DV_A), lambda b, g, t: (b, g, 0, 0)),
                  pl.BlockSpec((1, DV_A), lambda b, g, t: (0, 0))],
        out_specs=[pl.BlockSpec((tr, width), lambda b, g, t: (b * nt + t, g)),
                   pl.BlockSpec((1, hb, DK_A, DV_A), lambda b, g, t: (b, g, 0, 0))],
        out_shape=[jax.ShapeDtypeStruct((t_rows, V_A), BF16),
                   jax.ShapeDtypeStruct((bsz, H_A, DK_A, DV_A), F32)],
        scratch_shapes=[pltpu.VMEM((tb + 8, width), F32)] * 3 + [pltpu.VMEM((hb, DK_A, DV_A), F32)],
        compiler_params=_cparams(("parallel", "parallel", "arbitrary")),
        name="gdn",
    )(h1, h1, h1, h1, gates3, gates3, conv_w, conv_w, conv_w,
      conv_state, conv_state, conv_state, s0, gn)


def _mla_pre_body(cq_ref, ckv_ref, kra_ref, krb_ref, cos_ref, sin_ref, qg_ref, wuq_ref, wuk_ref,
                  kvg_ref, q_ref, kc_ref, ckvo_ref, kro_ref, maybe_vt_ref=None):
    cq = cq_ref[...]
    cqn = cq * lax.rsqrt(jnp.mean(cq * cq, -1, keepdims=True) + EPS) * qg_ref[...]
    qf = jnp.dot(cqn.astype(BF16), wuq_ref[...], preferred_element_type=F32)
    cos_k = cos_ref[...]
    sin_k = sin_ref[...]
    reps = H_B * ROPE // LANES
    cos_t = jnp.concatenate([cos_k] * reps, axis=1)
    sin_t = jnp.concatenate([sin_k] * reps, axis=1)
    n_nope = H_B * NOPE
    n_rope = H_B * ROPE
    qr = (qf[:, n_nope:n_nope + n_rope] * cos_t + qf[:, n_nope + n_rope:] * sin_t) * Q_SCALE
    lane = lax.broadcasted_iota(jnp.int32, (cq.shape[0], LANES), 1)
    for h in range(H_B):
        ql = jnp.dot(qf[:, h * NOPE:(h + 1) * NOPE].astype(BF16), wuk_ref[h],
                     preferred_element_type=F32) * Q_SCALE
        blk = qr[:, (h // 2) * LANES:(h // 2 + 1) * LANES]
        keep = (lane < ROPE) if h % 2 == 0 else (lane >= ROPE)
        q_ref[h, :, 0:KV_RANK] = ql.astype(BF16)
        q_ref[h, :, KV_RANK:KC_W] = jnp.where(keep, blk, 0.0).astype(BF16)
    ckv = ckv_ref[...]
    ckvn = ckv * lax.rsqrt(jnp.mean(ckv * ckv, -1, keepdims=True) + EPS) * kvg_ref[...]
    kr2 = kra_ref[...] * cos_k + krb_ref[...] * sin_k
    ckvo_ref[...] = ckvn
    kro_ref[...] = kr2[:, 0:ROPE]
    kc_ref[:, 0:KV_RANK] = ckvn.astype(BF16)
    kc_ref[:, KV_RANK:KC_W] = kr2.astype(BF16)
    if maybe_vt_ref is not None:
        maybe_vt_ref[0] = ckvn.T.astype(BF16)


def _mla_pre(h1, cos_t, sin_t, qg, wuq, wuk, kvg, bsz, seq, tm, with_vt):
    t_rows = bsz * seq
    ntab = cos_t.shape[0] // tm
    npb = max(seq // tm, 1)
    const2 = lambda i: (0, 0)
    out_specs = [pl.BlockSpec((H_B, tm, KC_W), lambda i: (0, i, 0)),
                 pl.BlockSpec((tm, KC_W), lambda i: (i, 0)),
                 pl.BlockSpec((tm, KV_RANK), lambda i: (i, 0)),
                 pl.BlockSpec((tm, ROPE), lambda i: (i, 0))]
    out_shape = [jax.ShapeDtypeStruct((H_B, t_rows, KC_W), BF16),
                 jax.ShapeDtypeStruct((t_rows, KC_W), BF16),
                 jax.ShapeDtypeStruct((t_rows, KV_RANK), F32),
                 jax.ShapeDtypeStruct((t_rows, ROPE), F32)]
    if with_vt:
        out_specs.append(pl.BlockSpec((1, KV_RANK, tm), lambda i: (i // npb, 0, i % npb)))
        out_shape.append(jax.ShapeDtypeStruct((bsz, KV_RANK, seq), BF16))
    return pl.pallas_call(
        _mla_pre_body,
        grid=(t_rows // tm,),
        in_specs=[pl.BlockSpec((tm, Q_RANK), lambda i: (i, COL_CQ // Q_RANK)),
                  pl.BlockSpec((tm, KV_RANK), lambda i: (i, COL_CKV // KV_RANK)),
                  pl.BlockSpec((tm, LANES), lambda i: (i, COL_KRA // LANES)),
                  pl.BlockSpec((tm, LANES), lambda i: (i, COL_KRB // LANES)),
                  pl.BlockSpec((tm, LANES), lambda i: (i % ntab, 0)),
                  pl.BlockSpec((tm, LANES), lambda i: (i % ntab, 0)),
                  pl.BlockSpec((1, Q_RANK), const2),
                  pl.BlockSpec(wuq.shape, const2),
                  pl.BlockSpec(wuk.shape, lambda i: (0, 0, 0)),
                  pl.BlockSpec((1, KV_RANK), const2)],
        out_specs=out_specs,
        out_shape=out_shape,
        compiler_params=_cparams(("parallel",)),
        name="mla_pre",
    )(h1, h1, h1, h1, cos_t, sin_t, qg, wuq, wuk, kvg)


def _attn_body(qi_ref, kj_ref, fl_ref, q_ref, k_ref, vt_ref, wuv_ref, o_ref, m_ref, l_ref, acc_ref,
               *, tq, tk, past, lk):
    s_idx = pl.program_id(1)
    flags = fl_ref[s_idx]

    @pl.when((flags & 1) != 0)
    def _():
        m_ref[...] = jnp.full(m_ref.shape, -jnp.inf, F32)
        l_ref[...] = jnp.zeros(l_ref.shape, F32)
        acc_ref[...] = jnp.zeros(acc_ref.shape, F32)

    def scores(h, bias):
        s = lax.dot_general(k_ref[0], q_ref[h], (((1,), (1,)), ((), ())),
                            preferred_element_type=F32)
        return s if bias is None else s + bias

    def update(masked):
        bias = None
        if masked:
            shift = CHUNK.bit_length() - 1
            kpos = kj_ref[s_idx] * tk + lax.broadcasted_iota(jnp.int32, (tk, tq), 0)
            qpos = past + qi_ref[s_idx] * tq + lax.broadcasted_iota(jnp.int32, (tk, tq), 1)
            ok = ((kpos >> shift) <= (qpos >> shift)) & (kpos < lk)
            bias = jnp.where(ok, 0.0, -jnp.inf).astype(F32)
        s_next = scores(0, bias)
        for h in range(H_B):
            s = s_next
            if h + 1 < H_B:
                s_next = scores(h + 1, bias)
            m_prev = m_ref[h]
            m_new = jnp.maximum(m_prev, jnp.max(s, 0, keepdims=True))
            p = jnp.exp2(s - m_new)
            alpha = jnp.exp2(m_prev - m_new)
            l_ref[h] = alpha * l_ref[h] + jnp.sum(p, 0, keepdims=True)
            pv = jnp.dot(vt_ref[0], p.astype(BF16), preferred_element_type=F32)
            acc_ref[h] = alpha * acc_ref[h] + pv
            m_ref[h] = m_new

    @pl.when((flags & 4) != 0)
    def _():
        update(True)

    @pl.when((flags & 4) == 0)
    def _():
        update(False)

    @pl.when((flags & 2) != 0)
    def _():
        for h in range(H_B):
            o_t = (acc_ref[h] * (1.0 / l_ref[h])).astype(BF16)
            ob_t = jnp.dot(wuv_ref[h], o_t, preferred_element_type=F32)
            o_ref[0, :, h * V_B:(h + 1) * V_B] = ob_t.T.astype(o_ref.dtype)


def _attn_steps(seq, past, lk, tq, tk):
    qi, kj, fl = [], [], []
    for i in range(seq // tq):
        p0 = past + i * tq
        p1 = p0 + tq - 1
        last_vis = min(lk - 1, (p1 // CHUNK) * CHUNK + CHUNK - 1)
        all_vis = min(lk - 1, (p0 // CHUNK) * CHUNK + CHUNK - 1)
        jmax = last_vis // tk
        for j in range(jmax + 1):
            f = (1 if j == 0 else 0) | (2 if j == jmax else 0)
            if (j + 1) * tk - 1 > all_vis:
                f |= 4
            qi.append(i), kj.append(j), fl.append(f)
    return (np.asarray(qi, np.int32), np.asarray(kj, np.int32), np.asarray(fl, np.int32))


def _attn(q, kc, vt, wuv_t, bsz, seq, past, lk, tq, tk):
    nq = seq // tq
    qi, kj, fl = _attn_steps(seq, past, lk, tq, tk)
    body = functools.partial(_attn_body, tq=tq, tk=tk, past=past, lk=lk)
    grid_spec = pltpu.PrefetchScalarGridSpec(
        num_scalar_prefetch=3,
        grid=(bsz, len(qi)),
        in_specs=[pl.BlockSpec((H_B, tq, KC_W), lambda b, s, qi, kj, fl: (0, b * nq + qi[s], 0)),
                  pl.BlockSpec((1, tk, KC_W), lambda b, s, qi, kj, fl: (b, kj[s], 0)),
                  pl.BlockSpec((1, KV_RANK, tk), lambda b, s, qi, kj, fl: (b, 0, kj[s])),
                  pl.BlockSpec(wuv_t.shape, lambda b, s, qi, kj, fl: (0, 0, 0))],
        out_specs=pl.BlockSpec((1, tq, H_B * V_B), lambda b, s, qi, kj, fl: (b, qi[s], 0)),
        scratch_shapes=[pltpu.VMEM((H_B, 1, tq), F32), pltpu.VMEM((H_B, 1, tq), F32),
                        pltpu.VMEM((H_B, KV_RANK, tq), F32)],
    )
    return pl.pallas_call(
        body,
        grid_spec=grid_spec,
        out_shape=jax.ShapeDtypeStruct((bsz, seq, H_B * V_B), BF16),
        compiler_params=_cparams(("parallel", "arbitrary")),
        name="attn",
    )(jnp.asarray(qi), jnp.asarray(kj), jnp.asarray(fl), q, kc, vt, wuv_t)


def _attn_dec_body(q_ref, ckv_ref, kr_ref, kn_ref, wuv_ref, o_ref, m_ref, l_ref, acc_ref,
                   *, tq, past, n_past):
    j = pl.program_id(1)
    cols = H_B * tq
    nt_dims = (((1,), (1,)), ((), ()))
    q2 = q_ref[...].reshape(cols, KC_W)

    @pl.when(j == 0)
    def _():
        m_ref[...] = jnp.full(m_ref.shape, -jnp.inf, F32)
        l_ref[...] = jnp.zeros(l_ref.shape, F32)
        acc_ref[...] = jnp.zeros(acc_ref.shape, F32)

    def accumulate(s, v_nat):
        m_prev = m_ref[...]
        m_new = jnp.maximum(m_prev, jnp.max(s, 0, keepdims=True))
        p = jnp.exp2(s - m_new)
        alpha = jnp.exp2(m_prev - m_new)
        l_ref[...] = alpha * l_ref[...] + jnp.sum(p, 0, keepdims=True)
        acc_ref[...] = alpha * acc_ref[...] + _mm_tn(v_nat, p)
        m_ref[...] = m_new

    @pl.when(j < n_past)
    def _():
        k_lat = ckv_ref[0].astype(BF16)
        kr = kr_ref[0].astype(BF16)
        kr2 = jnp.concatenate([kr, kr], axis=1)
        s = (lax.dot_general(k_lat, q2[:, 0:KV_RANK], nt_dims, preferred_element_type=F32)
             + lax.dot_general(kr2, q2[:, KV_RANK:KC_W], nt_dims, preferred_element_type=F32))
        accumulate(s, k_lat)

    @pl.when(j == n_past)
    def _():
        kn = kn_ref[...]
        s = lax.dot_general(kn, q2, nt_dims, preferred_element_type=F32)
        shift = CHUNK.bit_length() - 1
        kpos = past + lax.broadcasted_iota(jnp.int32, s.shape, 0)
        qpos = past + (lax.broadcasted_iota(jnp.int32, s.shape, 1) & (tq - 1))
        s = jnp.where((kpos >> shift) <= (qpos >> shift), s, -jnp.inf)
        accumulate(s, kn[:, 0:KV_RANK])
        o = (acc_ref[...] * (1.0 / l_ref[...])).T
        for h in range(H_B):
            oh = o[h * tq:(h + 1) * tq, :].astype(BF16)
            o_ref[:, h * V_B:(h + 1) * V_B] = jnp.dot(
                oh, wuv_ref[h], preferred_element_type=F32).astype(o_ref.dtype)


def _attn_dec(q, cache_ckv, cache_kr, kc_new, wuv, bsz, seq, tk):
    past = cache_ckv.shape[1]
    assert past % tk == 0 and past % CHUNK == 0 and seq & (seq - 1) == 0
    n_past = past // tk
    body = functools.partial(_attn_dec_body, tq=seq, past=past, n_past=n_past)
    cache_idx = lambda b, j: (b, jnp.minimum(j, n_past - 1), 0)
    return pl.pallas_call(
        body,
        grid=(bsz, n_past + 1),
        in_specs=[pl.BlockSpec((H_B, seq, KC_W), lambda b, j: (0, b, 0)),
                  pl.BlockSpec((1, tk, KV_RANK), cache_idx),
                  pl.BlockSpec((1, tk, ROPE), cache_idx),
                  pl.BlockSpec((seq, KC_W), lambda b, j: (b, 0)),
                  pl.BlockSpec(wuv.shape, lambda b, j: (0, 0, 0))],
        out_specs=pl.BlockSpec((seq, H_B * V_B), lambda b, j: (b, 0)),
        out_shape=jax.ShapeDtypeStruct((bsz * seq, H_B * V_B), BF16),
        scratch_shapes=[pltpu.VMEM((1, H_B * seq), F32), pltpu.VMEM((1, H_B * seq), F32),
                        pltpu.VMEM((KV_RANK, H_B * seq), F32)],
        compiler_params=_cparams(("parallel", "arbitrary")),
        name="attn_dec",
    )(q, cache_ckv, cache_kr, kc_new, wuv)


def _layer_norm(r, g, b):
    mu = jnp.mean(r, -1, keepdims=True)
    d = r - mu
    var = jnp.mean(d * d, -1, keepdims=True)
    return d * lax.rsqrt(var + EPS) * g + b


def _merge_body(oa_ref, ob_ref, ga_ref, gb_ref, x_ref, woa_ref, wob_ref, wout_ref, g_ref, b_ref, o_ref):
    ya = jnp.dot(oa_ref[...], woa_ref[...], preferred_element_type=F32)
    yb = jnp.dot(ob_ref[...], wob_ref[...], preferred_element_type=F32)
    m = _sigmoid(ga_ref[...]) * ya + _sigmoid(gb_ref[...]) * yb
    r = ALPHA * x_ref[...] + jnp.dot(m.astype(BF16), wout_ref[...], preferred_element_type=F32)
    o_ref[...] = _layer_norm(r, g_ref[...], b_ref[...])


def _merge(oa, ob, h1, x, woa, wob, wout, g, b, tm):
    t = x.shape[0]
    row = lambda i: (i, 0)
    const = lambda i: (0, 0)
    wspec = pl.BlockSpec((D_MODEL, D_MODEL), const)
    return pl.pallas_call(
        _merge_body,
        grid=(t // tm,),
        in_specs=[pl.BlockSpec((tm, V_A), row), pl.BlockSpec((tm, H_B * V_B), row),
                  pl.BlockSpec((tm, D_MODEL), lambda i: (i, COL_GA // D_MODEL)),
                  pl.BlockSpec((tm, D_MODEL), lambda i: (i, COL_GB // D_MODEL)),
                  pl.BlockSpec((tm, D_MODEL), row), wspec, wspec, wspec,
                  pl.BlockSpec((1, D_MODEL), const), pl.BlockSpec((1, D_MODEL), const)],
        out_specs=pl.BlockSpec((tm, D_MODEL), row),
        out_shape=jax.ShapeDtypeStruct((t, D_MODEL), F32),
        compiler_params=_cparams(("parallel",)),
        name="merge",
    )(oa, ob, h1, h1, x, woa, wob, wout, g, b)


def _ffn_body(x_ref, wg_ref, wu_ref, wd_ref, g_ref, b_ref, o_ref, h_ref, *, tf):
    x = x_ref[...]
    xb = x.astype(BF16)
    for j in range(D_FF // tf):
        sl = slice(j * tf, (j + 1) * tf)
        f1 = jnp.dot(xb, wg_ref[:, sl], preferred_element_type=F32)
        f3 = jnp.dot(xb, wu_ref[:, sl], preferred_element_type=F32)
        h_ref[:, sl] = (_silu(f1) * f3).astype(BF16)
    y = jnp.dot(h_ref[...], wd_ref[...], preferred_element_type=F32)
    o_ref[...] = _layer_norm(ALPHA * x + y, g_ref[...], b_ref[...])


def _ffn(x, wg, wu, wd, g, b, tm, tf):
    t = x.shape[0]
    row = lambda i: (i, 0)
    const = lambda i: (0, 0)
    single = pl.Buffered(1)
    return pl.pallas_call(
        functools.partial(_ffn_body, tf=tf),
        grid=(t // tm,),
        in_specs=[pl.BlockSpec((tm, D_MODEL), row),
                  pl.BlockSpec((D_MODEL, D_FF), const, pipeline_mode=single),
                  pl.BlockSpec((D_MODEL, D_FF), const, pipeline_mode=single),
                  pl.BlockSpec((D_FF, D_MODEL), const, pipeline_mode=single),
                  pl.BlockSpec((1, D_MODEL), const), pl.BlockSpec((1, D_MODEL), const)],
        out_specs=pl.BlockSpec((tm, D_MODEL), row),
        out_shape=jax.ShapeDtypeStruct((t, D_MODEL), F32),
        scratch_shapes=[pltpu.VMEM((tm, D_FF), BF16)],
        compiler_params=_cparams(("parallel",)),
        name="ffn",
    )(x, wg, wu, wd, g, b)


def _prep_layer_weights(w_in, conv_w, a_log, dt_bias, gdn_norm_g, w_oa, q_norm_g, w_uq, kv_norm_g,
                        w_ukv, w_ob, w_out, ln1_g, ln1_b, w_gu, w_down, ln2_g, ln2_b):
    seg = lambda i: w_in[:, _OFFS[i]:_OFFS[i + 1]]
    qkv, z, a, b, c_q, c_kv, k_r, g_a, g_b = (seg(i) for i in range(9))
    half = ROPE // 2
    k_r_rot = jnp.concatenate([-k_r[:, half:], k_r[:, :half]], axis=1)
    pad = jnp.zeros((D_MODEL, LANES - 2 * H_A), w_in.dtype)
    w_proj = jnp.concatenate([qkv, z, g_a, g_b, c_kv, k_r, k_r, c_q, k_r_rot, k_r_rot, a, b, pad],
                             axis=1).astype(BF16)
    lane_pad = jnp.zeros((LANES - H_A,), F32)
    al_lane = jnp.concatenate([a_log.astype(F32), lane_pad]).reshape(1, LANES)
    dt_lane = jnp.concatenate([dt_bias.astype(F32), lane_pad]).reshape(1, LANES)
    uq = w_uq.reshape(Q_RANK, H_B, NOPE + ROPE)
    uq_nope = uq[:, :, :NOPE].reshape(Q_RANK, H_B * NOPE)
    uq_rope = uq[:, :, NOPE:]
    uq_rot = jnp.concatenate([-uq_rope[:, :, half:], uq_rope[:, :, :half]], axis=2)
    w_uq_ext = jnp.concatenate([uq_nope, uq_rope.reshape(Q_RANK, H_B * ROPE),
                                uq_rot.reshape(Q_RANK, H_B * ROPE)], axis=1).astype(BF16)
    ukv = w_ukv.reshape(KV_RANK, H_B, NOPE + V_B)
    w_uk_t = jnp.transpose(ukv[:, :, :NOPE], (1, 2, 0)).astype(BF16)
    w_uv = jnp.transpose(ukv[:, :, NOPE:], (1, 0, 2)).astype(BF16)
    w_uv_t = jnp.transpose(ukv[:, :, NOPE:], (1, 2, 0)).astype(BF16)
    return dict(
        w_proj=w_proj, conv_w=conv_w.astype(F32), al_lane=al_lane, dt_lane=dt_lane,
        gn=gdn_norm_g.reshape(1, DV_A).astype(F32), w_oa=w_oa.astype(BF16),
        qg=q_norm_g.reshape(1, Q_RANK).astype(F32), w_uq=w_uq_ext, w_uk_t=w_uk_t, w_uv=w_uv, w_uv_t=w_uv_t,
        kvg=kv_norm_g.reshape(1, KV_RANK).astype(F32), w_ob=w_ob.astype(BF16),
        w_out=w_out.astype(BF16), ln1_g=ln1_g.reshape(1, D_MODEL), ln1_b=ln1_b.reshape(1, D_MODEL),
        w_g=w_gu[:, :D_FF].astype(BF16), w_u=w_gu[:, D_FF:].astype(BF16), w_down=w_down.astype(BF16),
        ln2_g=ln2_g.reshape(1, D_MODEL), ln2_b=ln2_b.reshape(1, D_MODEL))


def _rope_tables(past, seq, reps):
    half = ROPE // 2
    inv = ROPE_THETA ** (-jnp.arange(half, dtype=F32) / half)
    ang = (past + jnp.arange(seq)).astype(F32)[:, None] * inv[None, :]
    cos = jnp.tile(jnp.cos(ang), (reps, LANES // half))
    sin = jnp.tile(jnp.sin(ang), (reps, LANES // half))
    return cos, sin


TM_PROJ, TN_PROJ, TM_GATES, TM_MLA, TM_MERGE, TM_FFN = 1024, 1024, 1024, 512, 512, 512
TF_FFN = D_FF // 2
TQ_ATTN, TK_ATTN, TK_DEC = 512, 512, 1024
TB_GDN, HB_GDN = 2 * GDN_CHUNK, H_A


def _trunk_layer(x, conv_state, s0, caches, wl, bsz, seq):
    decode = caches is not None
    t_rows = bsz * seq
    h1 = _proj_in(x, wl["w_proj"], min(TM_PROJ, t_rows), TN_PROJ)
    seq_pad = -(-seq // GDN_CHUNK) * GDN_CHUNK
    if seq_pad == seq:
        gates = _gates(h1, COL_AB // LANES, wl["al_lane"], wl["dt_lane"], min(TM_GATES, t_rows), seq, seq)
    else:
        ab = h1[:, COL_AB:COL_AB + LANES].reshape(bsz, seq, LANES)
        ab = jnp.pad(ab, ((0, 0), (0, seq_pad - seq), (0, 0))).reshape(bsz * seq_pad, LANES)
        gates = _gates(ab, 0, wl["al_lane"], wl["dt_lane"], min(TM_GATES, bsz * seq_pad), seq_pad, seq)
    o_a, s_new = _gdn(h1, gates.reshape(2 * H_A, 1, bsz * seq_pad), wl["conv_w"], conv_state, s0, wl["gn"],
                      bsz, seq, min(TB_GDN, seq_pad), HB_GDN)
    if decode:
        past = caches[0].shape[1]
        tm_mla = t_rows if t_rows <= TM_MLA else seq
        cos_t, sin_t = _rope_tables(past, seq, tm_mla // seq)
        q, kc, ckv_new, kr_new = _mla_pre(h1, cos_t, sin_t, wl["qg"], wl["w_uq"], wl["w_uk_t"], wl["kvg"],
                                          bsz, seq, tm_mla, False)
        o_b = _attn_dec(q, caches[0], caches[1], kc, wl["w_uv"], bsz, seq, TK_DEC)
    else:
        cos_t, sin_t = _rope_tables(0, seq, 1)
        q, kc, ckv_new, kr_new, vt = _mla_pre(h1, cos_t, sin_t, wl["qg"], wl["w_uq"], wl["w_uk_t"],
                                              wl["kvg"], bsz, seq, TM_MLA, True)
        o_b = _attn(q, kc.reshape(bsz, seq, KC_W), vt, wl["w_uv_t"], bsz, seq, 0, seq, TQ_ATTN, TK_ATTN)
        o_b = o_b.reshape(t_rows, H_B * V_B)
    x1 = _merge(o_a, o_b, h1, x, wl["w_oa"], wl["w_ob"], wl["w_out"], wl["ln1_g"], wl["ln1_b"],
                min(TM_MERGE, t_rows))
    x2 = _ffn(x1, wl["w_g"], wl["w_u"], wl["w_down"], wl["ln2_g"], wl["ln2_b"], min(TM_FFN, t_rows), TF_FFN)
    conv_new = h1.reshape(bsz, seq, N_PROJ)[:, seq - (CONV_W - 1):, COL_QKV:COL_QKV + C_QKV]
    return (x2, conv_new, s_new, ckv_new.reshape(bsz, seq, KV_RANK), kr_new.reshape(bsz, seq, ROPE))


def kernel(x_prompt, x_sample, state_conv, state_gdn, cache_ckv, cache_krope, w_in, conv_w, a_log, dt_bias, gdn_norm_g, w_oa, q_norm_g, w_uq, kv_norm_g, w_ukv, w_ob, w_out, ln1_g, ln1_b, w_gu, w_down, ln2_g, ln2_b):
    bp, lp, _ = x_prompt.shape
    bs, ls, _ = x_sample.shape
    yp = x_prompt.reshape(bp * lp, D_MODEL)
    ys = x_sample.reshape(bs * ls, D_MODEL)
    zero_conv = jnp.zeros((bp, CONV_W - 1, C_QKV), F32)
    zero_s = jnp.zeros((bp, H_A, DK_A, DV_A), F32)
    outs_p, outs_s = [], []
    for l in range(w_in.shape[0]):
        wl = _prep_layer_weights(w_in[l], conv_w[l], a_log[l], dt_bias[l], gdn_norm_g[l], w_oa[l],
                                 q_norm_g[l], w_uq[l], kv_norm_g[l], w_ukv[l], w_ob[l], w_out[l],
                                 ln1_g[l], ln1_b[l], w_gu[l], w_down[l], ln2_g[l], ln2_b[l])
        yp, *rest_p = _trunk_layer(yp, zero_conv, zero_s, None, wl, bp, lp)
        ys, *rest_s = _trunk_layer(ys, state_conv[l], state_gdn[l], (cache_ckv[l], cache_krope[l]), wl, bs, ls)
        outs_p.append(rest_p)
        outs_s.append(rest_s)
    stack = lambda outs, i: jnp.stack([o[i] for o in outs])
    return (yp.reshape(bp, lp, D_MODEL), ys.reshape(bs, ls, D_MODEL),
            stack(outs_p, 0), stack(outs_p, 1), stack(outs_p, 2), stack(outs_p, 3),
            stack(outs_s, 0), stack(outs_s, 1), stack(outs_s, 2), stack(outs_s, 3))
```

```python
import functools

import numpy as np
import jax
import jax.numpy as jnp
from jax import lax
from jax.experimental import pallas as pl
from jax.experimental.pallas import tpu as pltpu

F32 = jnp.float32
BF16 = jnp.bfloat16

D_MODEL = 1024
DEPTH = 2
CHUNK = 64
H_A = 8
DK_A = 128
DV_A = 128
QK_A = H_A * DK_A
V_A = H_A * DV_A
C_QKV = 2 * QK_A + V_A
CONV_W = 4
H_B = 8
NOPE = 128
ROPE = 64
V_B = 128
Q_RANK = 384
KV_RANK = 256
ROPE_THETA = 10000.0
ATTN_SCALE = (NOPE + ROPE) ** -0.5
LOG2_E = float(np.log2(np.e))
Q_SCALE = ATTN_SCALE * LOG2_E
D_FF = -(-8 * D_MODEL // (3 * 256)) * 256
ALPHA = (2 * DEPTH) ** 0.25
EPS = 1e-6
_SIZES = (C_QKV, V_A, H_A, H_A, Q_RANK, KV_RANK, ROPE, D_MODEL, D_MODEL)
_OFFS = tuple(int(v) for v in np.cumsum((0,) + _SIZES))

LANES = 128
VMEM_LIMIT = 56 * 1024 * 1024

COL_QKV = 0
COL_Z = COL_QKV + C_QKV
COL_GA = COL_Z + V_A
COL_GB = COL_GA + D_MODEL
COL_CKV = COL_GB + D_MODEL
COL_KRA = COL_CKV + KV_RANK
COL_CQ = COL_KRA + LANES
COL_KRB = COL_CQ + Q_RANK
COL_AB = COL_KRB + LANES
N_PROJ = COL_AB + LANES
KC_W = KV_RANK + LANES

GDN_CHUNK = 128


def _cparams(sem):
    return pltpu.CompilerParams(dimension_semantics=sem, vmem_limit_bytes=VMEM_LIMIT)


def _sigmoid(x):
    return jax.nn.sigmoid(x)


def _silu(x):
    return x * jax.nn.sigmoid(x)


def _mm(a, b):
    return jnp.dot(a.astype(BF16), b.astype(BF16), preferred_element_type=F32)


def _mm_nt(a, b):
    return lax.dot_general(a.astype(BF16), b.astype(BF16), (((1,), (1,)), ((), ())),
                           preferred_element_type=F32)


def _mm_tn(a, b):
    return lax.dot_general(a.astype(BF16), b.astype(BF16), (((0,), (0,)), ((), ())),
                           preferred_element_type=F32)


def _proj_body(x_ref, w_ref, o_ref, xb_ref):
    @pl.when(pl.program_id(1) == 0)
    def _():
        xb_ref[...] = x_ref[...].astype(BF16)

    o_ref[...] = jnp.dot(xb_ref[...], w_ref[...], preferred_element_type=F32)


def _proj_in(x, w, tm, tn):
    t, k = x.shape
    n = w.shape[1]
    return pl.pallas_call(
        _proj_body,
        grid=(t // tm, n // tn),
        in_specs=[pl.BlockSpec((tm, k), lambda i, j: (i, 0)),
                  pl.BlockSpec((k, tn), lambda i, j: (0, j))],
        out_specs=pl.BlockSpec((tm, tn), lambda i, j: (i, j)),
        out_shape=jax.ShapeDtypeStruct((t, n), F32),
        scratch_shapes=[pltpu.VMEM((tm, k), BF16)],
        compiler_params=_cparams(("parallel", "arbitrary")),
        name="proj_in",
    )(x, w)


def _gates_body(ab_ref, al_ref, dt_ref, o_ref, *, tm, l_pad, l_valid):
    x = ab_ref[...]
    lane = lax.broadcasted_iota(jnp.int32, x.shape, 1)
    xa = x + dt_ref[...]
    sp = jnp.maximum(xa, 0.0) + jnp.log1p(jnp.exp(-jnp.abs(xa)))
    g = -jnp.exp(al_ref[...]) * sp * LOG2_E
    y = jnp.where(lane < H_A, g, _sigmoid(x))
    yt = y.T[0:2 * H_A, :]
    if l_valid < l_pad:
        col = lax.broadcasted_iota(jnp.int32, yt.shape, 1) + pl.program_id(0) * tm
        yt = jnp.where(col % l_pad < l_valid, yt, 0.0)
    r = lax.broadcasted_iota(jnp.int32, (GDN_CHUNK, GDN_CHUNK), 0)
    c = lax.broadcasted_iota(jnp.int32, (GDN_CHUNK, GDN_CHUNK), 1)
    tri = jnp.where(r <= c, 1.0, 0.0).astype(F32)
    for s in range(tm // GDN_CHUNK):
        sl = slice(s * GDN_CHUNK, (s + 1) * GDN_CHUNK)
        o_ref[0:H_A, sl] = jnp.dot(yt[0:H_A, sl], tri, precision=lax.Precision.HIGHEST,
                                   preferred_element_type=F32)
    o_ref[H_A:2 * H_A, :] = yt[H_A:2 * H_A, :]


def _gates(src, col_block, al_lane, dt_lane, tm, l_pad, l_valid):
    t = src.shape[0]
    body = functools.partial(_gates_body, tm=tm, l_pad=l_pad, l_valid=l_valid)
    return pl.pallas_call(
        body,
        grid=(t // tm,),
        in_specs=[pl.BlockSpec((tm, LANES), lambda i: (i, col_block)),
                  pl.BlockSpec((1, LANES), lambda i: (0, 0)),
                  pl.BlockSpec((1, LANES), lambda i: (0, 0))],
        out_specs=pl.BlockSpec((2 * H_A, tm), lambda i: (0, i)),
        out_shape=jax.ShapeDtypeStruct((2 * H_A, t), F32),
        compiler_params=_cparams(("parallel",)),
        name="gates",
    )(src, al_lane, dt_lane)


def _gdn_body(q_ref, k_ref, v_ref, z_ref, gc_ref, bt_ref, cwq_ref, cwk_ref, cwv_ref,
              csq_ref, csk_ref, csv_ref, s0_ref, gn_ref, o_ref, so_ref,
              xq_ref, xk_ref, xv_ref, s_ref, *, tb, tr, nt, hb):
    t = pl.program_id(2)
    hist = CONV_W - 1
    base = 8

    @pl.when(t == 0)
    def _():
        s_ref[...] = s0_ref[0]
        xq_ref[base - hist:base, :] = csq_ref[0]
        xk_ref[base - hist:base, :] = csk_ref[0]
        xv_ref[base - hist:base, :] = csv_ref[0]

    for xs_ref, raw_ref in ((xq_ref, q_ref), (xk_ref, k_ref), (xv_ref, v_ref)):
        xs_ref[base:base + tr, :] = raw_ref[...]
        if tr < tb:
            xs_ref[base + tr:base + tb, :] = jnp.zeros((tb - tr, xs_ref.shape[1]), F32)

    def conv(xs_ref, cw_ref, ln):
        slab = xs_ref[:, ln]
        y = pltpu.roll(slab, hist, 0)[base:base + tb] * cw_ref[0:1, ln]
        for j in range(1, CONV_W):
            tap = slab if j == hist else pltpu.roll(slab, hist - j, 0)
            y = y + tap[base:base + tb] * cw_ref[j:j + 1, ln]
        return _silu(y)

    cc = GDN_CHUNK
    ii = lax.broadcasted_iota(jnp.int32, (cc, cc), 0)
    jj = lax.broadcasted_iota(jnp.int32, (cc, cc), 1)
    strict = ii > jj
    incl = ii >= jj
    eye = jnp.where(ii == jj, 1.0, 0.0).astype(F32)
    heads = range(hb)
    lanes = [slice(hd * LANES, (hd + 1) * LANES) for hd in heads]
    units = [(hd, c) for hd in heads for c in range(tb // cc)]
    rows = {u: slice(u[1] * cc, (u[1] + 1) * cc) for u in units}
    q = [conv(xq_ref, cwq_ref, ln) for ln in lanes]
    k = [conv(xk_ref, cwk_ref, ln) for ln in lanes]
    v = [conv(xv_ref, cwv_ref, ln) for ln in lanes]
    q = [x * (lax.rsqrt(jnp.sum(x * x, -1, keepdims=True) + EPS) * (DK_A ** -0.5)) for x in q]
    k = [x * lax.rsqrt(jnp.sum(x * x, -1, keepdims=True) + EPS) for x in k]
    qc = {u: q[u[0]][rows[u]] for u in units}
    kc = {u: k[u[0]][rows[u]] for u in units}
    vc = {u: v[u[0]][rows[u]] for u in units}
    m_row = {u: jnp.broadcast_to(gc_ref[u[0], :, rows[u]], (cc, cc)) for u in units}
    m_col = {u: m_row[u].T for u in units}
    b_col = {u: jnp.broadcast_to(bt_ref[u[0], :, rows[u]], (cc, cc)).T for u in units}
    kq = {u: _mm_nt(jnp.concatenate([kc[u], qc[u]], axis=0), kc[u]) for u in units}
    diff = {u: m_col[u] - m_row[u] for u in units}
    a_mat = {u: b_col[u] * kq[u][0:cc] * jnp.exp2(jnp.where(strict, diff[u], -jnp.inf)) for u in units}
    qk = {u: kq[u][cc:2 * cc] * jnp.exp2(jnp.where(incl, diff[u], -jnp.inf)) for u in units}
    e_g = {u: jnp.exp2(m_col[u]) for u in units}
    x_inv = {u: eye - jnp.where((ii >> 1) == (jj >> 1), a_mat[u], 0.0) for u in units}
    sft = 1
    while (1 << sft) < cc:
        off = ((ii >> (sft + 1)) == (jj >> (sft + 1))) & ((ii >> sft) != (jj >> sft))
        lx = {u: _mm(jnp.where(off, a_mat[u], 0.0), x_inv[u]) for u in units}
        x_inv = {u: x_inv[u] - _mm(x_inv[u], lx[u]) for u in units}
        sft += 1
    uw = {u: _mm(x_inv[u], jnp.concatenate([vc[u] * b_col[u], kc[u] * (b_col[u] * e_g[u])], axis=1))
          for u in units}
    wq = {u: jnp.concatenate([uw[u][:, DV_A:], qc[u] * e_g[u]], axis=0) for u in units}
    g_last = {u: m_col[u][cc - 1:cc, :] for u in units}
    kd = {u: kc[u] * jnp.exp2(g_last[u] - m_col[u]) for u in units}
    s_state = [s_ref[hd] for hd in heads]
    for c in range(tb // cc):
        ws = [_mm(wq[(hd, c)], s_state[hd]) for hd in heads]
        v_new = [uw[(hd, c)][:, 0:DV_A] - ws[hd][0:cc] for hd in heads]
        s_state = [s_state[hd] * jnp.exp2(g_last[(hd, c)]) + _mm_tn(kd[(hd, c)], v_new[hd]) for hd in heads]
        o = [ws[hd][cc:2 * cc] + _mm(qk[(hd, c)], v_new[hd]) for hd in heads]
        o = [x * lax.rsqrt(jnp.mean(x * x, -1, keepdims=True) + EPS) * gn_ref[...] for x in o]
        nr = min(cc, tr - c * cc)
        r = slice(c * cc, c * cc + nr)
        for hd in heads:
            o_ref[r, lanes[hd]] = (o[hd][0:nr] * _silu(z_ref[r, lanes[hd]])).astype(o_ref.dtype)
    for hd in heads:
        s_ref[hd] = s_state[hd]

    xq_ref[base - hist:base, :] = xq_ref[base + tb - hist:base + tb, :]
    xk_ref[base - hist:base, :] = xk_ref[base + tb - hist:base + tb, :]
    xv_ref[base - hist:base, :] = xv_ref[base + tb - hist:base + tb, :]

    @pl.when(t == nt - 1)
    def _():
        so_ref[0] = s_ref[...]


def _gdn(h1, gates3, conv_w, conv_state, s0, gn, bsz, seq, tb, hb):
    nt = -(-seq // tb)
    tr = min(tb, seq)
    assert seq % tr == 0 and (tr == tb or (nt == 1 and tb == GDN_CHUNK))
    t_rows = bsz * seq
    body = functools.partial(_gdn_body, tb=tb, tr=tr, nt=nt, hb=hb)
    width = hb * LANES
    ng = H_A // hb

    def rows(seg):
        return pl.BlockSpec((tr, width), lambda b, g, t: (b * nt + t, seg * ng + g))

    def cw(seg):
        return pl.BlockSpec((CONV_W, width), lambda b, g, t: (0, seg * ng + g))

    def cs(seg):
        return pl.BlockSpec((1, CONV_W - 1, width), lambda b, g, t: (b, 0, seg * ng + g))

    return pl.pallas_call(
        body,
        grid=(bsz, ng, nt),
        in_specs=[rows(0), rows(1), rows(2), rows(COL_Z // QK_A),
                  pl.BlockSpec((hb, 1, tb), lambda b, g, t: (g, 0, b * nt + t)),
                  pl.BlockSpec((hb, 1, tb), lambda b, g, t: (ng + g, 0, b * nt + t)),
                  cw(0), cw(1), cw(2), cs(0), cs(1), cs(2),
                  pl.BlockSpec((1, hb, DK_A, DV_A), lambda b, g, t: (b, g, 0, 0)),
                  pl.BlockSpec((1, DV_A), lambda b, g, t: (0, 0))],
        out_specs=[pl.BlockSpec((tr, width), lambda b, g, t: (b * nt + t, g)),
                   pl.BlockSpec((1, hb, DK_A, DV_A), lambda b, g, t: (b, g, 0, 0))],
        out_shape=[jax.ShapeDtypeStruct((t_rows, V_A), BF16),
                   jax.ShapeDtypeStruct((bsz, H_A, DK_A, DV_A), F32)],
        scratch_shapes=[pltpu.VMEM((tb + 8, width), F32)] * 3 + [pltpu.VMEM((hb, DK_A, DV_A), F32)],
        compiler_params=_cparams(("parallel", "parallel", "arbitrary")),
        name="gdn",
    )(h1, h1, h1, h1, gates3, gates3, conv_w, conv_w, conv_w,
      conv_state, conv_state, conv_state, s0, gn)


def _mla_pre_body(cq_ref, ckv_ref, kra_ref, krb_ref, cos_ref, sin_ref, qg_ref, wuq_ref, wuk_ref,
                  kvg_ref, q_ref, kc_ref, ckvo_ref, kro_ref, maybe_vt_ref=None):
    cq = cq_ref[...]
    cqn = cq * lax.rsqrt(jnp.mean(cq * cq, -1, keepdims=True) + EPS) * qg_ref[...]
    qf = jnp.dot(cqn.astype(BF16), wuq_ref[...], preferred_element_type=F32)
    cos_k = cos_ref[...]
    sin_k = sin_ref[...]
    reps = H_B * ROPE // LANES
    cos_t = jnp.concatenate([cos_k] * reps, axis=1)
    sin_t = jnp.concatenate([sin_k] * reps, axis=1)
    n_nope = H_B * NOPE
    n_rope = H_B * ROPE
    qr = (qf[:, n_nope:n_nope + n_rope] * cos_t + qf[:, n_nope + n_rope:] * sin_t) * Q_SCALE
    lane = lax.broadcasted_iota(jnp.int32, (cq.shape[0], LANES), 1)
    for h in range(H_B):
        ql = jnp.dot(qf[:, h * NOPE:(h + 1) * NOPE].astype(BF16), wuk_ref[h],
                     preferred_element_type=F32) * Q_SCALE
        blk = qr[:, (h // 2) * LANES:(h // 2 + 1) * LANES]
        keep = (lane < ROPE) if h % 2 == 0 else (lane >= ROPE)
        q_ref[h, :, 0:KV_RANK] = ql.astype(BF16)
        q_ref[h, :, KV_RANK:KC_W] = jnp.where(keep, blk, 0.0).astype(BF16)
    ckv = ckv_ref[...]
    ckvn = ckv * lax.rsqrt(jnp.mean(ckv * ckv, -1, keepdims=True) + EPS) * kvg_ref[...]
    kr2 = kra_ref[...] * cos_k + krb_ref[...] * sin_k
    ckvo_ref[...] = ckvn
    kro_ref[...] = kr2[:, 0:ROPE]
    kc_ref[:, 0:KV_RANK] = ckvn.astype(BF16)
    kc_ref[:, KV_RANK:KC_W] = kr2.astype(BF16)
    if maybe_vt_ref is not None:
        maybe_vt_ref[0] = ckvn.T.astype(BF16)


def _mla_pre(h1, cos_t, sin_t, qg, wuq, wuk, kvg, bsz, seq, tm, with_vt):
    t_rows = bsz * seq
    ntab = cos_t.shape[0] // tm
    npb = max(seq // tm, 1)
    const2 = lambda i: (0, 0)
    out_specs = [pl.BlockSpec((H_B, tm, KC_W), lambda i: (0, i, 0)),
                 pl.BlockSpec((tm, KC_W), lambda i: (i, 0)),
                 pl.BlockSpec((tm, KV_RANK), lambda i: (i, 0)),
                 pl.BlockSpec((tm, ROPE), lambda i: (i, 0))]
    out_shape = [jax.ShapeDtypeStruct((H_B, t_rows, KC_W), BF16),
                 jax.ShapeDtypeStruct((t_rows, KC_W), BF16),
                 jax.ShapeDtypeStruct((t_rows, KV_RANK), F32),
                 jax.ShapeDtypeStruct((t_rows, ROPE), F32)]
    if with_vt:
        out_specs.append(pl.BlockSpec((1, KV_RANK, tm), lambda i: (i // npb, 0, i % npb)))
        out_shape.append(jax.ShapeDtypeStruct((bsz, KV_RANK, seq), BF16))
    return pl.pallas_call(
        _mla_pre_body,
        grid=(t_rows // tm,),
        in_specs=[pl.BlockSpec((tm, Q_RANK), lambda i: (i, COL_CQ // Q_RANK)),
                  pl.BlockSpec((tm, KV_RANK), lambda i: (i, COL_CKV // KV_RANK)),
                  pl.BlockSpec((tm, LANES), lambda i: (i, COL_KRA // LANES)),
                  pl.BlockSpec((tm, LANES), lambda i: (i, COL_KRB // LANES)),
                  pl.BlockSpec((tm, LANES), lambda i: (i % ntab, 0)),
                  pl.BlockSpec((tm, LANES), lambda i: (i % ntab, 0)),
                  pl.BlockSpec((1, Q_RANK), const2),
                  pl.BlockSpec(wuq.shape, const2),
                  pl.BlockSpec(wuk.shape, lambda i: (0, 0, 0)),
                  pl.BlockSpec((1, KV_RANK), const2)],
        out_specs=out_specs,
        out_shape=out_shape,
        compiler_params=_cparams(("parallel",)),
        name="mla_pre",
    )(h1, h1, h1, h1, cos_t, sin_t, qg, wuq, wuk, kvg)


def _attn_body(q_ref, k_ref, vt_ref, wuv_ref, o_ref, m_ref, l_ref, acc_ref, *, tq, tk, lk):
    i = pl.program_id(1)
    shift = CHUNK.bit_length() - 1
    m_ref[...] = jnp.full(m_ref.shape, -jnp.inf, F32)
    l_ref[...] = jnp.zeros(l_ref.shape, F32)
    acc_ref[...] = jnp.zeros(acc_ref.shape, F32)

    def update(j, masked):
        k0 = pl.multiple_of(j * tk, tk)
        kt = k_ref[0, pl.ds(k0, tk), :]
        vt = vt_ref[0, :, pl.ds(k0, tk)]
        bias = None
        if masked:
            kpos = k0 + lax.broadcasted_iota(jnp.int32, (tk, tq), 0)
            qpos = i * tq + lax.broadcasted_iota(jnp.int32, (tk, tq), 1)
            bias = jnp.where((kpos >> shift) <= (qpos >> shift), 0.0, -jnp.inf).astype(F32)

        def scores(h):
            s = lax.dot_general(kt, q_ref[h], (((1,), (1,)), ((), ())),
                                preferred_element_type=F32)
            return s if bias is None else s + bias

        s_next = scores(0)
        for h in range(H_B):
            s = s_next
            if h + 1 < H_B:
                s_next = scores(h + 1)
            m_prev = m_ref[h]
            m_new = jnp.maximum(m_prev, jnp.max(s, 0, keepdims=True))
            p = jnp.exp2(s - m_new)
            alpha = jnp.exp2(m_prev - m_new)
            l_ref[h] = alpha * l_ref[h] + jnp.sum(p, 0, keepdims=True)
            pv = jnp.dot(vt, p.astype(BF16), preferred_element_type=F32)
            acc_ref[h] = alpha * acc_ref[h] + pv
            m_ref[h] = m_new

    n_full = ((((i * tq) >> shift) + 1) << shift) // tk
    n_all = jnp.minimum(((((i * tq + tq - 1) >> shift) + 1) << shift) + tk - 1, lk + tk - 1) // tk

    def full_step(j, carry):
        update(j, False)
        return carry

    def masked_step(j, carry):
        update(j, True)
        return carry

    lax.fori_loop(0, n_full, full_step, 0)
    lax.fori_loop(n_full, n_all, masked_step, 0)

    for h in range(H_B):
        o_t = (acc_ref[h] * (1.0 / l_ref[h])).astype(BF16)
        ob_t = jnp.dot(wuv_ref[h], o_t, preferred_element_type=F32)
        o_ref[0, :, h * V_B:(h + 1) * V_B] = ob_t.T.astype(o_ref.dtype)


def _attn(q, kc, vt, wuv_t, bsz, seq, tq, tk):
    assert seq % tq == 0 and seq % tk == 0
    nq = seq // tq
    body = functools.partial(_attn_body, tq=tq, tk=tk, lk=seq)
    return pl.pallas_call(
        body,
        grid=(bsz, nq),
        in_specs=[pl.BlockSpec((H_B, tq, KC_W), lambda b, i: (0, b * nq + i, 0)),
                  pl.BlockSpec((1, seq, KC_W), lambda b, i: (b, 0, 0)),
                  pl.BlockSpec((1, KV_RANK, seq), lambda b, i: (b, 0, 0)),
                  pl.BlockSpec(wuv_t.shape, lambda b, i: (0, 0, 0))],
        out_specs=pl.BlockSpec((1, tq, H_B * V_B), lambda b, i: (b, i, 0)),
        out_shape=jax.ShapeDtypeStruct((bsz, seq, H_B * V_B), BF16),
        scratch_shapes=[pltpu.VMEM((H_B, 1, tq), F32), pltpu.VMEM((H_B, 1, tq), F32),
                        pltpu.VMEM((H_B, KV_RANK, tq), F32)],
        compiler_params=_cparams(("parallel", "arbitrary")),
        name="attn",
    )(q, kc, vt, wuv_t)


def _attn_dec_body(q_ref, ckv_ref, kr_ref, kn_ref, wuv_ref, o_ref, m_ref, l_ref, acc_ref,
                   *, tq, past, n_past):
    j = pl.program_id(1)
    cols = H_B * tq
    nt_dims = (((1,), (1,)), ((), ()))
    q2 = q_ref[...].reshape(cols, KC_W)

    @pl.when(j == 0)
    def _():
        m_ref[...] = jnp.full(m_ref.shape, -jnp.inf, F32)
        l_ref[...] = jnp.zeros(l_ref.shape, F32)
        acc_ref[...] = jnp.zeros(acc_ref.shape, F32)

    def accumulate(s, v_nat):
        m_prev = m_ref[...]
        m_new = jnp.maximum(m_prev, jnp.max(s, 0, keepdims=True))
        p = jnp.exp2(s - m_new)
        alpha = jnp.exp2(m_prev - m_new)
        l_ref[...] = alpha * l_ref[...] + jnp.sum(p, 0, keepdims=True)
        acc_ref[...] = alpha * acc_ref[...] + _mm_tn(v_nat, p)
        m_ref[...] = m_new

    @pl.when(j < n_past)
    def _():
        k_lat = ckv_ref[0].astype(BF16)
        kr = kr_ref[0].astype(BF16)
        kr2 = jnp.concatenate([kr, kr], axis=1)
        s = (lax.dot_general(k_lat, q2[:, 0:KV_RANK], nt_dims, preferred_element_type=F32)
             + lax.dot_general(kr2, q2[:, KV_RANK:KC_W], nt_dims, preferred_element_type=F32))
        accumulate(s, k_lat)

    @pl.when(j == n_past)
    def _():
        kn = kn_ref[...]
        s = lax.dot_general(kn, q2, nt_dims, preferred_element_type=F32)
        shift = CHUNK.bit_length() - 1
        kpos = past + lax.broadcasted_iota(jnp.int32, s.shape, 0)
        qpos = past + (lax.broadcasted_iota(jnp.int32, s.shape, 1) & (tq - 1))
        s = jnp.where((kpos >> shift) <= (qpos >> shift), s, -jnp.inf)
        accumulate(s, kn[:, 0:KV_RANK])
        o = (acc_ref[...] * (1.0 / l_ref[...])).T
        for h in range(H_B):
            oh = o[h * tq:(h + 1) * tq, :].astype(BF16)
            o_ref[:, h * V_B:(h + 1) * V_B] = jnp.dot(
                oh, wuv_ref[h], preferred_element_type=F32).astype(o_ref.dtype)


def _attn_dec(q, cache_ckv, cache_kr, layer, kc_new, wuv, bsz, seq, tk):
    past = cache_ckv.shape[2]
    assert past % tk == 0 and past % CHUNK == 0 and seq & (seq - 1) == 0
    n_past = past // tk
    body = functools.partial(_attn_dec_body, tq=seq, past=past, n_past=n_past)
    cache_idx = lambda b, j: (layer, b, jnp.minimum(j, n_past - 1), 0)
    return pl.pallas_call(
        body,
        grid=(bsz, n_past + 1),
        in_specs=[pl.BlockSpec((H_B, seq, KC_W), lambda b, j: (0, b, 0)),
                  pl.BlockSpec((None, 1, tk, KV_RANK), cache_idx),
                  pl.BlockSpec((None, 1, tk, ROPE), cache_idx),
                  pl.BlockSpec((seq, KC_W), lambda b, j: (b, 0)),
                  pl.BlockSpec(wuv.shape, lambda b, j: (0, 0, 0))],
        out_specs=pl.BlockSpec((seq, H_B * V_B), lambda b, j: (b, 0)),
        out_shape=jax.ShapeDtypeStruct((bsz * seq, H_B * V_B), BF16),
        scratch_shapes=[pltpu.VMEM((1, H_B * seq), F32), pltpu.VMEM((1, H_B * seq), F32),
                        pltpu.VMEM((KV_RANK, H_B * seq), F32)],
        compiler_params=_cparams(("parallel", "arbitrary")),
        name="attn_dec",
    )(q, cache_ckv, cache_kr, kc_new, wuv)


def _layer_norm(r, g, b):
    mu = jnp.mean(r, -1, keepdims=True)
    d = r - mu
    var = jnp.mean(d * d, -1, keepdims=True)
    return d * lax.rsqrt(var + EPS) * g + b


def _merge_body(oa_ref, ob_ref, ga_ref, gb_ref, x_ref, woa_ref, wob_ref, wout_ref, g_ref, b_ref, o_ref):
    ya = jnp.dot(oa_ref[...], woa_ref[...], preferred_element_type=F32)
    yb = jnp.dot(ob_ref[...], wob_ref[...], preferred_element_type=F32)
    m = _sigmoid(ga_ref[...]) * ya + _sigmoid(gb_ref[...]) * yb
    r = ALPHA * x_ref[...] + jnp.dot(m.astype(BF16), wout_ref[...], preferred_element_type=F32)
    o_ref[...] = _layer_norm(r, g_ref[...], b_ref[...])


def _merge(oa, ob, h1, x, woa, wob, wout, g, b, tm):
    t = x.shape[0]
    row = lambda i: (i, 0)
    const = lambda i: (0, 0)
    wspec = pl.BlockSpec((D_MODEL, D_MODEL), const)
    return pl.pallas_call(
        _merge_body,
        grid=(t // tm,),
        in_specs=[pl.BlockSpec((tm, V_A), row), pl.BlockSpec((tm, H_B * V_B), row),
                  pl.BlockSpec((tm, D_MODEL), lambda i: (i, COL_GA // D_MODEL)),
                  pl.BlockSpec((tm, D_MODEL), lambda i: (i, COL_GB // D_MODEL)),
                  pl.BlockSpec((tm, D_MODEL), row), wspec, wspec, wspec,
                  pl.BlockSpec((1, D_MODEL), const), pl.BlockSpec((1, D_MODEL), const)],
        out_specs=pl.BlockSpec((tm, D_MODEL), row),
        out_shape=jax.ShapeDtypeStruct((t, D_MODEL), F32),
        compiler_params=_cparams(("parallel",)),
        name="merge",
    )(oa, ob, h1, h1, x, woa, wob, wout, g, b)


def _ffn_body(x_ref, wg_ref, wu_ref, wd_ref, g_ref, b_ref, o_ref, h_ref, *, tf):
    x = x_ref[...]
    xb = x.astype(BF16)
    for j in range(D_FF // tf):
        sl = slice(j * tf, (j + 1) * tf)
        f1 = jnp.dot(xb, wg_ref[:, sl], preferred_element_type=F32)
        f3 = jnp.dot(xb, wu_ref[:, sl], preferred_element_type=F32)
        h_ref[:, sl] = (_silu(f1) * f3).astype(BF16)
    y = jnp.dot(h_ref[...], wd_ref[...], preferred_element_type=F32)
    o_ref[...] = _layer_norm(ALPHA * x + y, g_ref[...], b_ref[...])


def _ffn(x, wg, wu, wd, g, b, tm, tf):
    t = x.shape[0]
    row = lambda i: (i, 0)
    const = lambda i: (0, 0)
    single = pl.Buffered(1)
    return pl.pallas_call(
        functools.partial(_ffn_body, tf=tf),
        grid=(t // tm,),
        in_specs=[pl.BlockSpec((tm, D_MODEL), row),
                  pl.BlockSpec((D_MODEL, D_FF), const, pipeline_mode=single),
                  pl.BlockSpec((D_MODEL, D_FF), const, pipeline_mode=single),
                  pl.BlockSpec((D_FF, D_MODEL), const, pipeline_mode=single),
                  pl.BlockSpec((1, D_MODEL), const), pl.BlockSpec((1, D_MODEL), const)],
        out_specs=pl.BlockSpec((tm, D_MODEL), row),
        out_shape=jax.ShapeDtypeStruct((t, D_MODEL), F32),
        scratch_shapes=[pltpu.VMEM((tm, D_FF), BF16)],
        compiler_params=_cparams(("parallel",)),
        name="ffn",
    )(x, wg, wu, wd, g, b)


def _prep_layer_weights(w_in, conv_w, a_log, dt_bias, gdn_norm_g, w_oa, q_norm_g, w_uq, kv_norm_g,
                        w_ukv, w_ob, w_out, ln1_g, ln1_b, w_gu, w_down, ln2_g, ln2_b):
    seg = lambda i: w_in[:, _OFFS[i]:_OFFS[i + 1]]
    qkv, z, a, b, c_q, c_kv, k_r, g_a, g_b = (seg(i) for i in range(9))
    half = ROPE // 2
    k_r_rot = jnp.concatenate([-k_r[:, half:], k_r[:, :half]], axis=1)
    pad = jnp.zeros((D_MODEL, LANES - 2 * H_A), w_in.dtype)
    w_proj = jnp.concatenate([qkv, z, g_a, g_b, c_kv, k_r, k_r, c_q, k_r_rot, k_r_rot, a, b, pad],
                             axis=1).astype(BF16)
    lane_pad = jnp.zeros((LANES - H_A,), F32)
    al_lane = jnp.concatenate([a_log.astype(F32), lane_pad]).reshape(1, LANES)
    dt_lane = jnp.concatenate([dt_bias.astype(F32), lane_pad]).reshape(1, LANES)
    uq = w_uq.reshape(Q_RANK, H_B, NOPE + ROPE)
    uq_nope = uq[:, :, :NOPE].reshape(Q_RANK, H_B * NOPE)
    uq_rope = uq[:, :, NOPE:]
    uq_rot = jnp.concatenate([-uq_rope[:, :, half:], uq_rope[:, :, :half]], axis=2)
    w_uq_ext = jnp.concatenate([uq_nope, uq_rope.reshape(Q_RANK, H_B * ROPE),
                                uq_rot.reshape(Q_RANK, H_B * ROPE)], axis=1).astype(BF16)
    ukv = w_ukv.reshape(KV_RANK, H_B, NOPE + V_B)
    w_uk_t = jnp.transpose(ukv[:, :, :NOPE], (1, 2, 0)).astype(BF16)
    w_uv = jnp.transpose(ukv[:, :, NOPE:], (1, 0, 2)).astype(BF16)
    w_uv_t = jnp.transpose(ukv[:, :, NOPE:], (1, 2, 0)).astype(BF16)
    return dict(
        w_proj=w_proj, conv_w=conv_w.astype(F32), al_lane=al_lane, dt_lane=dt_lane,
        gn=gdn_norm_g.reshape(1, DV_A).astype(F32), w_oa=w_oa.astype(BF16),
        qg=q_norm_g.reshape(1, Q_RANK).astype(F32), w_uq=w_uq_ext, w_uk_t=w_uk_t, w_uv=w_uv, w_uv_t=w_uv_t,
        kvg=kv_norm_g.reshape(1, KV_RANK).astype(F32), w_ob=w_ob.astype(BF16),
        w_out=w_out.astype(BF16), ln1_g=ln1_g.reshape(1, D_MODEL), ln1_b=ln1_b.reshape(1, D_MODEL),
        w_g=w_gu[:, :D_FF].astype(BF16), w_u=w_gu[:, D_FF:].astype(BF16), w_down=w_down.astype(BF16),
        ln2_g=ln2_g.reshape(1, D_MODEL), ln2_b=ln2_b.reshape(1, D_MODEL))


def _rope_tables(past, seq, reps):
    half = ROPE // 2
    inv = ROPE_THETA ** (-jnp.arange(half, dtype=F32) / half)
    ang = (past + jnp.arange(seq)).astype(F32)[:, None] * inv[None, :]
    cos = jnp.tile(jnp.cos(ang), (reps, LANES // half))
    sin = jnp.tile(jnp.sin(ang), (reps, LANES // half))
    return cos, sin


TM_PROJ, TN_PROJ, TM_GATES, TM_MLA, TM_MERGE, TM_FFN = 1024, 1024, 1024, 512, 512, 512
TF_FFN = D_FF // 2
TQ_ATTN, TK_ATTN, TK_DEC = 512, 512, 1024
TB_GDN, HB_GDN = 2 * GDN_CHUNK, H_A


def _trunk_layer(x, conv_state, s0, caches, wl, bsz, seq):
    decode = caches is not None
    t_rows = bsz * seq
    h1 = _proj_in(x, wl["w_proj"], min(TM_PROJ, t_rows), TN_PROJ)
    seq_pad = -(-seq // GDN_CHUNK) * GDN_CHUNK
    if seq_pad == seq:
        gates = _gates(h1, COL_AB // LANES, wl["al_lane"], wl["dt_lane"], min(TM_GATES, t_rows), seq, seq)
    else:
        ab = h1[:, COL_AB:COL_AB + LANES].reshape(bsz, seq, LANES)
        ab = jnp.pad(ab, ((0, 0), (0, seq_pad - seq), (0, 0))).reshape(bsz * seq_pad, LANES)
        gates = _gates(ab, 0, wl["al_lane"], wl["dt_lane"], min(TM_GATES, bsz * seq_pad), seq_pad, seq)
    o_a, s_new = _gdn(h1, gates.reshape(2 * H_A, 1, bsz * seq_pad), wl["conv_w"], conv_state, s0, wl["gn"],
                      bsz, seq, min(TB_GDN, seq_pad), HB_GDN)
    if decode:
        cache_ckv, cache_kr, layer = caches
        tm_mla = t_rows if t_rows <= TM_MLA else seq
        cos_t, sin_t = _rope_tables(cache_ckv.shape[2], seq, tm_mla // seq)
        q, kc, ckv_new, kr_new = _mla_pre(h1, cos_t, sin_t, wl["qg"], wl["w_uq"], wl["w_uk_t"], wl["kvg"],
                                          bsz, seq, tm_mla, False)
        o_b = _attn_dec(q, cache_ckv, cache_kr, layer, kc, wl["w_uv"], bsz, seq, TK_DEC)
    else:
        cos_t, sin_t = _rope_tables(0, seq, 1)
        q, kc, ckv_new, kr_new, vt = _mla_pre(h1, cos_t, sin_t, wl["qg"], wl["w_uq"], wl["w_uk_t"],
                                              wl["kvg"], bsz, seq, TM_MLA, True)
        o_b = _attn(q, kc.reshape(bsz, seq, KC_W), vt, wl["w_uv_t"], bsz, seq, TQ_ATTN, TK_ATTN)
        o_b = o_b.reshape(t_rows, H_B * V_B)
    x1 = _merge(o_a, o_b, h1, x, wl["w_oa"], wl["w_ob"], wl["w_out"], wl["ln1_g"], wl["ln1_b"],
                min(TM_MERGE, t_rows))
    x2 = _ffn(x1, wl["w_g"], wl["w_u"], wl["w_down"], wl["ln2_g"], wl["ln2_b"], min(TM_FFN, t_rows), TF_FFN)
    conv_new = h1.reshape(bsz, seq, N_PROJ)[:, seq - (CONV_W - 1):, COL_QKV:COL_QKV + C_QKV]
    return (x2, conv_new, s_new, ckv_new.reshape(bsz, seq, KV_RANK), kr_new.reshape(bsz, seq, ROPE))


def kernel(x_prompt, x_sample, state_conv, state_gdn, cache_ckv, cache_krope, w_in, conv_w, a_log, dt_bias, gdn_norm_g, w_oa, q_norm_g, w_uq, kv_norm_g, w_ukv, w_ob, w_out, ln1_g, ln1_b, w_gu, w_down, ln2_g, ln2_b):
    bp, lp, _ = x_prompt.shape
    bs, ls, _ = x_sample.shape
    yp = x_prompt.reshape(bp * lp, D_MODEL)
    ys = x_sample.reshape(bs * ls, D_MODEL)
    zero_conv = jnp.zeros((bp, CONV_W - 1, C_QKV), F32)
    zero_s = jnp.zeros((bp, H_A, DK_A, DV_A), F32)
    outs_p, outs_s = [], []
    for l in range(w_in.shape[0]):
        wl = _prep_layer_weights(w_in[l], conv_w[l], a_log[l], dt_bias[l], gdn_norm_g[l], w_oa[l],
                                 q_norm_g[l], w_uq[l], kv_norm_g[l], w_ukv[l], w_ob[l], w_out[l],
                                 ln1_g[l], ln1_b[l], w_gu[l], w_down[l], ln2_g[l], ln2_b[l])
        yp, *rest_p = _trunk_layer(yp, zero_conv, zero_s, None, wl, bp, lp)
        ys, *rest_s = _trunk_layer(ys, state_conv[l], state_gdn[l], (cache_ckv, cache_krope, l), wl, bs, ls)
        outs_p.append(rest_p)
        outs_s.append(rest_s)
    stack = lambda outs, i: jnp.stack([o[i] for o in outs])
    return (yp.reshape(bp, lp, D_MODEL), ys.reshape(bs, ls, D_MODEL),
            stack(outs_p, 0), stack(outs_p, 1), stack(outs_p, 2), stack(outs_p, 3),
            stack(outs_s, 0), stack(outs_s, 1), stack(outs_s, 2), stack(outs_s, 3))
```

```python
import functools

import numpy as np
import jax
import jax.numpy as jnp
from jax import lax
from jax.experimental import pallas as pl
from jax.experimental.pallas import tpu as pltpu

F32 = jnp.float32
BF16 = jnp.bfloat16

D_MODEL = 1024
DEPTH = 2
CHUNK = 64
H_A = 8
DK_A = 128
DV_A = 128
QK_A = H_A * DK_A
V_A = H_A * DV_A
C_QKV = 2 * QK_A + V_A
CONV_W = 4
H_B = 8
NOPE = 128
ROPE = 64
V_B = 128
Q_RANK = 384
KV_RANK = 256
ROPE_THETA = 10000.0
ATTN_SCALE = (NOPE + ROPE) ** -0.5
LOG2_E = float(np.log2(np.e))
Q_SCALE = ATTN_SCALE * LOG2_E
D_FF = -(-8 * D_MODEL // (3 * 256)) * 256
ALPHA = (2 * DEPTH) ** 0.25
EPS = 1e-6
_SIZES = (C_QKV, V_A, H_A, H_A, Q_RANK, KV_RANK, ROPE, D_MODEL, D_MODEL)
_OFFS = tuple(int(v) for v in np.cumsum((0,) + _SIZES))

LANES = 128
VMEM_LIMIT = 56 * 1024 * 1024

COL_QKV = 0
COL_Z = COL_QKV + C_QKV
COL_GA = COL_Z + V_A
COL_GB = COL_GA + D_MODEL
COL_CKV = COL_GB + D_MODEL
COL_KRA = COL_CKV + KV_RANK
COL_CQ = COL_KRA + LANES
COL_KRB = COL_CQ + Q_RANK
COL_AB = COL_KRB + LANES
N_PROJ = COL_AB + LANES
KC_W = KV_RANK + LANES

GDN_CHUNK = 128
MAX_STALE_EXCESS = 64.0


def _cparams(sem):
    return pltpu.CompilerParams(dimension_semantics=sem, vmem_limit_bytes=VMEM_LIMIT)


def _sigmoid(x):
    return jax.nn.sigmoid(x)


def _silu(x):
    return x * jax.nn.sigmoid(x)


def _mm(a, b):
    return jnp.dot(a.astype(BF16), b.astype(BF16), preferred_element_type=F32)


def _mm_nt(a, b):
    return lax.dot_general(a.astype(BF16), b.astype(BF16), (((1,), (1,)), ((), ())),
                           preferred_element_type=F32)


def _mm_tn(a, b):
    return lax.dot_general(a.astype(BF16), b.astype(BF16), (((0,), (0,)), ((), ())),
                           preferred_element_type=F32)


def _proj_body(x_ref, w_ref, o_ref, xb_ref):
    @pl.when(pl.program_id(1) == 0)
    def _():
        xb_ref[...] = x_ref[...].astype(BF16)

    o_ref[...] = jnp.dot(xb_ref[...], w_ref[...], preferred_element_type=F32)


def _proj_in(x, w, tm, tn):
    t, k = x.shape
    n = w.shape[1]
    return pl.pallas_call(
        _proj_body,
        grid=(t // tm, n // tn),
        in_specs=[pl.BlockSpec((tm, k), lambda i, j: (i, 0)),
                  pl.BlockSpec((k, tn), lambda i, j: (0, j))],
        out_specs=pl.BlockSpec((tm, tn), lambda i, j: (i, j)),
        out_shape=jax.ShapeDtypeStruct((t, n), F32),
        scratch_shapes=[pltpu.VMEM((tm, k), BF16)],
        compiler_params=_cparams(("parallel", "arbitrary")),
        name="proj_in",
    )(x, w)


def _gates_body(ab_ref, al_ref, dt_ref, o_ref, *, tm, l_pad, l_valid):
    x = ab_ref[...]
    lane = lax.broadcasted_iota(jnp.int32, x.shape, 1)
    xa = x + dt_ref[...]
    sp = jnp.maximum(xa, 0.0) + jnp.log1p(jnp.exp(-jnp.abs(xa)))
    g = -jnp.exp(al_ref[...]) * sp * LOG2_E
    y = jnp.where(lane < H_A, g, _sigmoid(x))
    yt = y.T[0:2 * H_A, :]
    if l_valid < l_pad:
        col = lax.broadcasted_iota(jnp.int32, yt.shape, 1) + pl.program_id(0) * tm
        yt = jnp.where(col % l_pad < l_valid, yt, 0.0)
    r = lax.broadcasted_iota(jnp.int32, (GDN_CHUNK, GDN_CHUNK), 0)
    c = lax.broadcasted_iota(jnp.int32, (GDN_CHUNK, GDN_CHUNK), 1)
    tri = jnp.where(r <= c, 1.0, 0.0).astype(F32)
    for s in range(tm // GDN_CHUNK):
        sl = slice(s * GDN_CHUNK, (s + 1) * GDN_CHUNK)
        o_ref[0:H_A, sl] = jnp.dot(yt[0:H_A, sl], tri, precision=lax.Precision.HIGHEST,
                                   preferred_element_type=F32)
    o_ref[H_A:2 * H_A, :] = yt[H_A:2 * H_A, :]


def _gates(src, col_block, al_lane, dt_lane, tm, l_pad, l_valid):
    t = src.shape[0]
    body = functools.partial(_gates_body, tm=tm, l_pad=l_pad, l_valid=l_valid)
    return pl.pallas_call(
        body,
        grid=(t // tm,),
        in_specs=[pl.BlockSpec((tm, LANES), lambda i: (i, col_block)),
                  pl.BlockSpec((1, LANES), lambda i: (0, 0)),
                  pl.BlockSpec((1, LANES), lambda i: (0, 0))],
        out_specs=pl.BlockSpec((2 * H_A, tm), lambda i: (0, i)),
        out_shape=jax.ShapeDtypeStruct((2 * H_A, t), F32),
        compiler_params=_cparams(("parallel",)),
        name="gates",
    )(src, al_lane, dt_lane)


def _gdn_body(q_ref, k_ref, v_ref, z_ref, gc_ref, bt_ref, cwq_ref, cwk_ref, cwv_ref,
              csq_ref, csk_ref, csv_ref, s0_ref, gn_ref, o_ref, so_ref,
              xq_ref, xk_ref, xv_ref, s_ref, *, tb, tr, nt, hb):
    t = pl.program_id(2)
    hist = CONV_W - 1
    base = 8

    @pl.when(t == 0)
    def _():
        s_ref[...] = s0_ref[0]
        xq_ref[base - hist:base, :] = csq_ref[0]
        xk_ref[base - hist:base, :] = csk_ref[0]
        xv_ref[base - hist:base, :] = csv_ref[0]

    for xs_ref, raw_ref in ((xq_ref, q_ref), (xk_ref, k_ref), (xv_ref, v_ref)):
        xs_ref[base:base + tr, :] = raw_ref[...]
        if tr < tb:
            xs_ref[base + tr:base + tb, :] = jnp.zeros((tb - tr, xs_ref.shape[1]), F32)

    def conv(xs_ref, cw_ref, ln):
        slab = xs_ref[:, ln]
        y = pltpu.roll(slab, hist, 0)[base:base + tb] * cw_ref[0:1, ln]
        for j in range(1, CONV_W):
            tap = slab if j == hist else pltpu.roll(slab, hist - j, 0)
            y = y + tap[base:base + tb] * cw_ref[j:j + 1, ln]
        return _silu(y)

    cc = GDN_CHUNK
    ii = lax.broadcasted_iota(jnp.int32, (cc, cc), 0)
    jj = lax.broadcasted_iota(jnp.int32, (cc, cc), 1)
    strict = ii > jj
    incl = ii >= jj
    eye = jnp.where(ii == jj, 1.0, 0.0).astype(F32)
    heads = range(hb)
    lanes = [slice(hd * LANES, (hd + 1) * LANES) for hd in heads]
    units = [(hd, c) for hd in heads for c in range(tb // cc)]
    rows = {u: slice(u[1] * cc, (u[1] + 1) * cc) for u in units}
    q = [conv(xq_ref, cwq_ref, ln) for ln in lanes]
    k = [conv(xk_ref, cwk_ref, ln) for ln in lanes]
    v = [conv(xv_ref, cwv_ref, ln) for ln in lanes]
    q = [x * (lax.rsqrt(jnp.sum(x * x, -1, keepdims=True) + EPS) * (DK_A ** -0.5)) for x in q]
    k = [x * lax.rsqrt(jnp.sum(x * x, -1, keepdims=True) + EPS) for x in k]
    qc = {u: q[u[0]][rows[u]] for u in units}
    kc = {u: k[u[0]][rows[u]] for u in units}
    vc = {u: v[u[0]][rows[u]] for u in units}
    m_row = {u: jnp.broadcast_to(gc_ref[u[0], :, rows[u]], (cc, cc)) for u in units}
    m_col = {u: m_row[u].T for u in units}
    b_col = {u: jnp.broadcast_to(bt_ref[u[0], :, rows[u]], (cc, cc)).T for u in units}
    kq = {u: _mm_nt(jnp.concatenate([kc[u], qc[u]], axis=0), kc[u]) for u in units}
    diff = {u: m_col[u] - m_row[u] for u in units}
    a_mat = {u: b_col[u] * kq[u][0:cc] * jnp.exp2(jnp.where(strict, diff[u], -jnp.inf)) for u in units}
    qk = {u: kq[u][cc:2 * cc] * jnp.exp2(jnp.where(incl, diff[u], -jnp.inf)) for u in units}
    e_g = {u: jnp.exp2(m_col[u]) for u in units}
    x_inv = {u: eye - jnp.where((ii >> 1) == (jj >> 1), a_mat[u], 0.0) for u in units}
    sft = 1
    while (1 << sft) < cc:
        off = ((ii >> (sft + 1)) == (jj >> (sft + 1))) & ((ii >> sft) != (jj >> sft))
        lx = {u: _mm(jnp.where(off, a_mat[u], 0.0), x_inv[u]) for u in units}
        x_inv = {u: x_inv[u] - _mm(x_inv[u], lx[u]) for u in units}
        sft += 1
    uw = {u: _mm(x_inv[u], jnp.concatenate([vc[u] * b_col[u], kc[u] * (b_col[u] * e_g[u])], axis=1))
          for u in units}
    wq = {u: jnp.concatenate([uw[u][:, DV_A:], qc[u] * e_g[u]], axis=0) for u in units}
    g_last = {u: m_col[u][cc - 1:cc, :] for u in units}
    kd = {u: kc[u] * jnp.exp2(g_last[u] - m_col[u]) for u in units}
    s_state = [s_ref[hd] for hd in heads]
    for c in range(tb // cc):
        ws = [_mm(wq[(hd, c)], s_state[hd]) for hd in heads]
        v_new = [uw[(hd, c)][:, 0:DV_A] - ws[hd][0:cc] for hd in heads]
        s_state = [s_state[hd] * jnp.exp2(g_last[(hd, c)]) + _mm_tn(kd[(hd, c)], v_new[hd]) for hd in heads]
        o = [ws[hd][cc:2 * cc] + _mm(qk[(hd, c)], v_new[hd]) for hd in heads]
        o = [x * lax.rsqrt(jnp.mean(x * x, -1, keepdims=True) + EPS) * gn_ref[...] for x in o]
        nr = min(cc, tr - c * cc)
        r = slice(c * cc, c * cc + nr)
        for hd in heads:
            o_ref[r, lanes[hd]] = (o[hd][0:nr] * _silu(z_ref[r, lanes[hd]])).astype(o_ref.dtype)
    for hd in heads:
        s_ref[hd] = s_state[hd]

    xq_ref[base - hist:base, :] = xq_ref[base + tb - hist:base + tb, :]
    xk_ref[base - hist:base, :] = xk_ref[base + tb - hist:base + tb, :]
    xv_ref[base - hist:base, :] = xv_ref[base + tb - hist:base + tb, :]

    @pl.when(t == nt - 1)
    def _():
        so_ref[0] = s_ref[...]


def _gdn(h1, gates3, conv_w, conv_state, s0, gn, bsz, seq, tb, hb):
    nt = -(-seq // tb)
    tr = min(tb, seq)
    assert seq % tr == 0 and (tr == tb or (nt == 1 and tb == GDN_CHUNK))
    t_rows = bsz * seq
    body = functools.partial(_gdn_body, tb=tb, tr=tr, nt=nt, hb=hb)
    width = hb * LANES
    ng = H_A // hb

    def rows(seg):
        return pl.BlockSpec((tr, width), lambda b, g, t: (b * nt + t, seg * ng + g))

    def cw(seg):
        return pl.BlockSpec((CONV_W, width), lambda b, g, t: (0, seg * ng + g))

    def cs(seg):
        return pl.BlockSpec((1, CONV_W - 1, width), lambda b, g, t: (b, 0, seg * ng + g))

    return pl.pallas_call(
        body,
        grid=(bsz, ng, nt),
        in_specs=[rows(0), rows(1), rows(2), rows(COL_Z // QK_A),
                  pl.BlockSpec((hb, 1, tb), lambda b, g, t: (g, 0, b * nt + t)),
                  pl.BlockSpec((hb, 1, tb), lambda b, g, t: (ng + g, 0, b * nt + t)),
                  cw(0), cw(1), cw(2), cs(0), cs(1), cs(2),
                  pl.BlockSpec((1, hb, DK_A, DV_A), lambda b, g, t: (b, g, 0, 0)),
                  pl.BlockSpec((1, DV_A), lambda b, g, t: (0, 0))],
        out_specs=[pl.BlockSpec((tr, width), lambda b, g, t: (b * nt + t, g)),
                   pl.BlockSpec((1, hb, DK_A, DV_A), lambda b, g, t: (b, g, 0, 0))],
        out_shape=[jax.ShapeDtypeStruct((t_rows, V_A), BF16),
                   jax.ShapeDtypeStruct((bsz, H_A, DK_A, DV_A), F32)],
        scratch_shapes=[pltpu.VMEM((tb + 8, width), F32)] * 3 + [pltpu.VMEM((hb, DK_A, DV_A), F32)],
        compiler_params=_cparams(("parallel", "parallel", "arbitrary")),
        name="gdn",
    )(h1, h1, h1, h1, gates3, gates3, conv_w, conv_w, conv_w,
      conv_state, conv_state, conv_state, s0, gn)


def _mla_pre_body(cq_ref, ckv_ref, kra_ref, krb_ref, cos_ref, sin_ref, qg_ref, wuq_ref, wuk_ref,
                  kvg_ref, q_ref, kc_ref, ckvo_ref, kro_ref, maybe_vt_ref=None):
    cq = cq_ref[...]
    cqn = cq * lax.rsqrt(jnp.mean(cq * cq, -1, keepdims=True) + EPS) * qg_ref[...]
    qf = jnp.dot(cqn.astype(BF16), wuq_ref[...], preferred_element_type=F32)
    cos_k = cos_ref[...]
    sin_k = sin_ref[...]
    reps = H_B * ROPE // LANES
    cos_t = jnp.concatenate([cos_k] * reps, axis=1)
    sin_t = jnp.concatenate([sin_k] * reps, axis=1)
    n_nope = H_B * NOPE
    n_rope = H_B * ROPE
    qr = (qf[:, n_nope:n_nope + n_rope] * cos_t + qf[:, n_nope + n_rope:] * sin_t) * Q_SCALE
    lane = lax.broadcasted_iota(jnp.int32, (cq.shape[0], LANES), 1)
    for h in range(H_B):
        ql = jnp.dot(qf[:, h * NOPE:(h + 1) * NOPE].astype(BF16), wuk_ref[h],
                     preferred_element_type=F32) * Q_SCALE
        blk = qr[:, (h // 2) * LANES:(h // 2 + 1) * LANES]
        keep = (lane < ROPE) if h % 2 == 0 else (lane >= ROPE)
        q_ref[h, :, 0:KV_RANK] = ql.astype(BF16)
        q_ref[h, :, KV_RANK:KC_W] = jnp.where(keep, blk, 0.0).astype(BF16)
    ckv = ckv_ref[...]
    ckvn = ckv * lax.rsqrt(jnp.mean(ckv * ckv, -1, keepdims=True) + EPS) * kvg_ref[...]
    kr2 = kra_ref[...] * cos_k + krb_ref[...] * sin_k
    ckvo_ref[...] = ckvn
    kro_ref[...] = kr2[:, 0:ROPE]
    kc_ref[:, 0:KV_RANK] = ckvn.astype(BF16)
    kc_ref[:, KV_RANK:KC_W] = kr2.astype(BF16)
    if maybe_vt_ref is not None:
        maybe_vt_ref[0] = ckvn.T.astype(BF16)


def _mla_pre(h1, cos_t, sin_t, qg, wuq, wuk, kvg, bsz, seq, tm, with_vt):
    t_rows = bsz * seq
    ntab = cos_t.shape[0] // tm
    npb = max(seq // tm, 1)
    const2 = lambda i: (0, 0)
    out_specs = [pl.BlockSpec((H_B, tm, KC_W), lambda i: (0, i, 0)),
                 pl.BlockSpec((tm, KC_W), lambda i: (i, 0)),
                 pl.BlockSpec((tm, KV_RANK), lambda i: (i, 0)),
                 pl.BlockSpec((tm, ROPE), lambda i: (i, 0))]
    out_shape = [jax.ShapeDtypeStruct((H_B, t_rows, KC_W), BF16),
                 jax.ShapeDtypeStruct((t_rows, KC_W), BF16),
                 jax.ShapeDtypeStruct((t_rows, KV_RANK), F32),
                 jax.ShapeDtypeStruct((t_rows, ROPE), F32)]
    if with_vt:
        out_specs.append(pl.BlockSpec((1, KV_RANK, tm), lambda i: (i // npb, 0, i % npb)))
        out_shape.append(jax.ShapeDtypeStruct((bsz, KV_RANK, seq), BF16))
    return pl.pallas_call(
        _mla_pre_body,
        grid=(t_rows // tm,),
        in_specs=[pl.BlockSpec((tm, Q_RANK), lambda i: (i, COL_CQ // Q_RANK)),
                  pl.BlockSpec((tm, KV_RANK), lambda i: (i, COL_CKV // KV_RANK)),
                  pl.BlockSpec((tm, LANES), lambda i: (i, COL_KRA // LANES)),
                  pl.BlockSpec((tm, LANES), lambda i: (i, COL_KRB // LANES)),
                  pl.BlockSpec((tm, LANES), lambda i: (i % ntab, 0)),
                  pl.BlockSpec((tm, LANES), lambda i: (i % ntab, 0)),
                  pl.BlockSpec((1, Q_RANK), const2),
                  pl.BlockSpec(wuq.shape, const2),
                  pl.BlockSpec(wuk.shape, lambda i: (0, 0, 0)),
                  pl.BlockSpec((1, KV_RANK), const2)],
        out_specs=out_specs,
        out_shape=out_shape,
        compiler_params=_cparams(("parallel",)),
        name="mla_pre",
    )(h1, h1, h1, h1, cos_t, sin_t, qg, wuq, wuk, kvg)


def _attn_body(q_ref, k_ref, vt_ref, wuv_ref, o_ref, m_ref, l_ref, acc_ref, ex_ref, *, tq, tk, cw, lk):
    i = pl.program_id(1)
    shift = CHUNK.bit_length() - 1

    def update(j, masked, lazy):
        k0 = pl.multiple_of(j * tk, tk)
        kt = k_ref[0, pl.ds(k0, tk), :]
        vt = vt_ref[0, :, pl.ds(k0, tk)]
        bias = None
        if masked:
            kpos = k0 + lax.broadcasted_iota(jnp.int32, (tk, tq), 0)
            qpos = i * tq + lax.broadcasted_iota(jnp.int32, (tk, tq), 1)
            bias = jnp.where((kpos >> shift) <= (qpos >> shift), 0.0, -jnp.inf).astype(F32)

        units = [(h, slice(c, c + cw)) for h in range(H_B) for c in range(0, tq, cw)]

        def scores(u):
            h, cs = u
            s = lax.dot_general(kt, q_ref[h, cs, :], (((1,), (1,)), ((), ())),
                                preferred_element_type=F32)
            return s if bias is None else s + bias[:, cs]

        def softmax(u, s):
            h, cs = u
            m_prev = m_ref[h, :, cs]
            c_max = jnp.max(s, 0, keepdims=True)
            m_new = jnp.maximum(m_prev, c_max)
            alpha = jnp.exp2(m_prev - m_new)
            if lazy:
                p = jnp.exp2(s - m_prev)
                ex_ref[h, :, cs] = jnp.maximum(ex_ref[h, :, cs], c_max - m_prev)
                l_ref[h, :, cs] = alpha * (l_ref[h, :, cs] + jnp.sum(p, 0, keepdims=True))
            else:
                p = jnp.exp2(s - m_new)
                l_ref[h, :, cs] = alpha * l_ref[h, :, cs] + jnp.sum(p, 0, keepdims=True)
            m_ref[h, :, cs] = m_new
            return p.astype(BF16), alpha

        def accumulate(u, alpha, pv):
            h, cs = u
            if lazy:
                acc_ref[h, :, cs] = alpha * (acc_ref[h, :, cs] + pv)
            else:
                acc_ref[h, :, cs] = alpha * acc_ref[h, :, cs] + pv

        n_units = len(units)
        s_q = {0: scores(units[0])}
        if n_units > 1:
            s_q[1] = scores(units[1])
        p0, alpha = softmax(units[0], s_q.pop(0))
        alphas = {0: alpha}
        pvs = {0: jnp.dot(vt, p0, preferred_element_type=F32)}
        for n in range(n_units):
            if n + 2 < n_units:
                s_q[n + 2] = scores(units[n + 2])
            if n + 1 < n_units:
                p_next, alphas[n + 1] = softmax(units[n + 1], s_q.pop(n + 1))
            accumulate(units[n], alphas.pop(n), pvs.pop(n))
            if n + 1 < n_units:
                pvs[n + 1] = jnp.dot(vt, p_next, preferred_element_type=F32)

    n_full = ((((i * tq) >> shift) + 1) << shift) // tk
    n_all = jnp.minimum(((((i * tq + tq - 1) >> shift) + 1) << shift) + tk - 1, lk + tk - 1) // tk

    def tile_loop(lo, hi, masked, lazy):
        def step(j, carry):
            update(j, masked, lazy)
            return carry
        lax.fori_loop(lo, hi, step, 0)

    def attempt(a, redo):
        first = a == 0
        run = jnp.logical_or(first, redo > 0)

        @pl.when(run)
        def _():
            m_ref[...] = jnp.full(m_ref.shape, -jnp.inf, F32)
            l_ref[...] = jnp.zeros(l_ref.shape, F32)
            acc_ref[...] = jnp.zeros(acc_ref.shape, F32)
            ex_ref[...] = jnp.full(ex_ref.shape, -jnp.inf, F32)

        tile_loop(0, jnp.where(first, 1, n_all) * run.astype(jnp.int32), True, False)
        tile_loop(1, jnp.where(first, n_full, 0), False, True)
        tile_loop(jnp.maximum(n_full, 1), jnp.where(first, n_all, 0), True, True)
        excess = jnp.max(ex_ref[...])
        return jnp.where(first, (excess > MAX_STALE_EXCESS).astype(jnp.int32), 0)

    lax.fori_loop(0, 2, attempt, jnp.int32(0))

    for h in range(H_B):
        o_t = (acc_ref[h] * (1.0 / l_ref[h])).astype(BF16)
        ob_t = jnp.dot(wuv_ref[h], o_t, preferred_element_type=F32)
        o_ref[0, :, h * V_B:(h + 1) * V_B] = ob_t.T.astype(o_ref.dtype)


def _attn(q, kc, vt, wuv_t, bsz, seq, tq, tk, cw):
    assert seq % tq == 0 and seq % tk == 0
    nq = seq // tq
    body = functools.partial(_attn_body, tq=tq, tk=tk, cw=cw, lk=seq)
    return pl.pallas_call(
        body,
        grid=(bsz, nq),
        in_specs=[pl.BlockSpec((H_B, tq, KC_W), lambda b, i: (0, b * nq + i, 0)),
                  pl.BlockSpec((1, seq, KC_W), lambda b, i: (b, 0, 0)),
                  pl.BlockSpec((1, KV_RANK, seq), lambda b, i: (b, 0, 0)),
                  pl.BlockSpec(wuv_t.shape, lambda b, i: (0, 0, 0))],
        out_specs=pl.BlockSpec((1, tq, H_B * V_B), lambda b, i: (b, i, 0)),
        out_shape=jax.ShapeDtypeStruct((bsz, seq, H_B * V_B), BF16),
        scratch_shapes=[pltpu.VMEM((H_B, 1, tq), F32), pltpu.VMEM((H_B, 1, tq), F32),
                        pltpu.VMEM((H_B, KV_RANK, tq), F32), pltpu.VMEM((H_B, 1, tq), F32)],
        compiler_params=_cparams(("parallel", "arbitrary")),
        name="attn",
    )(q, kc, vt, wuv_t)


def _attn_dec_body(q_ref, ckv_ref, kr_ref, kn_ref, wuv_ref, o_ref, m_ref, l_ref, acc_ref,
                   *, tq, past, n_past):
    j = pl.program_id(1)
    cols = H_B * tq
    nt_dims = (((1,), (1,)), ((), ()))
    q2 = q_ref[...].reshape(cols, KC_W)

    @pl.when(j == 0)
    def _():
        m_ref[...] = jnp.full(m_ref.shape, -jnp.inf, F32)
        l_ref[...] = jnp.zeros(l_ref.shape, F32)
        acc_ref[...] = jnp.zeros(acc_ref.shape, F32)

    def accumulate(s, v_nat):
        m_prev = m_ref[...]
        m_new = jnp.maximum(m_prev, jnp.max(s, 0, keepdims=True))
        p = jnp.exp2(s - m_new)
        alpha = jnp.exp2(m_prev - m_new)
        l_ref[...] = alpha * l_ref[...] + jnp.sum(p, 0, keepdims=True)
        acc_ref[...] = alpha * acc_ref[...] + _mm_tn(v_nat, p)
        m_ref[...] = m_new

    @pl.when(j < n_past)
    def _():
        k_lat = ckv_ref[0].astype(BF16)
        kr = kr_ref[0].astype(BF16)
        kr2 = jnp.concatenate([kr, kr], axis=1)
        s = (lax.dot_general(k_lat, q2[:, 0:KV_RANK], nt_dims, preferred_element_type=F32)
             + lax.dot_general(kr2, q2[:, KV_RANK:KC_W], nt_dims, preferred_element_type=F32))
        accumulate(s, k_lat)

    @pl.when(j == n_past)
    def _():
        kn = kn_ref[...]
        s = lax.dot_general(kn, q2, nt_dims, preferred_element_type=F32)
        shift = CHUNK.bit_length() - 1
        kpos = past + lax.broadcasted_iota(jnp.int32, s.shape, 0)
        qpos = past + (lax.broadcasted_iota(jnp.int32, s.shape, 1) & (tq - 1))
        s = jnp.where((kpos >> shift) <= (qpos >> shift), s, -jnp.inf)
        accumulate(s, kn[:, 0:KV_RANK])
        o = (acc_ref[...] * (1.0 / l_ref[...])).T
        for h in range(H_B):
            oh = o[h * tq:(h + 1) * tq, :].astype(BF16)
            o_ref[:, h * V_B:(h + 1) * V_B] = jnp.dot(
                oh, wuv_ref[h], preferred_element_type=F32).astype(o_ref.dtype)


def _attn_dec(q, cache_ckv, cache_kr, layer, kc_new, wuv, bsz, seq, tk):
    past = cache_ckv.shape[2]
    assert past % tk == 0 and past % CHUNK == 0 and seq & (seq - 1) == 0
    n_past = past // tk
    body = functools.partial(_attn_dec_body, tq=seq, past=past, n_past=n_past)
    cache_idx = lambda b, j: (layer, b, jnp.minimum(j, n_past - 1), 0)
    return pl.pallas_call(
        body,
        grid=(bsz, n_past + 1),
        in_specs=[pl.BlockSpec((H_B, seq, KC_W), lambda b, j: (0, b, 0)),
                  pl.BlockSpec((None, 1, tk, KV_RANK), cache_idx),
                  pl.BlockSpec((None, 1, tk, ROPE), cache_idx),
                  pl.BlockSpec((seq, KC_W), lambda b, j: (b, 0)),
                  pl.BlockSpec(wuv.shape, lambda b, j: (0, 0, 0))],
        out_specs=pl.BlockSpec((seq, H_B * V_B), lambda b, j: (b, 0)),
        out_shape=jax.ShapeDtypeStruct((bsz * seq, H_B * V_B), BF16),
        scratch_shapes=[pltpu.VMEM((1, H_B * seq), F32), pltpu.VMEM((1, H_B * seq), F32),
                        pltpu.VMEM((KV_RANK, H_B * seq), F32)],
        compiler_params=_cparams(("parallel", "arbitrary")),
        name="attn_dec",
    )(q, cache_ckv, cache_kr, kc_new, wuv)


def _layer_norm(r, g, b):
    mu = jnp.mean(r, -1, keepdims=True)
    d = r - mu
    var = jnp.mean(d * d, -1, keepdims=True)
    return d * lax.rsqrt(var + EPS) * g + b


def _merge_body(oa_ref, ob_ref, ga_ref, gb_ref, x_ref, woa_ref, wob_ref, wout_ref, g_ref, b_ref, o_ref):
    ya = jnp.dot(oa_ref[...], woa_ref[...], preferred_element_type=F32)
    yb = jnp.dot(ob_ref[...], wob_ref[...], preferred_element_type=F32)
    m = _sigmoid(ga_ref[...]) * ya + _sigmoid(gb_ref[...]) * yb
    r = ALPHA * x_ref[...] + jnp.dot(m.astype(BF16), wout_ref[...], preferred_element_type=F32)
    o_ref[...] = _layer_norm(r, g_ref[...], b_ref[...])


def _merge(oa, ob, h1, x, woa, wob, wout, g, b, tm):
    t = x.shape[0]
    row = lambda i: (i, 0)
    const = lambda i: (0, 0)
    wspec = pl.BlockSpec((D_MODEL, D_MODEL), const)
    return pl.pallas_call(
        _merge_body,
        grid=(t // tm,),
        in_specs=[pl.BlockSpec((tm, V_A), row), pl.BlockSpec((tm, H_B * V_B), row),
                  pl.BlockSpec((tm, D_MODEL), lambda i: (i, COL_GA // D_MODEL)),
                  pl.BlockSpec((tm, D_MODEL), lambda i: (i, COL_GB // D_MODEL)),
                  pl.BlockSpec((tm, D_MODEL), row), wspec, wspec, wspec,
                  pl.BlockSpec((1, D_MODEL), const), pl.BlockSpec((1, D_MODEL), const)],
        out_specs=pl.BlockSpec((tm, D_MODEL), row),
        out_shape=jax.ShapeDtypeStruct((t, D_MODEL), F32),
        compiler_params=_cparams(("parallel",)),
        name="merge",
    )(oa, ob, h1, h1, x, woa, wob, wout, g, b)


def _ffn_body(x_ref, wg_ref, wu_ref, wd_ref, g_ref, b_ref, o_ref, h_ref, *, tf):
    x = x_ref[...]
    xb = x.astype(BF16)
    for j in range(D_FF // tf):
        sl = slice(j * tf, (j + 1) * tf)
        f1 = jnp.dot(xb, wg_ref[:, sl], preferred_element_type=F32)
        f3 = jnp.dot(xb, wu_ref[:, sl], preferred_element_type=F32)
        h_ref[:, sl] = (_silu(f1) * f3).astype(BF16)
    y = jnp.dot(h_ref[...], wd_ref[...], preferred_element_type=F32)
    o_ref[...] = _layer_norm(ALPHA * x + y, g_ref[...], b_ref[...])


def _ffn(x, wg, wu, wd, g, b, tm, tf):
    t = x.shape[0]
    row = lambda i: (i, 0)
    const = lambda i: (0, 0)
    single = pl.Buffered(1)
    return pl.pallas_call(
        functools.partial(_ffn_body, tf=tf),
        grid=(t // tm,),
        in_specs=[pl.BlockSpec((tm, D_MODEL), row),
                  pl.BlockSpec((D_MODEL, D_FF), const, pipeline_mode=single),
                  pl.BlockSpec((D_MODEL, D_FF), const, pipeline_mode=single),
                  pl.BlockSpec((D_FF, D_MODEL), const, pipeline_mode=single),
                  pl.BlockSpec((1, D_MODEL), const), pl.BlockSpec((1, D_MODEL), const)],
        out_specs=pl.BlockSpec((tm, D_MODEL), row),
        out_shape=jax.ShapeDtypeStruct((t, D_MODEL), F32),
        scratch_shapes=[pltpu.VMEM((tm, D_FF), BF16)],
        compiler_params=_cparams(("parallel",)),
        name="ffn",
    )(x, wg, wu, wd, g, b)


def _prep_layer_weights(w_in, conv_w, a_log, dt_bias, gdn_norm_g, w_oa, q_norm_g, w_uq, kv_norm_g,
                        w_ukv, w_ob, w_out, ln1_g, ln1_b, w_gu, w_down, ln2_g, ln2_b):
    seg = lambda i: w_in[:, _OFFS[i]:_OFFS[i + 1]]
    qkv, z, a, b, c_q, c_kv, k_r, g_a, g_b = (seg(i) for i in range(9))
    half = ROPE // 2
    k_r_rot = jnp.concatenate([-k_r[:, half:], k_r[:, :half]], axis=1)
    pad = jnp.zeros((D_MODEL, LANES - 2 * H_A), w_in.dtype)
    w_proj = jnp.concatenate([qkv, z, g_a, g_b, c_kv, k_r, k_r, c_q, k_r_rot, k_r_rot, a, b, pad],
                             axis=1).astype(BF16)
    lane_pad = jnp.zeros((LANES - H_A,), F32)
    al_lane = jnp.concatenate([a_log.astype(F32), lane_pad]).reshape(1, LANES)
    dt_lane = jnp.concatenate([dt_bias.astype(F32), lane_pad]).reshape(1, LANES)
    uq = w_uq.reshape(Q_RANK, H_B, NOPE + ROPE)
    uq_nope = uq[:, :, :NOPE].reshape(Q_RANK, H_B * NOPE)
    uq_rope = uq[:, :, NOPE:]
    uq_rot = jnp.concatenate([-uq_rope[:, :, half:], uq_rope[:, :, :half]], axis=2)
    w_uq_ext = jnp.concatenate([uq_nope, uq_rope.reshape(Q_RANK, H_B * ROPE),
                                uq_rot.reshape(Q_RANK, H_B * ROPE)], axis=1).astype(BF16)
    ukv = w_ukv.reshape(KV_RANK, H_B, NOPE + V_B)
    w_uk_t = jnp.transpose(ukv[:, :, :NOPE], (1, 2, 0)).astype(BF16)
    w_uv = jnp.transpose(ukv[:, :, NOPE:], (1, 0, 2)).astype(BF16)
    w_uv_t = jnp.transpose(ukv[:, :, NOPE:], (1, 2, 0)).astype(BF16)
    return dict(
        w_proj=w_proj, conv_w=conv_w.astype(F32), al_lane=al_lane, dt_lane=dt_lane,
        gn=gdn_norm_g.reshape(1, DV_A).astype(F32), w_oa=w_oa.astype(BF16),
        qg=q_norm_g.reshape(1, Q_RANK).astype(F32), w_uq=w_uq_ext, w_uk_t=w_uk_t, w_uv=w_uv, w_uv_t=w_uv_t,
        kvg=kv_norm_g.reshape(1, KV_RANK).astype(F32), w_ob=w_ob.astype(BF16),
        w_out=w_out.astype(BF16), ln1_g=ln1_g.reshape(1, D_MODEL), ln1_b=ln1_b.reshape(1, D_MODEL),
        w_g=w_gu[:, :D_FF].astype(BF16), w_u=w_gu[:, D_FF:].astype(BF16), w_down=w_down.astype(BF16),
        ln2_g=ln2_g.reshape(1, D_MODEL), ln2_b=ln2_b.reshape(1, D_MODEL))


def _rope_tables(past, seq, reps):
    half = ROPE // 2
    inv = ROPE_THETA ** (-jnp.arange(half, dtype=F32) / half)
    ang = (past + jnp.arange(seq)).astype(F32)[:, None] * inv[None, :]
    cos = jnp.tile(jnp.cos(ang), (reps, LANES // half))
    sin = jnp.tile(jnp.sin(ang), (reps, LANES // half))
    return cos, sin


TM_PROJ, TN_PROJ, TM_GATES, TM_MLA, TM_MERGE, TM_FFN = 1024, 1024, 1024, 512, 512, 512
TF_FFN = D_FF // 2
TQ_ATTN, TK_ATTN, CW_ATTN, TK_DEC = 512, 512, 512, 1024
TB_GDN, HB_GDN = 2 * GDN_CHUNK, H_A


def _trunk_layer(x, conv_state, s0, caches, wl, bsz, seq):
    decode = caches is not None
    t_rows = bsz * seq
    h1 = _proj_in(x, wl["w_proj"], min(TM_PROJ, t_rows), TN_PROJ)
    seq_pad = -(-seq // GDN_CHUNK) * GDN_CHUNK
    if seq_pad == seq:
        gates = _gates(h1, COL_AB // LANES, wl["al_lane"], wl["dt_lane"], min(TM_GATES, t_rows), seq, seq)
    else:
        ab = h1[:, COL_AB:COL_AB + LANES].reshape(bsz, seq, LANES)
        ab = jnp.pad(ab, ((0, 0), (0, seq_pad - seq), (0, 0))).reshape(bsz * seq_pad, LANES)
        gates = _gates(ab, 0, wl["al_lane"], wl["dt_lane"], min(TM_GATES, bsz * seq_pad), seq_pad, seq)
    o_a, s_new = _gdn(h1, gates.reshape(2 * H_A, 1, bsz * seq_pad), wl["conv_w"], conv_state, s0, wl["gn"],
                      bsz, seq, min(TB_GDN, seq_pad), HB_GDN)
    if decode:
        cache_ckv, cache_kr, layer = caches
        tm_mla = t_rows if t_rows <= TM_MLA else seq
        cos_t, sin_t = _rope_tables(cache_ckv.shape[2], seq, tm_mla // seq)
        q, kc, ckv_new, kr_new = _mla_pre(h1, cos_t, sin_t, wl["qg"], wl["w_uq"], wl["w_uk_t"], wl["kvg"],
                                          bsz, seq, tm_mla, False)
        o_b = _attn_dec(q, cache_ckv, cache_kr, layer, kc, wl["w_uv"], bsz, seq, TK_DEC)
    else:
        cos_t, sin_t = _rope_tables(0, seq, 1)
        q, kc, ckv_new, kr_new, vt = _mla_pre(h1, cos_t, sin_t, wl["qg"], wl["w_uq"], wl["w_uk_t"],
                                              wl["kvg"], bsz, seq, TM_MLA, True)
        o_b = _attn(q, kc.reshape(bsz, seq, KC_W), vt, wl["w_uv_t"], bsz, seq, TQ_ATTN, TK_ATTN, CW_ATTN)
        o_b = o_b.reshape(t_rows, H_B * V_B)
    x1 = _merge(o_a, o_b, h1, x, wl["w_oa"], wl["w_ob"], wl["w_out"], wl["ln1_g"], wl["ln1_b"],
                min(TM_MERGE, t_rows))
    x2 = _ffn(x1, wl["w_g"], wl["w_u"], wl["w_down"], wl["ln2_g"], wl["ln2_b"], min(TM_FFN, t_rows), TF_FFN)
    conv_new = h1.reshape(bsz, seq, N_PROJ)[:, seq - (CONV_W - 1):, COL_QKV:COL_QKV + C_QKV]
    return (x2, conv_new, s_new, ckv_new.reshape(bsz, seq, KV_RANK), kr_new.reshape(bsz, seq, ROPE))


def kernel(x_prompt, x_sample, state_conv, state_gdn, cache_ckv, cache_krope, w_in, conv_w, a_log, dt_bias, gdn_norm_g, w_oa, q_norm_g, w_uq, kv_norm_g, w_ukv, w_ob, w_out, ln1_g, ln1_b, w_gu, w_down, ln2_g, ln2_b):
    bp, lp, _ = x_prompt.shape
    bs, ls, _ = x_sample.shape
    yp = x_prompt.reshape(bp * lp, D_MODEL)
    ys = x_sample.reshape(bs * ls, D_MODEL)
    zero_conv = jnp.zeros((bp, CONV_W - 1, C_QKV), F32)
    zero_s = jnp.zeros((bp, H_A, DK_A, DV_A), F32)
    outs_p, outs_s = [], []
    for l in range(w_in.shape[0]):
        wl = _prep_layer_weights(w_in[l], conv_w[l], a_log[l], dt_bias[l], gdn_norm_g[l], w_oa[l],
                                 q_norm_g[l], w_uq[l], kv_norm_g[l], w_ukv[l], w_ob[l], w_out[l],
                                 ln1_g[l], ln1_b[l], w_gu[l], w_down[l], ln2_g[l], ln2_b[l])
        yp, *rest_p = _trunk_layer(yp, zero_conv, zero_s, None, wl, bp, lp)
        ys, *rest_s = _trunk_layer(ys, state_conv[l], state_gdn[l], (cache_ckv, cache_krope, l), wl, bs, ls)
        outs_p.append(rest_p)
        outs_s.append(rest_s)
    stack = lambda outs, i: jnp.stack([o[i] for o in outs])
    return (yp.reshape(bp, lp, D_MODEL), ys.reshape(bs, ls, D_MODEL),
            stack(outs_p, 0), stack(outs_p, 1), stack(outs_p, 2), stack(outs_p, 3),
            stack(outs_s, 0), stack(outs_s, 1), stack(outs_s, 2), stack(outs_s, 3))
```

```python
import functools

import numpy as np
import jax
import jax.numpy as jnp
from jax import lax
from jax.experimental import pallas as pl
from jax.experimental.pallas import tpu as pltpu

F32 = jnp.float32
BF16 = jnp.bfloat16

D_MODEL = 1024
DEPTH = 2
CHUNK = 64
H_A = 8
DK_A = 128
DV_A = 128
QK_A = H_A * DK_A
V_A = H_A * DV_A
C_QKV = 2 * QK_A + V_A
CONV_W = 4
H_B = 8
NOPE = 128
ROPE = 64
V_B = 128
Q_RANK = 384
KV_RANK = 256
ROPE_THETA = 10000.0
ATTN_SCALE = (NOPE + ROPE) ** -0.5
LOG2_E = float(np.log2(np.e))
Q_SCALE = ATTN_SCALE * LOG2_E
D_FF = -(-8 * D_MODEL // (3 * 256)) * 256
ALPHA = (2 * DEPTH) ** 0.25
EPS = 1e-6
_SIZES = (C_QKV, V_A, H_A, H_A, Q_RANK, KV_RANK, ROPE, D_MODEL, D_MODEL)
_OFFS = tuple(int(v) for v in np.cumsum((0,) + _SIZES))

LANES = 128
VMEM_LIMIT = 56 * 1024 * 1024

COL_QKV = 0
COL_Z = COL_QKV + C_QKV
COL_GA = COL_Z + V_A
COL_GB = COL_GA + D_MODEL
COL_CKV = COL_GB + D_MODEL
COL_KRA = COL_CKV + KV_RANK
COL_CQ = COL_KRA + LANES
COL_KRB = COL_CQ + Q_RANK
COL_AB = COL_KRB + LANES
N_PROJ = COL_AB + LANES
KC_W = KV_RANK + LANES

GDN_CHUNK = 128
MAX_STALE_EXCESS = 64.0


def _cparams(sem):
    return pltpu.CompilerParams(dimension_semantics=sem, vmem_limit_bytes=VMEM_LIMIT)


def _sigmoid(x):
    return jax.nn.sigmoid(x)


def _silu(x):
    return x * jax.nn.sigmoid(x)


def _mm(a, b):
    return jnp.dot(a.astype(BF16), b.astype(BF16), preferred_element_type=F32)


def _mm_nt(a, b):
    return lax.dot_general(a.astype(BF16), b.astype(BF16), (((1,), (1,)), ((), ())),
                           preferred_element_type=F32)


def _mm_tn(a, b):
    return lax.dot_general(a.astype(BF16), b.astype(BF16), (((0,), (0,)), ((), ())),
                           preferred_element_type=F32)


def _proj_body(x_ref, w_ref, o_ref, xb_ref):
    @pl.when(pl.program_id(1) == 0)
    def _():
        xb_ref[...] = x_ref[...].astype(BF16)

    o_ref[...] = jnp.dot(xb_ref[...], w_ref[...], preferred_element_type=F32)


def _proj_in(x, w, tm, tn):
    t, k = x.shape
    n = w.shape[1]
    return pl.pallas_call(
        _proj_body,
        grid=(t // tm, n // tn),
        in_specs=[pl.BlockSpec((tm, k), lambda i, j: (i, 0)),
                  pl.BlockSpec((k, tn), lambda i, j: (0, j))],
        out_specs=pl.BlockSpec((tm, tn), lambda i, j: (i, j)),
        out_shape=jax.ShapeDtypeStruct((t, n), F32),
        scratch_shapes=[pltpu.VMEM((tm, k), BF16)],
        compiler_params=_cparams(("parallel", "arbitrary")),
        name="proj_in",
    )(x, w)


def _gates_body(ab_ref, al_ref, dt_ref, o_ref, *, tm, l_pad, l_valid):
    x = ab_ref[...]
    lane = lax.broadcasted_iota(jnp.int32, x.shape, 1)
    xa = x + dt_ref[...]
    sp = jnp.maximum(xa, 0.0) + jnp.log1p(jnp.exp(-jnp.abs(xa)))
    g = -jnp.exp(al_ref[...]) * sp * LOG2_E
    y = jnp.where(lane < H_A, g, _sigmoid(x))
    yt = y.T[0:2 * H_A, :]
    if l_valid < l_pad:
        col = lax.broadcasted_iota(jnp.int32, yt.shape, 1) + pl.program_id(0) * tm
        yt = jnp.where(col % l_pad < l_valid, yt, 0.0)
    r = lax.broadcasted_iota(jnp.int32, (GDN_CHUNK, GDN_CHUNK), 0)
    c = lax.broadcasted_iota(jnp.int32, (GDN_CHUNK, GDN_CHUNK), 1)
    tri = jnp.where(r <= c, 1.0, 0.0).astype(F32)
    for s in range(tm // GDN_CHUNK):
        sl = slice(s * GDN_CHUNK, (s + 1) * GDN_CHUNK)
        o_ref[0:H_A, sl] = jnp.dot(yt[0:H_A, sl], tri, precision=lax.Precision.HIGHEST,
                                   preferred_element_type=F32)
    o_ref[H_A:2 * H_A, :] = yt[H_A:2 * H_A, :]


def _gates(src, col_block, al_lane, dt_lane, tm, l_pad, l_valid):
    t = src.shape[0]
    body = functools.partial(_gates_body, tm=tm, l_pad=l_pad, l_valid=l_valid)
    return pl.pallas_call(
        body,
        grid=(t // tm,),
        in_specs=[pl.BlockSpec((tm, LANES), lambda i: (i, col_block)),
                  pl.BlockSpec((1, LANES), lambda i: (0, 0)),
                  pl.BlockSpec((1, LANES), lambda i: (0, 0))],
        out_specs=pl.BlockSpec((2 * H_A, tm), lambda i: (0, i)),
        out_shape=jax.ShapeDtypeStruct((2 * H_A, t), F32),
        compiler_params=_cparams(("parallel",)),
        name="gates",
    )(src, al_lane, dt_lane)


def _gdn_body(q_ref, k_ref, v_ref, z_ref, gc_ref, bt_ref, cwq_ref, cwk_ref, cwv_ref,
              csq_ref, csk_ref, csv_ref, s0_ref, gn_ref, o_ref, so_ref,
              xq_ref, xk_ref, xv_ref, s_ref, *, tb, tr, nt, hb):
    t = pl.program_id(2)
    hist = CONV_W - 1
    base = 8

    @pl.when(t == 0)
    def _():
        s_ref[...] = s0_ref[0]
        xq_ref[base - hist:base, :] = csq_ref[0]
        xk_ref[base - hist:base, :] = csk_ref[0]
        xv_ref[base - hist:base, :] = csv_ref[0]

    for xs_ref, raw_ref in ((xq_ref, q_ref), (xk_ref, k_ref), (xv_ref, v_ref)):
        xs_ref[base:base + tr, :] = raw_ref[...]
        if tr < tb:
            xs_ref[base + tr:base + tb, :] = jnp.zeros((tb - tr, xs_ref.shape[1]), F32)

    def conv(xs_ref, cw_ref, ln):
        slab = xs_ref[:, ln]
        y = pltpu.roll(slab, hist, 0)[base:base + tb] * cw_ref[0:1, ln]
        for j in range(1, CONV_W):
            tap = slab if j == hist else pltpu.roll(slab, hist - j, 0)
            y = y + tap[base:base + tb] * cw_ref[j:j + 1, ln]
        return _silu(y)

    cc = GDN_CHUNK
    ii = lax.broadcasted_iota(jnp.int32, (cc, cc), 0)
    jj = lax.broadcasted_iota(jnp.int32, (cc, cc), 1)
    strict = ii > jj
    eye = jnp.where(ii == jj, 1.0, 0.0).astype(F32)
    heads = range(hb)
    lanes = [slice(hd * LANES, (hd + 1) * LANES) for hd in heads]
    units = [(hd, c) for hd in heads for c in range(tb // cc)]
    rows = {u: slice(u[1] * cc, (u[1] + 1) * cc) for u in units}
    q = [conv(xq_ref, cwq_ref, ln) for ln in lanes]
    k = [conv(xk_ref, cwk_ref, ln) for ln in lanes]
    v = [conv(xv_ref, cwv_ref, ln) for ln in lanes]
    q = [x * (lax.rsqrt(jnp.sum(x * x, -1, keepdims=True) + EPS) * (DK_A ** -0.5)) for x in q]
    k = [x * lax.rsqrt(jnp.sum(x * x, -1, keepdims=True) + EPS) for x in k]
    qc = {u: q[u[0]][rows[u]] for u in units}
    kc = {u: k[u[0]][rows[u]] for u in units}
    vc = {u: v[u[0]][rows[u]] for u in units}
    m_row = {u: jnp.broadcast_to(gc_ref[u[0], :, rows[u]], (cc, cc)) for u in units}
    m_col = {u: m_row[u].T for u in units}
    b_col = {u: jnp.broadcast_to(bt_ref[u[0], :, rows[u]], (cc, cc)).T for u in units}
    kq = {u: _mm_nt(jnp.concatenate([kc[u], qc[u]], axis=0), kc[u]) for u in units}
    diff = {u: m_col[u] - m_row[u] for u in units}
    dec = {u: jnp.exp2(jnp.where(strict, diff[u], -jnp.inf)) for u in units}
    a_mat = {u: b_col[u] * kq[u][0:cc] * dec[u] for u in units}
    qk = {u: kq[u][cc:2 * cc] * (dec[u] + eye) for u in units}
    e_g = {u: jnp.exp2(m_col[u]) for u in units}
    x_inv = {u: eye - jnp.where((ii >> 1) == (jj >> 1), a_mat[u], 0.0) for u in units}
    sft = 1
    while (1 << sft) < cc:
        off = ((ii >> (sft + 1)) == (jj >> (sft + 1))) & ((ii >> sft) != (jj >> sft))
        lx = {u: _mm(jnp.where(off, a_mat[u], 0.0), x_inv[u]) for u in units}
        x_inv = {u: x_inv[u] - _mm(x_inv[u], lx[u]) for u in units}
        sft += 1
    uw = {u: _mm(x_inv[u], jnp.concatenate([vc[u] * b_col[u], kc[u] * (b_col[u] * e_g[u])], axis=1))
          for u in units}
    wq = {u: jnp.concatenate([uw[u][:, DV_A:], qc[u] * e_g[u]], axis=0) for u in units}
    g_last = {u: m_col[u][cc - 1:cc, :] for u in units}
    kd = {u: kc[u] * jnp.exp2(g_last[u] - m_col[u]) for u in units}
    s_state = [s_ref[hd] for hd in heads]
    for c in range(tb // cc):
        ws = [_mm(wq[(hd, c)], s_state[hd]) for hd in heads]
        v_new = [uw[(hd, c)][:, 0:DV_A] - ws[hd][0:cc] for hd in heads]
        s_state = [s_state[hd] * jnp.exp2(g_last[(hd, c)]) + _mm_tn(kd[(hd, c)], v_new[hd]) for hd in heads]
        o = [ws[hd][cc:2 * cc] + _mm(qk[(hd, c)], v_new[hd]) for hd in heads]
        o = [x * lax.rsqrt(jnp.mean(x * x, -1, keepdims=True) + EPS) * gn_ref[...] for x in o]
        nr = min(cc, tr - c * cc)
        r = slice(c * cc, c * cc + nr)
        for hd in heads:
            o_ref[r, lanes[hd]] = (o[hd][0:nr] * _silu(z_ref[r, lanes[hd]])).astype(o_ref.dtype)
    for hd in heads:
        s_ref[hd] = s_state[hd]

    xq_ref[base - hist:base, :] = xq_ref[base + tb - hist:base + tb, :]
    xk_ref[base - hist:base, :] = xk_ref[base + tb - hist:base + tb, :]
    xv_ref[base - hist:base, :] = xv_ref[base + tb - hist:base + tb, :]

    @pl.when(t == nt - 1)
    def _():
        so_ref[0] = s_ref[...]


def _gdn(h1, gates3, conv_w, conv_state, s0, gn, bsz, seq, tb, hb):
    nt = -(-seq // tb)
    tr = min(tb, seq)
    assert seq % tr == 0 and (tr == tb or (nt == 1 and tb == GDN_CHUNK))
    t_rows = bsz * seq
    body = functools.partial(_gdn_body, tb=tb, tr=tr, nt=nt, hb=hb)
    width = hb * LANES
    ng = H_A // hb

    def rows(seg):
        return pl.BlockSpec((tr, width), lambda b, g, t: (b * nt + t, seg * ng + g))

    def cw(seg):
        return pl.BlockSpec((CONV_W, width), lambda b, g, t: (0, seg * ng + g))

    def cs(seg):
        return pl.BlockSpec((1, CONV_W - 1, width), lambda b, g, t: (b, 0, seg * ng + g))

    return pl.pallas_call(
        body,
        grid=(bsz, ng, nt),
        in_specs=[rows(0), rows(1), rows(2), rows(COL_Z // QK_A),
                  pl.BlockSpec((hb, 1, tb), lambda b, g, t: (g, 0, b * nt + t)),
                  pl.BlockSpec((hb, 1, tb), lambda b, g, t: (ng + g, 0, b * nt + t)),
                  cw(0), cw(1), cw(2), cs(0), cs(1), cs(2),
                  pl.BlockSpec((1, hb, DK_A, DV_A), lambda b, g, t: (b, g, 0, 0)),
                  pl.BlockSpec((1, DV_A), lambda b, g, t: (0, 0))],
        out_specs=[pl.BlockSpec((tr, width), lambda b, g, t: (b * nt + t, g)),
                   pl.BlockSpec((1, hb, DK_A, DV_A), lambda b, g, t: (b, g, 0, 0))],
        out_shape=[jax.ShapeDtypeStruct((t_rows, V_A), BF16),
                   jax.ShapeDtypeStruct((bsz, H_A, DK_A, DV_A), F32)],
        scratch_shapes=[pltpu.VMEM((tb + 8, width), F32)] * 3 + [pltpu.VMEM((hb, DK_A, DV_A), F32)],
        compiler_params=_cparams(("parallel", "parallel", "arbitrary")),
        name="gdn",
    )(h1, h1, h1, h1, gates3, gates3, conv_w, conv_w, conv_w,
      conv_state, conv_state, conv_state, s0, gn)


def _mla_pre_body(cq_ref, ckv_ref, kra_ref, krb_ref, cos_ref, sin_ref, qg_ref, wuq_ref, wuk_ref,
                  kvg_ref, q_ref, kc_ref, ckvo_ref, kro_ref, maybe_vt_ref=None):
    cq = cq_ref[...]
    cqn = cq * lax.rsqrt(jnp.mean(cq * cq, -1, keepdims=True) + EPS) * qg_ref[...]
    qf = jnp.dot(cqn.astype(BF16), wuq_ref[...], preferred_element_type=F32)
    cos_k = cos_ref[...]
    sin_k = sin_ref[...]
    reps = H_B * ROPE // LANES
    cos_t = jnp.concatenate([cos_k] * reps, axis=1)
    sin_t = jnp.concatenate([sin_k] * reps, axis=1)
    n_nope = H_B * NOPE
    n_rope = H_B * ROPE
    qr = (qf[:, n_nope:n_nope + n_rope] * cos_t + qf[:, n_nope + n_rope:] * sin_t) * Q_SCALE
    lane = lax.broadcasted_iota(jnp.int32, (cq.shape[0], LANES), 1)
    for h in range(H_B):
        ql = jnp.dot(qf[:, h * NOPE:(h + 1) * NOPE].astype(BF16), wuk_ref[h],
                     preferred_element_type=F32) * Q_SCALE
        blk = qr[:, (h // 2) * LANES:(h // 2 + 1) * LANES]
        keep = (lane < ROPE) if h % 2 == 0 else (lane >= ROPE)
        q_ref[h, :, 0:KV_RANK] = ql.astype(BF16)
        q_ref[h, :, KV_RANK:KC_W] = jnp.where(keep, blk, 0.0).astype(BF16)
    ckv = ckv_ref[...]
    ckvn = ckv * lax.rsqrt(jnp.mean(ckv * ckv, -1, keepdims=True) + EPS) * kvg_ref[...]
    kr2 = kra_ref[...] * cos_k + krb_ref[...] * sin_k
    ckvo_ref[...] = ckvn
    kro_ref[...] = kr2[:, 0:ROPE]
    kc_ref[:, 0:KV_RANK] = ckvn.astype(BF16)
    kc_ref[:, KV_RANK:KC_W] = kr2.astype(BF16)
    if maybe_vt_ref is not None:
        maybe_vt_ref[0] = ckvn.T.astype(BF16)


def _mla_pre(h1, cos_t, sin_t, qg, wuq, wuk, kvg, bsz, seq, tm, with_vt):
    t_rows = bsz * seq
    ntab = cos_t.shape[0] // tm
    npb = max(seq // tm, 1)
    const2 = lambda i: (0, 0)
    out_specs = [pl.BlockSpec((H_B, tm, KC_W), lambda i: (0, i, 0)),
                 pl.BlockSpec((tm, KC_W), lambda i: (i, 0)),
                 pl.BlockSpec((tm, KV_RANK), lambda i: (i, 0)),
                 pl.BlockSpec((tm, ROPE), lambda i: (i, 0))]
    out_shape = [jax.ShapeDtypeStruct((H_B, t_rows, KC_W), BF16),
                 jax.ShapeDtypeStruct((t_rows, KC_W), BF16),
                 jax.ShapeDtypeStruct((t_rows, KV_RANK), F32),
                 jax.ShapeDtypeStruct((t_rows, ROPE), F32)]
    if with_vt:
        out_specs.append(pl.BlockSpec((1, KV_RANK, tm), lambda i: (i // npb, 0, i % npb)))
        out_shape.append(jax.ShapeDtypeStruct((bsz, KV_RANK, seq), BF16))
    return pl.pallas_call(
        _mla_pre_body,
        grid=(t_rows // tm,),
        in_specs=[pl.BlockSpec((tm, Q_RANK), lambda i: (i, COL_CQ // Q_RANK)),
                  pl.BlockSpec((tm, KV_RANK), lambda i: (i, COL_CKV // KV_RANK)),
                  pl.BlockSpec((tm, LANES), lambda i: (i, COL_KRA // LANES)),
                  pl.BlockSpec((tm, LANES), lambda i: (i, COL_KRB // LANES)),
                  pl.BlockSpec((tm, LANES), lambda i: (i % ntab, 0)),
                  pl.BlockSpec((tm, LANES), lambda i: (i % ntab, 0)),
                  pl.BlockSpec((1, Q_RANK), const2),
                  pl.BlockSpec(wuq.shape, const2),
                  pl.BlockSpec(wuk.shape, lambda i: (0, 0, 0)),
                  pl.BlockSpec((1, KV_RANK), const2)],
        out_specs=out_specs,
        out_shape=out_shape,
        compiler_params=_cparams(("parallel",)),
        name="mla_pre",
    )(h1, h1, h1, h1, cos_t, sin_t, qg, wuq, wuk, kvg)


def _attn_body(q_ref, k_ref, vt_ref, wuv_ref, o_ref, m_ref, l_ref, acc_ref, ex_ref, *, tq, tk, cw, lk):
    i = pl.program_id(1)
    shift = CHUNK.bit_length() - 1

    def update(j, masked, lazy, opening=False):
        k0 = pl.multiple_of(j * tk, tk)
        kt = k_ref[0, pl.ds(k0, tk), :]
        vt = vt_ref[0, :, pl.ds(k0, tk)]
        bias = None
        if masked:
            kpos = k0 + lax.broadcasted_iota(jnp.int32, (tk, tq), 0)
            qpos = i * tq + lax.broadcasted_iota(jnp.int32, (tk, tq), 1)
            bias = jnp.where((kpos >> shift) <= (qpos >> shift), 0.0, -jnp.inf).astype(F32)

        units = [(h, slice(c, c + cw)) for h in range(H_B) for c in range(0, tq, cw)]

        def scores(u):
            h, cs = u
            s = lax.dot_general(kt, q_ref[h, cs, :], (((1,), (1,)), ((), ())),
                                preferred_element_type=F32)
            return s if bias is None else s + bias[:, cs]

        def softmax(u, s):
            h, cs = u
            c_max = jnp.max(s, 0, keepdims=True)
            if opening:
                m_prev = jnp.zeros_like(c_max)
                m_new = c_max
                gap = jnp.abs(c_max)
            else:
                m_prev = m_ref[h, :, cs]
                m_new = jnp.maximum(m_prev, c_max)
                gap = c_max - m_prev
            alpha = jnp.exp2(m_prev - m_new)
            if lazy:
                p = jnp.exp2(s - m_prev)
                ex_ref[h, :, cs] = jnp.maximum(ex_ref[h, :, cs], gap)
                l_ref[h, :, cs] = alpha * (l_ref[h, :, cs] + jnp.sum(p, 0, keepdims=True))
            else:
                p = jnp.exp2(s - m_new)
                l_ref[h, :, cs] = alpha * l_ref[h, :, cs] + jnp.sum(p, 0, keepdims=True)
            m_ref[h, :, cs] = m_new
            return p.astype(BF16), alpha

        def accumulate(u, alpha, pv):
            h, cs = u
            if lazy:
                acc_ref[h, :, cs] = alpha * (acc_ref[h, :, cs] + pv)
            else:
                acc_ref[h, :, cs] = alpha * acc_ref[h, :, cs] + pv

        n_units = len(units)
        s_q = {0: scores(units[0])}
        if n_units > 1:
            s_q[1] = scores(units[1])
        p0, alpha = softmax(units[0], s_q.pop(0))
        alphas = {0: alpha}
        pvs = {0: jnp.dot(vt, p0, preferred_element_type=F32)}
        for n in range(n_units):
            if n + 2 < n_units:
                s_q[n + 2] = scores(units[n + 2])
            if n + 1 < n_units:
                p_next, alphas[n + 1] = softmax(units[n + 1], s_q.pop(n + 1))
            accumulate(units[n], alphas.pop(n), pvs.pop(n))
            if n + 1 < n_units:
                pvs[n + 1] = jnp.dot(vt, p_next, preferred_element_type=F32)

    n_full = ((((i * tq) >> shift) + 1) << shift) // tk
    n_all = jnp.minimum(((((i * tq + tq - 1) >> shift) + 1) << shift) + tk - 1, lk + tk - 1) // tk

    def tile_loop(lo, hi, masked, lazy, opening=False):
        def step(j, carry):
            update(j, masked, lazy, opening)
            return carry
        lax.fori_loop(lo, hi, step, 0)

    def attempt(a, redo):
        first = a == 0
        run = jnp.logical_or(first, redo > 0)

        @pl.when(run)
        def _():
            m_ref[...] = jnp.full(m_ref.shape, -jnp.inf, F32)
            l_ref[...] = jnp.zeros(l_ref.shape, F32)
            acc_ref[...] = jnp.zeros(acc_ref.shape, F32)
            ex_ref[...] = jnp.full(ex_ref.shape, -jnp.inf, F32)

        tile_loop(0, jnp.where(first, 0, n_all) * run.astype(jnp.int32), True, False)
        tile_loop(0, jnp.where(first, 1, 0), True, True, True)
        tile_loop(1, jnp.where(first, n_full, 0), False, True)
        tile_loop(jnp.maximum(n_full, 1), jnp.where(first, n_all, 0), True, True)
        excess = jnp.max(ex_ref[...])
        return jnp.where(first, (excess > MAX_STALE_EXCESS).astype(jnp.int32), 0)

    lax.fori_loop(0, 2, attempt, jnp.int32(0))

    for h in range(H_B):
        o_t = (acc_ref[h] * (1.0 / l_ref[h])).astype(BF16)
        ob_t = jnp.dot(wuv_ref[h], o_t, preferred_element_type=F32)
        o_ref[0, :, h * V_B:(h + 1) * V_B] = ob_t.T.astype(o_ref.dtype)


def _attn(q, kc, vt, wuv_t, bsz, seq, tq, tk, cw):
    assert seq % tq == 0 and seq % tk == 0
    nq = seq // tq
    body = functools.partial(_attn_body, tq=tq, tk=tk, cw=cw, lk=seq)
    return pl.pallas_call(
        body,
        grid=(bsz, nq),
        in_specs=[pl.BlockSpec((H_B, tq, KC_W), lambda b, i: (0, b * nq + i, 0)),
                  pl.BlockSpec((1, seq, KC_W), lambda b, i: (b, 0, 0)),
                  pl.BlockSpec((1, KV_RANK, seq), lambda b, i: (b, 0, 0)),
                  pl.BlockSpec(wuv_t.shape, lambda b, i: (0, 0, 0))],
        out_specs=pl.BlockSpec((1, tq, H_B * V_B), lambda b, i: (b, i, 0)),
        out_shape=jax.ShapeDtypeStruct((bsz, seq, H_B * V_B), BF16),
        scratch_shapes=[pltpu.VMEM((H_B, 1, tq), F32), pltpu.VMEM((H_B, 1, tq), F32),
                        pltpu.VMEM((H_B, KV_RANK, tq), F32), pltpu.VMEM((H_B, 1, tq), F32)],
        compiler_params=_cparams(("parallel", "arbitrary")),
        name="attn",
    )(q, kc, vt, wuv_t)


def _attn_dec_body(q_ref, ckv_ref, kr_ref, kn_ref, wuv_ref, o_ref, m_ref, l_ref, acc_ref,
                   *, tq, past, n_past):
    j = pl.program_id(1)
    cols = H_B * tq
    nt_dims = (((1,), (1,)), ((), ()))
    q2 = q_ref[...].reshape(cols, KC_W)

    @pl.when(j == 0)
    def _():
        m_ref[...] = jnp.full(m_ref.shape, -jnp.inf, F32)
        l_ref[...] = jnp.zeros(l_ref.shape, F32)
        acc_ref[...] = jnp.zeros(acc_ref.shape, F32)

    def accumulate(s, v_nat):
        m_prev = m_ref[...]
        m_new = jnp.maximum(m_prev, jnp.max(s, 0, keepdims=True))
        p = jnp.exp2(s - m_new)
        alpha = jnp.exp2(m_prev - m_new)
        l_ref[...] = alpha * l_ref[...] + jnp.sum(p, 0, keepdims=True)
        acc_ref[...] = alpha * acc_ref[...] + _mm_tn(v_nat, p)
        m_ref[...] = m_new

    @pl.when(j < n_past)
    def _():
        k_lat = ckv_ref[0].astype(BF16)
        kr = kr_ref[0].astype(BF16)
        kr2 = jnp.concatenate([kr, kr], axis=1)
        s = (lax.dot_general(k_lat, q2[:, 0:KV_RANK], nt_dims, preferred_element_type=F32)
             + lax.dot_general(kr2, q2[:, KV_RANK:KC_W], nt_dims, preferred_element_type=F32))
        accumulate(s, k_lat)

    @pl.when(j == n_past)
    def _():
        kn = kn_ref[...]
        s = lax.dot_general(kn, q2, nt_dims, preferred_element_type=F32)
        shift = CHUNK.bit_length() - 1
        kpos = past + lax.broadcasted_iota(jnp.int32, s.shape, 0)
        qpos = past + (lax.broadcasted_iota(jnp.int32, s.shape, 1) & (tq - 1))
        s = jnp.where((kpos >> shift) <= (qpos >> shift), s, -jnp.inf)
        accumulate(s, kn[:, 0:KV_RANK])
        o = (acc_ref[...] * (1.0 / l_ref[...])).T
        for h in range(H_B):
            oh = o[h * tq:(h + 1) * tq, :].astype(BF16)
            o_ref[:, h * V_B:(h + 1) * V_B] = jnp.dot(
                oh, wuv_ref[h], preferred_element_type=F32).astype(o_ref.dtype)


def _attn_dec(q, cache_ckv, cache_kr, layer, kc_new, wuv, bsz, seq, tk):
    past = cache_ckv.shape[2]
    assert past % tk == 0 and past % CHUNK == 0 and seq & (seq - 1) == 0
    n_past = past // tk
    body = functools.partial(_attn_dec_body, tq=seq, past=past, n_past=n_past)
    cache_idx = lambda b, j: (layer, b, jnp.minimum(j, n_past - 1), 0)
    return pl.pallas_call(
        body,
        grid=(bsz, n_past + 1),
        in_specs=[pl.BlockSpec((H_B, seq, KC_W), lambda b, j: (0, b, 0)),
                  pl.BlockSpec((None, 1, tk, KV_RANK), cache_idx),
                  pl.BlockSpec((None, 1, tk, ROPE), cache_idx),
                  pl.BlockSpec((seq, KC_W), lambda b, j: (b, 0)),
                  pl.BlockSpec(wuv.shape, lambda b, j: (0, 0, 0))],
        out_specs=pl.BlockSpec((seq, H_B * V_B), lambda b, j: (b, 0)),
        out_shape=jax.ShapeDtypeStruct((bsz * seq, H_B * V_B), BF16),
        scratch_shapes=[pltpu.VMEM((1, H_B * seq), F32), pltpu.VMEM((1, H_B * seq), F32),
                        pltpu.VMEM((KV_RANK, H_B * seq), F32)],
        compiler_params=_cparams(("parallel", "arbitrary")),
        name="attn_dec",
    )(q, cache_ckv, cache_kr, kc_new, wuv)


def _layer_norm(r, g, b):
    mu = jnp.mean(r, -1, keepdims=True)
    d = r - mu
    var = jnp.mean(d * d, -1, keepdims=True)
    return d * lax.rsqrt(var + EPS) * g + b


def _merge_body(oa_ref, ob_ref, ga_ref, gb_ref, x_ref, woa_ref, wob_ref, wout_ref, g_ref, b_ref, o_ref):
    ya = jnp.dot(oa_ref[...], woa_ref[...], preferred_element_type=F32)
    yb = jnp.dot(ob_ref[...], wob_ref[...], preferred_element_type=F32)
    m = _sigmoid(ga_ref[...]) * ya + _sigmoid(gb_ref[...]) * yb
    r = ALPHA * x_ref[...] + jnp.dot(m.astype(BF16), wout_ref[...], preferred_element_type=F32)
    o_ref[...] = _layer_norm(r, g_ref[...], b_ref[...])


def _merge(oa, ob, h1, x, woa, wob, wout, g, b, tm):
    t = x.shape[0]
    row = lambda i: (i, 0)
    const = lambda i: (0, 0)
    wspec = pl.BlockSpec((D_MODEL, D_MODEL), const)
    return pl.pallas_call(
        _merge_body,
        grid=(t // tm,),
        in_specs=[pl.BlockSpec((tm, V_A), row), pl.BlockSpec((tm, H_B * V_B), row),
                  pl.BlockSpec((tm, D_MODEL), lambda i: (i, COL_GA // D_MODEL)),
                  pl.BlockSpec((tm, D_MODEL), lambda i: (i, COL_GB // D_MODEL)),
                  pl.BlockSpec((tm, D_MODEL), row), wspec, wspec, wspec,
                  pl.BlockSpec((1, D_MODEL), const), pl.BlockSpec((1, D_MODEL), const)],
        out_specs=pl.BlockSpec((tm, D_MODEL), row),
        out_shape=jax.ShapeDtypeStruct((t, D_MODEL), F32),
        compiler_params=_cparams(("parallel",)),
        name="merge",
    )(oa, ob, h1, h1, x, woa, wob, wout, g, b)


def _ffn_body(x_ref, wg_ref, wu_ref, wd_ref, g_ref, b_ref, o_ref, *, chunks):
    x = x_ref[...]
    xb = x.astype(BF16)
    y = ALPHA * x
    off = 0
    for width in chunks:
        sl = slice(off, off + width)
        f1 = jnp.dot(xb, wg_ref[:, sl], preferred_element_type=F32)
        f3 = jnp.dot(xb, wu_ref[:, sl], preferred_element_type=F32)
        hc = (_silu(f1) * f3).astype(BF16)
        y = y + jnp.dot(hc, wd_ref[sl, :], preferred_element_type=F32)
        off += width
    o_ref[...] = _layer_norm(y, g_ref[...], b_ref[...])


def _ffn(x, wg, wu, wd, g, b, tm, chunks):
    assert sum(chunks) == D_FF
    t = x.shape[0]
    row = lambda i: (i, 0)
    const = lambda i: (0, 0)
    single = pl.Buffered(1)
    return pl.pallas_call(
        functools.partial(_ffn_body, chunks=chunks),
        grid=(t // tm,),
        in_specs=[pl.BlockSpec((tm, D_MODEL), row),
                  pl.BlockSpec((D_MODEL, D_FF), const, pipeline_mode=single),
                  pl.BlockSpec((D_MODEL, D_FF), const, pipeline_mode=single),
                  pl.BlockSpec((D_FF, D_MODEL), const, pipeline_mode=single),
                  pl.BlockSpec((1, D_MODEL), const), pl.BlockSpec((1, D_MODEL), const)],
        out_specs=pl.BlockSpec((tm, D_MODEL), row),
        out_shape=jax.ShapeDtypeStruct((t, D_MODEL), F32),
        compiler_params=_cparams(("parallel",)),
        name="ffn",
    )(x, wg, wu, wd, g, b)


def _prep_layer_weights(w_in, conv_w, a_log, dt_bias, gdn_norm_g, w_oa, q_norm_g, w_uq, kv_norm_g,
                        w_ukv, w_ob, w_out, ln1_g, ln1_b, w_gu, w_down, ln2_g, ln2_b):
    seg = lambda i: w_in[:, _OFFS[i]:_OFFS[i + 1]]
    qkv, z, a, b, c_q, c_kv, k_r, g_a, g_b = (seg(i) for i in range(9))
    half = ROPE // 2
    k_r_rot = jnp.concatenate([-k_r[:, half:], k_r[:, :half]], axis=1)
    pad = jnp.zeros((D_MODEL, LANES - 2 * H_A), w_in.dtype)
    w_proj = jnp.concatenate([qkv, z, g_a, g_b, c_kv, k_r, k_r, c_q, k_r_rot, k_r_rot, a, b, pad],
                             axis=1).astype(BF16)
    lane_pad = jnp.zeros((LANES - H_A,), F32)
    al_lane = jnp.concatenate([a_log.astype(F32), lane_pad]).reshape(1, LANES)
    dt_lane = jnp.concatenate([dt_bias.astype(F32), lane_pad]).reshape(1, LANES)
    uq = w_uq.reshape(Q_RANK, H_B, NOPE + ROPE)
    uq_nope = uq[:, :, :NOPE].reshape(Q_RANK, H_B * NOPE)
    uq_rope = uq[:, :, NOPE:]
    uq_rot = jnp.concatenate([-uq_rope[:, :, half:], uq_rope[:, :, :half]], axis=2)
    w_uq_ext = jnp.concatenate([uq_nope, uq_rope.reshape(Q_RANK, H_B * ROPE),
                                uq_rot.reshape(Q_RANK, H_B * ROPE)], axis=1).astype(BF16)
    ukv = w_ukv.reshape(KV_RANK, H_B, NOPE + V_B)
    w_uk_t = jnp.transpose(ukv[:, :, :NOPE], (1, 2, 0)).astype(BF16)
    w_uv = jnp.transpose(ukv[:, :, NOPE:], (1, 0, 2)).astype(BF16)
    w_uv_t = jnp.transpose(ukv[:, :, NOPE:], (1, 2, 0)).astype(BF16)
    return dict(
        w_proj=w_proj, conv_w=conv_w.astype(F32), al_lane=al_lane, dt_lane=dt_lane,
        gn=gdn_norm_g.reshape(1, DV_A).astype(F32), w_oa=w_oa.astype(BF16),
        qg=q_norm_g.reshape(1, Q_RANK).astype(F32), w_uq=w_uq_ext, w_uk_t=w_uk_t, w_uv=w_uv, w_uv_t=w_uv_t,
        kvg=kv_norm_g.reshape(1, KV_RANK).astype(F32), w_ob=w_ob.astype(BF16),
        w_out=w_out.astype(BF16), ln1_g=ln1_g.reshape(1, D_MODEL), ln1_b=ln1_b.reshape(1, D_MODEL),
        w_g=w_gu[:, :D_FF].astype(BF16), w_u=w_gu[:, D_FF:].astype(BF16), w_down=w_down.astype(BF16),
        ln2_g=ln2_g.reshape(1, D_MODEL), ln2_b=ln2_b.reshape(1, D_MODEL))


def _rope_tables(past, seq, reps):
    half = ROPE // 2
    inv = ROPE_THETA ** (-jnp.arange(half, dtype=F32) / half)
    ang = (past + jnp.arange(seq)).astype(F32)[:, None] * inv[None, :]
    cos = jnp.tile(jnp.cos(ang), (reps, LANES // half))
    sin = jnp.tile(jnp.sin(ang), (reps, LANES // half))
    return cos, sin


TM_PROJ, TN_PROJ, TM_GATES, TM_MLA, TM_MERGE, TM_FFN = 1024, 1024, 1024, 512, 512, 512
FFN_CHUNKS = (768, 768, 768, D_FF - 3 * 768)
TQ_ATTN, TK_ATTN, CW_ATTN, TK_DEC = 512, 512, 512, 1024
TB_GDN, HB_GDN = 2 * GDN_CHUNK, H_A


def _trunk_layer(x, conv_state, s0, caches, wl, bsz, seq):
    decode = caches is not None
    t_rows = bsz * seq
    h1 = _proj_in(x, wl["w_proj"], min(TM_PROJ, t_rows), TN_PROJ)
    seq_pad = -(-seq // GDN_CHUNK) * GDN_CHUNK
    if seq_pad == seq:
        gates = _gates(h1, COL_AB // LANES, wl["al_lane"], wl["dt_lane"], min(TM_GATES, t_rows), seq, seq)
    else:
        ab = h1[:, COL_AB:COL_AB + LANES].reshape(bsz, seq, LANES)
        ab = jnp.pad(ab, ((0, 0), (0, seq_pad - seq), (0, 0))).reshape(bsz * seq_pad, LANES)
        gates = _gates(ab, 0, wl["al_lane"], wl["dt_lane"], min(TM_GATES, bsz * seq_pad), seq_pad, seq)
    o_a, s_new = _gdn(h1, gates.reshape(2 * H_A, 1, bsz * seq_pad), wl["conv_w"], conv_state, s0, wl["gn"],
                      bsz, seq, min(TB_GDN, seq_pad), HB_GDN)
    if decode:
        cache_ckv, cache_kr, layer = caches
        tm_mla = t_rows if t_rows <= TM_MLA else seq
        cos_t, sin_t = _rope_tables(cache_ckv.shape[2], seq, tm_mla // seq)
        q, kc, ckv_new, kr_new = _mla_pre(h1, cos_t, sin_t, wl["qg"], wl["w_uq"], wl["w_uk_t"], wl["kvg"],
                                          bsz, seq, tm_mla, False)
        o_b = _attn_dec(q, cache_ckv, cache_kr, layer, kc, wl["w_uv"], bsz, seq, TK_DEC)
    else:
        cos_t, sin_t = _rope_tables(0, seq, 1)
        q, kc, ckv_new, kr_new, vt = _mla_pre(h1, cos_t, sin_t, wl["qg"], wl["w_uq"], wl["w_uk_t"],
                                              wl["kvg"], bsz, seq, TM_MLA, True)
        o_b = _attn(q, kc.reshape(bsz, seq, KC_W), vt, wl["w_uv_t"], bsz, seq, TQ_ATTN, TK_ATTN, CW_ATTN)
        o_b = o_b.reshape(t_rows, H_B * V_B)
    x1 = _merge(o_a, o_b, h1, x, wl["w_oa"], wl["w_ob"], wl["w_out"], wl["ln1_g"], wl["ln1_b"],
                min(TM_MERGE, t_rows))
    x2 = _ffn(x1, wl["w_g"], wl["w_u"], wl["w_down"], wl["ln2_g"], wl["ln2_b"], min(TM_FFN, t_rows), FFN_CHUNKS)
    conv_new = h1.reshape(bsz, seq, N_PROJ)[:, seq - (CONV_W - 1):, COL_QKV:COL_QKV + C_QKV]
    return (x2, conv_new, s_new, ckv_new.reshape(bsz, seq, KV_RANK), kr_new.reshape(bsz, seq, ROPE))


def kernel(x_prompt, x_sample, state_conv, state_gdn, cache_ckv, cache_krope, w_in, conv_w, a_log, dt_bias, gdn_norm_g, w_oa, q_norm_g, w_uq, kv_norm_g, w_ukv, w_ob, w_out, ln1_g, ln1_b, w_gu, w_down, ln2_g, ln2_b):
    bp, lp, _ = x_prompt.shape
    bs, ls, _ = x_sample.shape
    yp = x_prompt.reshape(bp * lp, D_MODEL)
    ys = x_sample.reshape(bs * ls, D_MODEL)
    zero_conv = jnp.zeros((bp, CONV_W - 1, C_QKV), F32)
    zero_s = jnp.zeros((bp, H_A, DK_A, DV_A), F32)
    outs_p, outs_s = [], []
    for l in range(w_in.shape[0]):
        wl = _prep_layer_weights(w_in[l], conv_w[l], a_log[l], dt_bias[l], gdn_norm_g[l], w_oa[l],
                                 q_norm_g[l], w_uq[l], kv_norm_g[l], w_ukv[l], w_ob[l], w_out[l],
                                 ln1_g[l], ln1_b[l], w_gu[l], w_down[l], ln2_g[l], ln2_b[l])
        yp, *rest_p = _trunk_layer(yp, zero_conv, zero_s, None, wl, bp, lp)
        ys, *rest_s = _trunk_layer(ys, state_conv[l], state_gdn[l], (cache_ckv, cache_krope, l), wl, bs, ls)
        outs_p.append(rest_p)
        outs_s.append(rest_s)
    stack = lambda outs, i: jnp.stack([o[i] for o in outs])
    return (yp.reshape(bp, lp, D_MODEL), ys.reshape(bs, ls, D_MODEL),
            stack(outs_p, 0), stack(outs_p, 1), stack(outs_p, 2), stack(outs_p, 3),
            stack(outs_s, 0), stack(outs_s, 1), stack(outs_s, 2), stack(outs_s, 3))
```

```python
import functools

import numpy as np
import jax
import jax.numpy as jnp
from jax import lax
from jax.experimental import pallas as pl
from jax.experimental.pallas import tpu as pltpu

F32 = jnp.float32
BF16 = jnp.bfloat16

D_MODEL = 1024
DEPTH = 2
CHUNK = 64
H_A = 8
DK_A = 128
DV_A = 128
QK_A = H_A * DK_A
V_A = H_A * DV_A
C_QKV = 2 * QK_A + V_A
CONV_W = 4
H_B = 8
NOPE = 128
ROPE = 64
V_B = 128
Q_RANK = 384
KV_RANK = 256
ROPE_THETA = 10000.0
ATTN_SCALE = (NOPE + ROPE) ** -0.5
LOG2_E = float(np.log2(np.e))
Q_SCALE = ATTN_SCALE * LOG2_E
D_FF = -(-8 * D_MODEL // (3 * 256)) * 256
ALPHA = (2 * DEPTH) ** 0.25
EPS = 1e-6
_SIZES = (C_QKV, V_A, H_A, H_A, Q_RANK, KV_RANK, ROPE, D_MODEL, D_MODEL)
_OFFS = tuple(int(v) for v in np.cumsum((0,) + _SIZES))

LANES = 128
VMEM_LIMIT = 56 * 1024 * 1024

COL_QKV = 0
COL_Z = COL_QKV + C_QKV
COL_GA = COL_Z + V_A
COL_GB = COL_GA + D_MODEL
COL_CKV = COL_GB + D_MODEL
COL_KRA = COL_CKV + KV_RANK
COL_CQ = COL_KRA + LANES
COL_KRB = COL_CQ + Q_RANK
COL_AB = COL_KRB + LANES
N_PROJ = COL_AB + LANES
KC_W = KV_RANK + LANES

GDN_CHUNK = 128
MAX_STALE_EXCESS = 64.0


def _cparams(sem):
    return pltpu.CompilerParams(dimension_semantics=sem, vmem_limit_bytes=VMEM_LIMIT)


def _sigmoid(x):
    return jax.nn.sigmoid(x)


def _silu(x):
    return x * jax.nn.sigmoid(x)


def _mm(a, b):
    return jnp.dot(a.astype(BF16), b.astype(BF16), preferred_element_type=F32)


def _mm_nt(a, b):
    return lax.dot_general(a.astype(BF16), b.astype(BF16), (((1,), (1,)), ((), ())),
                           preferred_element_type=F32)


def _mm_tn(a, b):
    return lax.dot_general(a.astype(BF16), b.astype(BF16), (((0,), (0,)), ((), ())),
                           preferred_element_type=F32)


def _proj_body(x_ref, w_ref, o_ref, xb_ref):
    @pl.when(pl.program_id(1) == 0)
    def _():
        xb_ref[...] = x_ref[...].astype(BF16)

    o_ref[...] = jnp.dot(xb_ref[...], w_ref[...], preferred_element_type=F32)


def _proj_in(x, w, tm, tn):
    t, k = x.shape
    n = w.shape[1]
    return pl.pallas_call(
        _proj_body,
        grid=(t // tm, n // tn),
        in_specs=[pl.BlockSpec((tm, k), lambda i, j: (i, 0)),
                  pl.BlockSpec((k, tn), lambda i, j: (0, j))],
        out_specs=pl.BlockSpec((tm, tn), lambda i, j: (i, j)),
        out_shape=jax.ShapeDtypeStruct((t, n), F32),
        scratch_shapes=[pltpu.VMEM((tm, k), BF16)],
        compiler_params=_cparams(("parallel", "arbitrary")),
        name="proj_in",
    )(x, w)


def _gates_body(ab_ref, al_ref, dt_ref, o_ref, *, tm, l_pad, l_valid):
    x = ab_ref[...]
    lane = lax.broadcasted_iota(jnp.int32, x.shape, 1)
    xa = x + dt_ref[...]
    sp = jnp.maximum(xa, 0.0) + jnp.log1p(jnp.exp(-jnp.abs(xa)))
    g = -jnp.exp(al_ref[...]) * sp * LOG2_E
    y = jnp.where(lane < H_A, g, _sigmoid(x))
    yt = y.T[0:2 * H_A, :]
    if l_valid < l_pad:
        col = lax.broadcasted_iota(jnp.int32, yt.shape, 1) + pl.program_id(0) * tm
        yt = jnp.where(col % l_pad < l_valid, yt, 0.0)
    r = lax.broadcasted_iota(jnp.int32, (GDN_CHUNK, GDN_CHUNK), 0)
    c = lax.broadcasted_iota(jnp.int32, (GDN_CHUNK, GDN_CHUNK), 1)
    tri = jnp.where(r <= c, 1.0, 0.0).astype(F32)
    for s in range(tm // GDN_CHUNK):
        sl = slice(s * GDN_CHUNK, (s + 1) * GDN_CHUNK)
        o_ref[0:H_A, sl] = jnp.dot(yt[0:H_A, sl], tri, precision=lax.Precision.HIGHEST,
                                   preferred_element_type=F32)
    o_ref[H_A:2 * H_A, :] = yt[H_A:2 * H_A, :]


def _gates(src, col_block, al_lane, dt_lane, tm, l_pad, l_valid):
    t = src.shape[0]
    body = functools.partial(_gates_body, tm=tm, l_pad=l_pad, l_valid=l_valid)
    return pl.pallas_call(
        body,
        grid=(t // tm,),
        in_specs=[pl.BlockSpec((tm, LANES), lambda i: (i, col_block)),
                  pl.BlockSpec((1, LANES), lambda i: (0, 0)),
                  pl.BlockSpec((1, LANES), lambda i: (0, 0))],
        out_specs=pl.BlockSpec((2 * H_A, tm), lambda i: (0, i)),
        out_shape=jax.ShapeDtypeStruct((2 * H_A, t), F32),
        compiler_params=_cparams(("parallel",)),
        name="gates",
    )(src, al_lane, dt_lane)


def _gdn_body(q_ref, k_ref, v_ref, z_ref, gc_ref, bt_ref, cwq_ref, cwk_ref, cwv_ref,
              csq_ref, csk_ref, csv_ref, s0_ref, gn_ref, o_ref, so_ref,
              xq_ref, xk_ref, xv_ref, s_ref, *, tb, tr, nt, hb):
    t = pl.program_id(2)
    hist = CONV_W - 1
    base = 8

    @pl.when(t == 0)
    def _():
        s_ref[...] = s0_ref[0]
        xq_ref[base - hist:base, :] = csq_ref[0]
        xk_ref[base - hist:base, :] = csk_ref[0]
        xv_ref[base - hist:base, :] = csv_ref[0]

    for xs_ref, raw_ref in ((xq_ref, q_ref), (xk_ref, k_ref), (xv_ref, v_ref)):
        xs_ref[base:base + tr, :] = raw_ref[...]
        if tr < tb:
            xs_ref[base + tr:base + tb, :] = jnp.zeros((tb - tr, xs_ref.shape[1]), F32)

    def conv(xs_ref, cw_ref, ln):
        slab = xs_ref[:, ln]
        y = pltpu.roll(slab, hist, 0)[base:base + tb] * cw_ref[0:1, ln]
        for j in range(1, CONV_W):
            tap = slab if j == hist else pltpu.roll(slab, hist - j, 0)
            y = y + tap[base:base + tb] * cw_ref[j:j + 1, ln]
        return _silu(y)

    cc = GDN_CHUNK
    ii = lax.broadcasted_iota(jnp.int32, (cc, cc), 0)
    jj = lax.broadcasted_iota(jnp.int32, (cc, cc), 1)
    strict = ii > jj
    eye = jnp.where(ii == jj, 1.0, 0.0).astype(F32)
    heads = range(hb)
    lanes = [slice(hd * LANES, (hd + 1) * LANES) for hd in heads]
    units = [(hd, c) for hd in heads for c in range(tb // cc)]
    rows = {u: slice(u[1] * cc, (u[1] + 1) * cc) for u in units}
    q = [conv(xq_ref, cwq_ref, ln) for ln in lanes]
    k = [conv(xk_ref, cwk_ref, ln) for ln in lanes]
    v = [conv(xv_ref, cwv_ref, ln) for ln in lanes]
    q = [x * (lax.rsqrt(jnp.sum(x * x, -1, keepdims=True) + EPS) * (DK_A ** -0.5)) for x in q]
    k = [x * lax.rsqrt(jnp.sum(x * x, -1, keepdims=True) + EPS) for x in k]
    qc = {u: q[u[0]][rows[u]] for u in units}
    kc = {u: k[u[0]][rows[u]] for u in units}
    vc = {u: v[u[0]][rows[u]] for u in units}
    m_row = {u: jnp.broadcast_to(gc_ref[u[0], :, rows[u]], (cc, cc)) for u in units}
    m_col = {u: m_row[u].T for u in units}
    b_col = {u: jnp.broadcast_to(bt_ref[u[0], :, rows[u]], (cc, cc)).T for u in units}
    kq = {u: _mm_nt(jnp.concatenate([kc[u], qc[u]], axis=0), kc[u]) for u in units}
    diff = {u: m_col[u] - m_row[u] for u in units}
    dec = {u: jnp.exp2(jnp.where(strict, diff[u], -jnp.inf)) for u in units}
    a_mat = {u: b_col[u] * kq[u][0:cc] * dec[u] for u in units}
    qk = {u: kq[u][cc:2 * cc] * (dec[u] + eye) for u in units}
    e_g = {u: jnp.exp2(m_col[u]) for u in units}
    x_inv = {u: eye - jnp.where((ii >> 1) == (jj >> 1), a_mat[u], 0.0) for u in units}
    sft = 1
    while (1 << sft) < cc:
        off = ((ii >> (sft + 1)) == (jj >> (sft + 1))) & ((ii >> sft) != (jj >> sft))
        lx = {u: _mm(jnp.where(off, a_mat[u], 0.0), x_inv[u]) for u in units}
        x_inv = {u: x_inv[u] - _mm(x_inv[u], lx[u]) for u in units}
        sft += 1
    uw = {u: _mm(x_inv[u], jnp.concatenate([vc[u] * b_col[u], kc[u] * (b_col[u] * e_g[u])], axis=1))
          for u in units}
    wq = {u: jnp.concatenate([uw[u][:, DV_A:], qc[u] * e_g[u]], axis=0) for u in units}
    g_last = {u: m_col[u][cc - 1:cc, :] for u in units}
    kd = {u: kc[u] * jnp.exp2(g_last[u] - m_col[u]) for u in units}
    s_state = [s_ref[hd] for hd in heads]
    for c in range(tb // cc):
        ws = [_mm(wq[(hd, c)], s_state[hd]) for hd in heads]
        v_new = [uw[(hd, c)][:, 0:DV_A] - ws[hd][0:cc] for hd in heads]
        s_state = [s_state[hd] * jnp.exp2(g_last[(hd, c)]) + _mm_tn(kd[(hd, c)], v_new[hd]) for hd in heads]
        o = [ws[hd][cc:2 * cc] + _mm(qk[(hd, c)], v_new[hd]) for hd in heads]
        o = [x * lax.rsqrt(jnp.mean(x * x, -1, keepdims=True) + EPS) * gn_ref[...] for x in o]
        nr = min(cc, tr - c * cc)
        r = slice(c * cc, c * cc + nr)
        for hd in heads:
            o_ref[r, lanes[hd]] = (o[hd][0:nr] * _silu(z_ref[r, lanes[hd]])).astype(o_ref.dtype)
    for hd in heads:
        s_ref[hd] = s_state[hd]

    xq_ref[base - hist:base, :] = xq_ref[base + tb - hist:base + tb, :]
    xk_ref[base - hist:base, :] = xk_ref[base + tb - hist:base + tb, :]
    xv_ref[base - hist:base, :] = xv_ref[base + tb - hist:base + tb, :]

    @pl.when(t == nt - 1)
    def _():
        so_ref[0] = s_ref[...]


def _gdn(h1, gates3, conv_w, conv_state, s0, gn, bsz, seq, tb, hb):
    nt = -(-seq // tb)
    tr = min(tb, seq)
    assert seq % tr == 0 and (tr == tb or (nt == 1 and tb == GDN_CHUNK))
    t_rows = bsz * seq
    body = functools.partial(_gdn_body, tb=tb, tr=tr, nt=nt, hb=hb)
    width = hb * LANES
    ng = H_A // hb

    def rows(seg):
        return pl.BlockSpec((tr, width), lambda b, g, t: (b * nt + t, seg * ng + g))

    def cw(seg):
        return pl.BlockSpec((CONV_W, width), lambda b, g, t: (0, seg * ng + g))

    def cs(seg):
        return pl.BlockSpec((1, CONV_W - 1, width), lambda b, g, t: (b, 0, seg * ng + g))

    return pl.pallas_call(
        body,
        grid=(bsz, ng, nt),
        in_specs=[rows(0), rows(1), rows(2), rows(COL_Z // QK_A),
                  pl.BlockSpec((hb, 1, tb), lambda b, g, t: (g, 0, b * nt + t)),
                  pl.BlockSpec((hb, 1, tb), lambda b, g, t: (ng + g, 0, b * nt + t)),
                  cw(0), cw(1), cw(2), cs(0), cs(1), cs(2),
                  pl.BlockSpec((1, hb, DK_A, DV_A), lambda b, g, t: (b, g, 0, 0)),
                  pl.BlockSpec((1, DV_A), lambda b, g, t: (0, 0))],
        out_specs=[pl.BlockSpec((tr, width), lambda b, g, t: (b * nt + t, g)),
                   pl.BlockSpec((1, hb, DK_A, DV_A), lambda b, g, t: (b, g, 0, 0))],
        out_shape=[jax.ShapeDtypeStruct((t_rows, V_A), BF16),
                   jax.ShapeDtypeStruct((bsz, H_A, DK_A, DV_A), F32)],
        scratch_shapes=[pltpu.VMEM((tb + 8, width), F32)] * 3 + [pltpu.VMEM((hb, DK_A, DV_A), F32)],
        compiler_params=_cparams(("parallel", "parallel", "arbitrary")),
        name="gdn",
    )(h1, h1, h1, h1, gates3, gates3, conv_w, conv_w, conv_w,
      conv_state, conv_state, conv_state, s0, gn)


def _mla_pre_body(cq_ref, ckv_ref, kra_ref, krb_ref, cos_ref, sin_ref, qg_ref, wuq_ref, wuk_ref,
                  kvg_ref, *refs, n_carried):
    q_ref, kc_ref, ckvo_ref, kro_ref, *maybe_vt_ref = refs[n_carried:]
    cq = cq_ref[...]
    cqn = cq * lax.rsqrt(jnp.mean(cq * cq, -1, keepdims=True) + EPS) * qg_ref[...]
    qf = jnp.dot(cqn.astype(BF16), wuq_ref[...], preferred_element_type=F32)
    cos_k = cos_ref[...]
    sin_k = sin_ref[...]
    reps = H_B * ROPE // LANES
    cos_t = jnp.concatenate([cos_k] * reps, axis=1)
    sin_t = jnp.concatenate([sin_k] * reps, axis=1)
    n_nope = H_B * NOPE
    n_rope = H_B * ROPE
    qr = (qf[:, n_nope:n_nope + n_rope] * cos_t + qf[:, n_nope + n_rope:] * sin_t) * Q_SCALE
    lane = lax.broadcasted_iota(jnp.int32, (cq.shape[0], LANES), 1)
    for h in range(H_B):
        ql = jnp.dot(qf[:, h * NOPE:(h + 1) * NOPE].astype(BF16), wuk_ref[h],
                     preferred_element_type=F32) * Q_SCALE
        blk = qr[:, (h // 2) * LANES:(h // 2 + 1) * LANES]
        keep = (lane < ROPE) if h % 2 == 0 else (lane >= ROPE)
        q_ref[h, :, 0:KV_RANK] = ql.astype(BF16)
        q_ref[h, :, KV_RANK:KC_W] = jnp.where(keep, blk, 0.0).astype(BF16)
    ckv = ckv_ref[...]
    ckvn = ckv * lax.rsqrt(jnp.mean(ckv * ckv, -1, keepdims=True) + EPS) * kvg_ref[...]
    kr2 = kra_ref[...] * cos_k + krb_ref[...] * sin_k
    ckvo_ref[...] = ckvn
    kro_ref[...] = kr2[:, 0:ROPE]
    kc_ref[:, 0:KV_RANK] = ckvn.astype(BF16)
    kc_ref[:, KV_RANK:KC_W] = kr2.astype(BF16)
    if maybe_vt_ref:
        maybe_vt_ref[0][0] = ckvn.T.astype(BF16)


def _mla_pre(h1, cos_t, sin_t, qg, wuq, wuk, kvg, bsz, seq, tm, with_vt, layer, depth, carried):
    t_rows = bsz * seq
    ntab = cos_t.shape[0] // tm
    npb = max(seq // tm, 1)
    const2 = lambda i: (0, 0)
    out_specs = [pl.BlockSpec((H_B, tm, KC_W), lambda i: (0, i, 0)),
                 pl.BlockSpec((tm, KC_W), lambda i: (i, 0)),
                 pl.BlockSpec((None, tm, KV_RANK), lambda i: (layer, i, 0)),
                 pl.BlockSpec((None, tm, ROPE), lambda i: (layer, i, 0))]
    out_shape = [jax.ShapeDtypeStruct((H_B, t_rows, KC_W), BF16),
                 jax.ShapeDtypeStruct((t_rows, KC_W), BF16),
                 jax.ShapeDtypeStruct((depth, t_rows, KV_RANK), F32),
                 jax.ShapeDtypeStruct((depth, t_rows, ROPE), F32)]
    if with_vt:
        out_specs.append(pl.BlockSpec((1, KV_RANK, tm), lambda i: (i // npb, 0, i % npb)))
        out_shape.append(jax.ShapeDtypeStruct((bsz, KV_RANK, seq), BF16))
    n_in = 10
    return pl.pallas_call(
        functools.partial(_mla_pre_body, n_carried=len(carried)),
        input_output_aliases={n_in + k: 2 + k for k in range(len(carried))},
        grid=(t_rows // tm,),
        in_specs=[pl.BlockSpec((tm, Q_RANK), lambda i: (i, COL_CQ // Q_RANK)),
                  pl.BlockSpec((tm, KV_RANK), lambda i: (i, COL_CKV // KV_RANK)),
                  pl.BlockSpec((tm, LANES), lambda i: (i, COL_KRA // LANES)),
                  pl.BlockSpec((tm, LANES), lambda i: (i, COL_KRB // LANES)),
                  pl.BlockSpec((tm, LANES), lambda i: (i % ntab, 0)),
                  pl.BlockSpec((tm, LANES), lambda i: (i % ntab, 0)),
                  pl.BlockSpec((1, Q_RANK), const2),
                  pl.BlockSpec(wuq.shape, const2),
                  pl.BlockSpec(wuk.shape, lambda i: (0, 0, 0)),
                  pl.BlockSpec((1, KV_RANK), const2)] + [pl.BlockSpec(memory_space=pl.ANY)] * len(carried),
        out_specs=out_specs,
        out_shape=out_shape,
        compiler_params=_cparams(("parallel",)),
        name="mla_pre",
    )(h1, h1, h1, h1, cos_t, sin_t, qg, wuq, wuk, kvg, *carried)


def _attn_body(q_ref, k_ref, vt_ref, wuv_ref, o_ref, m_ref, l_ref, acc_ref, ex_ref, *, tq, tk, cw, lk):
    i = pl.program_id(1)
    shift = CHUNK.bit_length() - 1

    def update(j, masked, lazy, opening=False):
        k0 = pl.multiple_of(j * tk, tk)
        kt = k_ref[0, pl.ds(k0, tk), :]
        vt = vt_ref[0, :, pl.ds(k0, tk)]
        bias = None
        if masked:
            kpos = k0 + lax.broadcasted_iota(jnp.int32, (tk, tq), 0)
            qpos = i * tq + lax.broadcasted_iota(jnp.int32, (tk, tq), 1)
            bias = jnp.where((kpos >> shift) <= (qpos >> shift), 0.0, -jnp.inf).astype(F32)

        units = [(h, slice(c, c + cw)) for h in range(H_B) for c in range(0, tq, cw)]

        def scores(u):
            h, cs = u
            s = lax.dot_general(kt, q_ref[h, cs, :], (((1,), (1,)), ((), ())),
                                preferred_element_type=F32)
            return s if bias is None else s + bias[:, cs]

        def softmax(u, s):
            h, cs = u
            c_max = jnp.max(s, 0, keepdims=True)
            if opening:
                m_prev = jnp.zeros_like(c_max)
                m_new = c_max
                gap = jnp.abs(c_max)
            else:
                m_prev = m_ref[h, :, cs]
                m_new = jnp.maximum(m_prev, c_max)
                gap = c_max - m_prev
            alpha = jnp.exp2(m_prev - m_new)
            if lazy:
                p = jnp.exp2(s - m_prev)
                ex_ref[h, :, cs] = jnp.maximum(ex_ref[h, :, cs], gap)
                l_ref[h, :, cs] = alpha * (l_ref[h, :, cs] + jnp.sum(p, 0, keepdims=True))
            else:
                p = jnp.exp2(s - m_new)
                l_ref[h, :, cs] = alpha * l_ref[h, :, cs] + jnp.sum(p, 0, keepdims=True)
            m_ref[h, :, cs] = m_new
            return p.astype(BF16), alpha

        def accumulate(u, alpha, pv):
            h, cs = u
            if lazy:
                acc_ref[h, :, cs] = alpha * (acc_ref[h, :, cs] + pv)
            else:
                acc_ref[h, :, cs] = alpha * acc_ref[h, :, cs] + pv

        n_units = len(units)
        s_q = {0: scores(units[0])}
        if n_units > 1:
            s_q[1] = scores(units[1])
        p0, alpha = softmax(units[0], s_q.pop(0))
        alphas = {0: alpha}
        pvs = {0: jnp.dot(vt, p0, preferred_element_type=F32)}
        for n in range(n_units):
            if n + 2 < n_units:
                s_q[n + 2] = scores(units[n + 2])
            if n + 1 < n_units:
                p_next, alphas[n + 1] = softmax(units[n + 1], s_q.pop(n + 1))
            accumulate(units[n], alphas.pop(n), pvs.pop(n))
            if n + 1 < n_units:
                pvs[n + 1] = jnp.dot(vt, p_next, preferred_element_type=F32)

    n_full = ((((i * tq) >> shift) + 1) << shift) // tk
    n_all = jnp.minimum(((((i * tq + tq - 1) >> shift) + 1) << shift) + tk - 1, lk + tk - 1) // tk

    def tile_loop(lo, hi, masked, lazy, opening=False):
        def step(j, carry):
            update(j, masked, lazy, opening)
            return carry
        lax.fori_loop(lo, hi, step, 0)

    def attempt(a, redo):
        first = a == 0
        run = jnp.logical_or(first, redo > 0)

        @pl.when(run)
        def _():
            m_ref[...] = jnp.full(m_ref.shape, -jnp.inf, F32)
            l_ref[...] = jnp.zeros(l_ref.shape, F32)
            acc_ref[...] = jnp.zeros(acc_ref.shape, F32)
            ex_ref[...] = jnp.full(ex_ref.shape, -jnp.inf, F32)

        tile_loop(0, jnp.where(first, 0, n_all) * run.astype(jnp.int32), True, False)
        tile_loop(0, jnp.where(first, 1, 0), True, True, True)
        tile_loop(1, jnp.where(first, n_full, 0), False, True)
        tile_loop(jnp.maximum(n_full, 1), jnp.where(first, n_all, 0), True, True)
        excess = jnp.max(ex_ref[...])
        return jnp.where(first, (excess > MAX_STALE_EXCESS).astype(jnp.int32), 0)

    lax.fori_loop(0, 2, attempt, jnp.int32(0))

    for h in range(H_B):
        o_t = (acc_ref[h] * (1.0 / l_ref[h])).astype(BF16)
        ob_t = jnp.dot(wuv_ref[h], o_t, preferred_element_type=F32)
        o_ref[0, :, h * V_B:(h + 1) * V_B] = ob_t.T.astype(o_ref.dtype)


def _attn(q, kc, vt, wuv_t, bsz, seq, tq, tk, cw):
    assert seq % tq == 0 and seq % tk == 0
    nq = seq // tq
    body = functools.partial(_attn_body, tq=tq, tk=tk, cw=cw, lk=seq)
    return pl.pallas_call(
        body,
        grid=(bsz, nq),
        in_specs=[pl.BlockSpec((H_B, tq, KC_W), lambda b, i: (0, b * nq + i, 0)),
                  pl.BlockSpec((1, seq, KC_W), lambda b, i: (b, 0, 0)),
                  pl.BlockSpec((1, KV_RANK, seq), lambda b, i: (b, 0, 0)),
                  pl.BlockSpec(wuv_t.shape, lambda b, i: (0, 0, 0))],
        out_specs=pl.BlockSpec((1, tq, H_B * V_B), lambda b, i: (b, i, 0)),
        out_shape=jax.ShapeDtypeStruct((bsz, seq, H_B * V_B), BF16),
        scratch_shapes=[pltpu.VMEM((H_B, 1, tq), F32), pltpu.VMEM((H_B, 1, tq), F32),
                        pltpu.VMEM((H_B, KV_RANK, tq), F32), pltpu.VMEM((H_B, 1, tq), F32)],
        compiler_params=_cparams(("parallel", "arbitrary")),
        name="attn",
    )(q, kc, vt, wuv_t)


def _attn_dec_body(q_ref, ckv_ref, krt_ref, kn_ref, wuv_ref, o_ref, m_ref, l_ref, acc_ref,
                   *, tq, past, n_past):
    j = pl.program_id(1)
    rows = H_B * tq
    nt_dims = (((1,), (1,)), ((), ()))
    q2 = q_ref[...].reshape(rows, KC_W)

    @pl.when(j == 0)
    def _():
        m_ref[...] = jnp.full(m_ref.shape, -jnp.inf, F32)
        l_ref[...] = jnp.zeros(l_ref.shape, F32)
        acc_ref[...] = jnp.zeros(acc_ref.shape, F32)

    def accumulate(s, v_nat):
        m_prev = m_ref[...]
        m_new = jnp.maximum(m_prev, jnp.max(s, -1, keepdims=True))
        p = jnp.exp2(s - m_new)
        alpha = jnp.exp2(m_prev - m_new)
        l_ref[...] = alpha * l_ref[...] + jnp.sum(p, -1, keepdims=True)
        acc_ref[...] = alpha * acc_ref[...] + jnp.dot(p.astype(BF16), v_nat, preferred_element_type=F32)
        m_ref[...] = m_new

    @pl.when(j < n_past)
    def _():
        k_lat = ckv_ref[0].astype(BF16)
        kr_t = krt_ref[0].astype(BF16)
        kr2_t = jnp.concatenate([kr_t, kr_t], axis=0)
        s = (lax.dot_general(q2[:, 0:KV_RANK], k_lat, nt_dims, preferred_element_type=F32)
             + jnp.dot(q2[:, KV_RANK:KC_W], kr2_t, preferred_element_type=F32))
        accumulate(s, k_lat)

    @pl.when(j == n_past)
    def _():
        kn = kn_ref[...]
        s = lax.dot_general(q2, kn, nt_dims, preferred_element_type=F32)
        shift = CHUNK.bit_length() - 1
        qpos = past + (lax.broadcasted_iota(jnp.int32, s.shape, 0) & (tq - 1))
        kpos = past + lax.broadcasted_iota(jnp.int32, s.shape, 1)
        s = jnp.where((kpos >> shift) <= (qpos >> shift), s, -jnp.inf)
        accumulate(s, kn[:, 0:KV_RANK])
        o = acc_ref[...] * (1.0 / l_ref[...])
        for h in range(H_B):
            oh = o[h * tq:(h + 1) * tq, :].astype(BF16)
            o_ref[:, h * V_B:(h + 1) * V_B] = jnp.dot(
                oh, wuv_ref[h], preferred_element_type=F32).astype(o_ref.dtype)


def _attn_dec(q, cache_ckv, cache_kr_t, layer, kc_new, wuv, bsz, seq, tk):
    past = cache_ckv.shape[2]
    assert past % tk == 0 and past % CHUNK == 0 and seq & (seq - 1) == 0
    n_past = past // tk
    body = functools.partial(_attn_dec_body, tq=seq, past=past, n_past=n_past)
    return pl.pallas_call(
        body,
        grid=(bsz, n_past + 1),
        in_specs=[pl.BlockSpec((H_B, seq, KC_W), lambda b, j: (0, b, 0)),
                  pl.BlockSpec((None, 1, tk, KV_RANK), lambda b, j: (layer, b, jnp.minimum(j, n_past - 1), 0)),
                  pl.BlockSpec((None, 1, ROPE, tk), lambda b, j: (layer, b, 0, jnp.minimum(j, n_past - 1))),
                  pl.BlockSpec((seq, KC_W), lambda b, j: (b, 0)),
                  pl.BlockSpec(wuv.shape, lambda b, j: (0, 0, 0))],
        out_specs=pl.BlockSpec((seq, H_B * V_B), lambda b, j: (b, 0)),
        out_shape=jax.ShapeDtypeStruct((bsz * seq, H_B * V_B), BF16),
        scratch_shapes=[pltpu.VMEM((H_B * seq, 1), F32), pltpu.VMEM((H_B * seq, 1), F32),
                        pltpu.VMEM((H_B * seq, KV_RANK), F32)],
        compiler_params=_cparams(("parallel", "arbitrary")),
        name="attn_dec",
    )(q, cache_ckv, cache_kr_t, kc_new, wuv)


def _layer_norm(r, g, b):
    mu = jnp.mean(r, -1, keepdims=True)
    d = r - mu
    var = jnp.mean(d * d, -1, keepdims=True)
    return d * lax.rsqrt(var + EPS) * g + b


def _merge_body(oa_ref, ob_ref, ga_ref, gb_ref, x_ref, woa_ref, wob_ref, wout_ref, g_ref, b_ref, o_ref):
    ya = jnp.dot(oa_ref[...], woa_ref[...], preferred_element_type=F32)
    yb = jnp.dot(ob_ref[...], wob_ref[...], preferred_element_type=F32)
    m = _sigmoid(ga_ref[...]) * ya + _sigmoid(gb_ref[...]) * yb
    r = ALPHA * x_ref[...] + jnp.dot(m.astype(BF16), wout_ref[...], preferred_element_type=F32)
    o_ref[...] = _layer_norm(r, g_ref[...], b_ref[...])


def _merge(oa, ob, h1, x, woa, wob, wout, g, b, tm):
    t = x.shape[0]
    row = lambda i: (i, 0)
    const = lambda i: (0, 0)
    wspec = pl.BlockSpec((D_MODEL, D_MODEL), const)
    return pl.pallas_call(
        _merge_body,
        grid=(t // tm,),
        in_specs=[pl.BlockSpec((tm, V_A), row), pl.BlockSpec((tm, H_B * V_B), row),
                  pl.BlockSpec((tm, D_MODEL), lambda i: (i, COL_GA // D_MODEL)),
                  pl.BlockSpec((tm, D_MODEL), lambda i: (i, COL_GB // D_MODEL)),
                  pl.BlockSpec((tm, D_MODEL), row), wspec, wspec, wspec,
                  pl.BlockSpec((1, D_MODEL), const), pl.BlockSpec((1, D_MODEL), const)],
        out_specs=pl.BlockSpec((tm, D_MODEL), row),
        out_shape=jax.ShapeDtypeStruct((t, D_MODEL), F32),
        compiler_params=_cparams(("parallel",)),
        name="merge",
    )(oa, ob, h1, h1, x, woa, wob, wout, g, b)


def _ffn_body(x_ref, wg_ref, wu_ref, wd_ref, g_ref, b_ref, o_ref, *, chunks):
    x = x_ref[...]
    xb = x.astype(BF16)
    y = ALPHA * x
    off = 0
    for width in chunks:
        sl = slice(off, off + width)
        f1 = jnp.dot(xb, wg_ref[:, sl], preferred_element_type=F32)
        f3 = jnp.dot(xb, wu_ref[:, sl], preferred_element_type=F32)
        hc = (_silu(f1) * f3).astype(BF16)
        y = y + jnp.dot(hc, wd_ref[sl, :], preferred_element_type=F32)
        off += width
    o_ref[...] = _layer_norm(y, g_ref[...], b_ref[...])


def _ffn(x, wg, wu, wd, g, b, tm, chunks):
    assert sum(chunks) == D_FF
    t = x.shape[0]
    row = lambda i: (i, 0)
    const = lambda i: (0, 0)
    single = pl.Buffered(1)
    return pl.pallas_call(
        functools.partial(_ffn_body, chunks=chunks),
        grid=(t // tm,),
        in_specs=[pl.BlockSpec((tm, D_MODEL), row),
                  pl.BlockSpec((D_MODEL, D_FF), const, pipeline_mode=single),
                  pl.BlockSpec((D_MODEL, D_FF), const, pipeline_mode=single),
                  pl.BlockSpec((D_FF, D_MODEL), const, pipeline_mode=single),
                  pl.BlockSpec((1, D_MODEL), const), pl.BlockSpec((1, D_MODEL), const)],
        out_specs=pl.BlockSpec((tm, D_MODEL), row),
        out_shape=jax.ShapeDtypeStruct((t, D_MODEL), F32),
        compiler_params=_cparams(("parallel",)),
        name="ffn",
    )(x, wg, wu, wd, g, b)


def _prep_layer_weights(w_in, conv_w, a_log, dt_bias, gdn_norm_g, w_oa, q_norm_g, w_uq, kv_norm_g,
                        w_ukv, w_ob, w_out, ln1_g, ln1_b, w_gu, w_down, ln2_g, ln2_b):
    seg = lambda i: w_in[:, _OFFS[i]:_OFFS[i + 1]]
    qkv, z, a, b, c_q, c_kv, k_r, g_a, g_b = (seg(i) for i in range(9))
    half = ROPE // 2
    k_r_rot = jnp.concatenate([-k_r[:, half:], k_r[:, :half]], axis=1)
    pad = jnp.zeros((D_MODEL, LANES - 2 * H_A), w_in.dtype)
    w_proj = jnp.concatenate([qkv, z, g_a, g_b, c_kv, k_r, k_r, c_q, k_r_rot, k_r_rot, a, b, pad],
                             axis=1).astype(BF16)
    lane_pad = jnp.zeros((LANES - H_A,), F32)
    al_lane = jnp.concatenate([a_log.astype(F32), lane_pad]).reshape(1, LANES)
    dt_lane = jnp.concatenate([dt_bias.astype(F32), lane_pad]).reshape(1, LANES)
    uq = w_uq.reshape(Q_RANK, H_B, NOPE + ROPE)
    uq_nope = uq[:, :, :NOPE].reshape(Q_RANK, H_B * NOPE)
    uq_rope = uq[:, :, NOPE:]
    uq_rot = jnp.concatenate([-uq_rope[:, :, half:], uq_rope[:, :, :half]], axis=2)
    w_uq_ext = jnp.concatenate([uq_nope, uq_rope.reshape(Q_RANK, H_B * ROPE),
                                uq_rot.reshape(Q_RANK, H_B * ROPE)], axis=1).astype(BF16)
    ukv = w_ukv.reshape(KV_RANK, H_B, NOPE + V_B)
    w_uk_t = jnp.transpose(ukv[:, :, :NOPE], (1, 2, 0)).astype(BF16)
    w_uv = jnp.transpose(ukv[:, :, NOPE:], (1, 0, 2)).astype(BF16)
    w_uv_t = jnp.transpose(ukv[:, :, NOPE:], (1, 2, 0)).astype(BF16)
    return dict(
        w_proj=w_proj, conv_w=conv_w.astype(F32), al_lane=al_lane, dt_lane=dt_lane,
        gn=gdn_norm_g.reshape(1, DV_A).astype(F32), w_oa=w_oa.astype(BF16),
        qg=q_norm_g.reshape(1, Q_RANK).astype(F32), w_uq=w_uq_ext, w_uk_t=w_uk_t, w_uv=w_uv, w_uv_t=w_uv_t,
        kvg=kv_norm_g.reshape(1, KV_RANK).astype(F32), w_ob=w_ob.astype(BF16),
        w_out=w_out.astype(BF16), ln1_g=ln1_g.reshape(1, D_MODEL), ln1_b=ln1_b.reshape(1, D_MODEL),
        w_g=w_gu[:, :D_FF].astype(BF16), w_u=w_gu[:, D_FF:].astype(BF16), w_down=w_down.astype(BF16),
        ln2_g=ln2_g.reshape(1, D_MODEL), ln2_b=ln2_b.reshape(1, D_MODEL))


def _rope_tables(past, seq, reps):
    half = ROPE // 2
    inv = ROPE_THETA ** (-jnp.arange(half, dtype=F32) / half)
    ang = (past + jnp.arange(seq)).astype(F32)[:, None] * inv[None, :]
    cos = jnp.tile(jnp.cos(ang), (reps, LANES // half))
    sin = jnp.tile(jnp.sin(ang), (reps, LANES // half))
    return cos, sin


TM_PROJ, TN_PROJ, TM_GATES, TM_MLA, TM_MERGE, TM_FFN = 1024, N_PROJ // 4, 1024, 512, 512, 512
FFN_CHUNKS = (768, 768, 768, D_FF - 3 * 768)
TQ_ATTN, TK_ATTN, CW_ATTN, TK_DEC = 512, 512, 512, 1024
TB_GDN, HB_GDN = 2 * GDN_CHUNK, H_A


def _trunk_layer(x, conv_state, s0, caches, wl, bsz, seq, layer, depth, carried):
    decode = caches is not None
    t_rows = bsz * seq
    h1 = _proj_in(x, wl["w_proj"], min(TM_PROJ, t_rows), TN_PROJ)
    seq_pad = -(-seq // GDN_CHUNK) * GDN_CHUNK
    if seq_pad == seq:
        gates = _gates(h1, COL_AB // LANES, wl["al_lane"], wl["dt_lane"], min(TM_GATES, t_rows), seq, seq)
    else:
        ab = h1[:, COL_AB:COL_AB + LANES].reshape(bsz, seq, LANES)
        ab = jnp.pad(ab, ((0, 0), (0, seq_pad - seq), (0, 0))).reshape(bsz * seq_pad, LANES)
        gates = _gates(ab, 0, wl["al_lane"], wl["dt_lane"], min(TM_GATES, bsz * seq_pad), seq_pad, seq)
    o_a, s_new = _gdn(h1, gates.reshape(2 * H_A, 1, bsz * seq_pad), wl["conv_w"], conv_state, s0, wl["gn"],
                      bsz, seq, min(TB_GDN, seq_pad), HB_GDN)
    if decode:
        cache_ckv, cache_kr_t = caches
        tm_mla = t_rows if t_rows <= TM_MLA else seq
        cos_t, sin_t = _rope_tables(cache_ckv.shape[2], seq, tm_mla // seq)
        q, kc, ckv_all, kr_all = _mla_pre(h1, cos_t, sin_t, wl["qg"], wl["w_uq"], wl["w_uk_t"], wl["kvg"],
                                          bsz, seq, tm_mla, False, layer, depth, carried)
        o_b = _attn_dec(q, cache_ckv, cache_kr_t, layer, kc, wl["w_uv"], bsz, seq, TK_DEC)
    else:
        cos_t, sin_t = _rope_tables(0, seq, 1)
        q, kc, ckv_all, kr_all, vt = _mla_pre(h1, cos_t, sin_t, wl["qg"], wl["w_uq"], wl["w_uk_t"],
                                              wl["kvg"], bsz, seq, TM_MLA, True, layer, depth, carried)
        o_b = _attn(q, kc.reshape(bsz, seq, KC_W), vt, wl["w_uv_t"], bsz, seq, TQ_ATTN, TK_ATTN, CW_ATTN)
        o_b = o_b.reshape(t_rows, H_B * V_B)
    x1 = _merge(o_a, o_b, h1, x, wl["w_oa"], wl["w_ob"], wl["w_out"], wl["ln1_g"], wl["ln1_b"],
                min(TM_MERGE, t_rows))
    x2 = _ffn(x1, wl["w_g"], wl["w_u"], wl["w_down"], wl["ln2_g"], wl["ln2_b"], min(TM_FFN, t_rows), FFN_CHUNKS)
    conv_new = h1.reshape(bsz, seq, N_PROJ)[:, seq - (CONV_W - 1):, COL_QKV:COL_QKV + C_QKV]
    return x2, conv_new, s_new, (ckv_all, kr_all)


def kernel(x_prompt, x_sample, state_conv, state_gdn, cache_ckv, cache_krope, w_in, conv_w, a_log, dt_bias, gdn_norm_g, w_oa, q_norm_g, w_uq, kv_norm_g, w_ukv, w_ob, w_out, ln1_g, ln1_b, w_gu, w_down, ln2_g, ln2_b):
    bp, lp, _ = x_prompt.shape
    bs, ls, _ = x_sample.shape
    yp = x_prompt.reshape(bp * lp, D_MODEL)
    ys = x_sample.reshape(bs * ls, D_MODEL)
    zero_conv = jnp.zeros((bp, CONV_W - 1, C_QKV), F32)
    zero_s = jnp.zeros((bp, H_A, DK_A, DV_A), F32)
    cache_kr_t = jnp.swapaxes(cache_krope, 2, 3)
    depth = w_in.shape[0]
    conv_p, gdn_p, conv_s, gdn_s = [], [], [], []
    kv_p = kv_s = ()
    for l in range(depth):
        wl = _prep_layer_weights(w_in[l], conv_w[l], a_log[l], dt_bias[l], gdn_norm_g[l], w_oa[l],
                                 q_norm_g[l], w_uq[l], kv_norm_g[l], w_ukv[l], w_ob[l], w_out[l],
                                 ln1_g[l], ln1_b[l], w_gu[l], w_down[l], ln2_g[l], ln2_b[l])
        yp, c_new, g_new, kv_p = _trunk_layer(yp, zero_conv, zero_s, None, wl, bp, lp, l, depth, kv_p)
        conv_p.append(c_new), gdn_p.append(g_new)
        ys, c_new, g_new, kv_s = _trunk_layer(ys, state_conv[l], state_gdn[l], (cache_ckv, cache_kr_t), wl,
                                              bs, ls, l, depth, kv_s)
        conv_s.append(c_new), gdn_s.append(g_new)
    return (yp.reshape(bp, lp, D_MODEL), ys.reshape(bs, ls, D_MODEL),
            jnp.stack(conv_p), jnp.stack(gdn_p),
            kv_p[0].reshape(depth, bp, lp, KV_RANK), kv_p[1].reshape(depth, bp, lp, ROPE),
            jnp.stack(conv_s), jnp.stack(gdn_s),
            kv_s[0].reshape(depth, bs, ls, KV_RANK), kv_s[1].reshape(depth, bs, ls, ROPE))
```

```python
import functools

import numpy as np
import jax
import jax.numpy as jnp
from jax import lax
from jax.experimental import pallas as pl
from jax.experimental.pallas import tpu as pltpu

F32 = jnp.float32
BF16 = jnp.bfloat16

D_MODEL = 1024
DEPTH = 2
CHUNK = 64
H_A = 8
DK_A = 128
DV_A = 128
QK_A = H_A * DK_A
V_A = H_A * DV_A
C_QKV = 2 * QK_A + V_A
CONV_W = 4
H_B = 8
NOPE = 128
ROPE = 64
V_B = 128
Q_RANK = 384
KV_RANK = 256
ROPE_THETA = 10000.0
ATTN_SCALE = (NOPE + ROPE) ** -0.5
LOG2_E = float(np.log2(np.e))
Q_SCALE = ATTN_SCALE * LOG2_E
D_FF = -(-8 * D_MODEL // (3 * 256)) * 256
ALPHA = (2 * DEPTH) ** 0.25
EPS = 1e-6
_SIZES = (C_QKV, V_A, H_A, H_A, Q_RANK, KV_RANK, ROPE, D_MODEL, D_MODEL)
_OFFS = tuple(int(v) for v in np.cumsum((0,) + _SIZES))

LANES = 128
VMEM_LIMIT = 56 * 1024 * 1024

COL_QKV = 0
COL_Z = COL_QKV + C_QKV
COL_GA = COL_Z + V_A
COL_GB = COL_GA + D_MODEL
COL_CKV = COL_GB + D_MODEL
COL_KRA = COL_CKV + KV_RANK
COL_CQ = COL_KRA + LANES
COL_KRB = COL_CQ + Q_RANK
COL_AB = COL_KRB + LANES
N_PROJ = COL_AB + LANES
KC_W = KV_RANK + LANES

GDN_CHUNK = 128
MAX_STALE_EXCESS = 64.0


def _cparams(sem):
    return pltpu.CompilerParams(dimension_semantics=sem, vmem_limit_bytes=VMEM_LIMIT)


def _sigmoid(x):
    return jax.nn.sigmoid(x)


def _silu(x):
    return x * jax.nn.sigmoid(x)


def _mm(a, b):
    return jnp.dot(a.astype(BF16), b.astype(BF16), preferred_element_type=F32)


def _mm_nt(a, b):
    return lax.dot_general(a.astype(BF16), b.astype(BF16), (((1,), (1,)), ((), ())),
                           preferred_element_type=F32)


def _mm_tn(a, b):
    return lax.dot_general(a.astype(BF16), b.astype(BF16), (((0,), (0,)), ((), ())),
                           preferred_element_type=F32)


def _proj_body(x_ref, w_ref, o_ref, *, tn):
    xb = x_ref[...].astype(BF16)
    for c in range(0, w_ref.shape[1], tn):
        o_ref[:, c:c + tn] = jnp.dot(xb, w_ref[:, c:c + tn], preferred_element_type=F32)


def _proj_in(x, w, tm, tn):
    t, k = x.shape
    n = w.shape[1]
    return pl.pallas_call(
        functools.partial(_proj_body, tn=tn),
        grid=(t // tm,),
        in_specs=[pl.BlockSpec((tm, k), lambda i: (i, 0)),
                  pl.BlockSpec((k, n), lambda i: (0, 0), pipeline_mode=pl.Buffered(1))],
        out_specs=pl.BlockSpec((tm, n), lambda i: (i, 0)),
        out_shape=jax.ShapeDtypeStruct((t, n), F32),
        compiler_params=_cparams(("parallel",)),
        name="proj_in",
    )(x, w)


def _gates_body(ab_ref, al_ref, dt_ref, o_ref, *, tm, l_pad, l_valid):
    x = ab_ref[...]
    lane = lax.broadcasted_iota(jnp.int32, x.shape, 1)
    xa = x + dt_ref[...]
    sp = jnp.maximum(xa, 0.0) + jnp.log1p(jnp.exp(-jnp.abs(xa)))
    g = -jnp.exp(al_ref[...]) * sp * LOG2_E
    y = jnp.where(lane < H_A, g, _sigmoid(x))
    yt = y.T[0:2 * H_A, :]
    if l_valid < l_pad:
        col = lax.broadcasted_iota(jnp.int32, yt.shape, 1) + pl.program_id(0) * tm
        yt = jnp.where(col % l_pad < l_valid, yt, 0.0)
    r = lax.broadcasted_iota(jnp.int32, (GDN_CHUNK, GDN_CHUNK), 0)
    c = lax.broadcasted_iota(jnp.int32, (GDN_CHUNK, GDN_CHUNK), 1)
    tri = jnp.where(r <= c, 1.0, 0.0).astype(F32)
    for s in range(tm // GDN_CHUNK):
        sl = slice(s * GDN_CHUNK, (s + 1) * GDN_CHUNK)
        o_ref[0:H_A, sl] = jnp.dot(yt[0:H_A, sl], tri, precision=lax.Precision.HIGHEST,
                                   preferred_element_type=F32)
    o_ref[H_A:2 * H_A, :] = yt[H_A:2 * H_A, :]


def _gates(src, col_block, al_lane, dt_lane, tm, l_pad, l_valid):
    t = src.shape[0]
    body = functools.partial(_gates_body, tm=tm, l_pad=l_pad, l_valid=l_valid)
    return pl.pallas_call(
        body,
        grid=(t // tm,),
        in_specs=[pl.BlockSpec((tm, LANES), lambda i: (i, col_block)),
                  pl.BlockSpec((1, LANES), lambda i: (0, 0)),
                  pl.BlockSpec((1, LANES), lambda i: (0, 0))],
        out_specs=pl.BlockSpec((2 * H_A, tm), lambda i: (0, i)),
        out_shape=jax.ShapeDtypeStruct((2 * H_A, t), F32),
        compiler_params=_cparams(("parallel",)),
        name="gates",
    )(src, al_lane, dt_lane)


def _gdn_body(q_ref, k_ref, v_ref, z_ref, gc_ref, bt_ref, cwq_ref, cwk_ref, cwv_ref,
              csq_ref, csk_ref, csv_ref, s0_ref, gn_ref, o_ref, so_ref,
              xq_ref, xk_ref, xv_ref, s_ref, *, tb, tr, nt, hb):
    t = pl.program_id(2)
    hist = CONV_W - 1
    base = 8

    @pl.when(t == 0)
    def _():
        s_ref[...] = s0_ref[0]
        xq_ref[base - hist:base, :] = csq_ref[0]
        xk_ref[base - hist:base, :] = csk_ref[0]
        xv_ref[base - hist:base, :] = csv_ref[0]

    for xs_ref, raw_ref in ((xq_ref, q_ref), (xk_ref, k_ref), (xv_ref, v_ref)):
        xs_ref[base:base + tr, :] = raw_ref[...]
        if tr < tb:
            xs_ref[base + tr:base + tb, :] = jnp.zeros((tb - tr, xs_ref.shape[1]), F32)

    def conv(xs_ref, cw_ref, ln):
        slab = xs_ref[:, ln]
        y = pltpu.roll(slab, hist, 0)[base:base + tb] * cw_ref[0:1, ln]
        for j in range(1, CONV_W):
            tap = slab if j == hist else pltpu.roll(slab, hist - j, 0)
            y = y + tap[base:base + tb] * cw_ref[j:j + 1, ln]
        return _silu(y)

    cc = GDN_CHUNK
    ii = lax.broadcasted_iota(jnp.int32, (cc, cc), 0)
    jj = lax.broadcasted_iota(jnp.int32, (cc, cc), 1)
    strict = ii > jj
    eye = jnp.where(ii == jj, 1.0, 0.0).astype(F32)
    heads = range(hb)
    lanes = [slice(hd * LANES, (hd + 1) * LANES) for hd in heads]
    units = [(hd, c) for hd in heads for c in range(tb // cc)]
    rows = {u: slice(u[1] * cc, (u[1] + 1) * cc) for u in units}
    q = [conv(xq_ref, cwq_ref, ln) for ln in lanes]
    k = [conv(xk_ref, cwk_ref, ln) for ln in lanes]
    v = [conv(xv_ref, cwv_ref, ln) for ln in lanes]
    q = [x * (lax.rsqrt(jnp.sum(x * x, -1, keepdims=True) + EPS) * (DK_A ** -0.5)) for x in q]
    k = [x * lax.rsqrt(jnp.sum(x * x, -1, keepdims=True) + EPS) for x in k]
    qc = {u: q[u[0]][rows[u]] for u in units}
    kc = {u: k[u[0]][rows[u]] for u in units}
    vc = {u: v[u[0]][rows[u]] for u in units}
    m_row = {u: jnp.broadcast_to(gc_ref[u[0], :, rows[u]], (cc, cc)) for u in units}
    m_col = {u: m_row[u].T for u in units}
    b_col = {u: jnp.broadcast_to(bt_ref[u[0], :, rows[u]], (cc, cc)).T for u in units}
    kq = {u: _mm_nt(jnp.concatenate([kc[u], qc[u]], axis=0), kc[u]) for u in units}
    diff = {u: m_col[u] - m_row[u] for u in units}
    dec = {u: jnp.exp2(jnp.where(strict, diff[u], -jnp.inf)) for u in units}
    a_mat = {u: b_col[u] * kq[u][0:cc] * dec[u] for u in units}
    qk = {u: kq[u][cc:2 * cc] * (dec[u] + eye) for u in units}
    e_g = {u: jnp.exp2(m_col[u]) for u in units}
    x_inv = {u: eye - jnp.where((ii >> 1) == (jj >> 1), a_mat[u], 0.0) for u in units}
    sft = 1
    while (1 << sft) < cc:
        off = ((ii >> (sft + 1)) == (jj >> (sft + 1))) & ((ii >> sft) != (jj >> sft))
        lx = {u: _mm(jnp.where(off, a_mat[u], 0.0), x_inv[u]) for u in units}
        x_inv = {u: x_inv[u] - _mm(x_inv[u], lx[u]) for u in units}
        sft += 1
    uw = {u: _mm(x_inv[u], jnp.concatenate([vc[u] * b_col[u], kc[u] * (b_col[u] * e_g[u])], axis=1))
          for u in units}
    wq = {u: jnp.concatenate([uw[u][:, DV_A:], qc[u] * e_g[u]], axis=0) for u in units}
    g_last = {u: m_col[u][cc - 1:cc, :] for u in units}
    kd = {u: kc[u] * jnp.exp2(g_last[u] - m_col[u]) for u in units}
    s_state = [s_ref[hd] for hd in heads]
    for c in range(tb // cc):
        ws = [_mm(wq[(hd, c)], s_state[hd]) for hd in heads]
        v_new = [uw[(hd, c)][:, 0:DV_A] - ws[hd][0:cc] for hd in heads]
        s_state = [s_state[hd] * jnp.exp2(g_last[(hd, c)]) + _mm_tn(kd[(hd, c)], v_new[hd]) for hd in heads]
        o = [ws[hd][cc:2 * cc] + _mm(qk[(hd, c)], v_new[hd]) for hd in heads]
        o = [x * lax.rsqrt(jnp.mean(x * x, -1, keepdims=True) + EPS) * gn_ref[...] for x in o]
        nr = min(cc, tr - c * cc)
        r = slice(c * cc, c * cc + nr)
        for hd in heads:
            o_ref[r, lanes[hd]] = (o[hd][0:nr] * _silu(z_ref[r, lanes[hd]])).astype(o_ref.dtype)
    for hd in heads:
        s_ref[hd] = s_state[hd]

    xq_ref[base - hist:base, :] = xq_ref[base + tb - hist:base + tb, :]
    xk_ref[base - hist:base, :] = xk_ref[base + tb - hist:base + tb, :]
    xv_ref[base - hist:base, :] = xv_ref[base + tb - hist:base + tb, :]

    @pl.when(t == nt - 1)
    def _():
        so_ref[0] = s_ref[...]


def _gdn(h1, gates3, conv_w, conv_state, s0, gn, bsz, seq, tb, hb):
    nt = -(-seq // tb)
    tr = min(tb, seq)
    assert seq % tr == 0 and (tr == tb or (nt == 1 and tb == GDN_CHUNK))
    t_rows = bsz * seq
    body = functools.partial(_gdn_body, tb=tb, tr=tr, nt=nt, hb=hb)
    width = hb * LANES
    ng = H_A // hb

    def rows(seg):
        return pl.BlockSpec((tr, width), lambda b, g, t: (b * nt + t, seg * ng + g))

    def cw(seg):
        return pl.BlockSpec((CONV_W, width), lambda b, g, t: (0, seg * ng + g))

    def cs(seg):
        return pl.BlockSpec((1, CONV_W - 1, width), lambda b, g, t: (b, 0, seg * ng + g))

    return pl.pallas_call(
        body,
        grid=(bsz, ng, nt),
        in_specs=[rows(0), rows(1), rows(2), rows(COL_Z // QK_A),
                  pl.BlockSpec((hb, 1, tb), lambda b, g, t: (g, 0, b * nt + t)),
                  pl.BlockSpec((hb, 1, tb), lambda b, g, t: (ng + g, 0, b * nt + t)),
                  cw(0), cw(1), cw(2), cs(0), cs(1), cs(2),
                  pl.BlockSpec((1, hb, DK_A, DV_A), lambda b, g, t: (b, g, 0, 0)),
                  pl.BlockSpec((1, DV_A), lambda b, g, t: (0, 0))],
        out_specs=[pl.BlockSpec((tr, width), lambda b, g, t: (b * nt + t, g)),
                   pl.BlockSpec((1, hb, DK_A, DV_A), lambda b, g, t: (b, g, 0, 0))],
        out_shape=[jax.ShapeDtypeStruct((t_rows, V_A), BF16),
                   jax.ShapeDtypeStruct((bsz, H_A, DK_A, DV_A), F32)],
        scratch_shapes=[pltpu.VMEM((tb + 8, width), F32)] * 3 + [pltpu.VMEM((hb, DK_A, DV_A), F32)],
        compiler_params=_cparams(("parallel", "parallel", "arbitrary")),
        name="gdn",
    )(h1, h1, h1, h1, gates3, gates3, conv_w, conv_w, conv_w,
      conv_state, conv_state, conv_state, s0, gn)


def _mla_pre_body(cq_ref, ckv_ref, kra_ref, krb_ref, cos_ref, sin_ref, qg_ref, wuq_ref, wuk_ref,
                  kvg_ref, *refs, n_carried):
    q_ref, kc_ref, ckvo_ref, kro_ref, *maybe_vt_ref = refs[n_carried:]
    cq = cq_ref[...]
    cqn = cq * lax.rsqrt(jnp.mean(cq * cq, -1, keepdims=True) + EPS) * qg_ref[...]
    qf = jnp.dot(cqn.astype(BF16), wuq_ref[...], preferred_element_type=F32)
    cos_k = cos_ref[...]
    sin_k = sin_ref[...]
    reps = H_B * ROPE // LANES
    cos_t = jnp.concatenate([cos_k] * reps, axis=1)
    sin_t = jnp.concatenate([sin_k] * reps, axis=1)
    n_nope = H_B * NOPE
    n_rope = H_B * ROPE
    qr = (qf[:, n_nope:n_nope + n_rope] * cos_t + qf[:, n_nope + n_rope:] * sin_t) * Q_SCALE
    lane = lax.broadcasted_iota(jnp.int32, (cq.shape[0], LANES), 1)
    for h in range(H_B):
        ql = jnp.dot(qf[:, h * NOPE:(h + 1) * NOPE].astype(BF16), wuk_ref[h],
                     preferred_element_type=F32) * Q_SCALE
        blk = qr[:, (h // 2) * LANES:(h // 2 + 1) * LANES]
        keep = (lane < ROPE) if h % 2 == 0 else (lane >= ROPE)
        q_ref[h, :, 0:KV_RANK] = ql.astype(BF16)
        q_ref[h, :, KV_RANK:KC_W] = jnp.where(keep, blk, 0.0).astype(BF16)
    ckv = ckv_ref[...]
    ckvn = ckv * lax.rsqrt(jnp.mean(ckv * ckv, -1, keepdims=True) + EPS) * kvg_ref[...]
    kr2 = kra_ref[...] * cos_k + krb_ref[...] * sin_k
    ckvo_ref[...] = ckvn
    kro_ref[...] = kr2[:, 0:ROPE]
    kc_ref[:, 0:KV_RANK] = ckvn.astype(BF16)
    kc_ref[:, KV_RANK:KC_W] = kr2.astype(BF16)
    if maybe_vt_ref:
        maybe_vt_ref[0][0] = ckvn.T.astype(BF16)


def _mla_pre(h1, cos_t, sin_t, qg, wuq, wuk, kvg, bsz, seq, tm, with_vt, layer, depth, carried):
    t_rows = bsz * seq
    ntab = cos_t.shape[0] // tm
    npb = max(seq // tm, 1)
    const2 = lambda i: (0, 0)
    out_specs = [pl.BlockSpec((H_B, tm, KC_W), lambda i: (0, i, 0)),
                 pl.BlockSpec((tm, KC_W), lambda i: (i, 0)),
                 pl.BlockSpec((None, tm, KV_RANK), lambda i: (layer, i, 0)),
                 pl.BlockSpec((None, tm, ROPE), lambda i: (layer, i, 0))]
    out_shape = [jax.ShapeDtypeStruct((H_B, t_rows, KC_W), BF16),
                 jax.ShapeDtypeStruct((t_rows, KC_W), BF16),
                 jax.ShapeDtypeStruct((depth, t_rows, KV_RANK), F32),
                 jax.ShapeDtypeStruct((depth, t_rows, ROPE), F32)]
    if with_vt:
        out_specs.append(pl.BlockSpec((1, KV_RANK, tm), lambda i: (i // npb, 0, i % npb)))
        out_shape.append(jax.ShapeDtypeStruct((bsz, KV_RANK, seq), BF16))
    n_in = 10
    return pl.pallas_call(
        functools.partial(_mla_pre_body, n_carried=len(carried)),
        input_output_aliases={n_in + k: 2 + k for k in range(len(carried))},
        grid=(t_rows // tm,),
        in_specs=[pl.BlockSpec((tm, Q_RANK), lambda i: (i, COL_CQ // Q_RANK)),
                  pl.BlockSpec((tm, KV_RANK), lambda i: (i, COL_CKV // KV_RANK)),
                  pl.BlockSpec((tm, LANES), lambda i: (i, COL_KRA // LANES)),
                  pl.BlockSpec((tm, LANES), lambda i: (i, COL_KRB // LANES)),
                  pl.BlockSpec((tm, LANES), lambda i: (i % ntab, 0)),
                  pl.BlockSpec((tm, LANES), lambda i: (i % ntab, 0)),
                  pl.BlockSpec((1, Q_RANK), const2),
                  pl.BlockSpec(wuq.shape, const2),
                  pl.BlockSpec(wuk.shape, lambda i: (0, 0, 0)),
                  pl.BlockSpec((1, KV_RANK), const2)] + [pl.BlockSpec(memory_space=pl.ANY)] * len(carried),
        out_specs=out_specs,
        out_shape=out_shape,
        compiler_params=_cparams(("parallel",)),
        name="mla_pre",
    )(h1, h1, h1, h1, cos_t, sin_t, qg, wuq, wuk, kvg, *carried)


def _attn_body(q_ref, k_ref, vt_ref, wuv_ref, o_ref, m_ref, l_ref, acc_ref, ex_ref, *, tq, tk, cw, lk):
    i = pl.program_id(1)
    shift = CHUNK.bit_length() - 1

    def update(j, masked, lazy, opening=False):
        k0 = pl.multiple_of(j * tk, tk)
        kt = k_ref[0, pl.ds(k0, tk), :]
        vt = vt_ref[0, :, pl.ds(k0, tk)]
        bias = None
        if masked:
            kpos = k0 + lax.broadcasted_iota(jnp.int32, (tk, tq), 0)
            qpos = i * tq + lax.broadcasted_iota(jnp.int32, (tk, tq), 1)
            bias = jnp.where((kpos >> shift) <= (qpos >> shift), 0.0, -jnp.inf).astype(F32)

        units = [(h, slice(c, c + cw)) for h in range(H_B) for c in range(0, tq, cw)]

        def scores(u):
            h, cs = u
            s = lax.dot_general(kt, q_ref[h, cs, :], (((1,), (1,)), ((), ())),
                                preferred_element_type=F32)
            return s if bias is None else s + bias[:, cs]

        def softmax(u, s):
            h, cs = u
            c_max = jnp.max(s, 0, keepdims=True)
            if opening:
                m_prev = jnp.zeros_like(c_max)
                m_new = c_max
                gap = jnp.abs(c_max)
            else:
                m_prev = m_ref[h, :, cs]
                m_new = jnp.maximum(m_prev, c_max)
                gap = c_max - m_prev
            alpha = jnp.exp2(m_prev - m_new)
            if lazy:
                p = jnp.exp2(s - m_prev)
                ex_ref[h, :, cs] = jnp.maximum(ex_ref[h, :, cs], gap)
                l_ref[h, :, cs] = alpha * (l_ref[h, :, cs] + jnp.sum(p, 0, keepdims=True))
            else:
                p = jnp.exp2(s - m_new)
                l_ref[h, :, cs] = alpha * l_ref[h, :, cs] + jnp.sum(p, 0, keepdims=True)
            m_ref[h, :, cs] = m_new
            return p.astype(BF16), alpha

        def accumulate(u, alpha, pv):
            h, cs = u
            if lazy:
                acc_ref[h, :, cs] = alpha * (acc_ref[h, :, cs] + pv)
            else:
                acc_ref[h, :, cs] = alpha * acc_ref[h, :, cs] + pv

        n_units = len(units)
        s_q = {0: scores(units[0])}
        if n_units > 1:
            s_q[1] = scores(units[1])
        p0, alpha = softmax(units[0], s_q.pop(0))
        alphas = {0: alpha}
        pvs = {0: jnp.dot(vt, p0, preferred_element_type=F32)}
        for n in range(n_units):
            if n + 2 < n_units:
                s_q[n + 2] = scores(units[n + 2])
            if n + 1 < n_units:
                p_next, alphas[n + 1] = softmax(units[n + 1], s_q.pop(n + 1))
            accumulate(units[n], alphas.pop(n), pvs.pop(n))
            if n + 1 < n_units:
                pvs[n + 1] = jnp.dot(vt, p_next, preferred_element_type=F32)

    n_full = ((((i * tq) >> shift) + 1) << shift) // tk
    n_all = jnp.minimum(((((i * tq + tq - 1) >> shift) + 1) << shift) + tk - 1, lk + tk - 1) // tk

    def tile_loop(lo, hi, masked, lazy, opening=False):
        def step(j, carry):
            update(j, masked, lazy, opening)
            return carry
        lax.fori_loop(lo, hi, step, 0)

    def attempt(a, redo):
        first = a == 0
        run = jnp.logical_or(first, redo > 0)

        @pl.when(run)
        def _():
            m_ref[...] = jnp.full(m_ref.shape, -jnp.inf, F32)
            l_ref[...] = jnp.zeros(l_ref.shape, F32)
            acc_ref[...] = jnp.zeros(acc_ref.shape, F32)
            ex_ref[...] = jnp.full(ex_ref.shape, -jnp.inf, F32)

        tile_loop(0, jnp.where(first, 0, n_all) * run.astype(jnp.int32), True, False)
        tile_loop(0, jnp.where(first, 1, 0), True, True, True)
        tile_loop(1, jnp.where(first, n_full, 0), False, True)
        tile_loop(jnp.maximum(n_full, 1), jnp.where(first, n_all, 0), True, True)
        excess = jnp.max(ex_ref[...])
        return jnp.where(first, (excess > MAX_STALE_EXCESS).astype(jnp.int32), 0)

    lax.fori_loop(0, 2, attempt, jnp.int32(0))

    for h in range(H_B):
        o_t = (acc_ref[h] * (1.0 / l_ref[h])).astype(BF16)
        ob_t = jnp.dot(wuv_ref[h], o_t, preferred_element_type=F32)
        o_ref[0, :, h * V_B:(h + 1) * V_B] = ob_t.T.astype(o_ref.dtype)


def _attn(q, kc, vt, wuv_t, bsz, seq, tq, tk, cw):
    assert seq % tq == 0 and seq % tk == 0
    nq = seq // tq
    body = functools.partial(_attn_body, tq=tq, tk=tk, cw=cw, lk=seq)
    return pl.pallas_call(
        body,
        grid=(bsz, nq),
        in_specs=[pl.BlockSpec((H_B, tq, KC_W), lambda b, i: (0, b * nq + i, 0)),
                  pl.BlockSpec((1, seq, KC_W), lambda b, i: (b, 0, 0)),
                  pl.BlockSpec((1, KV_RANK, seq), lambda b, i: (b, 0, 0)),
                  pl.BlockSpec(wuv_t.shape, lambda b, i: (0, 0, 0))],
        out_specs=pl.BlockSpec((1, tq, H_B * V_B), lambda b, i: (b, i, 0)),
        out_shape=jax.ShapeDtypeStruct((bsz, seq, H_B * V_B), BF16),
        scratch_shapes=[pltpu.VMEM((H_B, 1, tq), F32), pltpu.VMEM((H_B, 1, tq), F32),
                        pltpu.VMEM((H_B, KV_RANK, tq), F32), pltpu.VMEM((H_B, 1, tq), F32)],
        compiler_params=_cparams(("parallel", "arbitrary")),
        name="attn",
    )(q, kc, vt, wuv_t)


def _attn_dec_body(q_ref, ckv_ref, krt_ref, kn_ref, wuv_ref, o_ref, m_ref, l_ref, acc_ref,
                   *, tq, past, n_past):
    j = pl.program_id(1)
    rows = H_B * tq
    nt_dims = (((1,), (1,)), ((), ()))
    q2 = q_ref[...].reshape(rows, KC_W)

    @pl.when(j == 0)
    def _():
        m_ref[...] = jnp.full(m_ref.shape, -jnp.inf, F32)
        l_ref[...] = jnp.zeros(l_ref.shape, F32)
        acc_ref[...] = jnp.zeros(acc_ref.shape, F32)

    def accumulate(s, v_nat):
        m_prev = m_ref[...]
        m_new = jnp.maximum(m_prev, jnp.max(s, -1, keepdims=True))
        p = jnp.exp2(s - m_new)
        alpha = jnp.exp2(m_prev - m_new)
        l_ref[...] = alpha * l_ref[...] + jnp.sum(p, -1, keepdims=True)
        acc_ref[...] = alpha * acc_ref[...] + jnp.dot(p.astype(BF16), v_nat, preferred_element_type=F32)
        m_ref[...] = m_new

    @pl.when(j < n_past)
    def _():
        k_lat = ckv_ref[0].astype(BF16)
        kr_t = krt_ref[0].astype(BF16)
        kr2_t = jnp.concatenate([kr_t, kr_t], axis=0)
        s = (lax.dot_general(q2[:, 0:KV_RANK], k_lat, nt_dims, preferred_element_type=F32)
             + jnp.dot(q2[:, KV_RANK:KC_W], kr2_t, preferred_element_type=F32))
        accumulate(s, k_lat)

    @pl.when(j == n_past)
    def _():
        kn = kn_ref[...]
        s = lax.dot_general(q2, kn, nt_dims, preferred_element_type=F32)
        shift = CHUNK.bit_length() - 1
        qpos = past + (lax.broadcasted_iota(jnp.int32, s.shape, 0) & (tq - 1))
        kpos = past + lax.broadcasted_iota(jnp.int32, s.shape, 1)
        s = jnp.where((kpos >> shift) <= (qpos >> shift), s, -jnp.inf)
        accumulate(s, kn[:, 0:KV_RANK])
        o = acc_ref[...] * (1.0 / l_ref[...])
        for h in range(H_B):
            oh = o[h * tq:(h + 1) * tq, :].astype(BF16)
            o_ref[:, h * V_B:(h + 1) * V_B] = jnp.dot(
                oh, wuv_ref[h], preferred_element_type=F32).astype(o_ref.dtype)


def _attn_dec(q, cache_ckv, cache_kr_t, layer, kc_new, wuv, bsz, seq, tk):
    past = cache_ckv.shape[2]
    assert past % tk == 0 and past % CHUNK == 0 and seq & (seq - 1) == 0
    n_past = past // tk
    body = functools.partial(_attn_dec_body, tq=seq, past=past, n_past=n_past)
    return pl.pallas_call(
        body,
        grid=(bsz, n_past + 1),
        in_specs=[pl.BlockSpec((H_B, seq, KC_W), lambda b, j: (0, b, 0)),
                  pl.BlockSpec((None, 1, tk, KV_RANK), lambda b, j: (layer, b, jnp.minimum(j, n_past - 1), 0)),
                  pl.BlockSpec((None, 1, ROPE, tk), lambda b, j: (layer, b, 0, jnp.minimum(j, n_past - 1))),
                  pl.BlockSpec((seq, KC_W), lambda b, j: (b, 0)),
                  pl.BlockSpec(wuv.shape, lambda b, j: (0, 0, 0))],
        out_specs=pl.BlockSpec((seq, H_B * V_B), lambda b, j: (b, 0)),
        out_shape=jax.ShapeDtypeStruct((bsz * seq, H_B * V_B), BF16),
        scratch_shapes=[pltpu.VMEM((H_B * seq, 1), F32), pltpu.VMEM((H_B * seq, 1), F32),
                        pltpu.VMEM((H_B * seq, KV_RANK), F32)],
        compiler_params=_cparams(("parallel", "arbitrary")),
        name="attn_dec",
    )(q, cache_ckv, cache_kr_t, kc_new, wuv)


def _layer_norm(r, g, b):
    mu = jnp.mean(r, -1, keepdims=True)
    d = r - mu
    var = jnp.mean(d * d, -1, keepdims=True)
    return d * lax.rsqrt(var + EPS) * g + b


def _merge_body(oa_ref, ob_ref, ga_ref, gb_ref, x_ref, woa_ref, wob_ref, wout_ref, g_ref, b_ref, o_ref):
    ya = jnp.dot(oa_ref[...], woa_ref[...], preferred_element_type=F32)
    yb = jnp.dot(ob_ref[...], wob_ref[...], preferred_element_type=F32)
    m = _sigmoid(ga_ref[...]) * ya + _sigmoid(gb_ref[...]) * yb
    r = ALPHA * x_ref[...] + jnp.dot(m.astype(BF16), wout_ref[...], preferred_element_type=F32)
    o_ref[...] = _layer_norm(r, g_ref[...], b_ref[...])


def _merge(oa, ob, h1, x, woa, wob, wout, g, b, tm):
    t = x.shape[0]
    row = lambda i: (i, 0)
    const = lambda i: (0, 0)
    wspec = pl.BlockSpec((D_MODEL, D_MODEL), const)
    return pl.pallas_call(
        _merge_body,
        grid=(t // tm,),
        in_specs=[pl.BlockSpec((tm, V_A), row), pl.BlockSpec((tm, H_B * V_B), row),
                  pl.BlockSpec((tm, D_MODEL), lambda i: (i, COL_GA // D_MODEL)),
                  pl.BlockSpec((tm, D_MODEL), lambda i: (i, COL_GB // D_MODEL)),
                  pl.BlockSpec((tm, D_MODEL), row), wspec, wspec, wspec,
                  pl.BlockSpec((1, D_MODEL), const), pl.BlockSpec((1, D_MODEL), const)],
        out_specs=pl.BlockSpec((tm, D_MODEL), row),
        out_shape=jax.ShapeDtypeStruct((t, D_MODEL), F32),
        compiler_params=_cparams(("parallel",)),
        name="merge",
    )(oa, ob, h1, h1, x, woa, wob, wout, g, b)


def _ffn_body(x_ref, wg_ref, wu_ref, wd_ref, g_ref, b_ref, o_ref, *, chunks):
    x = x_ref[...]
    xb = x.astype(BF16)
    y = ALPHA * x
    off = 0
    for width in chunks:
        sl = slice(off, off + width)
        f1 = jnp.dot(xb, wg_ref[:, sl], preferred_element_type=F32)
        f3 = jnp.dot(xb, wu_ref[:, sl], preferred_element_type=F32)
        hc = (_silu(f1) * f3).astype(BF16)
        y = y + jnp.dot(hc, wd_ref[sl, :], preferred_element_type=F32)
        off += width
    o_ref[...] = _layer_norm(y, g_ref[...], b_ref[...])


def _ffn(x, wg, wu, wd, g, b, tm, chunks):
    assert sum(chunks) == D_FF
    t = x.shape[0]
    row = lambda i: (i, 0)
    const = lambda i: (0, 0)
    single = pl.Buffered(1)
    return pl.pallas_call(
        functools.partial(_ffn_body, chunks=chunks),
        grid=(t // tm,),
        in_specs=[pl.BlockSpec((tm, D_MODEL), row),
                  pl.BlockSpec((D_MODEL, D_FF), const, pipeline_mode=single),
                  pl.BlockSpec((D_MODEL, D_FF), const, pipeline_mode=single),
                  pl.BlockSpec((D_FF, D_MODEL), const, pipeline_mode=single),
                  pl.BlockSpec((1, D_MODEL), const), pl.BlockSpec((1, D_MODEL), const)],
        out_specs=pl.BlockSpec((tm, D_MODEL), row),
        out_shape=jax.ShapeDtypeStruct((t, D_MODEL), F32),
        compiler_params=_cparams(("parallel",)),
        name="ffn",
    )(x, wg, wu, wd, g, b)


def _prep_layer_weights(w_in, conv_w, a_log, dt_bias, gdn_norm_g, w_oa, q_norm_g, w_uq, kv_norm_g,
                        w_ukv, w_ob, w_out, ln1_g, ln1_b, w_gu, w_down, ln2_g, ln2_b):
    seg = lambda i: w_in[:, _OFFS[i]:_OFFS[i + 1]]
    qkv, z, a, b, c_q, c_kv, k_r, g_a, g_b = (seg(i) for i in range(9))
    half = ROPE // 2
    k_r_rot = jnp.concatenate([-k_r[:, half:], k_r[:, :half]], axis=1)
    pad = jnp.zeros((D_MODEL, LANES - 2 * H_A), w_in.dtype)
    w_proj = jnp.concatenate([qkv, z, g_a, g_b, c_kv, k_r, k_r, c_q, k_r_rot, k_r_rot, a, b, pad],
                             axis=1).astype(BF16)
    lane_pad = jnp.zeros((LANES - H_A,), F32)
    al_lane = jnp.concatenate([a_log.astype(F32), lane_pad]).reshape(1, LANES)
    dt_lane = jnp.concatenate([dt_bias.astype(F32), lane_pad]).reshape(1, LANES)
    uq = w_uq.reshape(Q_RANK, H_B, NOPE + ROPE)
    uq_nope = uq[:, :, :NOPE].reshape(Q_RANK, H_B * NOPE)
    uq_rope = uq[:, :, NOPE:]
    uq_rot = jnp.concatenate([-uq_rope[:, :, half:], uq_rope[:, :, :half]], axis=2)
    w_uq_ext = jnp.concatenate([uq_nope, uq_rope.reshape(Q_RANK, H_B * ROPE),
                                uq_rot.reshape(Q_RANK, H_B * ROPE)], axis=1).astype(BF16)
    ukv = w_ukv.reshape(KV_RANK, H_B, NOPE + V_B)
    w_uk_t = jnp.transpose(ukv[:, :, :NOPE], (1, 2, 0)).astype(BF16)
    w_uv = jnp.transpose(ukv[:, :, NOPE:], (1, 0, 2)).astype(BF16)
    w_uv_t = jnp.transpose(ukv[:, :, NOPE:], (1, 2, 0)).astype(BF16)
    return dict(
        w_proj=w_proj, conv_w=conv_w.astype(F32), al_lane=al_lane, dt_lane=dt_lane,
        gn=gdn_norm_g.reshape(1, DV_A).astype(F32), w_oa=w_oa.astype(BF16),
        qg=q_norm_g.reshape(1, Q_RANK).astype(F32), w_uq=w_uq_ext, w_uk_t=w_uk_t, w_uv=w_uv, w_uv_t=w_uv_t,
        kvg=kv_norm_g.reshape(1, KV_RANK).astype(F32), w_ob=w_ob.astype(BF16),
        w_out=w_out.astype(BF16), ln1_g=ln1_g.reshape(1, D_MODEL), ln1_b=ln1_b.reshape(1, D_MODEL),
        w_g=w_gu[:, :D_FF].astype(BF16), w_u=w_gu[:, D_FF:].astype(BF16), w_down=w_down.astype(BF16),
        ln2_g=ln2_g.reshape(1, D_MODEL), ln2_b=ln2_b.reshape(1, D_MODEL))


def _rope_tables(past, seq, reps):
    half = ROPE // 2
    inv = ROPE_THETA ** (-jnp.arange(half, dtype=F32) / half)
    ang = (past + jnp.arange(seq)).astype(F32)[:, None] * inv[None, :]
    cos = jnp.tile(jnp.cos(ang), (reps, LANES // half))
    sin = jnp.tile(jnp.sin(ang), (reps, LANES // half))
    return cos, sin


TM_PROJ, TN_PROJ, TM_GATES, TM_MLA, TM_MERGE, TM_FFN = 512, N_PROJ // 4, 1024, 512, 512, 512
FFN_CHUNKS = (768, 768, 768, D_FF - 3 * 768)
TQ_ATTN, TK_ATTN, CW_ATTN, TK_DEC = 512, 512, 512, 1024
TB_GDN, HB_GDN = 2 * GDN_CHUNK, H_A


def _trunk_layer(x, conv_state, s0, caches, wl, bsz, seq, layer, depth, carried):
    decode = caches is not None
    t_rows = bsz * seq
    h1 = _proj_in(x, wl["w_proj"], min(TM_PROJ, t_rows), TN_PROJ)
    seq_pad = -(-seq // GDN_CHUNK) * GDN_CHUNK
    if seq_pad == seq:
        gates = _gates(h1, COL_AB // LANES, wl["al_lane"], wl["dt_lane"], min(TM_GATES, t_rows), seq, seq)
    else:
        ab = h1[:, COL_AB:COL_AB + LANES].reshape(bsz, seq, LANES)
        ab = jnp.pad(ab, ((0, 0), (0, seq_pad - seq), (0, 0))).reshape(bsz * seq_pad, LANES)
        gates = _gates(ab, 0, wl["al_lane"], wl["dt_lane"], min(TM_GATES, bsz * seq_pad), seq_pad, seq)
    o_a, s_new = _gdn(h1, gates.reshape(2 * H_A, 1, bsz * seq_pad), wl["conv_w"], conv_state, s0, wl["gn"],
                      bsz, seq, min(TB_GDN, seq_pad), HB_GDN)
    if decode:
        cache_ckv, cache_kr_t = caches
        tm_mla = t_rows if t_rows <= TM_MLA else seq
        cos_t, sin_t = _rope_tables(cache_ckv.shape[2], seq, tm_mla // seq)
        q, kc, ckv_all, kr_all = _mla_pre(h1, cos_t, sin_t, wl["qg"], wl["w_uq"], wl["w_uk_t"], wl["kvg"],
                                          bsz, seq, tm_mla, False, layer, depth, carried)
        o_b = _attn_dec(q, cache_ckv, cache_kr_t, layer, kc, wl["w_uv"], bsz, seq, TK_DEC)
    else:
        cos_t, sin_t = _rope_tables(0, seq, 1)
        q, kc, ckv_all, kr_all, vt = _mla_pre(h1, cos_t, sin_t, wl["qg"], wl["w_uq"], wl["w_uk_t"],
                                              wl["kvg"], bsz, seq, TM_MLA, True, layer, depth, carried)
        o_b = _attn(q, kc.reshape(bsz, seq, KC_W), vt, wl["w_uv_t"], bsz, seq, TQ_ATTN, TK_ATTN, CW_ATTN)
        o_b = o_b.reshape(t_rows, H_B * V_B)
    x1 = _merge(o_a, o_b, h1, x, wl["w_oa"], wl["w_ob"], wl["w_out"], wl["ln1_g"], wl["ln1_b"],
                min(TM_MERGE, t_rows))
    x2 = _ffn(x1, wl["w_g"], wl["w_u"], wl["w_down"], wl["ln2_g"], wl["ln2_b"], min(TM_FFN, t_rows), FFN_CHUNKS)
    conv_new = h1.reshape(bsz, seq, N_PROJ)[:, seq - (CONV_W - 1):, COL_QKV:COL_QKV + C_QKV]
    return x2, conv_new, s_new, (ckv_all, kr_all)


def kernel(x_prompt, x_sample, state_conv, state_gdn, cache_ckv, cache_krope, w_in, conv_w, a_log, dt_bias, gdn_norm_g, w_oa, q_norm_g, w_uq, kv_norm_g, w_ukv, w_ob, w_out, ln1_g, ln1_b, w_gu, w_down, ln2_g, ln2_b):
    bp, lp, _ = x_prompt.shape
    bs, ls, _ = x_sample.shape
    yp = x_prompt.reshape(bp * lp, D_MODEL)
    ys = x_sample.reshape(bs * ls, D_MODEL)
    zero_conv = jnp.zeros((bp, CONV_W - 1, C_QKV), F32)
    zero_s = jnp.zeros((bp, H_A, DK_A, DV_A), F32)
    cache_kr_t = jnp.swapaxes(cache_krope, 2, 3)
    depth = w_in.shape[0]
    conv_p, gdn_p, conv_s, gdn_s = [], [], [], []
    kv_p = kv_s = ()
    for l in range(depth):
        wl = _prep_layer_weights(w_in[l], conv_w[l], a_log[l], dt_bias[l], gdn_norm_g[l], w_oa[l],
                                 q_norm_g[l], w_uq[l], kv_norm_g[l], w_ukv[l], w_ob[l], w_out[l],
                                 ln1_g[l], ln1_b[l], w_gu[l], w_down[l], ln2_g[l], ln2_b[l])
        yp, c_new, g_new, kv_p = _trunk_layer(yp, zero_conv, zero_s, None, wl, bp, lp, l, depth, kv_p)
        conv_p.append(c_new), gdn_p.append(g_new)
        ys, c_new, g_new, kv_s = _trunk_layer(ys, state_conv[l], state_gdn[l], (cache_ckv, cache_kr_t), wl,
                                              bs, ls, l, depth, kv_s)
        conv_s.append(c_new), gdn_s.append(g_new)
    return (yp.reshape(bp, lp, D_MODEL), ys.reshape(bs, ls, D_MODEL),
            jnp.stack(conv_p), jnp.stack(gdn_p),
            kv_p[0].reshape(depth, bp, lp, KV_RANK), kv_p[1].reshape(depth, bp, lp, ROPE),
            jnp.stack(conv_s), jnp.stack(gdn_s),
            kv_s[0].reshape(depth, bs, ls, KV_RANK), kv_s[1].reshape(depth, bs, ls, ROPE))
```

```python
import functools

import numpy as np
import jax
import jax.numpy as jnp
from jax import lax
from jax.experimental import pallas as pl
from jax.experimental.pallas import tpu as pltpu

F32 = jnp.float32
BF16 = jnp.bfloat16

D_MODEL = 1024
DEPTH = 2
CHUNK = 64
H_A = 8
DK_A = 128
DV_A = 128
QK_A = H_A * DK_A
V_A = H_A * DV_A
C_QKV = 2 * QK_A + V_A
CONV_W = 4
H_B = 8
NOPE = 128
ROPE = 64
V_B = 128
Q_RANK = 384
KV_RANK = 256
ROPE_THETA = 10000.0
ATTN_SCALE = (NOPE + ROPE) ** -0.5
LOG2_E = float(np.log2(np.e))
Q_SCALE = ATTN_SCALE * LOG2_E
D_FF = -(-8 * D_MODEL // (3 * 256)) * 256
ALPHA = (2 * DEPTH) ** 0.25
EPS = 1e-6
_SIZES = (C_QKV, V_A, H_A, H_A, Q_RANK, KV_RANK, ROPE, D_MODEL, D_MODEL)
_OFFS = tuple(int(v) for v in np.cumsum((0,) + _SIZES))

LANES = 128
VMEM_LIMIT = 56 * 1024 * 1024

COL_QKV = 0
COL_Z = COL_QKV + C_QKV
COL_GA = COL_Z + V_A
COL_GB = COL_GA + D_MODEL
COL_CKV = COL_GB + D_MODEL
COL_KRA = COL_CKV + KV_RANK
COL_CQ = COL_KRA + LANES
COL_KRB = COL_CQ + Q_RANK
COL_AB = COL_KRB + LANES
N_PROJ = COL_AB + LANES
KC_W = KV_RANK + LANES

GDN_CHUNK = 128
MAX_STALE_EXCESS = 64.0


def _cparams(sem):
    return pltpu.CompilerParams(dimension_semantics=sem, vmem_limit_bytes=VMEM_LIMIT)


def _sigmoid(x):
    return jax.nn.sigmoid(x)


def _silu(x):
    return x * jax.nn.sigmoid(x)


def _mm(a, b):
    return jnp.dot(a.astype(BF16), b.astype(BF16), preferred_element_type=F32)


def _mm_nt(a, b):
    return lax.dot_general(a.astype(BF16), b.astype(BF16), (((1,), (1,)), ((), ())),
                           preferred_element_type=F32)


def _mm_tn(a, b):
    return lax.dot_general(a.astype(BF16), b.astype(BF16), (((0,), (0,)), ((), ())),
                           preferred_element_type=F32)


def _proj_body(x_ref, w_ref, o_ref, *, tn):
    xb = x_ref[...].astype(BF16)
    for c in range(0, w_ref.shape[1], tn):
        o_ref[:, c:c + tn] = jnp.dot(xb, w_ref[:, c:c + tn], preferred_element_type=F32)


def _proj_in(x, w, tm, tn):
    t, k = x.shape
    n = w.shape[1]
    return pl.pallas_call(
        functools.partial(_proj_body, tn=tn),
        grid=(t // tm,),
        in_specs=[pl.BlockSpec((tm, k), lambda i: (i, 0)),
                  pl.BlockSpec((k, n), lambda i: (0, 0), pipeline_mode=pl.Buffered(1))],
        out_specs=pl.BlockSpec((tm, n), lambda i: (i, 0)),
        out_shape=jax.ShapeDtypeStruct((t, n), F32),
        compiler_params=_cparams(("parallel",)),
        name="proj_in",
    )(x, w)


def _gates_body(ab_ref, al_ref, dt_ref, o_ref, *, tm, l_pad, l_valid):
    x = ab_ref[...]
    lane = lax.broadcasted_iota(jnp.int32, x.shape, 1)
    xa = x + dt_ref[...]
    sp = jnp.maximum(xa, 0.0) + jnp.log1p(jnp.exp(-jnp.abs(xa)))
    g = -jnp.exp(al_ref[...]) * sp * LOG2_E
    y = jnp.where(lane < H_A, g, _sigmoid(x))
    yt = y.T[0:2 * H_A, :]
    if l_valid < l_pad:
        col = lax.broadcasted_iota(jnp.int32, yt.shape, 1) + pl.program_id(0) * tm
        yt = jnp.where(col % l_pad < l_valid, yt, 0.0)
    r = lax.broadcasted_iota(jnp.int32, (GDN_CHUNK, GDN_CHUNK), 0)
    c = lax.broadcasted_iota(jnp.int32, (GDN_CHUNK, GDN_CHUNK), 1)
    tri = jnp.where(r <= c, 1.0, 0.0).astype(F32)
    for s in range(tm // GDN_CHUNK):
        sl = slice(s * GDN_CHUNK, (s + 1) * GDN_CHUNK)
        o_ref[0:H_A, sl] = jnp.dot(yt[0:H_A, sl], tri, precision=lax.Precision.HIGHEST,
                                   preferred_element_type=F32)
    o_ref[H_A:2 * H_A, :] = yt[H_A:2 * H_A, :]


def _gates(src, col_block, al_lane, dt_lane, tm, l_pad, l_valid):
    t = src.shape[0]
    body = functools.partial(_gates_body, tm=tm, l_pad=l_pad, l_valid=l_valid)
    return pl.pallas_call(
        body,
        grid=(t // tm,),
        in_specs=[pl.BlockSpec((tm, LANES), lambda i: (i, col_block)),
                  pl.BlockSpec((1, LANES), lambda i: (0, 0)),
                  pl.BlockSpec((1, LANES), lambda i: (0, 0))],
        out_specs=pl.BlockSpec((2 * H_A, tm), lambda i: (0, i)),
        out_shape=jax.ShapeDtypeStruct((2 * H_A, t), F32),
        compiler_params=_cparams(("parallel",)),
        name="gates",
    )(src, al_lane, dt_lane)


def _gdn_body(q_ref, k_ref, v_ref, z_ref, gc_ref, bt_ref, cwq_ref, cwk_ref, cwv_ref,
              csq_ref, csk_ref, csv_ref, s0_ref, gn_ref, o_ref, so_ref,
              xq_ref, xk_ref, xv_ref, s_ref, *, tb, tr, nt, hb):
    t = pl.program_id(2)
    hist = CONV_W - 1
    base = 8

    @pl.when(t == 0)
    def _():
        s_ref[...] = s0_ref[0]
        xq_ref[base - hist:base, :] = csq_ref[0]
        xk_ref[base - hist:base, :] = csk_ref[0]
        xv_ref[base - hist:base, :] = csv_ref[0]

    for xs_ref, raw_ref in ((xq_ref, q_ref), (xk_ref, k_ref), (xv_ref, v_ref)):
        xs_ref[base:base + tr, :] = raw_ref[...]
        if tr < tb:
            xs_ref[base + tr:base + tb, :] = jnp.zeros((tb - tr, xs_ref.shape[1]), F32)

    def conv(xs_ref, cw_ref, ln):
        slab = xs_ref[:, ln]
        y = pltpu.roll(slab, hist, 0)[base:base + tb] * cw_ref[0:1, ln]
        for j in range(1, CONV_W):
            tap = slab if j == hist else pltpu.roll(slab, hist - j, 0)
            y = y + tap[base:base + tb] * cw_ref[j:j + 1, ln]
        return _silu(y)

    cc = GDN_CHUNK
    ii = lax.broadcasted_iota(jnp.int32, (cc, cc), 0)
    jj = lax.broadcasted_iota(jnp.int32, (cc, cc), 1)
    strict = ii > jj
    eye = jnp.where(ii == jj, 1.0, 0.0).astype(F32)
    heads = range(hb)
    lanes = [slice(hd * LANES, (hd + 1) * LANES) for hd in heads]
    units = [(hd, c) for hd in heads for c in range(tb // cc)]
    rows = {u: slice(u[1] * cc, (u[1] + 1) * cc) for u in units}
    q = [conv(xq_ref, cwq_ref, ln) for ln in lanes]
    k = [conv(xk_ref, cwk_ref, ln) for ln in lanes]
    v = [conv(xv_ref, cwv_ref, ln) for ln in lanes]
    q = [x * (lax.rsqrt(jnp.sum(x * x, -1, keepdims=True) + EPS) * (DK_A ** -0.5)) for x in q]
    k = [x * lax.rsqrt(jnp.sum(x * x, -1, keepdims=True) + EPS) for x in k]
    qc = {u: q[u[0]][rows[u]] for u in units}
    kc = {u: k[u[0]][rows[u]] for u in units}
    vc = {u: v[u[0]][rows[u]] for u in units}
    m_row = {u: jnp.broadcast_to(gc_ref[u[0], :, rows[u]], (cc, cc)) for u in units}
    m_col = {u: m_row[u].T for u in units}
    b_col = {u: jnp.broadcast_to(bt_ref[u[0], :, rows[u]], (cc, cc)).T for u in units}
    kq = {u: _mm_nt(jnp.concatenate([kc[u], qc[u]], axis=0), kc[u]) for u in units}
    diff = {u: m_col[u] - m_row[u] for u in units}
    dec = {u: jnp.exp2(jnp.where(strict, diff[u], -jnp.inf)) for u in units}
    a_mat = {u: b_col[u] * kq[u][0:cc] * dec[u] for u in units}
    qk = {u: kq[u][cc:2 * cc] * (dec[u] + eye) for u in units}
    e_g = {u: jnp.exp2(m_col[u]) for u in units}
    x_inv = {u: eye - jnp.where((ii >> 1) == (jj >> 1), a_mat[u], 0.0) for u in units}
    sft = 1
    while (1 << sft) < cc:
        off = ((ii >> (sft + 1)) == (jj >> (sft + 1))) & ((ii >> sft) != (jj >> sft))
        lx = {u: _mm(jnp.where(off, a_mat[u], 0.0), x_inv[u]) for u in units}
        x_inv = {u: x_inv[u] - _mm(x_inv[u], lx[u]) for u in units}
        sft += 1
    uw = {u: _mm(x_inv[u], jnp.concatenate([vc[u] * b_col[u], kc[u] * (b_col[u] * e_g[u])], axis=1))
          for u in units}
    wq = {u: jnp.concatenate([uw[u][:, DV_A:], qc[u] * e_g[u]], axis=0) for u in units}
    g_last = {u: m_col[u][cc - 1:cc, :] for u in units}
    kd = {u: kc[u] * jnp.exp2(g_last[u] - m_col[u]) for u in units}
    s_state = [s_ref[hd] for hd in heads]
    for c in range(tb // cc):
        ws = [_mm(wq[(hd, c)], s_state[hd]) for hd in heads]
        v_new = [uw[(hd, c)][:, 0:DV_A] - ws[hd][0:cc] for hd in heads]
        s_state = [s_state[hd] * jnp.exp2(g_last[(hd, c)]) + _mm_tn(kd[(hd, c)], v_new[hd]) for hd in heads]
        o = [ws[hd][cc:2 * cc] + _mm(qk[(hd, c)], v_new[hd]) for hd in heads]
        o = [x * lax.rsqrt(jnp.mean(x * x, -1, keepdims=True) + EPS) * gn_ref[...] for x in o]
        nr = min(cc, tr - c * cc)
        r = slice(c * cc, c * cc + nr)
        for hd in heads:
            o_ref[r, lanes[hd]] = (o[hd][0:nr] * _silu(z_ref[r, lanes[hd]])).astype(o_ref.dtype)
    for hd in heads:
        s_ref[hd] = s_state[hd]

    xq_ref[base - hist:base, :] = xq_ref[base + tb - hist:base + tb, :]
    xk_ref[base - hist:base, :] = xk_ref[base + tb - hist:base + tb, :]
    xv_ref[base - hist:base, :] = xv_ref[base + tb - hist:base + tb, :]

    @pl.when(t == nt - 1)
    def _():
        so_ref[0] = s_ref[...]


def _gdn(h1, gates3, conv_w, conv_state, s0, gn, bsz, seq, tb, hb):
    nt = -(-seq // tb)
    tr = min(tb, seq)
    assert seq % tr == 0 and (tr == tb or (nt == 1 and tb == GDN_CHUNK))
    t_rows = bsz * seq
    body = functools.partial(_gdn_body, tb=tb, tr=tr, nt=nt, hb=hb)
    width = hb * LANES
    ng = H_A // hb

    def rows(seg):
        return pl.BlockSpec((tr, width), lambda b, g, t: (b * nt + t, seg * ng + g))

    def cw(seg):
        return pl.BlockSpec((CONV_W, width), lambda b, g, t: (0, seg * ng + g))

    def cs(seg):
        return pl.BlockSpec((1, CONV_W - 1, width), lambda b, g, t: (b, 0, seg * ng + g))

    return pl.pallas_call(
        body,
        grid=(bsz, ng, nt),
        in_specs=[rows(0), rows(1), rows(2), rows(COL_Z // QK_A),
                  pl.BlockSpec((hb, 1, tb), lambda b, g, t: (g, 0, b * nt + t)),
                  pl.BlockSpec((hb, 1, tb), lambda b, g, t: (ng + g, 0, b * nt + t)),
                  cw(0), cw(1), cw(2), cs(0), cs(1), cs(2),
                  pl.BlockSpec((1, hb, DK_A, DV_A), lambda b, g, t: (b, g, 0, 0)),
                  pl.BlockSpec((1, DV_A), lambda b, g, t: (0, 0))],
        out_specs=[pl.BlockSpec((tr, width), lambda b, g, t: (b * nt + t, g)),
                   pl.BlockSpec((1, hb, DK_A, DV_A), lambda b, g, t: (b, g, 0, 0))],
        out_shape=[jax.ShapeDtypeStruct((t_rows, V_A), BF16),
                   jax.ShapeDtypeStruct((bsz, H_A, DK_A, DV_A), F32)],
        scratch_shapes=[pltpu.VMEM((tb + 8, width), F32)] * 3 + [pltpu.VMEM((hb, DK_A, DV_A), F32)],
        compiler_params=_cparams(("parallel", "parallel", "arbitrary")),
        name="gdn",
    )(h1, h1, h1, h1, gates3, gates3, conv_w, conv_w, conv_w,
      conv_state, conv_state, conv_state, s0, gn)


def _mla_pre_body(cq_ref, ckv_ref, kra_ref, krb_ref, cos_ref, sin_ref, qg_ref, wuq_ref, wuk_ref,
                  kvg_ref, *refs, n_carried):
    q_ref, kc_ref, ckvo_ref, kro_ref, *maybe_vt_ref = refs[n_carried:]
    cq = cq_ref[...]
    cqn = cq * lax.rsqrt(jnp.mean(cq * cq, -1, keepdims=True) + EPS) * qg_ref[...]
    qf = jnp.dot(cqn.astype(BF16), wuq_ref[...], preferred_element_type=F32)
    cos_k = cos_ref[...]
    sin_k = sin_ref[...]
    reps = H_B * ROPE // LANES
    cos_t = jnp.concatenate([cos_k] * reps, axis=1)
    sin_t = jnp.concatenate([sin_k] * reps, axis=1)
    n_nope = H_B * NOPE
    n_rope = H_B * ROPE
    qr = (qf[:, n_nope:n_nope + n_rope] * cos_t + qf[:, n_nope + n_rope:] * sin_t) * Q_SCALE
    lane = lax.broadcasted_iota(jnp.int32, (cq.shape[0], LANES), 1)
    for h in range(H_B):
        ql = jnp.dot(qf[:, h * NOPE:(h + 1) * NOPE].astype(BF16), wuk_ref[h],
                     preferred_element_type=F32) * Q_SCALE
        blk = qr[:, (h // 2) * LANES:(h // 2 + 1) * LANES]
        keep = (lane < ROPE) if h % 2 == 0 else (lane >= ROPE)
        q_ref[h, :, 0:KV_RANK] = ql.astype(BF16)
        q_ref[h, :, KV_RANK:KC_W] = jnp.where(keep, blk, 0.0).astype(BF16)
    ckv = ckv_ref[...]
    ckvn = ckv * lax.rsqrt(jnp.mean(ckv * ckv, -1, keepdims=True) + EPS) * kvg_ref[...]
    kr2 = kra_ref[...] * cos_k + krb_ref[...] * sin_k
    ckvo_ref[...] = ckvn
    kro_ref[...] = kr2[:, 0:ROPE]
    kc_ref[:, 0:KV_RANK] = ckvn.astype(BF16)
    kc_ref[:, KV_RANK:KC_W] = kr2.astype(BF16)
    if maybe_vt_ref:
        maybe_vt_ref[0][0] = ckvn.T.astype(BF16)


def _mla_pre(h1, cos_t, sin_t, qg, wuq, wuk, kvg, bsz, seq, tm, with_vt, layer, depth, carried):
    t_rows = bsz * seq
    ntab = cos_t.shape[0] // tm
    npb = max(seq // tm, 1)
    const2 = lambda i: (0, 0)
    out_specs = [pl.BlockSpec((H_B, tm, KC_W), lambda i: (0, i, 0)),
                 pl.BlockSpec((tm, KC_W), lambda i: (i, 0)),
                 pl.BlockSpec((None, tm, KV_RANK), lambda i: (layer, i, 0)),
                 pl.BlockSpec((None, tm, ROPE), lambda i: (layer, i, 0))]
    out_shape = [jax.ShapeDtypeStruct((H_B, t_rows, KC_W), BF16),
                 jax.ShapeDtypeStruct((t_rows, KC_W), BF16),
                 jax.ShapeDtypeStruct((depth, t_rows, KV_RANK), F32),
                 jax.ShapeDtypeStruct((depth, t_rows, ROPE), F32)]
    if with_vt:
        out_specs.append(pl.BlockSpec((1, KV_RANK, tm), lambda i: (i // npb, 0, i % npb)))
        out_shape.append(jax.ShapeDtypeStruct((bsz, KV_RANK, seq), BF16))
    n_in = 10
    return pl.pallas_call(
        functools.partial(_mla_pre_body, n_carried=len(carried)),
        input_output_aliases={n_in + k: 2 + k for k in range(len(carried))},
        grid=(t_rows // tm,),
        in_specs=[pl.BlockSpec((tm, Q_RANK), lambda i: (i, COL_CQ // Q_RANK)),
                  pl.BlockSpec((tm, KV_RANK), lambda i: (i, COL_CKV // KV_RANK)),
                  pl.BlockSpec((tm, LANES), lambda i: (i, COL_KRA // LANES)),
                  pl.BlockSpec((tm, LANES), lambda i: (i, COL_KRB // LANES)),
                  pl.BlockSpec((tm, LANES), lambda i: (i % ntab, 0)),
                  pl.BlockSpec((tm, LANES), lambda i: (i % ntab, 0)),
                  pl.BlockSpec((1, Q_RANK), const2),
                  pl.BlockSpec(wuq.shape, const2),
                  pl.BlockSpec(wuk.shape, lambda i: (0, 0, 0)),
                  pl.BlockSpec((1, KV_RANK), const2)] + [pl.BlockSpec(memory_space=pl.ANY)] * len(carried),
        out_specs=out_specs,
        out_shape=out_shape,
        compiler_params=_cparams(("parallel",)),
        name="mla_pre",
    )(h1, h1, h1, h1, cos_t, sin_t, qg, wuq, wuk, kvg, *carried)


def _attn_body(q_ref, k_ref, vt_ref, wuv_ref, o_ref, m_ref, l_ref, acc_ref, ex_ref, *, tq, tk, cw, lk):
    i = pl.program_id(1)
    shift = CHUNK.bit_length() - 1

    def update(j, masked, lazy, opening=False):
        k0 = pl.multiple_of(j * tk, tk)
        kt = k_ref[0, pl.ds(k0, tk), :]
        vt = vt_ref[0, :, pl.ds(k0, tk)]
        bias = None
        if masked:
            kpos = k0 + lax.broadcasted_iota(jnp.int32, (tk, tq), 0)
            qpos = i * tq + lax.broadcasted_iota(jnp.int32, (tk, tq), 1)
            bias = jnp.where((kpos >> shift) <= (qpos >> shift), 0.0, -jnp.inf).astype(F32)

        units = [(h, slice(c, c + cw)) for h in range(H_B) for c in range(0, tq, cw)]

        def scores(u):
            h, cs = u
            s = lax.dot_general(kt, q_ref[h, cs, :], (((1,), (1,)), ((), ())),
                                preferred_element_type=F32)
            return s if bias is None else s + bias[:, cs]

        def softmax(u, s):
            h, cs = u
            c_max = jnp.max(s, 0, keepdims=True)
            if opening:
                m_prev = jnp.zeros_like(c_max)
                m_new = c_max
                gap = jnp.abs(c_max)
            else:
                m_prev = m_ref[h, :, cs]
                m_new = jnp.maximum(m_prev, c_max)
                gap = c_max - m_prev
            alpha = jnp.exp2(m_prev - m_new)
            if lazy:
                p = jnp.exp2(s - m_prev)
                ex_ref[h, :, cs] = jnp.maximum(ex_ref[h, :, cs], gap)
                l_ref[h, :, cs] = alpha * (l_ref[h, :, cs] + jnp.sum(p, 0, keepdims=True))
            else:
                p = jnp.exp2(s - m_new)
                l_ref[h, :, cs] = alpha * l_ref[h, :, cs] + jnp.sum(p, 0, keepdims=True)
            m_ref[h, :, cs] = m_new
            return p.astype(BF16), alpha

        def accumulate(u, alpha, pv):
            h, cs = u
            if lazy:
                acc_ref[h, :, cs] = alpha * (acc_ref[h, :, cs] + pv)
            else:
                acc_ref[h, :, cs] = alpha * acc_ref[h, :, cs] + pv

        n_units = len(units)
        s_q = {0: scores(units[0])}
        if n_units > 1:
            s_q[1] = scores(units[1])
        p0, alpha = softmax(units[0], s_q.pop(0))
        alphas = {0: alpha}
        pvs = {0: jnp.dot(vt, p0, preferred_element_type=F32)}
        for n in range(n_units):
            if n + 2 < n_units:
                s_q[n + 2] = scores(units[n + 2])
            if n + 1 < n_units:
                p_next, alphas[n + 1] = softmax(units[n + 1], s_q.pop(n + 1))
            accumulate(units[n], alphas.pop(n), pvs.pop(n))
            if n + 1 < n_units:
                pvs[n + 1] = jnp.dot(vt, p_next, preferred_element_type=F32)

    n_full = ((((i * tq) >> shift) + 1) << shift) // tk
    n_all = jnp.minimum(((((i * tq + tq - 1) >> shift) + 1) << shift) + tk - 1, lk + tk - 1) // tk

    def tile_loop(lo, hi, masked, lazy, opening=False):
        def step(j, carry):
            update(j, masked, lazy, opening)
            return carry
        lax.fori_loop(lo, hi, step, 0)

    def attempt(a, redo):
        first = a == 0
        run = jnp.logical_or(first, redo > 0)

        @pl.when(run)
        def _():
            m_ref[...] = jnp.full(m_ref.shape, -jnp.inf, F32)
            l_ref[...] = jnp.zeros(l_ref.shape, F32)
            acc_ref[...] = jnp.zeros(acc_ref.shape, F32)
            ex_ref[...] = jnp.full(ex_ref.shape, -jnp.inf, F32)

        tile_loop(0, jnp.where(first, 0, n_all) * run.astype(jnp.int32), True, False)
        tile_loop(0, jnp.where(first, 1, 0), True, True, True)
        tile_loop(1, jnp.where(first, n_full, 0), False, True)
        tile_loop(jnp.maximum(n_full, 1), jnp.where(first, n_all, 0), True, True)
        excess = jnp.max(ex_ref[...])
        return jnp.where(first, (excess > MAX_STALE_EXCESS).astype(jnp.int32), 0)

    lax.fori_loop(0, 2, attempt, jnp.int32(0))

    for h in range(H_B):
        o_t = (acc_ref[h] * (1.0 / l_ref[h])).astype(BF16)
        ob_t = jnp.dot(wuv_ref[h], o_t, preferred_element_type=F32)
        o_ref[0, :, h * V_B:(h + 1) * V_B] = ob_t.T.astype(o_ref.dtype)


def _attn(q, kc, vt, wuv_t, bsz, seq, tq, tk, cw):
    assert seq % tq == 0 and seq % tk == 0
    nq = seq // tq
    body = functools.partial(_attn_body, tq=tq, tk=tk, cw=cw, lk=seq)
    return pl.pallas_call(
        body,
        grid=(bsz, nq),
        in_specs=[pl.BlockSpec((H_B, tq, KC_W), lambda b, i: (0, b * nq + i, 0)),
                  pl.BlockSpec((1, seq, KC_W), lambda b, i: (b, 0, 0)),
                  pl.BlockSpec((1, KV_RANK, seq), lambda b, i: (b, 0, 0)),
                  pl.BlockSpec(wuv_t.shape, lambda b, i: (0, 0, 0))],
        out_specs=pl.BlockSpec((1, tq, H_B * V_B), lambda b, i: (b, i, 0)),
        out_shape=jax.ShapeDtypeStruct((bsz, seq, H_B * V_B), BF16),
        scratch_shapes=[pltpu.VMEM((H_B, 1, tq), F32), pltpu.VMEM((H_B, 1, tq), F32),
                        pltpu.VMEM((H_B, KV_RANK, tq), F32), pltpu.VMEM((H_B, 1, tq), F32)],
        compiler_params=_cparams(("parallel", "arbitrary")),
        name="attn",
    )(q, kc, vt, wuv_t)


def _attn_dec_body(q_ref, ckv_ref, krt_ref, kn_ref, wuv_ref, o_ref, m_ref, l_ref, acc_ref,
                   *, tq, past, n_past):
    j = pl.program_id(1)
    rows = H_B * tq
    nt_dims = (((1,), (1,)), ((), ()))
    q2 = q_ref[...].reshape(rows, KC_W)

    @pl.when(j == 0)
    def _():
        m_ref[...] = jnp.full(m_ref.shape, -jnp.inf, F32)
        l_ref[...] = jnp.zeros(l_ref.shape, F32)
        acc_ref[...] = jnp.zeros(acc_ref.shape, F32)

    def accumulate(s, v_nat):
        m_prev = m_ref[...]
        m_new = jnp.maximum(m_prev, jnp.max(s, -1, keepdims=True))
        p = jnp.exp2(s - m_new)
        alpha = jnp.exp2(m_prev - m_new)
        l_ref[...] = alpha * l_ref[...] + jnp.sum(p, -1, keepdims=True)
        acc_ref[...] = alpha * acc_ref[...] + jnp.dot(p.astype(BF16), v_nat, preferred_element_type=F32)
        m_ref[...] = m_new

    @pl.when(j < n_past)
    def _():
        k_lat = ckv_ref[0].astype(BF16)
        kr_t = krt_ref[0].astype(BF16)
        kr2_t = jnp.concatenate([kr_t, kr_t], axis=0)
        s = (lax.dot_general(q2[:, 0:KV_RANK], k_lat, nt_dims, preferred_element_type=F32)
             + jnp.dot(q2[:, KV_RANK:KC_W], kr2_t, preferred_element_type=F32))
        accumulate(s, k_lat)

    @pl.when(j == n_past)
    def _():
        kn = kn_ref[...]
        s = lax.dot_general(q2, kn, nt_dims, preferred_element_type=F32)
        shift = CHUNK.bit_length() - 1
        qpos = past + (lax.broadcasted_iota(jnp.int32, s.shape, 0) & (tq - 1))
        kpos = past + lax.broadcasted_iota(jnp.int32, s.shape, 1)
        s = jnp.where((kpos >> shift) <= (qpos >> shift), s, -jnp.inf)
        accumulate(s, kn[:, 0:KV_RANK])
        o = acc_ref[...] * (1.0 / l_ref[...])
        for h in range(H_B):
            oh = o[h * tq:(h + 1) * tq, :].astype(BF16)
            o_ref[:, h * V_B:(h + 1) * V_B] = jnp.dot(
                oh, wuv_ref[h], preferred_element_type=F32).astype(o_ref.dtype)


def _attn_dec(q, cache_ckv, cache_kr_t, layer, kc_new, wuv, bsz, seq, tk):
    past = cache_ckv.shape[2]
    assert past % tk == 0 and past % CHUNK == 0 and seq & (seq - 1) == 0
    n_past = past // tk
    body = functools.partial(_attn_dec_body, tq=seq, past=past, n_past=n_past)
    return pl.pallas_call(
        body,
        grid=(bsz, n_past + 1),
        in_specs=[pl.BlockSpec((H_B, seq, KC_W), lambda b, j: (0, b, 0)),
                  pl.BlockSpec((None, 1, tk, KV_RANK), lambda b, j: (layer, b, jnp.minimum(j, n_past - 1), 0)),
                  pl.BlockSpec((None, 1, ROPE, tk), lambda b, j: (layer, b, 0, jnp.minimum(j, n_past - 1))),
                  pl.BlockSpec((seq, KC_W), lambda b, j: (b, 0)),
                  pl.BlockSpec(wuv.shape, lambda b, j: (0, 0, 0))],
        out_specs=pl.BlockSpec((seq, H_B * V_B), lambda b, j: (b, 0)),
        out_shape=jax.ShapeDtypeStruct((bsz * seq, H_B * V_B), BF16),
        scratch_shapes=[pltpu.VMEM((H_B * seq, 1), F32), pltpu.VMEM((H_B * seq, 1), F32),
                        pltpu.VMEM((H_B * seq, KV_RANK), F32)],
        compiler_params=_cparams(("parallel", "arbitrary")),
        name="attn_dec",
    )(q, cache_ckv, cache_kr_t, kc_new, wuv)


def _layer_norm(r, g, b):
    mu = jnp.mean(r, -1, keepdims=True)
    d = r - mu
    var = jnp.mean(d * d, -1, keepdims=True)
    return d * lax.rsqrt(var + EPS) * g + b


def _merge_body(oa_ref, ob_ref, ga_ref, gb_ref, x_ref, woa_ref, wob_ref, wout_ref, g_ref, b_ref, o_ref):
    tm = x_ref.shape[0]
    for r0 in range(0, tm, tm // MERGE_SPLIT):
        rs = slice(r0, r0 + tm // MERGE_SPLIT)
        ya = jnp.dot(oa_ref[rs, :], woa_ref[...], preferred_element_type=F32)
        yb = jnp.dot(ob_ref[rs, :], wob_ref[...], preferred_element_type=F32)
        m = _sigmoid(ga_ref[rs, :]) * ya + _sigmoid(gb_ref[rs, :]) * yb
        r = ALPHA * x_ref[rs, :] + jnp.dot(m.astype(BF16), wout_ref[...], preferred_element_type=F32)
        o_ref[rs, :] = _layer_norm(r, g_ref[...], b_ref[...])


def _merge(oa, ob, h1, x, woa, wob, wout, g, b, tm):
    t = x.shape[0]
    row = lambda i: (i, 0)
    const = lambda i: (0, 0)
    wspec = pl.BlockSpec((D_MODEL, D_MODEL), const)
    return pl.pallas_call(
        _merge_body,
        grid=(t // tm,),
        in_specs=[pl.BlockSpec((tm, V_A), row), pl.BlockSpec((tm, H_B * V_B), row),
                  pl.BlockSpec((tm, D_MODEL), lambda i: (i, COL_GA // D_MODEL)),
                  pl.BlockSpec((tm, D_MODEL), lambda i: (i, COL_GB // D_MODEL)),
                  pl.BlockSpec((tm, D_MODEL), row), wspec, wspec, wspec,
                  pl.BlockSpec((1, D_MODEL), const), pl.BlockSpec((1, D_MODEL), const)],
        out_specs=pl.BlockSpec((tm, D_MODEL), row),
        out_shape=jax.ShapeDtypeStruct((t, D_MODEL), F32),
        compiler_params=_cparams(("parallel",)),
        name="merge",
    )(oa, ob, h1, h1, x, woa, wob, wout, g, b)


def _ffn_body(x_ref, wg_ref, wu_ref, wd_ref, g_ref, b_ref, o_ref, *, chunks):
    tm = x_ref.shape[0]
    for r0 in range(0, tm, tm // FFN_SPLIT):
        rs = slice(r0, r0 + tm // FFN_SPLIT)
        x = x_ref[rs, :]
        xb = x.astype(BF16)
        y = ALPHA * x
        off = 0
        for width in chunks:
            sl = slice(off, off + width)
            f1 = jnp.dot(xb, wg_ref[:, sl], preferred_element_type=F32)
            f3 = jnp.dot(xb, wu_ref[:, sl], preferred_element_type=F32)
            hc = (_silu(f1) * f3).astype(BF16)
            y = y + jnp.dot(hc, wd_ref[sl, :], preferred_element_type=F32)
            off += width
        o_ref[rs, :] = _layer_norm(y, g_ref[...], b_ref[...])


def _ffn(x, wg, wu, wd, g, b, tm, chunks):
    assert sum(chunks) == D_FF
    t = x.shape[0]
    row = lambda i: (i, 0)
    const = lambda i: (0, 0)
    single = pl.Buffered(1)
    return pl.pallas_call(
        functools.partial(_ffn_body, chunks=chunks),
        grid=(t // tm,),
        in_specs=[pl.BlockSpec((tm, D_MODEL), row),
                  pl.BlockSpec((D_MODEL, D_FF), const, pipeline_mode=single),
                  pl.BlockSpec((D_MODEL, D_FF), const, pipeline_mode=single),
                  pl.BlockSpec((D_FF, D_MODEL), const, pipeline_mode=single),
                  pl.BlockSpec((1, D_MODEL), const), pl.BlockSpec((1, D_MODEL), const)],
        out_specs=pl.BlockSpec((tm, D_MODEL), row),
        out_shape=jax.ShapeDtypeStruct((t, D_MODEL), F32),
        compiler_params=_cparams(("parallel",)),
        name="ffn",
    )(x, wg, wu, wd, g, b)


def _prep_layer_weights(w_in, conv_w, a_log, dt_bias, gdn_norm_g, w_oa, q_norm_g, w_uq, kv_norm_g,
                        w_ukv, w_ob, w_out, ln1_g, ln1_b, w_gu, w_down, ln2_g, ln2_b):
    seg = lambda i: w_in[:, _OFFS[i]:_OFFS[i + 1]]
    qkv, z, a, b, c_q, c_kv, k_r, g_a, g_b = (seg(i) for i in range(9))
    half = ROPE // 2
    k_r_rot = jnp.concatenate([-k_r[:, half:], k_r[:, :half]], axis=1)
    pad = jnp.zeros((D_MODEL, LANES - 2 * H_A), w_in.dtype)
    w_proj = jnp.concatenate([qkv, z, g_a, g_b, c_kv, k_r, k_r, c_q, k_r_rot, k_r_rot, a, b, pad],
                             axis=1).astype(BF16)
    lane_pad = jnp.zeros((LANES - H_A,), F32)
    al_lane = jnp.concatenate([a_log.astype(F32), lane_pad]).reshape(1, LANES)
    dt_lane = jnp.concatenate([dt_bias.astype(F32), lane_pad]).reshape(1, LANES)
    uq = w_uq.reshape(Q_RANK, H_B, NOPE + ROPE)
    uq_nope = uq[:, :, :NOPE].reshape(Q_RANK, H_B * NOPE)
    uq_rope = uq[:, :, NOPE:]
    uq_rot = jnp.concatenate([-uq_rope[:, :, half:], uq_rope[:, :, :half]], axis=2)
    w_uq_ext = jnp.concatenate([uq_nope, uq_rope.reshape(Q_RANK, H_B * ROPE),
                                uq_rot.reshape(Q_RANK, H_B * ROPE)], axis=1).astype(BF16)
    ukv = w_ukv.reshape(KV_RANK, H_B, NOPE + V_B)
    w_uk_t = jnp.transpose(ukv[:, :, :NOPE], (1, 2, 0)).astype(BF16)
    w_uv = jnp.transpose(ukv[:, :, NOPE:], (1, 0, 2)).astype(BF16)
    w_uv_t = jnp.transpose(ukv[:, :, NOPE:], (1, 2, 0)).astype(BF16)
    return dict(
        w_proj=w_proj, conv_w=conv_w.astype(F32), al_lane=al_lane, dt_lane=dt_lane,
        gn=gdn_norm_g.reshape(1, DV_A).astype(F32), w_oa=w_oa.astype(BF16),
        qg=q_norm_g.reshape(1, Q_RANK).astype(F32), w_uq=w_uq_ext, w_uk_t=w_uk_t, w_uv=w_uv, w_uv_t=w_uv_t,
        kvg=kv_norm_g.reshape(1, KV_RANK).astype(F32), w_ob=w_ob.astype(BF16),
        w_out=w_out.astype(BF16), ln1_g=ln1_g.reshape(1, D_MODEL), ln1_b=ln1_b.reshape(1, D_MODEL),
        w_g=w_gu[:, :D_FF].astype(BF16), w_u=w_gu[:, D_FF:].astype(BF16), w_down=w_down.astype(BF16),
        ln2_g=ln2_g.reshape(1, D_MODEL), ln2_b=ln2_b.reshape(1, D_MODEL))


def _rope_tables(past, seq, reps):
    half = ROPE // 2
    inv = ROPE_THETA ** (-jnp.arange(half, dtype=F32) / half)
    ang = (past + jnp.arange(seq)).astype(F32)[:, None] * inv[None, :]
    cos = jnp.tile(jnp.cos(ang), (reps, LANES // half))
    sin = jnp.tile(jnp.sin(ang), (reps, LANES // half))
    return cos, sin


TM_PROJ, TN_PROJ, TM_GATES, TM_MLA, TM_MERGE, TM_FFN = 512, N_PROJ // 4, 4096, 512, 512, 1024
FFN_CHUNKS = (768, 768, 768, D_FF - 3 * 768)
MERGE_SPLIT = 2
FFN_SPLIT = 2
TQ_ATTN, TK_ATTN, CW_ATTN, TK_DEC = 512, 512, 512, 2048
TB_GDN, HB_GDN = 2 * GDN_CHUNK, H_A


def _trunk_layer(x, conv_state, s0, caches, wl, bsz, seq, layer, depth, carried):
    decode = caches is not None
    t_rows = bsz * seq
    h1 = _proj_in(x, wl["w_proj"], min(TM_PROJ, t_rows), TN_PROJ)
    seq_pad = -(-seq // GDN_CHUNK) * GDN_CHUNK
    if seq_pad == seq:
        gates = _gates(h1, COL_AB // LANES, wl["al_lane"], wl["dt_lane"], min(TM_GATES, t_rows), seq, seq)
    else:
        ab = h1[:, COL_AB:COL_AB + LANES].reshape(bsz, seq, LANES)
        ab = jnp.pad(ab, ((0, 0), (0, seq_pad - seq), (0, 0))).reshape(bsz * seq_pad, LANES)
        gates = _gates(ab, 0, wl["al_lane"], wl["dt_lane"], min(TM_GATES, bsz * seq_pad), seq_pad, seq)
    o_a, s_new = _gdn(h1, gates.reshape(2 * H_A, 1, bsz * seq_pad), wl["conv_w"], conv_state, s0, wl["gn"],
                      bsz, seq, min(TB_GDN, seq_pad), HB_GDN)
    if decode:
        cache_ckv, cache_kr_t = caches
        tm_mla = t_rows if t_rows <= TM_MLA else seq
        cos_t, sin_t = _rope_tables(cache_ckv.shape[2], seq, tm_mla // seq)
        q, kc, ckv_all, kr_all = _mla_pre(h1, cos_t, sin_t, wl["qg"], wl["w_uq"], wl["w_uk_t"], wl["kvg"],
                                          bsz, seq, tm_mla, False, layer, depth, carried)
        tk_dec = TK_DEC
        while cache_ckv.shape[2] % tk_dec:
            tk_dec //= 2
        o_b = _attn_dec(q, cache_ckv, cache_kr_t, layer, kc, wl["w_uv"], bsz, seq, tk_dec)
    else:
        cos_t, sin_t = _rope_tables(0, seq, 1)
        q, kc, ckv_all, kr_all, vt = _mla_pre(h1, cos_t, sin_t, wl["qg"], wl["w_uq"], wl["w_uk_t"],
                                              wl["kvg"], bsz, seq, TM_MLA, True, layer, depth, carried)
        o_b = _attn(q, kc.reshape(bsz, seq, KC_W), vt, wl["w_uv_t"], bsz, seq, TQ_ATTN, TK_ATTN, CW_ATTN)
        o_b = o_b.reshape(t_rows, H_B * V_B)
    x1 = _merge(o_a, o_b, h1, x, wl["w_oa"], wl["w_ob"], wl["w_out"], wl["ln1_g"], wl["ln1_b"],
                min(TM_MERGE, t_rows))
    x2 = _ffn(x1, wl["w_g"], wl["w_u"], wl["w_down"], wl["ln2_g"], wl["ln2_b"], min(TM_FFN, t_rows), FFN_CHUNKS)
    conv_new = h1.reshape(bsz, seq, N_PROJ)[:, seq - (CONV_W - 1):, COL_QKV:COL_QKV + C_QKV]
    return x2, conv_new, s_new, (ckv_all, kr_all)


def kernel(x_prompt, x_sample, state_conv, state_gdn, cache_ckv, cache_krope, w_in, conv_w, a_log, dt_bias, gdn_norm_g, w_oa, q_norm_g, w_uq, kv_norm_g, w_ukv, w_ob, w_out, ln1_g, ln1_b, w_gu, w_down, ln2_g, ln2_b):
    bp, lp, _ = x_prompt.shape
    bs, ls, _ = x_sample.shape
    yp = x_prompt.reshape(bp * lp, D_MODEL)
    ys = x_sample.reshape(bs * ls, D_MODEL)
    zero_conv = jnp.zeros((bp, CONV_W - 1, C_QKV), F32)
    zero_s = jnp.zeros((bp, H_A, DK_A, DV_A), F32)
    cache_kr_t = jnp.swapaxes(cache_krope, 2, 3)
    depth = w_in.shape[0]
    conv_p, gdn_p, conv_s, gdn_s = [], [], [], []
    kv_p = kv_s = ()
    for l in range(depth):
        wl = _prep_layer_weights(w_in[l], conv_w[l], a_log[l], dt_bias[l], gdn_norm_g[l], w_oa[l],
                                 q_norm_g[l], w_uq[l], kv_norm_g[l], w_ukv[l], w_ob[l], w_out[l],
                                 ln1_g[l], ln1_b[l], w_gu[l], w_down[l], ln2_g[l], ln2_b[l])
        yp, c_new, g_new, kv_p = _trunk_layer(yp, zero_conv, zero_s, None, wl, bp, lp, l, depth, kv_p)
        conv_p.append(c_new), gdn_p.append(g_new)
        ys, c_new, g_new, kv_s = _trunk_layer(ys, state_conv[l], state_gdn[l], (cache_ckv, cache_kr_t), wl,
                                              bs, ls, l, depth, kv_s)
        conv_s.append(c_new), gdn_s.append(g_new)
    return (yp.reshape(bp, lp, D_MODEL), ys.reshape(bs, ls, D_MODEL),
            jnp.stack(conv_p), jnp.stack(gdn_p),
            kv_p[0].reshape(depth, bp, lp, KV_RANK), kv_p[1].reshape(depth, bp, lp, ROPE),
            jnp.stack(conv_s), jnp.stack(gdn_s),
            kv_s[0].reshape(depth, bs, ls, KV_RANK), kv_s[1].reshape(depth, bs, ls, ROPE))
```

```python
import functools

import numpy as np
import jax
import jax.numpy as jnp
from jax import lax
from jax.experimental import pallas as pl
from jax.experimental.pallas import tpu as pltpu

F32 = jnp.float32
BF16 = jnp.bfloat16

D_MODEL = 1024
DEPTH = 2
CHUNK = 64
H_A = 8
DK_A = 128
DV_A = 128
QK_A = H_A * DK_A
V_A = H_A * DV_A
C_QKV = 2 * QK_A + V_A
CONV_W = 4
H_B = 8
NOPE = 128
ROPE = 64
V_B = 128
Q_RANK = 384
KV_RANK = 256
ROPE_THETA = 10000.0
ATTN_SCALE = (NOPE + ROPE) ** -0.5
LOG2_E = float(np.log2(np.e))
Q_SCALE = ATTN_SCALE * LOG2_E
D_FF = -(-8 * D_MODEL // (3 * 256)) * 256
ALPHA = (2 * DEPTH) ** 0.25
EPS = 1e-6
_SIZES = (C_QKV, V_A, H_A, H_A, Q_RANK, KV_RANK, ROPE, D_MODEL, D_MODEL)
_OFFS = tuple(int(v) for v in np.cumsum((0,) + _SIZES))

LANES = 128
VMEM_LIMIT = 56 * 1024 * 1024

COL_QKV = 0
COL_Z = COL_QKV + C_QKV
COL_GA = COL_Z + V_A
COL_GB = COL_GA + D_MODEL
COL_CKV = COL_GB + D_MODEL
COL_KRA = COL_CKV + KV_RANK
COL_CQ = COL_KRA + LANES
COL_KRB = COL_CQ + Q_RANK
COL_AB = COL_KRB + LANES
N_PROJ = COL_AB + LANES
KC_W = KV_RANK + LANES

GDN_CHUNK = 128
MAX_STALE_EXCESS = 64.0


def _cparams(sem):
    return pltpu.CompilerParams(dimension_semantics=sem, vmem_limit_bytes=VMEM_LIMIT)


def _sigmoid(x):
    return jax.nn.sigmoid(x)


def _silu(x):
    return x * jax.nn.sigmoid(x)


def _mm(a, b):
    return jnp.dot(a.astype(BF16), b.astype(BF16), preferred_element_type=F32)


def _mm_nt(a, b):
    return lax.dot_general(a.astype(BF16), b.astype(BF16), (((1,), (1,)), ((), ())),
                           preferred_element_type=F32)


def _mm_tn(a, b):
    return lax.dot_general(a.astype(BF16), b.astype(BF16), (((0,), (0,)), ((), ())),
                           preferred_element_type=F32)


def _proj_body(x_ref, wt_ref, o_ref, *, tn):
    xb = x_ref[...].astype(BF16)
    for c in range(0, wt_ref.shape[0], tn):
        o_ref[:, c:c + tn] = lax.dot_general(xb, wt_ref[c:c + tn, :], (((1,), (1,)), ((), ())),
                                             preferred_element_type=F32)


def _proj_in(x, w_t, tm, tn):
    t, k = x.shape
    n = w_t.shape[0]
    return pl.pallas_call(
        functools.partial(_proj_body, tn=tn),
        grid=(t // tm,),
        in_specs=[pl.BlockSpec((tm, k), lambda i: (i, 0)),
                  pl.BlockSpec((n, k), lambda i: (0, 0), pipeline_mode=pl.Buffered(1))],
        out_specs=pl.BlockSpec((tm, n), lambda i: (i, 0)),
        out_shape=jax.ShapeDtypeStruct((t, n), F32),
        compiler_params=_cparams(("parallel",)),
        name="proj_in",
    )(x, w_t)


def _gates_body(ab_ref, al_ref, dt_ref, o_ref, *, tm, l_pad, l_valid):
    x = ab_ref[...]
    lane = lax.broadcasted_iota(jnp.int32, x.shape, 1)
    xa = x + dt_ref[...]
    sp = jnp.maximum(xa, 0.0) + jnp.log1p(jnp.exp(-jnp.abs(xa)))
    g = -jnp.exp(al_ref[...]) * sp * LOG2_E
    y = jnp.where(lane < H_A, g, _sigmoid(x))
    yt = y.T[0:2 * H_A, :]
    if l_valid < l_pad:
        col = lax.broadcasted_iota(jnp.int32, yt.shape, 1) + pl.program_id(0) * tm
        yt = jnp.where(col % l_pad < l_valid, yt, 0.0)
    r = lax.broadcasted_iota(jnp.int32, (GDN_CHUNK, GDN_CHUNK), 0)
    c = lax.broadcasted_iota(jnp.int32, (GDN_CHUNK, GDN_CHUNK), 1)
    tri = jnp.where(r <= c, 1.0, 0.0).astype(F32)
    for s in range(tm // GDN_CHUNK):
        sl = slice(s * GDN_CHUNK, (s + 1) * GDN_CHUNK)
        o_ref[0:H_A, sl] = jnp.dot(yt[0:H_A, sl], tri, precision=lax.Precision.HIGHEST,
                                   preferred_element_type=F32)
    o_ref[H_A:2 * H_A, :] = yt[H_A:2 * H_A, :]


def _gates(src, col_block, al_lane, dt_lane, tm, l_pad, l_valid):
    t = src.shape[0]
    body = functools.partial(_gates_body, tm=tm, l_pad=l_pad, l_valid=l_valid)
    return pl.pallas_call(
        body,
        grid=(t // tm,),
        in_specs=[pl.BlockSpec((tm, LANES), lambda i: (i, col_block)),
                  pl.BlockSpec((1, LANES), lambda i: (0, 0)),
                  pl.BlockSpec((1, LANES), lambda i: (0, 0))],
        out_specs=pl.BlockSpec((2 * H_A, tm), lambda i: (0, i)),
        out_shape=jax.ShapeDtypeStruct((2 * H_A, t), F32),
        compiler_params=_cparams(("parallel",)),
        name="gates",
    )(src, al_lane, dt_lane)


def _gdn_body(q_ref, k_ref, v_ref, z_ref, gc_ref, bt_ref, cwq_ref, cwk_ref, cwv_ref,
              csq_ref, csk_ref, csv_ref, s0_ref, gn_ref, o_ref, so_ref,
              xq_ref, xk_ref, xv_ref, s_ref, *, tb, tr, nt, hb):
    t = pl.program_id(2)
    hist = CONV_W - 1
    base = 8

    @pl.when(t == 0)
    def _():
        s_ref[...] = s0_ref[0]
        xq_ref[base - hist:base, :] = csq_ref[0]
        xk_ref[base - hist:base, :] = csk_ref[0]
        xv_ref[base - hist:base, :] = csv_ref[0]

    for xs_ref, raw_ref in ((xq_ref, q_ref), (xk_ref, k_ref), (xv_ref, v_ref)):
        xs_ref[base:base + tr, :] = raw_ref[...]
        if tr < tb:
            xs_ref[base + tr:base + tb, :] = jnp.zeros((tb - tr, xs_ref.shape[1]), F32)

    def conv(xs_ref, cw_ref, ln):
        slab = xs_ref[:, ln]
        y = pltpu.roll(slab, hist, 0)[base:base + tb] * cw_ref[0:1, ln]
        for j in range(1, CONV_W):
            tap = slab if j == hist else pltpu.roll(slab, hist - j, 0)
            y = y + tap[base:base + tb] * cw_ref[j:j + 1, ln]
        return _silu(y)

    cc = GDN_CHUNK
    ii = lax.broadcasted_iota(jnp.int32, (cc, cc), 0)
    jj = lax.broadcasted_iota(jnp.int32, (cc, cc), 1)
    strict = ii > jj
    eye = jnp.where(ii == jj, 1.0, 0.0).astype(F32)
    heads = range(hb)
    lanes = [slice(hd * LANES, (hd + 1) * LANES) for hd in heads]
    units = [(hd, c) for hd in heads for c in range(tb // cc)]
    rows = {u: slice(u[1] * cc, (u[1] + 1) * cc) for u in units}
    q = [conv(xq_ref, cwq_ref, ln) for ln in lanes]
    k = [conv(xk_ref, cwk_ref, ln) for ln in lanes]
    v = [conv(xv_ref, cwv_ref, ln) for ln in lanes]
    q = [x * (lax.rsqrt(jnp.sum(x * x, -1, keepdims=True) + EPS) * (DK_A ** -0.5)) for x in q]
    k = [x * lax.rsqrt(jnp.sum(x * x, -1, keepdims=True) + EPS) for x in k]
    qc = {u: q[u[0]][rows[u]] for u in units}
    kc = {u: k[u[0]][rows[u]] for u in units}
    vc = {u: v[u[0]][rows[u]] for u in units}
    m_row = {u: jnp.broadcast_to(gc_ref[u[0], :, rows[u]], (cc, cc)) for u in units}
    m_col = {u: m_row[u].T for u in units}
    b_col = {u: jnp.broadcast_to(bt_ref[u[0], :, rows[u]], (cc, cc)).T for u in units}
    kq = {u: _mm_nt(jnp.concatenate([kc[u], qc[u]], axis=0), kc[u]) for u in units}
    diff = {u: m_col[u] - m_row[u] for u in units}
    dec = {u: jnp.exp2(jnp.where(strict, diff[u], -jnp.inf)) for u in units}
    a_mat = {u: b_col[u] * kq[u][0:cc] * dec[u] for u in units}
    qk = {u: kq[u][cc:2 * cc] * (dec[u] + eye) for u in units}
    e_g = {u: jnp.exp2(m_col[u]) for u in units}
    x_inv = {u: eye - jnp.where((ii >> 1) == (jj >> 1), a_mat[u], 0.0) for u in units}
    sft = 1
    while (1 << sft) < cc:
        off = ((ii >> (sft + 1)) == (jj >> (sft + 1))) & ((ii >> sft) != (jj >> sft))
        lx = {u: _mm(jnp.where(off, a_mat[u], 0.0), x_inv[u]) for u in units}
        x_inv = {u: x_inv[u] - _mm(x_inv[u], lx[u]) for u in units}
        sft += 1
    uw = {u: _mm(x_inv[u], jnp.concatenate([vc[u] * b_col[u], kc[u] * (b_col[u] * e_g[u])], axis=1))
          for u in units}
    wq = {u: jnp.concatenate([uw[u][:, DV_A:], qc[u] * e_g[u]], axis=0) for u in units}
    g_last = {u: m_col[u][cc - 1:cc, :] for u in units}
    kd = {u: kc[u] * jnp.exp2(g_last[u] - m_col[u]) for u in units}
    s_state = [s_ref[hd] for hd in heads]
    for c in range(tb // cc):
        ws = [_mm(wq[(hd, c)], s_state[hd]) for hd in heads]
        v_new = [uw[(hd, c)][:, 0:DV_A] - ws[hd][0:cc] for hd in heads]
        s_state = [s_state[hd] * jnp.exp2(g_last[(hd, c)]) + _mm_tn(kd[(hd, c)], v_new[hd]) for hd in heads]
        o = [ws[hd][cc:2 * cc] + _mm(qk[(hd, c)], v_new[hd]) for hd in heads]
        o = [x * lax.rsqrt(jnp.mean(x * x, -1, keepdims=True) + EPS) * gn_ref[...] for x in o]
        nr = min(cc, tr - c * cc)
        r = slice(c * cc, c * cc + nr)
        for hd in heads:
            o_ref[r, lanes[hd]] = (o[hd][0:nr] * _silu(z_ref[r, lanes[hd]])).astype(o_ref.dtype)
    for hd in heads:
        s_ref[hd] = s_state[hd]

    xq_ref[base - hist:base, :] = xq_ref[base + tb - hist:base + tb, :]
    xk_ref[base - hist:base, :] = xk_ref[base + tb - hist:base + tb, :]
    xv_ref[base - hist:base, :] = xv_ref[base + tb - hist:base + tb, :]

    @pl.when(t == nt - 1)
    def _():
        so_ref[0] = s_ref[...]


def _gdn(h1, gates3, conv_w, conv_state, s0, gn, bsz, seq, tb, hb):
    nt = -(-seq // tb)
    tr = min(tb, seq)
    assert seq % tr == 0 and (tr == tb or (nt == 1 and tb == GDN_CHUNK))
    t_rows = bsz * seq
    body = functools.partial(_gdn_body, tb=tb, tr=tr, nt=nt, hb=hb)
    width = hb * LANES
    ng = H_A // hb

    def rows(seg):
        return pl.BlockSpec((tr, width), lambda b, g, t: (b * nt + t, seg * ng + g))

    def cw(seg):
        return pl.BlockSpec((CONV_W, width), lambda b, g, t: (0, seg * ng + g))

    def cs(seg):
        return pl.BlockSpec((1, CONV_W - 1, width), lambda b, g, t: (b, 0, seg * ng + g))

    return pl.pallas_call(
        body,
        grid=(bsz, ng, nt),
        in_specs=[rows(0), rows(1), rows(2), rows(COL_Z // QK_A),
                  pl.BlockSpec((hb, 1, tb), lambda b, g, t: (g, 0, b * nt + t)),
                  pl.BlockSpec((hb, 1, tb), lambda b, g, t: (ng + g, 0, b * nt + t)),
                  cw(0), cw(1), cw(2), cs(0), cs(1), cs(2),
                  pl.BlockSpec((1, hb, DK_A, DV_A), lambda b, g, t: (b, g, 0, 0)),
                  pl.BlockSpec((1, DV_A), lambda b, g, t: (0, 0))],
        out_specs=[pl.BlockSpec((tr, width), lambda b, g, t: (b * nt + t, g)),
                   pl.BlockSpec((1, hb, DK_A, DV_A), lambda b, g, t: (b, g, 0, 0))],
        out_shape=[jax.ShapeDtypeStruct((t_rows, V_A), BF16),
                   jax.ShapeDtypeStruct((bsz, H_A, DK_A, DV_A), F32)],
        scratch_shapes=[pltpu.VMEM((tb + 8, width), F32)] * 3 + [pltpu.VMEM((hb, DK_A, DV_A), F32)],
        compiler_params=_cparams(("parallel", "parallel", "arbitrary")),
        name="gdn",
    )(h1, h1, h1, h1, gates3, gates3, conv_w, conv_w, conv_w,
      conv_state, conv_state, conv_state, s0, gn)


def _mla_pre_body(cq_ref, ckv_ref, kra_ref, krb_ref, cos_ref, sin_ref, qg_ref, wuq_ref, wuk_ref,
                  kvg_ref, *refs, n_carried):
    q_ref, kc_ref, ckvo_ref, kro_ref, *maybe_vt_ref = refs[n_carried:]
    cq = cq_ref[...]
    cqn = cq * lax.rsqrt(jnp.mean(cq * cq, -1, keepdims=True) + EPS) * qg_ref[...]
    qf = jnp.dot(cqn.astype(BF16), wuq_ref[...], preferred_element_type=F32)
    cos_k = cos_ref[...]
    sin_k = sin_ref[...]
    reps = H_B * ROPE // LANES
    cos_t = jnp.concatenate([cos_k] * reps, axis=1)
    sin_t = jnp.concatenate([sin_k] * reps, axis=1)
    n_nope = H_B * NOPE
    n_rope = H_B * ROPE
    qr = (qf[:, n_nope:n_nope + n_rope] * cos_t + qf[:, n_nope + n_rope:] * sin_t) * Q_SCALE
    lane = lax.broadcasted_iota(jnp.int32, (cq.shape[0], LANES), 1)
    for h in range(H_B):
        ql = jnp.dot(qf[:, h * NOPE:(h + 1) * NOPE].astype(BF16), wuk_ref[h],
                     preferred_element_type=F32) * Q_SCALE
        blk = qr[:, (h // 2) * LANES:(h // 2 + 1) * LANES]
        keep = (lane < ROPE) if h % 2 == 0 else (lane >= ROPE)
        q_ref[h, :, 0:KV_RANK] = ql.astype(BF16)
        q_ref[h, :, KV_RANK:KC_W] = jnp.where(keep, blk, 0.0).astype(BF16)
    ckv = ckv_ref[...]
    ckvn = ckv * lax.rsqrt(jnp.mean(ckv * ckv, -1, keepdims=True) + EPS) * kvg_ref[...]
    kr2 = kra_ref[...] * cos_k + krb_ref[...] * sin_k
    ckvo_ref[...] = ckvn
    kro_ref[...] = kr2[:, 0:ROPE]
    kc_ref[:, 0:KV_RANK] = ckvn.astype(BF16)
    kc_ref[:, KV_RANK:KC_W] = kr2.astype(BF16)
    if maybe_vt_ref:
        maybe_vt_ref[0][0] = ckvn.T.astype(BF16)


def _mla_pre(h1, cos_t, sin_t, qg, wuq, wuk, kvg, bsz, seq, tm, with_vt, layer, depth, carried):
    t_rows = bsz * seq
    ntab = cos_t.shape[0] // tm
    npb = max(seq // tm, 1)
    const2 = lambda i: (0, 0)
    out_specs = [pl.BlockSpec((H_B, tm, KC_W), lambda i: (0, i, 0)),
                 pl.BlockSpec((tm, KC_W), lambda i: (i, 0)),
                 pl.BlockSpec((None, tm, KV_RANK), lambda i: (layer, i, 0)),
                 pl.BlockSpec((None, tm, ROPE), lambda i: (layer, i, 0))]
    out_shape = [jax.ShapeDtypeStruct((H_B, t_rows, KC_W), BF16),
                 jax.ShapeDtypeStruct((t_rows, KC_W), BF16),
                 jax.ShapeDtypeStruct((depth, t_rows, KV_RANK), F32),
                 jax.ShapeDtypeStruct((depth, t_rows, ROPE), F32)]
    if with_vt:
        out_specs.append(pl.BlockSpec((1, KV_RANK, tm), lambda i: (i // npb, 0, i % npb)))
        out_shape.append(jax.ShapeDtypeStruct((bsz, KV_RANK, seq), BF16))
    n_in = 10
    return pl.pallas_call(
        functools.partial(_mla_pre_body, n_carried=len(carried)),
        input_output_aliases={n_in + k: 2 + k for k in range(len(carried))},
        grid=(t_rows // tm,),
        in_specs=[pl.BlockSpec((tm, Q_RANK), lambda i: (i, COL_CQ // Q_RANK)),
                  pl.BlockSpec((tm, KV_RANK), lambda i: (i, COL_CKV // KV_RANK)),
                  pl.BlockSpec((tm, LANES), lambda i: (i, COL_KRA // LANES)),
                  pl.BlockSpec((tm, LANES), lambda i: (i, COL_KRB // LANES)),
                  pl.BlockSpec((tm, LANES), lambda i: (i % ntab, 0)),
                  pl.BlockSpec((tm, LANES), lambda i: (i % ntab, 0)),
                  pl.BlockSpec((1, Q_RANK), const2),
                  pl.BlockSpec(wuq.shape, const2),
                  pl.BlockSpec(wuk.shape, lambda i: (0, 0, 0)),
                  pl.BlockSpec((1, KV_RANK), const2)] + [pl.BlockSpec(memory_space=pl.ANY)] * len(carried),
        out_specs=out_specs,
        out_shape=out_shape,
        compiler_params=_cparams(("parallel",)),
        name="mla_pre",
    )(h1, h1, h1, h1, cos_t, sin_t, qg, wuq, wuk, kvg, *carried)


def _attn_body(q_ref, k_ref, vt_ref, wuv_ref, o_ref, m_ref, l_ref, acc_ref, ex_ref, *, tq, tk, cw, lk):
    i = pl.program_id(1)
    shift = CHUNK.bit_length() - 1

    def update(j, masked, lazy, opening=False):
        k0 = pl.multiple_of(j * tk, tk)
        kt = k_ref[0, pl.ds(k0, tk), :]
        vt = vt_ref[0, :, pl.ds(k0, tk)]
        bias = None
        if masked:
            kpos = k0 + lax.broadcasted_iota(jnp.int32, (tk, tq), 0)
            qpos = i * tq + lax.broadcasted_iota(jnp.int32, (tk, tq), 1)
            bias = jnp.where((kpos >> shift) <= (qpos >> shift), 0.0, -jnp.inf).astype(F32)

        units = [(h, slice(c, c + cw)) for h in range(H_B) for c in range(0, tq, cw)]

        def scores(u):
            h, cs = u
            s = lax.dot_general(kt, q_ref[h, cs, :], (((1,), (1,)), ((), ())),
                                preferred_element_type=F32)
            return s if bias is None else s + bias[:, cs]

        def softmax(u, s):
            h, cs = u
            c_max = jnp.max(s, 0, keepdims=True)
            if opening:
                m_prev = jnp.zeros_like(c_max)
                m_new = c_max
                gap = jnp.abs(c_max)
            else:
                m_prev = m_ref[h, :, cs]
                m_new = jnp.maximum(m_prev, c_max)
                gap = c_max - m_prev
            alpha = jnp.exp2(m_prev - m_new)
            if lazy:
                p = jnp.exp2(s - m_prev)
                ex_ref[h, :, cs] = jnp.maximum(ex_ref[h, :, cs], gap)
                l_ref[h, :, cs] = alpha * (l_ref[h, :, cs] + jnp.sum(p, 0, keepdims=True))
            else:
                p = jnp.exp2(s - m_new)
                l_ref[h, :, cs] = alpha * l_ref[h, :, cs] + jnp.sum(p, 0, keepdims=True)
            m_ref[h, :, cs] = m_new
            return p.astype(BF16), alpha

        def accumulate(u, alpha, pv):
            h, cs = u
            if lazy:
                acc_ref[h, :, cs] = alpha * (acc_ref[h, :, cs] + pv)
            else:
                acc_ref[h, :, cs] = alpha * acc_ref[h, :, cs] + pv

        n_units = len(units)
        s_q = {0: scores(units[0])}
        if n_units > 1:
            s_q[1] = scores(units[1])
        p0, alpha = softmax(units[0], s_q.pop(0))
        alphas = {0: alpha}
        pvs = {0: jnp.dot(vt, p0, preferred_element_type=F32)}
        for n in range(n_units):
            if n + 2 < n_units:
                s_q[n + 2] = scores(units[n + 2])
            if n + 1 < n_units:
                p_next, alphas[n + 1] = softmax(units[n + 1], s_q.pop(n + 1))
            accumulate(units[n], alphas.pop(n), pvs.pop(n))
            if n + 1 < n_units:
                pvs[n + 1] = jnp.dot(vt, p_next, preferred_element_type=F32)

    n_full = ((((i * tq) >> shift) + 1) << shift) // tk
    n_all = jnp.minimum(((((i * tq + tq - 1) >> shift) + 1) << shift) + tk - 1, lk + tk - 1) // tk

    def tile_loop(lo, hi, masked, lazy, opening=False):
        def step(j, carry):
            update(j, masked, lazy, opening)
            return carry
        lax.fori_loop(lo, hi, step, 0)

    def attempt(a, redo):
        first = a == 0
        run = jnp.logical_or(first, redo > 0)

        @pl.when(run)
        def _():
            m_ref[...] = jnp.full(m_ref.shape, -jnp.inf, F32)
            l_ref[...] = jnp.zeros(l_ref.shape, F32)
            acc_ref[...] = jnp.zeros(acc_ref.shape, F32)
            ex_ref[...] = jnp.full(ex_ref.shape, -jnp.inf, F32)

        tile_loop(0, jnp.where(first, 0, n_all) * run.astype(jnp.int32), True, False)
        tile_loop(0, jnp.where(first, 1, 0), True, True, True)
        tile_loop(1, jnp.where(first, n_full, 0), False, True)
        tile_loop(jnp.maximum(n_full, 1), jnp.where(first, n_all, 0), True, True)
        excess = jnp.max(ex_ref[...])
        return jnp.where(first, (excess > MAX_STALE_EXCESS).astype(jnp.int32), 0)

    lax.fori_loop(0, 2, attempt, jnp.int32(0))

    for h in range(H_B):
        o_t = (acc_ref[h] * (1.0 / l_ref[h])).astype(BF16)
        ob_t = jnp.dot(wuv_ref[h], o_t, preferred_element_type=F32)
        o_ref[0, :, h * V_B:(h + 1) * V_B] = ob_t.T.astype(o_ref.dtype)


def _attn(q, kc, vt, wuv_t, bsz, seq, tq, tk, cw):
    assert seq % tq == 0 and seq % tk == 0
    nq = seq // tq
    body = functools.partial(_attn_body, tq=tq, tk=tk, cw=cw, lk=seq)
    return pl.pallas_call(
        body,
        grid=(bsz, nq),
        in_specs=[pl.BlockSpec((H_B, tq, KC_W), lambda b, i: (0, b * nq + i, 0)),
                  pl.BlockSpec((1, seq, KC_W), lambda b, i: (b, 0, 0)),
                  pl.BlockSpec((1, KV_RANK, seq), lambda b, i: (b, 0, 0)),
                  pl.BlockSpec(wuv_t.shape, lambda b, i: (0, 0, 0))],
        out_specs=pl.BlockSpec((1, tq, H_B * V_B), lambda b, i: (b, i, 0)),
        out_shape=jax.ShapeDtypeStruct((bsz, seq, H_B * V_B), BF16),
        scratch_shapes=[pltpu.VMEM((H_B, 1, tq), F32), pltpu.VMEM((H_B, 1, tq), F32),
                        pltpu.VMEM((H_B, KV_RANK, tq), F32), pltpu.VMEM((H_B, 1, tq), F32)],
        compiler_params=_cparams(("parallel", "arbitrary")),
        name="attn",
    )(q, kc, vt, wuv_t)


def _attn_dec_body(q_ref, ckv_ref, krt_ref, kn_ref, wuv_ref, o_ref, m_ref, l_ref, acc_ref,
                   *, tq, past, n_past):
    j = pl.program_id(1)
    rows = H_B * tq
    nt_dims = (((1,), (1,)), ((), ()))
    q2 = q_ref[...].reshape(rows, KC_W)

    @pl.when(j == 0)
    def _():
        m_ref[...] = jnp.full(m_ref.shape, -jnp.inf, F32)
        l_ref[...] = jnp.zeros(l_ref.shape, F32)
        acc_ref[...] = jnp.zeros(acc_ref.shape, F32)

    def accumulate(s, v_nat):
        m_prev = m_ref[...]
        m_new = jnp.maximum(m_prev, jnp.max(s, -1, keepdims=True))
        p = jnp.exp2(s - m_new)
        alpha = jnp.exp2(m_prev - m_new)
        l_ref[...] = alpha * l_ref[...] + jnp.sum(p, -1, keepdims=True)
        acc_ref[...] = alpha * acc_ref[...] + jnp.dot(p.astype(BF16), v_nat, preferred_element_type=F32)
        m_ref[...] = m_new

    @pl.when(j < n_past)
    def _():
        k_lat = ckv_ref[0].astype(BF16)
        kr_t = krt_ref[0].astype(BF16)
        kr2_t = jnp.concatenate([kr_t, kr_t], axis=0)
        s = (lax.dot_general(q2[:, 0:KV_RANK], k_lat, nt_dims, preferred_element_type=F32)
             + jnp.dot(q2[:, KV_RANK:KC_W], kr2_t, preferred_element_type=F32))
        accumulate(s, k_lat)

    @pl.when(j == n_past)
    def _():
        kn = kn_ref[...]
        s = lax.dot_general(q2, kn, nt_dims, preferred_element_type=F32)
        shift = CHUNK.bit_length() - 1
        qpos = past + (lax.broadcasted_iota(jnp.int32, s.shape, 0) & (tq - 1))
        kpos = past + lax.broadcasted_iota(jnp.int32, s.shape, 1)
        s = jnp.where((kpos >> shift) <= (qpos >> shift), s, -jnp.inf)
        accumulate(s, kn[:, 0:KV_RANK])
        o = acc_ref[...] * (1.0 / l_ref[...])
        for h in range(H_B):
            oh = o[h * tq:(h + 1) * tq, :].astype(BF16)
            o_ref[:, h * V_B:(h + 1) * V_B] = jnp.dot(
                oh, wuv_ref[h], preferred_element_type=F32).astype(o_ref.dtype)


def _attn_dec(q, cache_ckv, cache_kr_t, layer, kc_new, wuv, bsz, seq, tk):
    past = cache_ckv.shape[2]
    assert past % tk == 0 and past % CHUNK == 0 and seq & (seq - 1) == 0
    n_past = past // tk
    body = functools.partial(_attn_dec_body, tq=seq, past=past, n_past=n_past)
    return pl.pallas_call(
        body,
        grid=(bsz, n_past + 1),
        in_specs=[pl.BlockSpec((H_B, seq, KC_W), lambda b, j: (0, b, 0)),
                  pl.BlockSpec((None, 1, tk, KV_RANK), lambda b, j: (layer, b, jnp.minimum(j, n_past - 1), 0)),
                  pl.BlockSpec((None, 1, ROPE, tk), lambda b, j: (layer, b, 0, jnp.minimum(j, n_past - 1))),
                  pl.BlockSpec((seq, KC_W), lambda b, j: (b, 0)),
                  pl.BlockSpec(wuv.shape, lambda b, j: (0, 0, 0))],
        out_specs=pl.BlockSpec((seq, H_B * V_B), lambda b, j: (b, 0)),
        out_shape=jax.ShapeDtypeStruct((bsz * seq, H_B * V_B), BF16),
        scratch_shapes=[pltpu.VMEM((H_B * seq, 1), F32), pltpu.VMEM((H_B * seq, 1), F32),
                        pltpu.VMEM((H_B * seq, KV_RANK), F32)],
        compiler_params=_cparams(("parallel", "arbitrary")),
        name="attn_dec",
    )(q, cache_ckv, cache_kr_t, kc_new, wuv)


def _layer_norm(r, g, b):
    mu = jnp.mean(r, -1, keepdims=True)
    d = r - mu
    var = jnp.mean(d * d, -1, keepdims=True)
    return d * lax.rsqrt(var + EPS) * g + b


def _merge_body(oa_ref, ob_ref, ga_ref, gb_ref, x_ref, woa_ref, wob_ref, wout_ref, g_ref, b_ref, o_ref):
    tm = x_ref.shape[0]
    for r0 in range(0, tm, tm // MERGE_SPLIT):
        rs = slice(r0, r0 + tm // MERGE_SPLIT)
        ya = jnp.dot(oa_ref[rs, :], woa_ref[...], preferred_element_type=F32)
        yb = jnp.dot(ob_ref[rs, :], wob_ref[...], preferred_element_type=F32)
        m = _sigmoid(ga_ref[rs, :]) * ya + _sigmoid(gb_ref[rs, :]) * yb
        r = ALPHA * x_ref[rs, :] + jnp.dot(m.astype(BF16), wout_ref[...], preferred_element_type=F32)
        o_ref[rs, :] = _layer_norm(r, g_ref[...], b_ref[...])


def _merge(oa, ob, h1, x, woa, wob, wout, g, b, tm):
    t = x.shape[0]
    row = lambda i: (i, 0)
    const = lambda i: (0, 0)
    wspec = pl.BlockSpec((D_MODEL, D_MODEL), const)
    return pl.pallas_call(
        _merge_body,
        grid=(t // tm,),
        in_specs=[pl.BlockSpec((tm, V_A), row), pl.BlockSpec((tm, H_B * V_B), row),
                  pl.BlockSpec((tm, D_MODEL), lambda i: (i, COL_GA // D_MODEL)),
                  pl.BlockSpec((tm, D_MODEL), lambda i: (i, COL_GB // D_MODEL)),
                  pl.BlockSpec((tm, D_MODEL), row), wspec, wspec, wspec,
                  pl.BlockSpec((1, D_MODEL), const), pl.BlockSpec((1, D_MODEL), const)],
        out_specs=pl.BlockSpec((tm, D_MODEL), row),
        out_shape=jax.ShapeDtypeStruct((t, D_MODEL), F32),
        compiler_params=_cparams(("parallel",)),
        name="merge",
    )(oa, ob, h1, h1, x, woa, wob, wout, g, b)


def _ffn_body(x_ref, wg_ref, wu_ref, wd_ref, g_ref, b_ref, o_ref, *, chunks):
    tm = x_ref.shape[0]
    for r0 in range(0, tm, tm // FFN_SPLIT):
        rs = slice(r0, r0 + tm // FFN_SPLIT)
        x = x_ref[rs, :]
        xb = x.astype(BF16)
        y = ALPHA * x
        off = 0
        for width in chunks:
            sl = slice(off, off + width)
            f1 = jnp.dot(xb, wg_ref[:, sl], preferred_element_type=F32)
            f3 = jnp.dot(xb, wu_ref[:, sl], preferred_element_type=F32)
            hc = (_silu(f1) * f3).astype(BF16)
            y = y + jnp.dot(hc, wd_ref[sl, :], preferred_element_type=F32)
            off += width
        o_ref[rs, :] = _layer_norm(y, g_ref[...], b_ref[...])


def _ffn(x, wg, wu, wd, g, b, tm, chunks):
    assert sum(chunks) == D_FF
    t = x.shape[0]
    row = lambda i: (i, 0)
    const = lambda i: (0, 0)
    single = pl.Buffered(1)
    return pl.pallas_call(
        functools.partial(_ffn_body, chunks=chunks),
        grid=(t // tm,),
        in_specs=[pl.BlockSpec((tm, D_MODEL), row),
                  pl.BlockSpec((D_MODEL, D_FF), const, pipeline_mode=single),
                  pl.BlockSpec((D_MODEL, D_FF), const, pipeline_mode=single),
                  pl.BlockSpec((D_FF, D_MODEL), const, pipeline_mode=single),
                  pl.BlockSpec((1, D_MODEL), const), pl.BlockSpec((1, D_MODEL), const)],
        out_specs=pl.BlockSpec((tm, D_MODEL), row),
        out_shape=jax.ShapeDtypeStruct((t, D_MODEL), F32),
        compiler_params=_cparams(("parallel",)),
        name="ffn",
    )(x, wg, wu, wd, g, b)


def _prep_layer_weights(w_in_t, conv_w, a_log, dt_bias, gdn_norm_g, w_oa, q_norm_g, w_uq, kv_norm_g,
                        w_ukv, w_ob, w_out, ln1_g, ln1_b, w_gu, w_down, ln2_g, ln2_b):
    seg = lambda i: w_in_t[_OFFS[i]:_OFFS[i + 1], :]
    qkv, z, a, b, c_q, c_kv, k_r, g_a, g_b = (seg(i) for i in range(9))
    half = ROPE // 2
    k_r_rot = jnp.concatenate([-k_r[half:], k_r[:half]], axis=0)
    pad = jnp.zeros((LANES - 2 * H_A, D_MODEL), w_in_t.dtype)
    w_proj = jnp.concatenate([qkv, z, g_a, g_b, c_kv, k_r, k_r, c_q, k_r_rot, k_r_rot, a, b, pad],
                             axis=0).astype(BF16)
    lane_pad = jnp.zeros((LANES - H_A,), F32)
    al_lane = jnp.concatenate([a_log.astype(F32), lane_pad]).reshape(1, LANES)
    dt_lane = jnp.concatenate([dt_bias.astype(F32), lane_pad]).reshape(1, LANES)
    uq = w_uq.reshape(Q_RANK, H_B, NOPE + ROPE)
    uq_nope = uq[:, :, :NOPE].reshape(Q_RANK, H_B * NOPE)
    uq_rope = uq[:, :, NOPE:]
    uq_rot = jnp.concatenate([-uq_rope[:, :, half:], uq_rope[:, :, :half]], axis=2)
    w_uq_ext = jnp.concatenate([uq_nope, uq_rope.reshape(Q_RANK, H_B * ROPE),
                                uq_rot.reshape(Q_RANK, H_B * ROPE)], axis=1).astype(BF16)
    ukv = w_ukv.reshape(KV_RANK, H_B, NOPE + V_B)
    w_uk_t = jnp.transpose(ukv[:, :, :NOPE], (1, 2, 0)).astype(BF16)
    w_uv = jnp.transpose(ukv[:, :, NOPE:], (1, 0, 2)).astype(BF16)
    w_uv_t = jnp.transpose(ukv[:, :, NOPE:], (1, 2, 0)).astype(BF16)
    return dict(
        w_proj=w_proj, conv_w=conv_w.astype(F32), al_lane=al_lane, dt_lane=dt_lane,
        gn=gdn_norm_g.reshape(1, DV_A).astype(F32), w_oa=w_oa.astype(BF16),
        qg=q_norm_g.reshape(1, Q_RANK).astype(F32), w_uq=w_uq_ext, w_uk_t=w_uk_t, w_uv=w_uv, w_uv_t=w_uv_t,
        kvg=kv_norm_g.reshape(1, KV_RANK).astype(F32), w_ob=w_ob.astype(BF16),
        w_out=w_out.astype(BF16), ln1_g=ln1_g.reshape(1, D_MODEL), ln1_b=ln1_b.reshape(1, D_MODEL),
        w_g=w_gu[:, :D_FF].astype(BF16), w_u=w_gu[:, D_FF:].astype(BF16), w_down=w_down.astype(BF16),
        ln2_g=ln2_g.reshape(1, D_MODEL), ln2_b=ln2_b.reshape(1, D_MODEL))


def _rope_tables(past, seq, reps):
    half = ROPE // 2
    inv = ROPE_THETA ** (-jnp.arange(half, dtype=F32) / half)
    ang = (past + jnp.arange(seq)).astype(F32)[:, None] * inv[None, :]
    cos = jnp.tile(jnp.cos(ang), (reps, LANES // half))
    sin = jnp.tile(jnp.sin(ang), (reps, LANES // half))
    return cos, sin


TM_PROJ, TN_PROJ, TM_GATES, TM_MLA, TM_MERGE, TM_FFN = 512, N_PROJ // 4, 4096, 512, 512, 1024
FFN_CHUNKS = (768, 768, 768, D_FF - 3 * 768)
MERGE_SPLIT = 2
FFN_SPLIT = 2
TQ_ATTN, TK_ATTN, CW_ATTN, TK_DEC = 512, 512, 512, 2048
TB_GDN, HB_GDN = 2 * GDN_CHUNK, H_A


def _trunk_layer(x, conv_state, s0, caches, wl, bsz, seq, layer, depth, carried):
    decode = caches is not None
    t_rows = bsz * seq
    h1 = _proj_in(x, wl["w_proj"], min(TM_PROJ, t_rows), TN_PROJ)
    seq_pad = -(-seq // GDN_CHUNK) * GDN_CHUNK
    if seq_pad == seq:
        gates = _gates(h1, COL_AB // LANES, wl["al_lane"], wl["dt_lane"], min(TM_GATES, t_rows), seq, seq)
    else:
        ab = h1[:, COL_AB:COL_AB + LANES].reshape(bsz, seq, LANES)
        ab = jnp.pad(ab, ((0, 0), (0, seq_pad - seq), (0, 0))).reshape(bsz * seq_pad, LANES)
        gates = _gates(ab, 0, wl["al_lane"], wl["dt_lane"], min(TM_GATES, bsz * seq_pad), seq_pad, seq)
    o_a, s_new = _gdn(h1, gates.reshape(2 * H_A, 1, bsz * seq_pad), wl["conv_w"], conv_state, s0, wl["gn"],
                      bsz, seq, min(TB_GDN, seq_pad), HB_GDN)
    if decode:
        cache_ckv, cache_kr_t = caches
        tm_mla = t_rows if t_rows <= TM_MLA else seq
        cos_t, sin_t = _rope_tables(cache_ckv.shape[2], seq, tm_mla // seq)
        q, kc, ckv_all, kr_all = _mla_pre(h1, cos_t, sin_t, wl["qg"], wl["w_uq"], wl["w_uk_t"], wl["kvg"],
                                          bsz, seq, tm_mla, False, layer, depth, carried)
        tk_dec = TK_DEC
        while cache_ckv.shape[2] % tk_dec:
            tk_dec //= 2
        o_b = _attn_dec(q, cache_ckv, cache_kr_t, layer, kc, wl["w_uv"], bsz, seq, tk_dec)
    else:
        cos_t, sin_t = _rope_tables(0, seq, 1)
        q, kc, ckv_all, kr_all, vt = _mla_pre(h1, cos_t, sin_t, wl["qg"], wl["w_uq"], wl["w_uk_t"],
                                              wl["kvg"], bsz, seq, TM_MLA, True, layer, depth, carried)
        o_b = _attn(q, kc.reshape(bsz, seq, KC_W), vt, wl["w_uv_t"], bsz, seq, TQ_ATTN, TK_ATTN, CW_ATTN)
        o_b = o_b.reshape(t_rows, H_B * V_B)
    x1 = _merge(o_a, o_b, h1, x, wl["w_oa"], wl["w_ob"], wl["w_out"], wl["ln1_g"], wl["ln1_b"],
                min(TM_MERGE, t_rows))
    x2 = _ffn(x1, wl["w_g"], wl["w_u"], wl["w_down"], wl["ln2_g"], wl["ln2_b"], min(TM_FFN, t_rows), FFN_CHUNKS)
    conv_new = h1.reshape(bsz, seq, N_PROJ)[:, seq - (CONV_W - 1):, COL_QKV:COL_QKV + C_QKV]
    return x2, conv_new, s_new, (ckv_all, kr_all)


def kernel(x_prompt, x_sample, state_conv, state_gdn, cache_ckv, cache_krope, w_in, conv_w, a_log, dt_bias, gdn_norm_g, w_oa, q_norm_g, w_uq, kv_norm_g, w_ukv, w_ob, w_out, ln1_g, ln1_b, w_gu, w_down, ln2_g, ln2_b):
    bp, lp, _ = x_prompt.shape
    bs, ls, _ = x_sample.shape
    yp = x_prompt.reshape(bp * lp, D_MODEL)
    ys = x_sample.reshape(bs * ls, D_MODEL)
    zero_conv = jnp.zeros((bp, CONV_W - 1, C_QKV), F32)
    zero_s = jnp.zeros((bp, H_A, DK_A, DV_A), F32)
    cache_kr_t = jnp.swapaxes(cache_krope, 2, 3)
    w_in_t = jnp.swapaxes(w_in, 1, 2)
    depth = w_in.shape[0]
    conv_p, gdn_p, conv_s, gdn_s = [], [], [], []
    kv_p = kv_s = ()
    for l in range(depth):
        wl = _prep_layer_weights(w_in_t[l], conv_w[l], a_log[l], dt_bias[l], gdn_norm_g[l], w_oa[l],
                                 q_norm_g[l], w_uq[l], kv_norm_g[l], w_ukv[l], w_ob[l], w_out[l],
                                 ln1_g[l], ln1_b[l], w_gu[l], w_down[l], ln2_g[l], ln2_b[l])
        yp, c_new, g_new, kv_p = _trunk_layer(yp, zero_conv, zero_s, None, wl, bp, lp, l, depth, kv_p)
        conv_p.append(c_new), gdn_p.append(g_new)
        ys, c_new, g_new, kv_s = _trunk_layer(ys, state_conv[l], state_gdn[l], (cache_ckv, cache_kr_t), wl,
                                              bs, ls, l, depth, kv_s)
        conv_s.append(c_new), gdn_s.append(g_new)
    return (yp.reshape(bp, lp, D_MODEL), ys.reshape(bs, ls, D_MODEL),
            jnp.stack(conv_p), jnp.stack(gdn_p),
            kv_p[0].reshape(depth, bp, lp, KV_RANK), kv_p[1].reshape(depth, bp, lp, ROPE),
            jnp.stack(conv_s), jnp.stack(gdn_s),
            kv_s[0].reshape(depth, bs, ls, KV_RANK), kv_s[1].reshape(depth, bs, ls, ROPE))
```

```python
import functools

import numpy as np
import jax
import jax.numpy as jnp
from jax import lax
from jax.experimental import pallas as pl
from jax.experimental.pallas import tpu as pltpu

F32 = jnp.float32
BF16 = jnp.bfloat16

D_MODEL = 1024
DEPTH = 2
CHUNK = 64
H_A = 8
DK_A = 128
DV_A = 128
QK_A = H_A * DK_A
V_A = H_A * DV_A
C_QKV = 2 * QK_A + V_A
CONV_W = 4
H_B = 8
NOPE = 128
ROPE = 64
V_B = 128
Q_RANK = 384
KV_RANK = 256
ROPE_THETA = 10000.0
ATTN_SCALE = (NOPE + ROPE) ** -0.5
LOG2_E = float(np.log2(np.e))
Q_SCALE = ATTN_SCALE * LOG2_E
D_FF = -(-8 * D_MODEL // (3 * 256)) * 256
ALPHA = (2 * DEPTH) ** 0.25
EPS = 1e-6
_SIZES = (C_QKV, V_A, H_A, H_A, Q_RANK, KV_RANK, ROPE, D_MODEL, D_MODEL)
_OFFS = tuple(int(v) for v in np.cumsum((0,) + _SIZES))

LANES = 128
VMEM_LIMIT = 56 * 1024 * 1024

COL_QKV = 0
COL_Z = COL_QKV + C_QKV
COL_GA = COL_Z + V_A
COL_GB = COL_GA + D_MODEL
COL_CKV = COL_GB + D_MODEL
COL_KRA = COL_CKV + KV_RANK
COL_CQ = COL_KRA + LANES
COL_KRB = COL_CQ + Q_RANK
COL_AB = COL_KRB + LANES
N_PROJ = COL_AB + LANES
KC_W = KV_RANK + LANES

GDN_CHUNK = 128
MAX_STALE_EXCESS = 64.0


def _cparams(sem):
    return pltpu.CompilerParams(dimension_semantics=sem, vmem_limit_bytes=VMEM_LIMIT)


def _sigmoid(x):
    return jax.nn.sigmoid(x)


def _silu(x):
    return x * jax.nn.sigmoid(x)


def _mm(a, b):
    return jnp.dot(a.astype(BF16), b.astype(BF16), preferred_element_type=F32)


def _mm_nt(a, b):
    return lax.dot_general(a.astype(BF16), b.astype(BF16), (((1,), (1,)), ((), ())),
                           preferred_element_type=F32)


def _mm_tn(a, b):
    return lax.dot_general(a.astype(BF16), b.astype(BF16), (((0,), (0,)), ((), ())),
                           preferred_element_type=F32)


def _proj_body(x_ref, wt_ref, o_ref, *, tn):
    xb = x_ref[...].astype(BF16)
    for c in range(0, wt_ref.shape[0], tn):
        o_ref[:, c:c + tn] = lax.dot_general(xb, wt_ref[c:c + tn, :], (((1,), (1,)), ((), ())),
                                             preferred_element_type=F32)


def _proj_in(x, w_t, tm, tn):
    t, k = x.shape
    n = w_t.shape[0]
    return pl.pallas_call(
        functools.partial(_proj_body, tn=tn),
        grid=(t // tm,),
        in_specs=[pl.BlockSpec((tm, k), lambda i: (i, 0)),
                  pl.BlockSpec((n, k), lambda i: (0, 0), pipeline_mode=pl.Buffered(1))],
        out_specs=pl.BlockSpec((tm, n), lambda i: (i, 0)),
        out_shape=jax.ShapeDtypeStruct((t, n), F32),
        compiler_params=_cparams(("parallel",)),
        name="proj_in",
    )(x, w_t)


def _gates_body(ab_ref, al_ref, dt_ref, o_ref, *, tm, l_pad, l_valid):
    x = ab_ref[...]
    lane = lax.broadcasted_iota(jnp.int32, x.shape, 1)
    xa = x + dt_ref[...]
    sp = jnp.maximum(xa, 0.0) + jnp.log1p(jnp.exp(-jnp.abs(xa)))
    g = -jnp.exp(al_ref[...]) * sp * LOG2_E
    y = jnp.where(lane < H_A, g, _sigmoid(x))
    yt = y.T[0:2 * H_A, :]
    if l_valid < l_pad:
        col = lax.broadcasted_iota(jnp.int32, yt.shape, 1) + pl.program_id(0) * tm
        yt = jnp.where(col % l_pad < l_valid, yt, 0.0)
    r = lax.broadcasted_iota(jnp.int32, (GDN_CHUNK, GDN_CHUNK), 0)
    c = lax.broadcasted_iota(jnp.int32, (GDN_CHUNK, GDN_CHUNK), 1)
    tri = jnp.where(r <= c, 1.0, 0.0).astype(F32)
    for s in range(tm // GDN_CHUNK):
        sl = slice(s * GDN_CHUNK, (s + 1) * GDN_CHUNK)
        o_ref[0:H_A, sl] = jnp.dot(yt[0:H_A, sl], tri, precision=lax.Precision.HIGHEST,
                                   preferred_element_type=F32)
    o_ref[H_A:2 * H_A, :] = yt[H_A:2 * H_A, :]


def _gates(src, col_block, al_lane, dt_lane, tm, l_pad, l_valid):
    t = src.shape[0]
    body = functools.partial(_gates_body, tm=tm, l_pad=l_pad, l_valid=l_valid)
    return pl.pallas_call(
        body,
        grid=(t // tm,),
        in_specs=[pl.BlockSpec((tm, LANES), lambda i: (i, col_block)),
                  pl.BlockSpec((1, LANES), lambda i: (0, 0)),
                  pl.BlockSpec((1, LANES), lambda i: (0, 0))],
        out_specs=pl.BlockSpec((2 * H_A, tm), lambda i: (0, i)),
        out_shape=jax.ShapeDtypeStruct((2 * H_A, t), F32),
        compiler_params=_cparams(("parallel",)),
        name="gates",
    )(src, al_lane, dt_lane)


def _gdn_body(q_ref, k_ref, v_ref, z_ref, gc_ref, bt_ref, cwq_ref, cwk_ref, cwv_ref,
              csq_ref, csk_ref, csv_ref, s0_ref, gn_ref, o_ref, so_ref,
              xq_ref, xk_ref, xv_ref, s_ref, *, tb, tr, nt, hb):
    t = pl.program_id(2)
    hist = CONV_W - 1
    base = 8

    @pl.when(t == 0)
    def _():
        s_ref[...] = s0_ref[0]
        xq_ref[base - hist:base, :] = csq_ref[0]
        xk_ref[base - hist:base, :] = csk_ref[0]
        xv_ref[base - hist:base, :] = csv_ref[0]

    for xs_ref, raw_ref in ((xq_ref, q_ref), (xk_ref, k_ref), (xv_ref, v_ref)):
        xs_ref[base:base + tr, :] = raw_ref[...]
        if tr < tb:
            xs_ref[base + tr:base + tb, :] = jnp.zeros((tb - tr, xs_ref.shape[1]), F32)

    def conv(xs_ref, cw_ref, ln):
        slab = xs_ref[:, ln]
        y = pltpu.roll(slab, hist, 0)[base:base + tb] * cw_ref[0:1, ln]
        for j in range(1, CONV_W):
            tap = slab if j == hist else pltpu.roll(slab, hist - j, 0)
            y = y + tap[base:base + tb] * cw_ref[j:j + 1, ln]
        return _silu(y)

    cc = GDN_CHUNK
    ii = lax.broadcasted_iota(jnp.int32, (cc, cc), 0)
    jj = lax.broadcasted_iota(jnp.int32, (cc, cc), 1)
    strict = ii > jj
    eye = jnp.where(ii == jj, 1.0, 0.0).astype(F32)
    heads = range(hb)
    lanes = [slice(hd * LANES, (hd + 1) * LANES) for hd in heads]
    units = [(hd, c) for hd in heads for c in range(tb // cc)]
    rows = {u: slice(u[1] * cc, (u[1] + 1) * cc) for u in units}
    q = [conv(xq_ref, cwq_ref, ln) for ln in lanes]
    k = [conv(xk_ref, cwk_ref, ln) for ln in lanes]
    v = [conv(xv_ref, cwv_ref, ln) for ln in lanes]
    q = [x * (lax.rsqrt(jnp.sum(x * x, -1, keepdims=True) + EPS) * (DK_A ** -0.5)) for x in q]
    k = [x * lax.rsqrt(jnp.sum(x * x, -1, keepdims=True) + EPS) for x in k]
    qc = {u: q[u[0]][rows[u]] for u in units}
    kc = {u: k[u[0]][rows[u]] for u in units}
    vc = {u: v[u[0]][rows[u]] for u in units}
    m_row = {u: jnp.broadcast_to(gc_ref[u[0], :, rows[u]], (cc, cc)) for u in units}
    m_col = {u: m_row[u].T for u in units}
    b_col = {u: jnp.broadcast_to(bt_ref[u[0], :, rows[u]], (cc, cc)).T for u in units}
    kq = {u: _mm_nt(jnp.concatenate([kc[u], qc[u]], axis=0), kc[u]) for u in units}
    diff = {u: m_col[u] - m_row[u] for u in units}
    dec = {u: jnp.exp2(jnp.where(strict, diff[u], -jnp.inf)) for u in units}
    a_mat = {u: b_col[u] * kq[u][0:cc] * dec[u] for u in units}
    qk = {u: kq[u][cc:2 * cc] * (dec[u] + eye) for u in units}
    e_g = {u: jnp.exp2(m_col[u]) for u in units}
    x_inv = {u: eye - jnp.where((ii >> 1) == (jj >> 1), a_mat[u], 0.0) for u in units}
    sft = 1
    while (1 << sft) < cc:
        off = ((ii >> (sft + 1)) == (jj >> (sft + 1))) & ((ii >> sft) != (jj >> sft))
        lx = {u: _mm(jnp.where(off, a_mat[u], 0.0), x_inv[u]) for u in units}
        x_inv = {u: x_inv[u] - _mm(x_inv[u], lx[u]) for u in units}
        sft += 1
    uw = {u: _mm(x_inv[u], jnp.concatenate([vc[u] * b_col[u], kc[u] * (b_col[u] * e_g[u])], axis=1))
          for u in units}
    wq = {u: jnp.concatenate([uw[u][:, DV_A:], qc[u] * e_g[u]], axis=0) for u in units}
    g_last = {u: m_col[u][cc - 1:cc, :] for u in units}
    kd = {u: kc[u] * jnp.exp2(g_last[u] - m_col[u]) for u in units}
    s_state = [s_ref[hd] for hd in heads]
    for c in range(tb // cc):
        ws = [_mm(wq[(hd, c)], s_state[hd]) for hd in heads]
        v_new = [uw[(hd, c)][:, 0:DV_A] - ws[hd][0:cc] for hd in heads]
        s_state = [s_state[hd] * jnp.exp2(g_last[(hd, c)]) + _mm_tn(kd[(hd, c)], v_new[hd]) for hd in heads]
        o = [ws[hd][cc:2 * cc] + _mm(qk[(hd, c)], v_new[hd]) for hd in heads]
        o = [x * lax.rsqrt(jnp.mean(x * x, -1, keepdims=True) + EPS) * gn_ref[...] for x in o]
        nr = min(cc, tr - c * cc)
        r = slice(c * cc, c * cc + nr)
        for hd in heads:
            o_ref[r, lanes[hd]] = (o[hd][0:nr] * _silu(z_ref[r, lanes[hd]])).astype(o_ref.dtype)
    for hd in heads:
        s_ref[hd] = s_state[hd]

    xq_ref[base - hist:base, :] = xq_ref[base + tb - hist:base + tb, :]
    xk_ref[base - hist:base, :] = xk_ref[base + tb - hist:base + tb, :]
    xv_ref[base - hist:base, :] = xv_ref[base + tb - hist:base + tb, :]

    @pl.when(t == nt - 1)
    def _():
        so_ref[0] = s_ref[...]


def _gdn(h1, gates3, conv_w, conv_state, s0, gn, bsz, seq, tb, hb):
    nt = -(-seq // tb)
    tr = min(tb, seq)
    assert seq % tr == 0 and (tr == tb or (nt == 1 and tb == GDN_CHUNK))
    t_rows = bsz * seq
    body = functools.partial(_gdn_body, tb=tb, tr=tr, nt=nt, hb=hb)
    width = hb * LANES
    ng = H_A // hb

    def rows(seg):
        return pl.BlockSpec((tr, width), lambda b, g, t: (b * nt + t, seg * ng + g))

    def cw(seg):
        return pl.BlockSpec((CONV_W, width), lambda b, g, t: (0, seg * ng + g))

    def cs(seg):
        return pl.BlockSpec((1, CONV_W - 1, width), lambda b, g, t: (b, 0, seg * ng + g))

    return pl.pallas_call(
        body,
        grid=(bsz, ng, nt),
        in_specs=[rows(0), rows(1), rows(2), rows(COL_Z // QK_A),
                  pl.BlockSpec((hb, 1, tb), lambda b, g, t: (g, 0, b * nt + t)),
                  pl.BlockSpec((hb, 1, tb), lambda b, g, t: (ng + g, 0, b * nt + t)),
                  cw(0), cw(1), cw(2), cs(0), cs(1), cs(2),
                  pl.BlockSpec((1, hb, DK_A, DV_A), lambda b, g, t: (b, g, 0, 0)),
                  pl.BlockSpec((1, DV_A), lambda b, g, t: (0, 0))],
        out_specs=[pl.BlockSpec((tr, width), lambda b, g, t: (b * nt + t, g)),
                   pl.BlockSpec((1, hb, DK_A, DV_A), lambda b, g, t: (b, g, 0, 0))],
        out_shape=[jax.ShapeDtypeStruct((t_rows, V_A), BF16),
                   jax.ShapeDtypeStruct((bsz, H_A, DK_A, DV_A), F32)],
        scratch_shapes=[pltpu.VMEM((tb + 8, width), F32)] * 3 + [pltpu.VMEM((hb, DK_A, DV_A), F32)],
        compiler_params=_cparams(("parallel", "parallel", "arbitrary")),
        name="gdn",
    )(h1, h1, h1, h1, gates3, gates3, conv_w, conv_w, conv_w,
      conv_state, conv_state, conv_state, s0, gn)


def _mla_pre_body(cq_ref, ckv_ref, kra_ref, krb_ref, cos_ref, sin_ref, qg_ref, wuq_ref, wuk_ref,
                  kvg_ref, *refs, n_carried):
    q_ref, kc_ref, ckvo_ref, kro_ref, *maybe_vt_ref = refs[n_carried:]
    cq = cq_ref[...]
    cqn = cq * lax.rsqrt(jnp.mean(cq * cq, -1, keepdims=True) + EPS) * qg_ref[...]
    qf = jnp.dot(cqn.astype(BF16), wuq_ref[...], preferred_element_type=F32)
    cos_k = cos_ref[...]
    sin_k = sin_ref[...]
    reps = H_B * ROPE // LANES
    cos_t = jnp.concatenate([cos_k] * reps, axis=1)
    sin_t = jnp.concatenate([sin_k] * reps, axis=1)
    n_nope = H_B * NOPE
    n_rope = H_B * ROPE
    qr = (qf[:, n_nope:n_nope + n_rope] * cos_t + qf[:, n_nope + n_rope:] * sin_t) * Q_SCALE
    lane = lax.broadcasted_iota(jnp.int32, (cq.shape[0], LANES), 1)
    for h in range(H_B):
        ql = jnp.dot(qf[:, h * NOPE:(h + 1) * NOPE].astype(BF16), wuk_ref[h],
                     preferred_element_type=F32) * Q_SCALE
        blk = qr[:, (h // 2) * LANES:(h // 2 + 1) * LANES]
        keep = (lane < ROPE) if h % 2 == 0 else (lane >= ROPE)
        q_ref[h, :, 0:KV_RANK] = ql.astype(BF16)
        q_ref[h, :, KV_RANK:KC_W] = jnp.where(keep, blk, 0.0).astype(BF16)
    ckv = ckv_ref[...]
    ckvn = ckv * lax.rsqrt(jnp.mean(ckv * ckv, -1, keepdims=True) + EPS) * kvg_ref[...]
    kr2 = kra_ref[...] * cos_k + krb_ref[...] * sin_k
    ckvo_ref[...] = ckvn
    kro_ref[...] = kr2[:, 0:ROPE]
    kc_ref[:, 0:KV_RANK] = ckvn.astype(BF16)
    kc_ref[:, KV_RANK:KC_W] = kr2.astype(BF16)
    if maybe_vt_ref:
        maybe_vt_ref[0][0] = ckvn.T.astype(BF16)


def _mla_pre(h1, cos_t, sin_t, qg, wuq, wuk, kvg, bsz, seq, tm, with_vt, layer, depth, carried):
    t_rows = bsz * seq
    ntab = cos_t.shape[0] // tm
    npb = max(seq // tm, 1)
    const2 = lambda i: (0, 0)
    out_specs = [pl.BlockSpec((H_B, tm, KC_W), lambda i: (0, i, 0)),
                 pl.BlockSpec((tm, KC_W), lambda i: (i, 0)),
                 pl.BlockSpec((None, tm, KV_RANK), lambda i: (layer, i, 0)),
                 pl.BlockSpec((None, tm, ROPE), lambda i: (layer, i, 0))]
    out_shape = [jax.ShapeDtypeStruct((H_B, t_rows, KC_W), BF16),
                 jax.ShapeDtypeStruct((t_rows, KC_W), BF16),
                 jax.ShapeDtypeStruct((depth, t_rows, KV_RANK), F32),
                 jax.ShapeDtypeStruct((depth, t_rows, ROPE), F32)]
    if with_vt:
        out_specs.append(pl.BlockSpec((1, KV_RANK, tm), lambda i: (i // npb, 0, i % npb)))
        out_shape.append(jax.ShapeDtypeStruct((bsz, KV_RANK, seq), BF16))
    n_in = 10
    return pl.pallas_call(
        functools.partial(_mla_pre_body, n_carried=len(carried)),
        input_output_aliases={n_in + k: 2 + k for k in range(len(carried))},
        grid=(t_rows // tm,),
        in_specs=[pl.BlockSpec((tm, Q_RANK), lambda i: (i, COL_CQ // Q_RANK)),
                  pl.BlockSpec((tm, KV_RANK), lambda i: (i, COL_CKV // KV_RANK)),
                  pl.BlockSpec((tm, LANES), lambda i: (i, COL_KRA // LANES)),
                  pl.BlockSpec((tm, LANES), lambda i: (i, COL_KRB // LANES)),
                  pl.BlockSpec((tm, LANES), lambda i: (i % ntab, 0)),
                  pl.BlockSpec((tm, LANES), lambda i: (i % ntab, 0)),
                  pl.BlockSpec((1, Q_RANK), const2),
                  pl.BlockSpec(wuq.shape, const2),
                  pl.BlockSpec(wuk.shape, lambda i: (0, 0, 0)),
                  pl.BlockSpec((1, KV_RANK), const2)] + [pl.BlockSpec(memory_space=pl.ANY)] * len(carried),
        out_specs=out_specs,
        out_shape=out_shape,
        compiler_params=_cparams(("parallel",)),
        name="mla_pre",
    )(h1, h1, h1, h1, cos_t, sin_t, qg, wuq, wuk, kvg, *carried)


def _attn_body(q_ref, k_ref, vt_ref, wuv_ref, o_ref, m_ref, l_ref, acc_ref, ex_ref, *, tq, tk, cw, lk):
    i = pl.program_id(1)
    shift = CHUNK.bit_length() - 1

    def update(j, masked, lazy, opening=False, wide=1):
        tkw = wide * tk
        k0 = pl.multiple_of(j * tk, tk)
        kt = k_ref[0, pl.ds(k0, tkw), :]
        vt = vt_ref[0, :, pl.ds(k0, tkw)]
        bias = None
        if masked:
            kpos = k0 + lax.broadcasted_iota(jnp.int32, (tkw, tq), 0)
            qpos = i * tq + lax.broadcasted_iota(jnp.int32, (tkw, tq), 1)
            bias = jnp.where((kpos >> shift) <= (qpos >> shift), 0.0, -jnp.inf).astype(F32)

        units = [(h, slice(c, c + cw)) for h in range(H_B) for c in range(0, tq, cw)]

        def scores(u):
            h, cs = u
            s = lax.dot_general(kt, q_ref[h, cs, :], (((1,), (1,)), ((), ())),
                                preferred_element_type=F32)
            return s if bias is None else s + bias[:, cs]

        def softmax(u, s):
            h, cs = u
            c_max = jnp.max(s, 0, keepdims=True)
            if opening:
                m_prev = jnp.zeros_like(c_max)
                m_new = c_max
                gap = jnp.abs(c_max)
            else:
                m_prev = m_ref[h, :, cs]
                m_new = jnp.maximum(m_prev, c_max)
                gap = c_max - m_prev
            alpha = jnp.exp2(m_prev - m_new)
            if lazy:
                p = jnp.exp2(s - m_prev)
                ex_ref[h, :, cs] = jnp.maximum(ex_ref[h, :, cs], gap)
                l_ref[h, :, cs] = alpha * (l_ref[h, :, cs] + jnp.sum(p, 0, keepdims=True))
            else:
                p = jnp.exp2(s - m_new)
                l_ref[h, :, cs] = alpha * l_ref[h, :, cs] + jnp.sum(p, 0, keepdims=True)
            m_ref[h, :, cs] = m_new
            return p.astype(BF16), alpha

        def accumulate(u, alpha, pv):
            h, cs = u
            if lazy:
                acc_ref[h, :, cs] = alpha * (acc_ref[h, :, cs] + pv)
            else:
                acc_ref[h, :, cs] = alpha * acc_ref[h, :, cs] + pv

        n_units = len(units)
        s_q = {0: scores(units[0])}
        if n_units > 1:
            s_q[1] = scores(units[1])
        p0, alpha = softmax(units[0], s_q.pop(0))
        alphas = {0: alpha}
        pvs = {0: jnp.dot(vt, p0, preferred_element_type=F32)}
        for n in range(n_units):
            if n + 2 < n_units:
                s_q[n + 2] = scores(units[n + 2])
            if n + 1 < n_units:
                p_next, alphas[n + 1] = softmax(units[n + 1], s_q.pop(n + 1))
            accumulate(units[n], alphas.pop(n), pvs.pop(n))
            if n + 1 < n_units:
                pvs[n + 1] = jnp.dot(vt, p_next, preferred_element_type=F32)

    n_full = ((((i * tq) >> shift) + 1) << shift) // tk
    n_all = jnp.minimum(((((i * tq + tq - 1) >> shift) + 1) << shift) + tk - 1, lk + tk - 1) // tk

    def tile_loop(lo, hi, masked, lazy, opening=False, wide=1, first_tile=0):
        def step(p, carry):
            update(first_tile + p * wide, masked, lazy, opening, wide)
            return carry
        lax.fori_loop(lo, hi, step, 0)

    def attempt(a, redo):
        first = a == 0
        run = jnp.logical_or(first, redo > 0)

        @pl.when(run)
        def _():
            m_ref[...] = jnp.full(m_ref.shape, -jnp.inf, F32)
            l_ref[...] = jnp.zeros(l_ref.shape, F32)
            acc_ref[...] = jnp.zeros(acc_ref.shape, F32)
            ex_ref[...] = jnp.full(ex_ref.shape, -jnp.inf, F32)

        tile_loop(0, jnp.where(first, 0, n_all) * run.astype(jnp.int32), True, False)
        tile_loop(0, jnp.where(first, 1, 0), True, True, True)
        n_wide = jnp.maximum(n_full - 1, 0) // WIDE_ATTN
        tile_loop(0, jnp.where(first, n_wide, 0), False, True, wide=WIDE_ATTN, first_tile=1)
        tile_loop(1 + n_wide * WIDE_ATTN, jnp.where(first, n_full, 0), False, True)
        tile_loop(jnp.maximum(n_full, 1), jnp.where(first, n_all, 0), True, True)
        excess = jnp.max(ex_ref[...])
        return jnp.where(first, (excess > MAX_STALE_EXCESS).astype(jnp.int32), 0)

    lax.fori_loop(0, 2, attempt, jnp.int32(0))

    for h in range(H_B):
        o_t = (acc_ref[h] * (1.0 / l_ref[h])).astype(BF16)
        ob_t = jnp.dot(wuv_ref[h], o_t, preferred_element_type=F32)
        o_ref[0, :, h * V_B:(h + 1) * V_B] = ob_t.T.astype(o_ref.dtype)


def _attn(q, kc, vt, wuv_t, bsz, seq, tq, tk, cw):
    assert seq % tq == 0 and seq % tk == 0
    nq = seq // tq
    body = functools.partial(_attn_body, tq=tq, tk=tk, cw=cw, lk=seq)
    return pl.pallas_call(
        body,
        grid=(bsz, nq),
        in_specs=[pl.BlockSpec((H_B, tq, KC_W), lambda b, i: (0, b * nq + i, 0)),
                  pl.BlockSpec((1, seq, KC_W), lambda b, i: (b, 0, 0)),
                  pl.BlockSpec((1, KV_RANK, seq), lambda b, i: (b, 0, 0)),
                  pl.BlockSpec(wuv_t.shape, lambda b, i: (0, 0, 0))],
        out_specs=pl.BlockSpec((1, tq, H_B * V_B), lambda b, i: (b, i, 0)),
        out_shape=jax.ShapeDtypeStruct((bsz, seq, H_B * V_B), BF16),
        scratch_shapes=[pltpu.VMEM((H_B, 1, tq), F32), pltpu.VMEM((H_B, 1, tq), F32),
                        pltpu.VMEM((H_B, KV_RANK, tq), F32), pltpu.VMEM((H_B, 1, tq), F32)],
        compiler_params=_cparams(("parallel", "arbitrary")),
        name="attn",
    )(q, kc, vt, wuv_t)


def _attn_dec_body(q_ref, ckv_ref, krt_ref, kn_ref, wuv_ref, o_ref, m_ref, l_ref, acc_ref,
                   *, tq, past, n_past):
    j = pl.program_id(1)
    rows = H_B * tq
    nt_dims = (((1,), (1,)), ((), ()))
    q2 = q_ref[...].reshape(rows, KC_W)

    @pl.when(j == 0)
    def _():
        m_ref[...] = jnp.full(m_ref.shape, -jnp.inf, F32)
        l_ref[...] = jnp.zeros(l_ref.shape, F32)
        acc_ref[...] = jnp.zeros(acc_ref.shape, F32)

    def accumulate(s, v_nat):
        m_prev = m_ref[...]
        m_new = jnp.maximum(m_prev, jnp.max(s, -1, keepdims=True))
        p = jnp.exp2(s - m_new)
        alpha = jnp.exp2(m_prev - m_new)
        l_ref[...] = alpha * l_ref[...] + jnp.sum(p, -1, keepdims=True)
        acc_ref[...] = alpha * acc_ref[...] + jnp.dot(p.astype(BF16), v_nat, preferred_element_type=F32)
        m_ref[...] = m_new

    @pl.when(j < n_past)
    def _():
        k_lat = ckv_ref[0].astype(BF16)
        kr_t = krt_ref[0].astype(BF16)
        kr2_t = jnp.concatenate([kr_t, kr_t], axis=0)
        s = (lax.dot_general(q2[:, 0:KV_RANK], k_lat, nt_dims, preferred_element_type=F32)
             + jnp.dot(q2[:, KV_RANK:KC_W], kr2_t, preferred_element_type=F32))
        accumulate(s, k_lat)

    @pl.when(j == n_past)
    def _():
        kn = kn_ref[...]
        s = lax.dot_general(q2, kn, nt_dims, preferred_element_type=F32)
        shift = CHUNK.bit_length() - 1
        qpos = past + (lax.broadcasted_iota(jnp.int32, s.shape, 0) & (tq - 1))
        kpos = past + lax.broadcasted_iota(jnp.int32, s.shape, 1)
        s = jnp.where((kpos >> shift) <= (qpos >> shift), s, -jnp.inf)
        accumulate(s, kn[:, 0:KV_RANK])
        o = acc_ref[...] * (1.0 / l_ref[...])
        for h in range(H_B):
            oh = o[h * tq:(h + 1) * tq, :].astype(BF16)
            o_ref[:, h * V_B:(h + 1) * V_B] = jnp.dot(
                oh, wuv_ref[h], preferred_element_type=F32).astype(o_ref.dtype)


def _attn_dec(q, cache_ckv, cache_kr_t, layer, kc_new, wuv, bsz, seq, tk):
    past = cache_ckv.shape[2]
    assert past % tk == 0 and past % CHUNK == 0 and seq & (seq - 1) == 0
    n_past = past // tk
    body = functools.partial(_attn_dec_body, tq=seq, past=past, n_past=n_past)
    return pl.pallas_call(
        body,
        grid=(bsz, n_past + 1),
        in_specs=[pl.BlockSpec((H_B, seq, KC_W), lambda b, j: (0, b, 0)),
                  pl.BlockSpec((None, 1, tk, KV_RANK), lambda b, j: (layer, b, jnp.minimum(j, n_past - 1), 0)),
                  pl.BlockSpec((None, 1, ROPE, tk), lambda b, j: (layer, b, 0, jnp.minimum(j, n_past - 1))),
                  pl.BlockSpec((seq, KC_W), lambda b, j: (b, 0)),
                  pl.BlockSpec(wuv.shape, lambda b, j: (0, 0, 0))],
        out_specs=pl.BlockSpec((seq, H_B * V_B), lambda b, j: (b, 0)),
        out_shape=jax.ShapeDtypeStruct((bsz * seq, H_B * V_B), BF16),
        scratch_shapes=[pltpu.VMEM((H_B * seq, 1), F32), pltpu.VMEM((H_B * seq, 1), F32),
                        pltpu.VMEM((H_B * seq, KV_RANK), F32)],
        compiler_params=_cparams(("parallel", "arbitrary")),
        name="attn_dec",
    )(q, cache_ckv, cache_kr_t, kc_new, wuv)


def _layer_norm(r, g, b):
    mu = jnp.mean(r, -1, keepdims=True)
    d = r - mu
    var = jnp.mean(d * d, -1, keepdims=True)
    return d * lax.rsqrt(var + EPS) * g + b


def _merge_body(oa_ref, ob_ref, ga_ref, gb_ref, x_ref, woa_ref, wob_ref, wout_ref, g_ref, b_ref, o_ref):
    tm = x_ref.shape[0]
    for r0 in range(0, tm, tm // MERGE_SPLIT):
        rs = slice(r0, r0 + tm // MERGE_SPLIT)
        ya = jnp.dot(oa_ref[rs, :], woa_ref[...], preferred_element_type=F32)
        yb = jnp.dot(ob_ref[rs, :], wob_ref[...], preferred_element_type=F32)
        m = _sigmoid(ga_ref[rs, :]) * ya + _sigmoid(gb_ref[rs, :]) * yb
        r = ALPHA * x_ref[rs, :] + jnp.dot(m.astype(BF16), wout_ref[...], preferred_element_type=F32)
        o_ref[rs, :] = _layer_norm(r, g_ref[...], b_ref[...])


def _merge(oa, ob, h1, x, woa, wob, wout, g, b, tm):
    t = x.shape[0]
    row = lambda i: (i, 0)
    const = lambda i: (0, 0)
    wspec = pl.BlockSpec((D_MODEL, D_MODEL), const)
    return pl.pallas_call(
        _merge_body,
        grid=(t // tm,),
        in_specs=[pl.BlockSpec((tm, V_A), row), pl.BlockSpec((tm, H_B * V_B), row),
                  pl.BlockSpec((tm, D_MODEL), lambda i: (i, COL_GA // D_MODEL)),
                  pl.BlockSpec((tm, D_MODEL), lambda i: (i, COL_GB // D_MODEL)),
                  pl.BlockSpec((tm, D_MODEL), row), wspec, wspec, wspec,
                  pl.BlockSpec((1, D_MODEL), const), pl.BlockSpec((1, D_MODEL), const)],
        out_specs=pl.BlockSpec((tm, D_MODEL), row),
        out_shape=jax.ShapeDtypeStruct((t, D_MODEL), F32),
        compiler_params=_cparams(("parallel",)),
        name="merge",
    )(oa, ob, h1, h1, x, woa, wob, wout, g, b)


def _ffn_body(x_ref, wg_ref, wu_ref, wd_ref, g_ref, b_ref, o_ref, *, chunks):
    tm = x_ref.shape[0]
    for r0 in range(0, tm, tm // FFN_SPLIT):
        rs = slice(r0, r0 + tm // FFN_SPLIT)
        x = x_ref[rs, :]
        xb = x.astype(BF16)
        y = ALPHA * x
        off = 0
        for width in chunks:
            sl = slice(off, off + width)
            f1 = jnp.dot(xb, wg_ref[:, sl], preferred_element_type=F32)
            f3 = jnp.dot(xb, wu_ref[:, sl], preferred_element_type=F32)
            hc = (_silu(f1) * f3).astype(BF16)
            y = y + jnp.dot(hc, wd_ref[sl, :], preferred_element_type=F32)
            off += width
        o_ref[rs, :] = _layer_norm(y, g_ref[...], b_ref[...])


def _ffn(x, wg, wu, wd, g, b, tm, chunks):
    assert sum(chunks) == D_FF
    t = x.shape[0]
    row = lambda i: (i, 0)
    const = lambda i: (0, 0)
    single = pl.Buffered(1)
    return pl.pallas_call(
        functools.partial(_ffn_body, chunks=chunks),
        grid=(t // tm,),
        in_specs=[pl.BlockSpec((tm, D_MODEL), row),
                  pl.BlockSpec((D_MODEL, D_FF), const, pipeline_mode=single),
                  pl.BlockSpec((D_MODEL, D_FF), const, pipeline_mode=single),
                  pl.BlockSpec((D_FF, D_MODEL), const, pipeline_mode=single),
                  pl.BlockSpec((1, D_MODEL), const), pl.BlockSpec((1, D_MODEL), const)],
        out_specs=pl.BlockSpec((tm, D_MODEL), row),
        out_shape=jax.ShapeDtypeStruct((t, D_MODEL), F32),
        compiler_params=_cparams(("parallel",)),
        name="ffn",
    )(x, wg, wu, wd, g, b)


def _prep_layer_weights(w_in_t, conv_w, a_log, dt_bias, gdn_norm_g, w_oa, q_norm_g, w_uq, kv_norm_g,
                        w_ukv, w_ob, w_out, ln1_g, ln1_b, w_gu, w_down, ln2_g, ln2_b):
    seg = lambda i: w_in_t[_OFFS[i]:_OFFS[i + 1], :]
    qkv, z, a, b, c_q, c_kv, k_r, g_a, g_b = (seg(i) for i in range(9))
    half = ROPE // 2
    k_r_rot = jnp.concatenate([-k_r[half:], k_r[:half]], axis=0)
    pad = jnp.zeros((LANES - 2 * H_A, D_MODEL), w_in_t.dtype)
    w_proj = jnp.concatenate([qkv, z, g_a, g_b, c_kv, k_r, k_r, c_q, k_r_rot, k_r_rot, a, b, pad],
                             axis=0).astype(BF16)
    lane_pad = jnp.zeros((LANES - H_A,), F32)
    al_lane = jnp.concatenate([a_log.astype(F32), lane_pad]).reshape(1, LANES)
    dt_lane = jnp.concatenate([dt_bias.astype(F32), lane_pad]).reshape(1, LANES)
    uq = w_uq.reshape(Q_RANK, H_B, NOPE + ROPE)
    uq_nope = uq[:, :, :NOPE].reshape(Q_RANK, H_B * NOPE)
    uq_rope = uq[:, :, NOPE:]
    uq_rot = jnp.concatenate([-uq_rope[:, :, half:], uq_rope[:, :, :half]], axis=2)
    w_uq_ext = jnp.concatenate([uq_nope, uq_rope.reshape(Q_RANK, H_B * ROPE),
                                uq_rot.reshape(Q_RANK, H_B * ROPE)], axis=1).astype(BF16)
    ukv = w_ukv.reshape(KV_RANK, H_B, NOPE + V_B)
    w_uk_t = jnp.transpose(ukv[:, :, :NOPE], (1, 2, 0)).astype(BF16)
    w_uv = jnp.transpose(ukv[:, :, NOPE:], (1, 0, 2)).astype(BF16)
    w_uv_t = jnp.transpose(ukv[:, :, NOPE:], (1, 2, 0)).astype(BF16)
    return dict(
        w_proj=w_proj, conv_w=conv_w.astype(F32), al_lane=al_lane, dt_lane=dt_lane,
        gn=gdn_norm_g.reshape(1, DV_A).astype(F32), w_oa=w_oa.astype(BF16),
        qg=q_norm_g.reshape(1, Q_RANK).astype(F32), w_uq=w_uq_ext, w_uk_t=w_uk_t, w_uv=w_uv, w_uv_t=w_uv_t,
        kvg=kv_norm_g.reshape(1, KV_RANK).astype(F32), w_ob=w_ob.astype(BF16),
        w_out=w_out.astype(BF16), ln1_g=ln1_g.reshape(1, D_MODEL), ln1_b=ln1_b.reshape(1, D_MODEL),
        w_g=w_gu[:, :D_FF].astype(BF16), w_u=w_gu[:, D_FF:].astype(BF16), w_down=w_down.astype(BF16),
        ln2_g=ln2_g.reshape(1, D_MODEL), ln2_b=ln2_b.reshape(1, D_MODEL))


def _rope_tables(past, seq, reps):
    half = ROPE // 2
    inv = ROPE_THETA ** (-jnp.arange(half, dtype=F32) / half)
    ang = (past + jnp.arange(seq)).astype(F32)[:, None] * inv[None, :]
    cos = jnp.tile(jnp.cos(ang), (reps, LANES // half))
    sin = jnp.tile(jnp.sin(ang), (reps, LANES // half))
    return cos, sin


TM_PROJ, TN_PROJ, TM_GATES, TM_MLA, TM_MERGE, TM_FFN = 512, N_PROJ // 4, 4096, 512, 512, 1024
FFN_CHUNKS = (768, 768, 768, D_FF - 3 * 768)
MERGE_SPLIT = 2
FFN_SPLIT = 2
TQ_ATTN, TK_ATTN, CW_ATTN, TK_DEC = 512, 512, 512, 2048
WIDE_ATTN = 2
TB_GDN, HB_GDN = 2 * GDN_CHUNK, H_A


def _trunk_layer(x, conv_state, s0, caches, wl, bsz, seq, layer, depth, carried):
    decode = caches is not None
    t_rows = bsz * seq
    h1 = _proj_in(x, wl["w_proj"], min(TM_PROJ, t_rows), TN_PROJ)
    seq_pad = -(-seq // GDN_CHUNK) * GDN_CHUNK
    if seq_pad == seq:
        gates = _gates(h1, COL_AB // LANES, wl["al_lane"], wl["dt_lane"], min(TM_GATES, t_rows), seq, seq)
    else:
        ab = h1[:, COL_AB:COL_AB + LANES].reshape(bsz, seq, LANES)
        ab = jnp.pad(ab, ((0, 0), (0, seq_pad - seq), (0, 0))).reshape(bsz * seq_pad, LANES)
        gates = _gates(ab, 0, wl["al_lane"], wl["dt_lane"], min(TM_GATES, bsz * seq_pad), seq_pad, seq)
    o_a, s_new = _gdn(h1, gates.reshape(2 * H_A, 1, bsz * seq_pad), wl["conv_w"], conv_state, s0, wl["gn"],
                      bsz, seq, min(TB_GDN, seq_pad), HB_GDN)
    if decode:
        cache_ckv, cache_kr_t = caches
        tm_mla = t_rows if t_rows <= TM_MLA else seq
        cos_t, sin_t = _rope_tables(cache_ckv.shape[2], seq, tm_mla // seq)
        q, kc, ckv_all, kr_all = _mla_pre(h1, cos_t, sin_t, wl["qg"], wl["w_uq"], wl["w_uk_t"], wl["kvg"],
                                          bsz, seq, tm_mla, False, layer, depth, carried)
        tk_dec = TK_DEC
        while cache_ckv.shape[2] % tk_dec:
            tk_dec //= 2
        o_b = _attn_dec(q, cache_ckv, cache_kr_t, layer, kc, wl["w_uv"], bsz, seq, tk_dec)
    else:
        cos_t, sin_t = _rope_tables(0, seq, 1)
        q, kc, ckv_all, kr_all, vt = _mla_pre(h1, cos_t, sin_t, wl["qg"], wl["w_uq"], wl["w_uk_t"],
                                              wl["kvg"], bsz, seq, TM_MLA, True, layer, depth, carried)
        o_b = _attn(q, kc.reshape(bsz, seq, KC_W), vt, wl["w_uv_t"], bsz, seq, TQ_ATTN, TK_ATTN, CW_ATTN)
        o_b = o_b.reshape(t_rows, H_B * V_B)
    x1 = _merge(o_a, o_b, h1, x, wl["w_oa"], wl["w_ob"], wl["w_out"], wl["ln1_g"], wl["ln1_b"],
                min(TM_MERGE, t_rows))
    x2 = _ffn(x1, wl["w_g"], wl["w_u"], wl["w_down"], wl["ln2_g"], wl["ln2_b"], min(TM_FFN, t_rows), FFN_CHUNKS)
    conv_new = h1.reshape(bsz, seq, N_PROJ)[:, seq - (CONV_W - 1):, COL_QKV:COL_QKV + C_QKV]
    return x2, conv_new, s_new, (ckv_all, kr_all)


def kernel(x_prompt, x_sample, state_conv, state_gdn, cache_ckv, cache_krope, w_in, conv_w, a_log, dt_bias, gdn_norm_g, w_oa, q_norm_g, w_uq, kv_norm_g, w_ukv, w_ob, w_out, ln1_g, ln1_b, w_gu, w_down, ln2_g, ln2_b):
    bp, lp, _ = x_prompt.shape
    bs, ls, _ = x_sample.shape
    yp = x_prompt.reshape(bp * lp, D_MODEL)
    ys = x_sample.reshape(bs * ls, D_MODEL)
    zero_conv = jnp.zeros((bp, CONV_W - 1, C_QKV), F32)
    zero_s = jnp.zeros((bp, H_A, DK_A, DV_A), F32)
    cache_kr_t = jnp.swapaxes(cache_krope, 2, 3)
    w_in_t = jnp.swapaxes(w_in, 1, 2)
    depth = w_in.shape[0]
    conv_p, gdn_p, conv_s, gdn_s = [], [], [], []
    kv_p = kv_s = ()
    for l in range(depth):
        wl = _prep_layer_weights(w_in_t[l], conv_w[l], a_log[l], dt_bias[l], gdn_norm_g[l], w_oa[l],
                                 q_norm_g[l], w_uq[l], kv_norm_g[l], w_ukv[l], w_ob[l], w_out[l],
                                 ln1_g[l], ln1_b[l], w_gu[l], w_down[l], ln2_g[l], ln2_b[l])
        yp, c_new, g_new, kv_p = _trunk_layer(yp, zero_conv, zero_s, None, wl, bp, lp, l, depth, kv_p)
        conv_p.append(c_new), gdn_p.append(g_new)
        ys, c_new, g_new, kv_s = _trunk_layer(ys, state_conv[l], state_gdn[l], (cache_ckv, cache_kr_t), wl,
                                              bs, ls, l, depth, kv_s)
        conv_s.append(c_new), gdn_s.append(g_new)
    return (yp.reshape(bp, lp, D_MODEL), ys.reshape(bs, ls, D_MODEL),
            jnp.stack(conv_p), jnp.stack(gdn_p),
            kv_p[0].reshape(depth, bp, lp, KV_RANK), kv_p[1].reshape(depth, bp, lp, ROPE),
            jnp.stack(conv_s), jnp.stack(gdn_s),
            kv_s[0].reshape(depth, bs, ls, KV_RANK), kv_s[1].reshape(depth, bs, ls, ROPE))
```

```python
import functools

import numpy as np
import jax
import jax.numpy as jnp
from jax import lax
from jax.experimental import pallas as pl
from jax.experimental.pallas import tpu as pltpu

F32 = jnp.float32
BF16 = jnp.bfloat16

D_MODEL = 1024
DEPTH = 2
CHUNK = 64
H_A = 8
DK_A = 128
DV_A = 128
QK_A = H_A * DK_A
V_A = H_A * DV_A
C_QKV = 2 * QK_A + V_A
CONV_W = 4
H_B = 8
NOPE = 128
ROPE = 64
V_B = 128
Q_RANK = 384
KV_RANK = 256
ROPE_THETA = 10000.0
ATTN_SCALE = (NOPE + ROPE) ** -0.5
LOG2_E = float(np.log2(np.e))
Q_SCALE = ATTN_SCALE * LOG2_E
D_FF = -(-8 * D_MODEL // (3 * 256)) * 256
ALPHA = (2 * DEPTH) ** 0.25
EPS = 1e-6
_SIZES = (C_QKV, V_A, H_A, H_A, Q_RANK, KV_RANK, ROPE, D_MODEL, D_MODEL)
_OFFS = tuple(int(v) for v in np.cumsum((0,) + _SIZES))

LANES = 128
VMEM_LIMIT = 56 * 1024 * 1024

COL_QKV = 0
COL_Z = COL_QKV + C_QKV
COL_GA = COL_Z + V_A
COL_GB = COL_GA + D_MODEL
COL_CKV = COL_GB + D_MODEL
COL_KRA = COL_CKV + KV_RANK
COL_CQ = COL_KRA + LANES
COL_KRB = COL_CQ + Q_RANK
COL_AB = COL_KRB + LANES
N_PROJ = COL_AB + LANES
KC_W = KV_RANK + LANES

GDN_CHUNK = 128
MAX_STALE_EXCESS = 64.0


def _cparams(sem):
    return pltpu.CompilerParams(dimension_semantics=sem, vmem_limit_bytes=VMEM_LIMIT)


def _sigmoid(x):
    return jax.nn.sigmoid(x)


def _silu(x):
    return x * jax.nn.sigmoid(x)


def _mm(a, b):
    return jnp.dot(a.astype(BF16), b.astype(BF16), preferred_element_type=F32)


def _mm_nt(a, b):
    return lax.dot_general(a.astype(BF16), b.astype(BF16), (((1,), (1,)), ((), ())),
                           preferred_element_type=F32)


def _mm_tn(a, b):
    return lax.dot_general(a.astype(BF16), b.astype(BF16), (((0,), (0,)), ((), ())),
                           preferred_element_type=F32)


def _proj_body(x_ref, wt_ref, o_ref, *, tn):
    xb = x_ref[...].astype(BF16)
    for c in range(0, wt_ref.shape[0], tn):
        o_ref[:, c:c + tn] = lax.dot_general(xb, wt_ref[c:c + tn, :], (((1,), (1,)), ((), ())),
                                             preferred_element_type=F32)


def _proj_in(x, w_t, tm, tn):
    t, k = x.shape
    n = w_t.shape[0]
    return pl.pallas_call(
        functools.partial(_proj_body, tn=tn),
        grid=(t // tm,),
        in_specs=[pl.BlockSpec((tm, k), lambda i: (i, 0)),
                  pl.BlockSpec((n, k), lambda i: (0, 0), pipeline_mode=pl.Buffered(1))],
        out_specs=pl.BlockSpec((tm, n), lambda i: (i, 0)),
        out_shape=jax.ShapeDtypeStruct((t, n), F32),
        compiler_params=_cparams(("parallel",)),
        name="proj_in",
    )(x, w_t)


def _gates_body(ab_ref, al_ref, dt_ref, o_ref, *, tm, l_pad, l_valid):
    x = ab_ref[...]
    lane = lax.broadcasted_iota(jnp.int32, x.shape, 1)
    xa = x + dt_ref[...]
    sp = jnp.maximum(xa, 0.0) + jnp.log1p(jnp.exp(-jnp.abs(xa)))
    g = -jnp.exp(al_ref[...]) * sp * LOG2_E
    y = jnp.where(lane < H_A, g, _sigmoid(x))
    yt = y.T[0:2 * H_A, :]
    if l_valid < l_pad:
        col = lax.broadcasted_iota(jnp.int32, yt.shape, 1) + pl.program_id(0) * tm
        yt = jnp.where(col % l_pad < l_valid, yt, 0.0)
    r = lax.broadcasted_iota(jnp.int32, (GDN_CHUNK, GDN_CHUNK), 0)
    c = lax.broadcasted_iota(jnp.int32, (GDN_CHUNK, GDN_CHUNK), 1)
    tri = jnp.where(r <= c, 1.0, 0.0).astype(F32)
    for s in range(tm // GDN_CHUNK):
        sl = slice(s * GDN_CHUNK, (s + 1) * GDN_CHUNK)
        o_ref[0:H_A, sl] = jnp.dot(yt[0:H_A, sl], tri, precision=lax.Precision.HIGHEST,
                                   preferred_element_type=F32)
    o_ref[H_A:2 * H_A, :] = yt[H_A:2 * H_A, :]


def _gates(src, col_block, al_lane, dt_lane, tm, l_pad, l_valid):
    t = src.shape[0]
    body = functools.partial(_gates_body, tm=tm, l_pad=l_pad, l_valid=l_valid)
    return pl.pallas_call(
        body,
        grid=(t // tm,),
        in_specs=[pl.BlockSpec((tm, LANES), lambda i: (i, col_block)),
                  pl.BlockSpec((1, LANES), lambda i: (0, 0)),
                  pl.BlockSpec((1, LANES), lambda i: (0, 0))],
        out_specs=pl.BlockSpec((2 * H_A, tm), lambda i: (0, i)),
        out_shape=jax.ShapeDtypeStruct((2 * H_A, t), F32),
        compiler_params=_cparams(("parallel",)),
        name="gates",
    )(src, al_lane, dt_lane)


def _gdn_body(q_ref, k_ref, v_ref, z_ref, gc_ref, bt_ref, cwq_ref, cwk_ref, cwv_ref,
              csq_ref, csk_ref, csv_ref, s0_ref, gn_ref, o_ref, so_ref,
              xq_ref, xk_ref, xv_ref, s_ref, *, tb, tr, nt, hb):
    t = pl.program_id(2)
    hist = CONV_W - 1
    base = 8

    @pl.when(t == 0)
    def _():
        s_ref[...] = s0_ref[0]
        xq_ref[base - hist:base, :] = csq_ref[0]
        xk_ref[base - hist:base, :] = csk_ref[0]
        xv_ref[base - hist:base, :] = csv_ref[0]

    for xs_ref, raw_ref in ((xq_ref, q_ref), (xk_ref, k_ref), (xv_ref, v_ref)):
        xs_ref[base:base + tr, :] = raw_ref[...]
        if tr < tb:
            xs_ref[base + tr:base + tb, :] = jnp.zeros((tb - tr, xs_ref.shape[1]), F32)

    def conv(xs_ref, cw_ref, ln):
        slab = xs_ref[:, ln]
        y = pltpu.roll(slab, hist, 0)[base:base + tb] * cw_ref[0:1, ln]
        for j in range(1, CONV_W):
            tap = slab if j == hist else pltpu.roll(slab, hist - j, 0)
            y = y + tap[base:base + tb] * cw_ref[j:j + 1, ln]
        return _silu(y)

    cc = GDN_CHUNK
    ii = lax.broadcasted_iota(jnp.int32, (cc, cc), 0)
    jj = lax.broadcasted_iota(jnp.int32, (cc, cc), 1)
    strict = ii > jj
    eye = jnp.where(ii == jj, 1.0, 0.0).astype(F32)
    heads = range(hb)
    lanes = [slice(hd * LANES, (hd + 1) * LANES) for hd in heads]
    units = [(hd, c) for hd in heads for c in range(tb // cc)]
    rows = {u: slice(u[1] * cc, (u[1] + 1) * cc) for u in units}
    q = [conv(xq_ref, cwq_ref, ln) for ln in lanes]
    k = [conv(xk_ref, cwk_ref, ln) for ln in lanes]
    v = [conv(xv_ref, cwv_ref, ln) for ln in lanes]
    q = [x * (lax.rsqrt(jnp.sum(x * x, -1, keepdims=True) + EPS) * (DK_A ** -0.5)) for x in q]
    k = [x * lax.rsqrt(jnp.sum(x * x, -1, keepdims=True) + EPS) for x in k]
    qc = {u: q[u[0]][rows[u]] for u in units}
    kc = {u: k[u[0]][rows[u]] for u in units}
    vc = {u: v[u[0]][rows[u]] for u in units}
    m_row = {u: jnp.broadcast_to(gc_ref[u[0], :, rows[u]], (cc, cc)) for u in units}
    m_col = {u: m_row[u].T for u in units}
    b_col = {u: jnp.broadcast_to(bt_ref[u[0], :, rows[u]], (cc, cc)).T for u in units}
    kq = {u: _mm_nt(jnp.concatenate([kc[u], qc[u]], axis=0), kc[u]) for u in units}
    diff = {u: m_col[u] - m_row[u] for u in units}
    dec = {u: jnp.exp2(jnp.where(strict, diff[u], -jnp.inf)) for u in units}
    a_mat = {u: b_col[u] * kq[u][0:cc] * dec[u] for u in units}
    qk = {u: kq[u][cc:2 * cc] * (dec[u] + eye) for u in units}
    e_g = {u: jnp.exp2(m_col[u]) for u in units}
    x_inv = {u: eye - jnp.where((ii >> 1) == (jj >> 1), a_mat[u], 0.0) for u in units}
    sft = 1
    while (1 << sft) < cc:
        off = ((ii >> (sft + 1)) == (jj >> (sft + 1))) & ((ii >> sft) != (jj >> sft))
        lx = {u: _mm(jnp.where(off, a_mat[u], 0.0), x_inv[u]) for u in units}
        x_inv = {u: x_inv[u] - _mm(x_inv[u], lx[u]) for u in units}
        sft += 1
    uw = {u: _mm(x_inv[u], jnp.concatenate([vc[u] * b_col[u], kc[u] * (b_col[u] * e_g[u])], axis=1))
          for u in units}
    wq = {u: jnp.concatenate([uw[u][:, DV_A:], qc[u] * e_g[u]], axis=0) for u in units}
    g_last = {u: m_col[u][cc - 1:cc, :] for u in units}
    kd = {u: kc[u] * jnp.exp2(g_last[u] - m_col[u]) for u in units}
    s_state = [s_ref[hd] for hd in heads]
    for c in range(tb // cc):
        ws = [_mm(wq[(hd, c)], s_state[hd]) for hd in heads]
        v_new = [uw[(hd, c)][:, 0:DV_A] - ws[hd][0:cc] for hd in heads]
        s_state = [s_state[hd] * jnp.exp2(g_last[(hd, c)]) + _mm_tn(kd[(hd, c)], v_new[hd]) for hd in heads]
        o = [ws[hd][cc:2 * cc] + _mm(qk[(hd, c)], v_new[hd]) for hd in heads]
        o = [x * lax.rsqrt(jnp.mean(x * x, -1, keepdims=True) + EPS) * gn_ref[...] for x in o]
        nr = min(cc, tr - c * cc)
        r = slice(c * cc, c * cc + nr)
        for hd in heads:
            o_ref[r, lanes[hd]] = (o[hd][0:nr] * _silu(z_ref[r, lanes[hd]])).astype(o_ref.dtype)
    for hd in heads:
        s_ref[hd] = s_state[hd]

    xq_ref[base - hist:base, :] = xq_ref[base + tb - hist:base + tb, :]
    xk_ref[base - hist:base, :] = xk_ref[base + tb - hist:base + tb, :]
    xv_ref[base - hist:base, :] = xv_ref[base + tb - hist:base + tb, :]

    @pl.when(t == nt - 1)
    def _():
        so_ref[0] = s_ref[...]


def _gdn(h1, gates3, conv_w, conv_state, s0, gn, bsz, seq, tb, hb):
    nt = -(-seq // tb)
    tr = min(tb, seq)
    assert seq % tr == 0 and (tr == tb or (nt == 1 and tb == GDN_CHUNK))
    t_rows = bsz * seq
    body = functools.partial(_gdn_body, tb=tb, tr=tr, nt=nt, hb=hb)
    width = hb * LANES
    ng = H_A // hb

    def rows(seg):
        return pl.BlockSpec((tr, width), lambda b, g, t: (b * nt + t, seg * ng + g))

    def cw(seg):
        return pl.BlockSpec((CONV_W, width), lambda b, g, t: (0, seg * ng + g))

    def cs(seg):
        return pl.BlockSpec((1, CONV_W - 1, width), lambda b, g, t: (b, 0, seg * ng + g))

    return pl.pallas_call(
        body,
        grid=(bsz, ng, nt),
        in_specs=[rows(0), rows(1), rows(2), rows(COL_Z // QK_A),
                  pl.BlockSpec((hb, 1, tb), lambda b, g, t: (g, 0, b * nt + t)),
                  pl.BlockSpec((hb, 1, tb), lambda b, g, t: (ng + g, 0, b * nt + t)),
                  cw(0), cw(1), cw(2), cs(0), cs(1), cs(2),
                  pl.BlockSpec((1, hb, DK_A, DV_A), lambda b, g, t: (b, g, 0, 0)),
                  pl.BlockSpec((1, DV_A), lambda b, g, t: (0, 0))],
        out_specs=[pl.BlockSpec((tr, width), lambda b, g, t: (b * nt + t, g)),
                   pl.BlockSpec((1, hb, DK_A, DV_A), lambda b, g, t: (b, g, 0, 0))],
        out_shape=[jax.ShapeDtypeStruct((t_rows, V_A), BF16),
                   jax.ShapeDtypeStruct((bsz, H_A, DK_A, DV_A), F32)],
        scratch_shapes=[pltpu.VMEM((tb + 8, width), F32)] * 3 + [pltpu.VMEM((hb, DK_A, DV_A), F32)],
        compiler_params=_cparams(("parallel", "parallel", "arbitrary")),
        name="gdn",
    )(h1, h1, h1, h1, gates3, gates3, conv_w, conv_w, conv_w,
      conv_state, conv_state, conv_state, s0, gn)


def _mla_pre_body(cq_ref, ckv_ref, kra_ref, krb_ref, cos_ref, sin_ref, qg_ref, wuq_ref, wuk_ref,
                  kvg_ref, *refs, n_carried):
    q_ref, kc_ref, ckvo_ref, kro_ref, *maybe_vt_ref = refs[n_carried:]
    cq = cq_ref[...]
    cqn = cq * lax.rsqrt(jnp.mean(cq * cq, -1, keepdims=True) + EPS) * qg_ref[...]
    qf = jnp.dot(cqn.astype(BF16), wuq_ref[...], preferred_element_type=F32)
    cos_k = cos_ref[...]
    sin_k = sin_ref[...]
    reps = H_B * ROPE // LANES
    cos_t = jnp.concatenate([cos_k] * reps, axis=1)
    sin_t = jnp.concatenate([sin_k] * reps, axis=1)
    n_nope = H_B * NOPE
    n_rope = H_B * ROPE
    qr = (qf[:, n_nope:n_nope + n_rope] * cos_t + qf[:, n_nope + n_rope:] * sin_t) * Q_SCALE
    lane = lax.broadcasted_iota(jnp.int32, (cq.shape[0], LANES), 1)
    for h in range(H_B):
        ql = jnp.dot(qf[:, h * NOPE:(h + 1) * NOPE].astype(BF16), wuk_ref[h],
                     preferred_element_type=F32) * Q_SCALE
        blk = qr[:, (h // 2) * LANES:(h // 2 + 1) * LANES]
        keep = (lane < ROPE) if h % 2 == 0 else (lane >= ROPE)
        q_ref[h, :, 0:KV_RANK] = ql.astype(BF16)
        q_ref[h, :, KV_RANK:KC_W] = jnp.where(keep, blk, 0.0).astype(BF16)
    ckv = ckv_ref[...]
    ckvn = ckv * lax.rsqrt(jnp.mean(ckv * ckv, -1, keepdims=True) + EPS) * kvg_ref[...]
    kr2 = kra_ref[...] * cos_k + krb_ref[...] * sin_k
    ckvo_ref[...] = ckvn
    kro_ref[...] = kr2[:, 0:ROPE]
    kc_ref[:, 0:KV_RANK] = ckvn.astype(BF16)
    kc_ref[:, KV_RANK:KC_W] = kr2.astype(BF16)
    if maybe_vt_ref:
        maybe_vt_ref[0][0] = ckvn.T.astype(BF16)


def _mla_pre(h1, cos_t, sin_t, qg, wuq, wuk, kvg, bsz, seq, tm, with_vt, layer, depth, carried):
    t_rows = bsz * seq
    ntab = cos_t.shape[0] // tm
    npb = max(seq // tm, 1)
    const2 = lambda i: (0, 0)
    out_specs = [pl.BlockSpec((H_B, tm, KC_W), lambda i: (0, i, 0)),
                 pl.BlockSpec((tm, KC_W), lambda i: (i, 0)),
                 pl.BlockSpec((None, tm, KV_RANK), lambda i: (layer, i, 0)),
                 pl.BlockSpec((None, tm, ROPE), lambda i: (layer, i, 0))]
    out_shape = [jax.ShapeDtypeStruct((H_B, t_rows, KC_W), BF16),
                 jax.ShapeDtypeStruct((t_rows, KC_W), BF16),
                 jax.ShapeDtypeStruct((depth, t_rows, KV_RANK), F32),
                 jax.ShapeDtypeStruct((depth, t_rows, ROPE), F32)]
    if with_vt:
        out_specs.append(pl.BlockSpec((1, KV_RANK, tm), lambda i: (i // npb, 0, i % npb)))
        out_shape.append(jax.ShapeDtypeStruct((bsz, KV_RANK, seq), BF16))
    n_in = 10
    return pl.pallas_call(
        functools.partial(_mla_pre_body, n_carried=len(carried)),
        input_output_aliases={n_in + k: 2 + k for k in range(len(carried))},
        grid=(t_rows // tm,),
        in_specs=[pl.BlockSpec((tm, Q_RANK), lambda i: (i, COL_CQ // Q_RANK)),
                  pl.BlockSpec((tm, KV_RANK), lambda i: (i, COL_CKV // KV_RANK)),
                  pl.BlockSpec((tm, LANES), lambda i: (i, COL_KRA // LANES)),
                  pl.BlockSpec((tm, LANES), lambda i: (i, COL_KRB // LANES)),
                  pl.BlockSpec((tm, LANES), lambda i: (i % ntab, 0)),
                  pl.BlockSpec((tm, LANES), lambda i: (i % ntab, 0)),
                  pl.BlockSpec((1, Q_RANK), const2),
                  pl.BlockSpec(wuq.shape, const2),
                  pl.BlockSpec(wuk.shape, lambda i: (0, 0, 0)),
                  pl.BlockSpec((1, KV_RANK), const2)] + [pl.BlockSpec(memory_space=pl.ANY)] * len(carried),
        out_specs=out_specs,
        out_shape=out_shape,
        compiler_params=_cparams(("parallel",)),
        name="mla_pre",
    )(h1, h1, h1, h1, cos_t, sin_t, qg, wuq, wuk, kvg, *carried)


def _attn_body(q_ref, k_ref, vt_ref, wuv_ref, o_ref, m_ref, l_ref, acc_ref, ex_ref, *, tq, tk, cw, lk):
    i = pl.program_id(1)
    shift = CHUNK.bit_length() - 1

    def update(j, masked, lazy, opening=False, wide=1):
        tkw = wide * tk
        k0 = pl.multiple_of(j * tk, tk)
        kt = k_ref[0, pl.ds(k0, tkw), :]
        vt = vt_ref[0, :, pl.ds(k0, tkw)]
        bias = None
        if masked:
            kpos = k0 + lax.broadcasted_iota(jnp.int32, (tkw, tq), 0)
            qpos = i * tq + lax.broadcasted_iota(jnp.int32, (tkw, tq), 1)
            bias = jnp.where((kpos >> shift) <= (qpos >> shift), 0.0, -jnp.inf).astype(F32)

        units = [(h, slice(c, c + cw)) for h in range(H_B) for c in range(0, tq, cw)]

        def scores(u):
            h, cs = u
            s = lax.dot_general(kt, q_ref[h, cs, :], (((1,), (1,)), ((), ())),
                                preferred_element_type=F32)
            return s if bias is None else s + bias[:, cs]

        def softmax(u, s):
            h, cs = u
            c_max = jnp.max(s, 0, keepdims=True)
            if opening:
                m_prev = jnp.zeros_like(c_max)
                m_new = c_max
                gap = jnp.abs(c_max)
            else:
                m_prev = m_ref[h, :, cs]
                m_new = jnp.maximum(m_prev, c_max)
                gap = c_max - m_prev
            alpha = jnp.exp2(m_prev - m_new)
            if lazy:
                p = jnp.exp2(s - m_prev)
                ex_ref[h, :, cs] = jnp.maximum(ex_ref[h, :, cs], gap)
                l_ref[h, :, cs] = alpha * (l_ref[h, :, cs] + jnp.sum(p, 0, keepdims=True))
            else:
                p = jnp.exp2(s - m_new)
                l_ref[h, :, cs] = alpha * l_ref[h, :, cs] + jnp.sum(p, 0, keepdims=True)
            m_ref[h, :, cs] = m_new
            return p.astype(BF16), alpha

        def accumulate(u, alpha, pv):
            h, cs = u
            if lazy:
                acc_ref[h, :, cs] = alpha * (acc_ref[h, :, cs] + pv)
            else:
                acc_ref[h, :, cs] = alpha * acc_ref[h, :, cs] + pv

        n_units = len(units)
        s_q = {0: scores(units[0])}
        if n_units > 1:
            s_q[1] = scores(units[1])
        p0, alpha = softmax(units[0], s_q.pop(0))
        alphas = {0: alpha}
        pvs = {0: jnp.dot(vt, p0, preferred_element_type=F32)}
        for n in range(n_units):
            if n + 2 < n_units:
                s_q[n + 2] = scores(units[n + 2])
            if n + 1 < n_units:
                p_next, alphas[n + 1] = softmax(units[n + 1], s_q.pop(n + 1))
            accumulate(units[n], alphas.pop(n), pvs.pop(n))
            if n + 1 < n_units:
                pvs[n + 1] = jnp.dot(vt, p_next, preferred_element_type=F32)

    n_full = ((((i * tq) >> shift) + 1) << shift) // tk
    n_all = jnp.minimum(((((i * tq + tq - 1) >> shift) + 1) << shift) + tk - 1, lk + tk - 1) // tk

    def tile_loop(lo, hi, masked, lazy, opening=False, wide=1, first_tile=0):
        def step(p, carry):
            update(first_tile + p * wide, masked, lazy, opening, wide)
            return carry
        lax.fori_loop(lo, hi, step, 0)

    def attempt(a, redo):
        first = a == 0
        run = jnp.logical_or(first, redo > 0)

        @pl.when(run)
        def _():
            m_ref[...] = jnp.full(m_ref.shape, -jnp.inf, F32)
            l_ref[...] = jnp.zeros(l_ref.shape, F32)
            acc_ref[...] = jnp.zeros(acc_ref.shape, F32)
            ex_ref[...] = jnp.full(ex_ref.shape, -jnp.inf, F32)

        tile_loop(0, jnp.where(first, 0, n_all) * run.astype(jnp.int32), True, False)
        tile_loop(0, jnp.where(first, 1, 0), True, True, True)
        n_wide = jnp.maximum(n_full - 1, 0) // WIDE_ATTN
        tile_loop(0, jnp.where(first, n_wide, 0), False, True, wide=WIDE_ATTN, first_tile=1)
        tile_loop(1 + n_wide * WIDE_ATTN, jnp.where(first, n_full, 0), False, True)
        tile_loop(jnp.maximum(n_full, 1), jnp.where(first, n_all, 0), True, True)
        excess = jnp.max(ex_ref[...])
        return jnp.where(first, (excess > MAX_STALE_EXCESS).astype(jnp.int32), 0)

    lax.fori_loop(0, 2, attempt, jnp.int32(0))

    for h in range(H_B):
        o_t = (acc_ref[h] * (1.0 / l_ref[h])).astype(BF16)
        ob_t = jnp.dot(wuv_ref[h], o_t, preferred_element_type=F32)
        o_ref[0, :, h * V_B:(h + 1) * V_B] = ob_t.T.astype(o_ref.dtype)


def _attn(q, kc, vt, wuv_t, bsz, seq, tq, tk, cw):
    assert seq % tq == 0 and seq % tk == 0
    nq = seq // tq
    body = functools.partial(_attn_body, tq=tq, tk=tk, cw=cw, lk=seq)
    return pl.pallas_call(
        body,
        grid=(bsz, nq),
        in_specs=[pl.BlockSpec((H_B, tq, KC_W), lambda b, i: (0, b * nq + i, 0)),
                  pl.BlockSpec((1, seq, KC_W), lambda b, i: (b, 0, 0)),
                  pl.BlockSpec((1, KV_RANK, seq), lambda b, i: (b, 0, 0)),
                  pl.BlockSpec(wuv_t.shape, lambda b, i: (0, 0, 0))],
        out_specs=pl.BlockSpec((1, tq, H_B * V_B), lambda b, i: (b, i, 0)),
        out_shape=jax.ShapeDtypeStruct((bsz, seq, H_B * V_B), BF16),
        scratch_shapes=[pltpu.VMEM((H_B, 1, tq), F32), pltpu.VMEM((H_B, 1, tq), F32),
                        pltpu.VMEM((H_B, KV_RANK, tq), F32), pltpu.VMEM((H_B, 1, tq), F32)],
        compiler_params=_cparams(("parallel", "arbitrary")),
        name="attn",
    )(q, kc, vt, wuv_t)


def _attn_dec_body(q_ref, ckv_ref, krt_ref, kn_ref, wuv_ref, o_ref, m_ref, l_ref, acc_ref,
                   *, tq, past, n_past):
    j = pl.program_id(1)
    rows = H_B * tq
    nt_dims = (((1,), (1,)), ((), ()))
    q2 = q_ref[...].reshape(rows, KC_W)

    @pl.when(j == 0)
    def _():
        m_ref[...] = jnp.full(m_ref.shape, -jnp.inf, F32)
        l_ref[...] = jnp.zeros(l_ref.shape, F32)
        acc_ref[...] = jnp.zeros(acc_ref.shape, F32)

    def accumulate(s, v_nat):
        m_prev = m_ref[...]
        m_new = jnp.maximum(m_prev, jnp.max(s, -1, keepdims=True))
        p = jnp.exp2(s - m_new)
        alpha = jnp.exp2(m_prev - m_new)
        l_ref[...] = alpha * l_ref[...] + jnp.sum(p, -1, keepdims=True)
        acc_ref[...] = alpha * acc_ref[...] + jnp.dot(p.astype(BF16), v_nat, preferred_element_type=F32)
        m_ref[...] = m_new

    @pl.when(j < n_past)
    def _():
        k_lat = ckv_ref[0].astype(BF16)
        kr_t = krt_ref[0].astype(BF16)
        kr2_t = jnp.concatenate([kr_t, kr_t], axis=0)
        s = (lax.dot_general(q2[:, 0:KV_RANK], k_lat, nt_dims, preferred_element_type=F32)
             + jnp.dot(q2[:, KV_RANK:KC_W], kr2_t, preferred_element_type=F32))
        accumulate(s, k_lat)

    @pl.when(j == n_past)
    def _():
        kn = kn_ref[...]
        s = lax.dot_general(q2, kn, nt_dims, preferred_element_type=F32)
        shift = CHUNK.bit_length() - 1
        qpos = past + (lax.broadcasted_iota(jnp.int32, s.shape, 0) & (tq - 1))
        kpos = past + lax.broadcasted_iota(jnp.int32, s.shape, 1)
        s = jnp.where((kpos >> shift) <= (qpos >> shift), s, -jnp.inf)
        accumulate(s, kn[:, 0:KV_RANK])
        o = acc_ref[...] * (1.0 / l_ref[...])
        for h in range(H_B):
            oh = o[h * tq:(h + 1) * tq, :].astype(BF16)
            o_ref[:, h * V_B:(h + 1) * V_B] = jnp.dot(
                oh, wuv_ref[h], preferred_element_type=F32).astype(o_ref.dtype)


def _attn_dec(q, cache_ckv, cache_kr_t, layer, kc_new, wuv, bsz, seq, tk):
    past = cache_ckv.shape[2]
    assert past % tk == 0 and past % CHUNK == 0 and seq & (seq - 1) == 0
    n_past = past // tk
    body = functools.partial(_attn_dec_body, tq=seq, past=past, n_past=n_past)
    return pl.pallas_call(
        body,
        grid=(bsz, n_past + 1),
        in_specs=[pl.BlockSpec((H_B, seq, KC_W), lambda b, j: (0, b, 0)),
                  pl.BlockSpec((None, 1, tk, KV_RANK), lambda b, j: (layer, b, jnp.minimum(j, n_past - 1), 0)),
                  pl.BlockSpec((None, 1, ROPE, tk), lambda b, j: (layer, b, 0, jnp.minimum(j, n_past - 1))),
                  pl.BlockSpec((seq, KC_W), lambda b, j: (b, 0)),
                  pl.BlockSpec(wuv.shape, lambda b, j: (0, 0, 0))],
        out_specs=pl.BlockSpec((seq, H_B * V_B), lambda b, j: (b, 0)),
        out_shape=jax.ShapeDtypeStruct((bsz * seq, H_B * V_B), BF16),
        scratch_shapes=[pltpu.VMEM((H_B * seq, 1), F32), pltpu.VMEM((H_B * seq, 1), F32),
                        pltpu.VMEM((H_B * seq, KV_RANK), F32)],
        compiler_params=_cparams(("parallel", "arbitrary")),
        name="attn_dec",
    )(q, cache_ckv, cache_kr_t, kc_new, wuv)


def _layer_norm(r, g, b):
    mu = jnp.mean(r, -1, keepdims=True)
    d = r - mu
    var = jnp.mean(d * d, -1, keepdims=True)
    return d * lax.rsqrt(var + EPS) * g + b


def _merge_body(oa_ref, ob_ref, ga_ref, gb_ref, x_ref, woa_ref, wob_ref, wout_ref, g_ref, b_ref, o_ref):
    tm = x_ref.shape[0]
    for r0 in range(0, tm, tm // MERGE_SPLIT):
        rs = slice(r0, r0 + tm // MERGE_SPLIT)
        ya = jnp.dot(oa_ref[rs, :], woa_ref[...], preferred_element_type=F32)
        yb = jnp.dot(ob_ref[rs, :], wob_ref[...], preferred_element_type=F32)
        m = _sigmoid(ga_ref[rs, :]) * ya + _sigmoid(gb_ref[rs, :]) * yb
        r = ALPHA * x_ref[rs, :] + jnp.dot(m.astype(BF16), wout_ref[...], preferred_element_type=F32)
        o_ref[rs, :] = _layer_norm(r, g_ref[...], b_ref[...])


def _merge(oa, ob, h1, x, woa, wob, wout, g, b, tm):
    t = x.shape[0]
    row = lambda i: (i, 0)
    const = lambda i: (0, 0)
    wspec = pl.BlockSpec((D_MODEL, D_MODEL), const, pipeline_mode=pl.Buffered(1))
    return pl.pallas_call(
        _merge_body,
        grid=(t // tm,),
        in_specs=[pl.BlockSpec((tm, V_A), row), pl.BlockSpec((tm, H_B * V_B), row),
                  pl.BlockSpec((tm, D_MODEL), lambda i: (i, COL_GA // D_MODEL)),
                  pl.BlockSpec((tm, D_MODEL), lambda i: (i, COL_GB // D_MODEL)),
                  pl.BlockSpec((tm, D_MODEL), row), wspec, wspec, wspec,
                  pl.BlockSpec((1, D_MODEL), const), pl.BlockSpec((1, D_MODEL), const)],
        out_specs=pl.BlockSpec((tm, D_MODEL), row),
        out_shape=jax.ShapeDtypeStruct((t, D_MODEL), F32),
        compiler_params=_cparams(("parallel",)),
        name="merge",
    )(oa, ob, h1, h1, x, woa, wob, wout, g, b)


def _ffn_body(x_ref, wg_ref, wu_ref, wd_ref, g_ref, b_ref, o_ref, *, chunks):
    tm = x_ref.shape[0]
    for r0 in range(0, tm, tm // FFN_SPLIT):
        rs = slice(r0, r0 + tm // FFN_SPLIT)
        x = x_ref[rs, :]
        xb = x.astype(BF16)
        y = ALPHA * x
        off = 0
        for width in chunks:
            sl = slice(off, off + width)
            f1 = jnp.dot(xb, wg_ref[:, sl], preferred_element_type=F32)
            f3 = jnp.dot(xb, wu_ref[:, sl], preferred_element_type=F32)
            hc = (_silu(f1) * f3).astype(BF16)
            y = y + jnp.dot(hc, wd_ref[sl, :], preferred_element_type=F32)
            off += width
        o_ref[rs, :] = _layer_norm(y, g_ref[...], b_ref[...])


def _ffn(x, wg, wu, wd, g, b, tm, chunks):
    assert sum(chunks) == D_FF
    t = x.shape[0]
    row = lambda i: (i, 0)
    const = lambda i: (0, 0)
    single = pl.Buffered(1)
    return pl.pallas_call(
        functools.partial(_ffn_body, chunks=chunks),
        grid=(t // tm,),
        in_specs=[pl.BlockSpec((tm, D_MODEL), row),
                  pl.BlockSpec((D_MODEL, D_FF), const, pipeline_mode=single),
                  pl.BlockSpec((D_MODEL, D_FF), const, pipeline_mode=single),
                  pl.BlockSpec((D_FF, D_MODEL), const, pipeline_mode=single),
                  pl.BlockSpec((1, D_MODEL), const), pl.BlockSpec((1, D_MODEL), const)],
        out_specs=pl.BlockSpec((tm, D_MODEL), row),
        out_shape=jax.ShapeDtypeStruct((t, D_MODEL), F32),
        compiler_params=_cparams(("parallel",)),
        name="ffn",
    )(x, wg, wu, wd, g, b)


def _prep_layer_weights(w_in_t, conv_w, a_log, dt_bias, gdn_norm_g, w_oa, q_norm_g, w_uq, kv_norm_g,
                        w_ukv, w_ob, w_out, ln1_g, ln1_b, w_gu, w_down, ln2_g, ln2_b):
    seg = lambda i: w_in_t[_OFFS[i]:_OFFS[i + 1], :]
    qkv, z, a, b, c_q, c_kv, k_r, g_a, g_b = (seg(i) for i in range(9))
    half = ROPE // 2
    k_r_rot = jnp.concatenate([-k_r[half:], k_r[:half]], axis=0)
    pad = jnp.zeros((LANES - 2 * H_A, D_MODEL), w_in_t.dtype)
    w_proj = jnp.concatenate([qkv, z, g_a, g_b, c_kv, k_r, k_r, c_q, k_r_rot, k_r_rot, a, b, pad],
                             axis=0).astype(BF16)
    lane_pad = jnp.zeros((LANES - H_A,), F32)
    al_lane = jnp.concatenate([a_log.astype(F32), lane_pad]).reshape(1, LANES)
    dt_lane = jnp.concatenate([dt_bias.astype(F32), lane_pad]).reshape(1, LANES)
    uq = w_uq.reshape(Q_RANK, H_B, NOPE + ROPE)
    uq_nope = uq[:, :, :NOPE].reshape(Q_RANK, H_B * NOPE)
    uq_rope = uq[:, :, NOPE:]
    uq_rot = jnp.concatenate([-uq_rope[:, :, half:], uq_rope[:, :, :half]], axis=2)
    w_uq_ext = jnp.concatenate([uq_nope, uq_rope.reshape(Q_RANK, H_B * ROPE),
                                uq_rot.reshape(Q_RANK, H_B * ROPE)], axis=1).astype(BF16)
    ukv = w_ukv.reshape(KV_RANK, H_B, NOPE + V_B)
    w_uk_t = jnp.transpose(ukv[:, :, :NOPE], (1, 2, 0)).astype(BF16)
    w_uv = jnp.transpose(ukv[:, :, NOPE:], (1, 0, 2)).astype(BF16)
    w_uv_t = jnp.transpose(ukv[:, :, NOPE:], (1, 2, 0)).astype(BF16)
    return dict(
        w_proj=w_proj, conv_w=conv_w.astype(F32), al_lane=al_lane, dt_lane=dt_lane,
        gn=gdn_norm_g.reshape(1, DV_A).astype(F32), w_oa=w_oa.astype(BF16),
        qg=q_norm_g.reshape(1, Q_RANK).astype(F32), w_uq=w_uq_ext, w_uk_t=w_uk_t, w_uv=w_uv, w_uv_t=w_uv_t,
        kvg=kv_norm_g.reshape(1, KV_RANK).astype(F32), w_ob=w_ob.astype(BF16),
        w_out=w_out.astype(BF16), ln1_g=ln1_g.reshape(1, D_MODEL), ln1_b=ln1_b.reshape(1, D_MODEL),
        w_g=w_gu[:, :D_FF].astype(BF16), w_u=w_gu[:, D_FF:].astype(BF16), w_down=w_down.astype(BF16),
        ln2_g=ln2_g.reshape(1, D_MODEL), ln2_b=ln2_b.reshape(1, D_MODEL))


def _rope_tables(past, seq, reps):
    half = ROPE // 2
    inv = ROPE_THETA ** (-jnp.arange(half, dtype=F32) / half)
    ang = (past + jnp.arange(seq)).astype(F32)[:, None] * inv[None, :]
    cos = jnp.tile(jnp.cos(ang), (reps, LANES // half))
    sin = jnp.tile(jnp.sin(ang), (reps, LANES // half))
    return cos, sin


TM_PROJ, TN_PROJ, TM_GATES, TM_MLA, TM_MERGE, TM_FFN = 512, N_PROJ // 4, 4096, 1024, 1024, 1024
FFN_CHUNKS = (768, 768, 768, D_FF - 3 * 768)
MERGE_SPLIT = 4
FFN_SPLIT = 4
TQ_ATTN, TK_ATTN, CW_ATTN, TK_DEC = 512, 512, 512, 2048
WIDE_ATTN = 2
TB_GDN, HB_GDN = 2 * GDN_CHUNK, H_A


def _trunk_layer(x, conv_state, s0, caches, wl, bsz, seq, layer, depth, carried):
    decode = caches is not None
    t_rows = bsz * seq
    h1 = _proj_in(x, wl["w_proj"], min(TM_PROJ, t_rows), TN_PROJ)
    seq_pad = -(-seq // GDN_CHUNK) * GDN_CHUNK
    if seq_pad == seq:
        gates = _gates(h1, COL_AB // LANES, wl["al_lane"], wl["dt_lane"], min(TM_GATES, t_rows), seq, seq)
    else:
        ab = h1[:, COL_AB:COL_AB + LANES].reshape(bsz, seq, LANES)
        ab = jnp.pad(ab, ((0, 0), (0, seq_pad - seq), (0, 0))).reshape(bsz * seq_pad, LANES)
        gates = _gates(ab, 0, wl["al_lane"], wl["dt_lane"], min(TM_GATES, bsz * seq_pad), seq_pad, seq)
    o_a, s_new = _gdn(h1, gates.reshape(2 * H_A, 1, bsz * seq_pad), wl["conv_w"], conv_state, s0, wl["gn"],
                      bsz, seq, min(TB_GDN, seq_pad), HB_GDN)
    if decode:
        cache_ckv, cache_kr_t = caches
        tm_mla = t_rows if t_rows <= TM_MLA else seq
        cos_t, sin_t = _rope_tables(cache_ckv.shape[2], seq, tm_mla // seq)
        q, kc, ckv_all, kr_all = _mla_pre(h1, cos_t, sin_t, wl["qg"], wl["w_uq"], wl["w_uk_t"], wl["kvg"],
                                          bsz, seq, tm_mla, False, layer, depth, carried)
        tk_dec = TK_DEC
        while cache_ckv.shape[2] % tk_dec:
            tk_dec //= 2
        o_b = _attn_dec(q, cache_ckv, cache_kr_t, layer, kc, wl["w_uv"], bsz, seq, tk_dec)
    else:
        cos_t, sin_t = _rope_tables(0, seq, 1)
        q, kc, ckv_all, kr_all, vt = _mla_pre(h1, cos_t, sin_t, wl["qg"], wl["w_uq"], wl["w_uk_t"],
                                              wl["kvg"], bsz, seq, TM_MLA, True, layer, depth, carried)
        o_b = _attn(q, kc.reshape(bsz, seq, KC_W), vt, wl["w_uv_t"], bsz, seq, TQ_ATTN, TK_ATTN, CW_ATTN)
        o_b = o_b.reshape(t_rows, H_B * V_B)
    x1 = _merge(o_a, o_b, h1, x, wl["w_oa"], wl["w_ob"], wl["w_out"], wl["ln1_g"], wl["ln1_b"],
                min(TM_MERGE, t_rows))
    x2 = _ffn(x1, wl["w_g"], wl["w_u"], wl["w_down"], wl["ln2_g"], wl["ln2_b"], min(TM_FFN, t_rows), FFN_CHUNKS)
    conv_new = h1.reshape(bsz, seq, N_PROJ)[:, seq - (CONV_W - 1):, COL_QKV:COL_QKV + C_QKV]
    return x2, conv_new, s_new, (ckv_all, kr_all)


def kernel(x_prompt, x_sample, state_conv, state_gdn, cache_ckv, cache_krope, w_in, conv_w, a_log, dt_bias, gdn_norm_g, w_oa, q_norm_g, w_uq, kv_norm_g, w_ukv, w_ob, w_out, ln1_g, ln1_b, w_gu, w_down, ln2_g, ln2_b):
    bp, lp, _ = x_prompt.shape
    bs, ls, _ = x_sample.shape
    yp = x_prompt.reshape(bp * lp, D_MODEL)
    ys = x_sample.reshape(bs * ls, D_MODEL)
    zero_conv = jnp.zeros((bp, CONV_W - 1, C_QKV), F32)
    zero_s = jnp.zeros((bp, H_A, DK_A, DV_A), F32)
    cache_kr_t = jnp.swapaxes(cache_krope, 2, 3)
    w_in_t = jnp.swapaxes(w_in, 1, 2)
    depth = w_in.shape[0]
    conv_p, gdn_p, conv_s, gdn_s = [], [], [], []
    kv_p = kv_s = ()
    for l in range(depth):
        wl = _prep_layer_weights(w_in_t[l], conv_w[l], a_log[l], dt_bias[l], gdn_norm_g[l], w_oa[l],
                                 q_norm_g[l], w_uq[l], kv_norm_g[l], w_ukv[l], w_ob[l], w_out[l],
                                 ln1_g[l], ln1_b[l], w_gu[l], w_down[l], ln2_g[l], ln2_b[l])
        yp, c_new, g_new, kv_p = _trunk_layer(yp, zero_conv, zero_s, None, wl, bp, lp, l, depth, kv_p)
        conv_p.append(c_new), gdn_p.append(g_new)
        ys, c_new, g_new, kv_s = _trunk_layer(ys, state_conv[l], state_gdn[l], (cache_ckv, cache_kr_t), wl,
                                              bs, ls, l, depth, kv_s)
        conv_s.append(c_new), gdn_s.append(g_new)
    return (yp.reshape(bp, lp, D_MODEL), ys.reshape(bs, ls, D_MODEL),
            jnp.stack(conv_p), jnp.stack(gdn_p),
            kv_p[0].reshape(depth, bp, lp, KV_RANK), kv_p[1].reshape(depth, bp, lp, ROPE),
            jnp.stack(conv_s), jnp.stack(gdn_s),
            kv_s[0].reshape(depth, bs, ls, KV_RANK), kv_s[1].reshape(depth, bs, ls, ROPE))
```

```python
import functools

import numpy as np
import jax
import jax.numpy as jnp
from jax import lax
from jax.experimental import pallas as pl
from jax.experimental.pallas import tpu as pltpu

F32 = jnp.float32
BF16 = jnp.bfloat16

D_MODEL = 1024
DEPTH = 2
CHUNK = 64
H_A = 8
DK_A = 128
DV_A = 128
QK_A = H_A * DK_A
V_A = H_A * DV_A
C_QKV = 2 * QK_A + V_A
CONV_W = 4
H_B = 8
NOPE = 128
ROPE = 64
V_B = 128
Q_RANK = 384
KV_RANK = 256
ROPE_THETA = 10000.0
ATTN_SCALE = (NOPE + ROPE) ** -0.5
LOG2_E = float(np.log2(np.e))
Q_SCALE = ATTN_SCALE * LOG2_E
D_FF = -(-8 * D_MODEL // (3 * 256)) * 256
ALPHA = (2 * DEPTH) ** 0.25
EPS = 1e-6
_SIZES = (C_QKV, V_A, H_A, H_A, Q_RANK, KV_RANK, ROPE, D_MODEL, D_MODEL)
_OFFS = tuple(int(v) for v in np.cumsum((0,) + _SIZES))

LANES = 128
VMEM_LIMIT = 56 * 1024 * 1024

COL_QKV = 0
COL_Z = COL_QKV + C_QKV
COL_GA = COL_Z + V_A
COL_GB = COL_GA + D_MODEL
COL_CKV = COL_GB + D_MODEL
COL_KRA = COL_CKV + KV_RANK
COL_CQ = COL_KRA + LANES
COL_KRB = COL_CQ + Q_RANK
COL_AB = COL_KRB + LANES
N_PROJ = COL_AB + LANES
KC_W = KV_RANK + LANES

GDN_CHUNK = 128
MAX_STALE_EXCESS = 64.0


def _cparams(sem):
    return pltpu.CompilerParams(dimension_semantics=sem, vmem_limit_bytes=VMEM_LIMIT)


def _sigmoid(x):
    return jax.nn.sigmoid(x)


def _silu(x):
    return x * jax.nn.sigmoid(x)


def _mm(a, b):
    return jnp.dot(a.astype(BF16), b.astype(BF16), preferred_element_type=F32)


def _mm_nt(a, b):
    return lax.dot_general(a.astype(BF16), b.astype(BF16), (((1,), (1,)), ((), ())),
                           preferred_element_type=F32)


def _mm_tn(a, b):
    return lax.dot_general(a.astype(BF16), b.astype(BF16), (((0,), (0,)), ((), ())),
                           preferred_element_type=F32)


def _proj_body(x_ref, wt_ref, o_ref, *, tn):
    xb = x_ref[...].astype(BF16)
    for c in range(0, wt_ref.shape[0], tn):
        o_ref[:, c:c + tn] = lax.dot_general(xb, wt_ref[c:c + tn, :], (((1,), (1,)), ((), ())),
                                             preferred_element_type=F32)


def _proj_in(x, w_t, tm, tn):
    t, k = x.shape
    n = w_t.shape[0]
    return pl.pallas_call(
        functools.partial(_proj_body, tn=tn),
        grid=(t // tm,),
        in_specs=[pl.BlockSpec((tm, k), lambda i: (i, 0)),
                  pl.BlockSpec((n, k), lambda i: (0, 0), pipeline_mode=pl.Buffered(1))],
        out_specs=pl.BlockSpec((tm, n), lambda i: (i, 0)),
        out_shape=jax.ShapeDtypeStruct((t, n), F32),
        compiler_params=_cparams(("parallel",)),
        name="proj_in",
    )(x, w_t)


def _gates_body(ab_ref, al_ref, dt_ref, o_ref, *, tm, l_pad, l_valid):
    x = ab_ref[...]
    lane = lax.broadcasted_iota(jnp.int32, x.shape, 1)
    xa = x + dt_ref[...]
    sp = jnp.maximum(xa, 0.0) + jnp.log1p(jnp.exp(-jnp.abs(xa)))
    g = -jnp.exp(al_ref[...]) * sp * LOG2_E
    y = jnp.where(lane < H_A, g, _sigmoid(x))
    yt = y.T[0:2 * H_A, :]
    if l_valid < l_pad:
        col = lax.broadcasted_iota(jnp.int32, yt.shape, 1) + pl.program_id(0) * tm
        yt = jnp.where(col % l_pad < l_valid, yt, 0.0)
    r = lax.broadcasted_iota(jnp.int32, (GDN_CHUNK, GDN_CHUNK), 0)
    c = lax.broadcasted_iota(jnp.int32, (GDN_CHUNK, GDN_CHUNK), 1)
    tri = jnp.where(r <= c, 1.0, 0.0).astype(F32)
    for s in range(tm // GDN_CHUNK):
        sl = slice(s * GDN_CHUNK, (s + 1) * GDN_CHUNK)
        o_ref[0:H_A, sl] = jnp.dot(yt[0:H_A, sl], tri, precision=lax.Precision.HIGHEST,
                                   preferred_element_type=F32)
    o_ref[H_A:2 * H_A, :] = yt[H_A:2 * H_A, :]


def _gates(src, col_block, al_lane, dt_lane, tm, l_pad, l_valid):
    t = src.shape[0]
    body = functools.partial(_gates_body, tm=tm, l_pad=l_pad, l_valid=l_valid)
    return pl.pallas_call(
        body,
        grid=(t // tm,),
        in_specs=[pl.BlockSpec((tm, LANES), lambda i: (i, col_block)),
                  pl.BlockSpec((1, LANES), lambda i: (0, 0)),
                  pl.BlockSpec((1, LANES), lambda i: (0, 0))],
        out_specs=pl.BlockSpec((2 * H_A, tm), lambda i: (0, i)),
        out_shape=jax.ShapeDtypeStruct((2 * H_A, t), F32),
        compiler_params=_cparams(("parallel",)),
        name="gates",
    )(src, al_lane, dt_lane)


def _gdn_body(q_ref, k_ref, v_ref, z_ref, gc_ref, bt_ref, cwq_ref, cwk_ref, cwv_ref,
              csq_ref, csk_ref, csv_ref, s0_ref, gn_ref, o_ref, so_ref,
              xq_ref, xk_ref, xv_ref, s_ref, *, tb, tr, nt, hb):
    t = pl.program_id(2)
    hist = CONV_W - 1
    base = 8

    @pl.when(t == 0)
    def _():
        s_ref[...] = s0_ref[0]
        xq_ref[base - hist:base, :] = csq_ref[0]
        xk_ref[base - hist:base, :] = csk_ref[0]
        xv_ref[base - hist:base, :] = csv_ref[0]

    for xs_ref, raw_ref in ((xq_ref, q_ref), (xk_ref, k_ref), (xv_ref, v_ref)):
        xs_ref[base:base + tr, :] = raw_ref[...]
        if tr < tb:
            xs_ref[base + tr:base + tb, :] = jnp.zeros((tb - tr, xs_ref.shape[1]), F32)

    def conv(xs_ref, cw_ref, ln):
        slab = xs_ref[0:base + tr, ln]
        y = pltpu.roll(slab, hist, 0)[base:base + tr] * cw_ref[0:1, ln]
        for j in range(1, CONV_W):
            tap = slab if j == hist else pltpu.roll(slab, hist - j, 0)
            y = y + tap[base:base + tr] * cw_ref[j:j + 1, ln]
        y = _silu(y)
        if tr < tb:
            y = jnp.concatenate([y, jnp.zeros((tb - tr, LANES), F32)], axis=0)
        return y

    cc = GDN_CHUNK
    ii = lax.broadcasted_iota(jnp.int32, (cc, cc), 0)
    jj = lax.broadcasted_iota(jnp.int32, (cc, cc), 1)
    strict = ii > jj
    eye = jnp.where(ii == jj, 1.0, 0.0).astype(F32)
    heads = range(hb)
    lanes = [slice(hd * LANES, (hd + 1) * LANES) for hd in heads]
    units = [(hd, c) for hd in heads for c in range(tb // cc)]
    rows = {u: slice(u[1] * cc, (u[1] + 1) * cc) for u in units}
    q = [conv(xq_ref, cwq_ref, ln) for ln in lanes]
    k = [conv(xk_ref, cwk_ref, ln) for ln in lanes]
    v = [conv(xv_ref, cwv_ref, ln) for ln in lanes]
    q = [x * (lax.rsqrt(jnp.sum(x * x, -1, keepdims=True) + EPS) * (DK_A ** -0.5)) for x in q]
    k = [x * lax.rsqrt(jnp.sum(x * x, -1, keepdims=True) + EPS) for x in k]
    qc = {u: q[u[0]][rows[u]] for u in units}
    kc = {u: k[u[0]][rows[u]] for u in units}
    vc = {u: v[u[0]][rows[u]] for u in units}
    m_row = {u: jnp.broadcast_to(gc_ref[u[0], :, rows[u]], (cc, cc)) for u in units}
    m_col = {u: m_row[u].T for u in units}
    b_col = {u: jnp.broadcast_to(bt_ref[u[0], :, rows[u]], (cc, cc)).T for u in units}
    kq = {u: _mm_nt(jnp.concatenate([kc[u], qc[u]], axis=0), kc[u]) for u in units}
    diff = {u: m_col[u] - m_row[u] for u in units}
    dec = {u: jnp.exp2(jnp.where(strict, diff[u], -jnp.inf)) for u in units}
    a_mat = {u: b_col[u] * kq[u][0:cc] * dec[u] for u in units}
    qk = {u: kq[u][cc:2 * cc] * (dec[u] + eye) for u in units}
    e_g = {u: jnp.exp2(m_col[u]) for u in units}
    x_inv = {u: eye - jnp.where((ii >> 1) == (jj >> 1), a_mat[u], 0.0) for u in units}
    sft = 1
    while (1 << sft) < min(cc, tr):
        off = ((ii >> (sft + 1)) == (jj >> (sft + 1))) & ((ii >> sft) != (jj >> sft))
        lx = {u: _mm(jnp.where(off, a_mat[u], 0.0), x_inv[u]) for u in units}
        x_inv = {u: x_inv[u] - _mm(x_inv[u], lx[u]) for u in units}
        sft += 1
    uw = {u: _mm(x_inv[u], jnp.concatenate([vc[u] * b_col[u], kc[u] * (b_col[u] * e_g[u])], axis=1))
          for u in units}
    wq = {u: jnp.concatenate([uw[u][:, DV_A:], qc[u] * e_g[u]], axis=0) for u in units}
    g_last = {u: m_col[u][cc - 1:cc, :] for u in units}
    kd = {u: kc[u] * jnp.exp2(g_last[u] - m_col[u]) for u in units}
    s_state = [s_ref[hd] for hd in heads]
    for c in range(tb // cc):
        ws = [_mm(wq[(hd, c)], s_state[hd]) for hd in heads]
        v_new = [uw[(hd, c)][:, 0:DV_A] - ws[hd][0:cc] for hd in heads]
        s_state = [s_state[hd] * jnp.exp2(g_last[(hd, c)]) + _mm_tn(kd[(hd, c)], v_new[hd]) for hd in heads]
        o = [ws[hd][cc:2 * cc] + _mm(qk[(hd, c)], v_new[hd]) for hd in heads]
        o = [x * lax.rsqrt(jnp.mean(x * x, -1, keepdims=True) + EPS) * gn_ref[...] for x in o]
        nr = min(cc, tr - c * cc)
        r = slice(c * cc, c * cc + nr)
        for hd in heads:
            o_ref[r, lanes[hd]] = (o[hd][0:nr] * _silu(z_ref[r, lanes[hd]])).astype(o_ref.dtype)
    for hd in heads:
        s_ref[hd] = s_state[hd]

    xq_ref[base - hist:base, :] = xq_ref[base + tb - hist:base + tb, :]
    xk_ref[base - hist:base, :] = xk_ref[base + tb - hist:base + tb, :]
    xv_ref[base - hist:base, :] = xv_ref[base + tb - hist:base + tb, :]

    @pl.when(t == nt - 1)
    def _():
        so_ref[0] = s_ref[...]


def _gdn(h1, gates3, conv_w, conv_state, s0, gn, bsz, seq, tb, hb):
    nt = -(-seq // tb)
    tr = min(tb, seq)
    assert seq % tr == 0 and (tr == tb or (nt == 1 and tb == GDN_CHUNK))
    t_rows = bsz * seq
    body = functools.partial(_gdn_body, tb=tb, tr=tr, nt=nt, hb=hb)
    width = hb * LANES
    ng = H_A // hb

    def rows(seg):
        return pl.BlockSpec((tr, width), lambda b, g, t: (b * nt + t, seg * ng + g))

    def cw(seg):
        return pl.BlockSpec((CONV_W, width), lambda b, g, t: (0, seg * ng + g))

    def cs(seg):
        return pl.BlockSpec((1, CONV_W - 1, width), lambda b, g, t: (b, 0, seg * ng + g))

    return pl.pallas_call(
        body,
        grid=(bsz, ng, nt),
        in_specs=[rows(0), rows(1), rows(2), rows(COL_Z // QK_A),
                  pl.BlockSpec((hb, 1, tb), lambda b, g, t: (g, 0, b * nt + t)),
                  pl.BlockSpec((hb, 1, tb), lambda b, g, t: (ng + g, 0, b * nt + t)),
                  cw(0), cw(1), cw(2), cs(0), cs(1), cs(2),
                  pl.BlockSpec((1, hb, DK_A, DV_A), lambda b, g, t: (b, g, 0, 0)),
                  pl.BlockSpec((1, DV_A), lambda b, g, t: (0, 0))],
        out_specs=[pl.BlockSpec((tr, width), lambda b, g, t: (b * nt + t, g)),
                   pl.BlockSpec((1, hb, DK_A, DV_A), lambda b, g, t: (b, g, 0, 0))],
        out_shape=[jax.ShapeDtypeStruct((t_rows, V_A), BF16),
                   jax.ShapeDtypeStruct((bsz, H_A, DK_A, DV_A), F32)],
        scratch_shapes=[pltpu.VMEM((tb + 8, width), F32)] * 3 + [pltpu.VMEM((hb, DK_A, DV_A), F32)],
        compiler_params=_cparams(("parallel", "parallel", "arbitrary")),
        name="gdn",
    )(h1, h1, h1, h1, gates3, gates3, conv_w, conv_w, conv_w,
      conv_state, conv_state, conv_state, s0, gn)


def _mla_pre_body(cq_ref, ckv_ref, kra_ref, krb_ref, cos_ref, sin_ref, qg_ref, wuq_ref, wuk_ref,
                  kvg_ref, *refs, n_carried):
    q_ref, kc_ref, ckvo_ref, kro_ref, *maybe_vt_ref = refs[n_carried:]
    cq = cq_ref[...]
    cqn = cq * lax.rsqrt(jnp.mean(cq * cq, -1, keepdims=True) + EPS) * qg_ref[...]
    qf = jnp.dot(cqn.astype(BF16), wuq_ref[...], preferred_element_type=F32)
    cos_k = cos_ref[...]
    sin_k = sin_ref[...]
    reps = H_B * ROPE // LANES
    cos_t = jnp.concatenate([cos_k] * reps, axis=1)
    sin_t = jnp.concatenate([sin_k] * reps, axis=1)
    n_nope = H_B * NOPE
    n_rope = H_B * ROPE
    qr = (qf[:, n_nope:n_nope + n_rope] * cos_t + qf[:, n_nope + n_rope:] * sin_t) * Q_SCALE
    lane = lax.broadcasted_iota(jnp.int32, (cq.shape[0], LANES), 1)
    for h in range(H_B):
        ql = jnp.dot(qf[:, h * NOPE:(h + 1) * NOPE].astype(BF16), wuk_ref[h],
                     preferred_element_type=F32) * Q_SCALE
        blk = qr[:, (h // 2) * LANES:(h // 2 + 1) * LANES]
        keep = (lane < ROPE) if h % 2 == 0 else (lane >= ROPE)
        q_ref[h, :, 0:KV_RANK] = ql.astype(BF16)
        q_ref[h, :, KV_RANK:KC_W] = jnp.where(keep, blk, 0.0).astype(BF16)
    ckv = ckv_ref[...]
    ckvn = ckv * lax.rsqrt(jnp.mean(ckv * ckv, -1, keepdims=True) + EPS) * kvg_ref[...]
    kr2 = kra_ref[...] * cos_k + krb_ref[...] * sin_k
    ckvo_ref[...] = ckvn
    kro_ref[...] = kr2[:, 0:ROPE]
    kc_ref[:, 0:KV_RANK] = ckvn.astype(BF16)
    kc_ref[:, KV_RANK:KC_W] = kr2.astype(BF16)
    if maybe_vt_ref:
        maybe_vt_ref[0][0] = ckvn.T.astype(BF16)


def _mla_pre(h1, cos_t, sin_t, qg, wuq, wuk, kvg, bsz, seq, tm, with_vt, layer, depth, carried):
    t_rows = bsz * seq
    ntab = cos_t.shape[0] // tm
    npb = max(seq // tm, 1)
    const2 = lambda i: (0, 0)
    out_specs = [pl.BlockSpec((H_B, tm, KC_W), lambda i: (0, i, 0)),
                 pl.BlockSpec((tm, KC_W), lambda i: (i, 0)),
                 pl.BlockSpec((None, tm, KV_RANK), lambda i: (layer, i, 0)),
                 pl.BlockSpec((None, tm, ROPE), lambda i: (layer, i, 0))]
    out_shape = [jax.ShapeDtypeStruct((H_B, t_rows, KC_W), BF16),
                 jax.ShapeDtypeStruct((t_rows, KC_W), BF16),
                 jax.ShapeDtypeStruct((depth, t_rows, KV_RANK), F32),
                 jax.ShapeDtypeStruct((depth, t_rows, ROPE), F32)]
    if with_vt:
        out_specs.append(pl.BlockSpec((1, KV_RANK, tm), lambda i: (i // npb, 0, i % npb)))
        out_shape.append(jax.ShapeDtypeStruct((bsz, KV_RANK, seq), BF16))
    n_in = 10
    return pl.pallas_call(
        functools.partial(_mla_pre_body, n_carried=len(carried)),
        input_output_aliases={n_in + k: 2 + k for k in range(len(carried))},
        grid=(t_rows // tm,),
        in_specs=[pl.BlockSpec((tm, Q_RANK), lambda i: (i, COL_CQ // Q_RANK)),
                  pl.BlockSpec((tm, KV_RANK), lambda i: (i, COL_CKV // KV_RANK)),
                  pl.BlockSpec((tm, LANES), lambda i: (i, COL_KRA // LANES)),
                  pl.BlockSpec((tm, LANES), lambda i: (i, COL_KRB // LANES)),
                  pl.BlockSpec((tm, LANES), lambda i: (i % ntab, 0)),
                  pl.BlockSpec((tm, LANES), lambda i: (i % ntab, 0)),
                  pl.BlockSpec((1, Q_RANK), const2),
                  pl.BlockSpec(wuq.shape, const2),
                  pl.BlockSpec(wuk.shape, lambda i: (0, 0, 0)),
                  pl.BlockSpec((1, KV_RANK), const2)] + [pl.BlockSpec(memory_space=pl.ANY)] * len(carried),
        out_specs=out_specs,
        out_shape=out_shape,
        compiler_params=_cparams(("parallel",)),
        name="mla_pre",
    )(h1, h1, h1, h1, cos_t, sin_t, qg, wuq, wuk, kvg, *carried)


def _attn_body(q_ref, k_ref, vt_ref, wuv_ref, o_ref, m_ref, l_ref, acc_ref, ex_ref, *, tq, tk, cw, lk):
    i = pl.program_id(1)
    shift = CHUNK.bit_length() - 1

    def update(j, masked, lazy, opening=False, wide=1):
        tkw = wide * tk
        k0 = pl.multiple_of(j * tk, tk)
        kt = k_ref[0, pl.ds(k0, tkw), :]
        vt = vt_ref[0, :, pl.ds(k0, tkw)]
        bias = None
        if masked:
            kpos = k0 + lax.broadcasted_iota(jnp.int32, (tkw, tq), 0)
            qpos = i * tq + lax.broadcasted_iota(jnp.int32, (tkw, tq), 1)
            bias = jnp.where((kpos >> shift) <= (qpos >> shift), 0.0, -jnp.inf).astype(F32)

        units = [(h, slice(c, c + cw)) for h in range(H_B) for c in range(0, tq, cw)]

        def scores(u):
            h, cs = u
            s = lax.dot_general(kt, q_ref[h, cs, :], (((1,), (1,)), ((), ())),
                                preferred_element_type=F32)
            return s if bias is None else s + bias[:, cs]

        def softmax(u, s):
            h, cs = u
            c_max = jnp.max(s, 0, keepdims=True)
            if opening:
                m_prev = jnp.zeros_like(c_max)
                m_new = c_max
                gap = jnp.abs(c_max)
            else:
                m_prev = m_ref[h, :, cs]
                m_new = jnp.maximum(m_prev, c_max)
                gap = c_max - m_prev
            alpha = jnp.exp2(m_prev - m_new)
            if lazy:
                p = jnp.exp2(s - m_prev)
                ex_ref[h, :, cs] = jnp.maximum(ex_ref[h, :, cs], gap)
                l_ref[h, :, cs] = alpha * (l_ref[h, :, cs] + jnp.sum(p, 0, keepdims=True))
            else:
                p = jnp.exp2(s - m_new)
                l_ref[h, :, cs] = alpha * l_ref[h, :, cs] + jnp.sum(p, 0, keepdims=True)
            m_ref[h, :, cs] = m_new
            return p.astype(BF16), alpha

        def accumulate(u, alpha, pv):
            h, cs = u
            if lazy:
                acc_ref[h, :, cs] = alpha * (acc_ref[h, :, cs] + pv)
            else:
                acc_ref[h, :, cs] = alpha * acc_ref[h, :, cs] + pv

        n_units = len(units)
        s_q = {0: scores(units[0])}
        if n_units > 1:
            s_q[1] = scores(units[1])
        p0, alpha = softmax(units[0], s_q.pop(0))
        alphas = {0: alpha}
        pvs = {0: jnp.dot(vt, p0, preferred_element_type=F32)}
        for n in range(n_units):
            if n + 2 < n_units:
                s_q[n + 2] = scores(units[n + 2])
            if n + 1 < n_units:
                p_next, alphas[n + 1] = softmax(units[n + 1], s_q.pop(n + 1))
            accumulate(units[n], alphas.pop(n), pvs.pop(n))
            if n + 1 < n_units:
                pvs[n + 1] = jnp.dot(vt, p_next, preferred_element_type=F32)

    n_full = ((((i * tq) >> shift) + 1) << shift) // tk
    n_all = jnp.minimum(((((i * tq + tq - 1) >> shift) + 1) << shift) + tk - 1, lk + tk - 1) // tk

    def tile_loop(lo, hi, masked, lazy, opening=False, wide=1, first_tile=0):
        def step(p, carry):
            update(first_tile + p * wide, masked, lazy, opening, wide)
            return carry
        lax.fori_loop(lo, hi, step, 0)

    def attempt(a, redo):
        first = a == 0
        run = jnp.logical_or(first, redo > 0)

        @pl.when(run)
        def _():
            m_ref[...] = jnp.full(m_ref.shape, -jnp.inf, F32)
            l_ref[...] = jnp.zeros(l_ref.shape, F32)
            acc_ref[...] = jnp.zeros(acc_ref.shape, F32)
            ex_ref[...] = jnp.full(ex_ref.shape, -jnp.inf, F32)

        tile_loop(0, jnp.where(first, 0, n_all) * run.astype(jnp.int32), True, False)
        tile_loop(0, jnp.where(first, 1, 0), True, True, True)
        n_wide = jnp.maximum(n_full - 1, 0) // WIDE_ATTN
        tile_loop(0, jnp.where(first, n_wide, 0), False, True, wide=WIDE_ATTN, first_tile=1)
        tile_loop(1 + n_wide * WIDE_ATTN, jnp.where(first, n_full, 0), False, True)
        tile_loop(jnp.maximum(n_full, 1), jnp.where(first, n_all, 0), True, True)
        excess = jnp.max(ex_ref[...])
        return jnp.where(first, (excess > MAX_STALE_EXCESS).astype(jnp.int32), 0)

    lax.fori_loop(0, 2, attempt, jnp.int32(0))

    for h in range(H_B):
        o_t = (acc_ref[h] * (1.0 / l_ref[h])).astype(BF16)
        ob_t = jnp.dot(wuv_ref[h], o_t, preferred_element_type=F32)
        o_ref[0, :, h * V_B:(h + 1) * V_B] = ob_t.T.astype(o_ref.dtype)


def _attn(q, kc, vt, wuv_t, bsz, seq, tq, tk, cw):
    assert seq % tq == 0 and seq % tk == 0
    nq = seq // tq
    body = functools.partial(_attn_body, tq=tq, tk=tk, cw=cw, lk=seq)
    return pl.pallas_call(
        body,
        grid=(bsz, nq),
        in_specs=[pl.BlockSpec((H_B, tq, KC_W), lambda b, i: (0, b * nq + i, 0)),
                  pl.BlockSpec((1, seq, KC_W), lambda b, i: (b, 0, 0)),
                  pl.BlockSpec((1, KV_RANK, seq), lambda b, i: (b, 0, 0)),
                  pl.BlockSpec(wuv_t.shape, lambda b, i: (0, 0, 0))],
        out_specs=pl.BlockSpec((1, tq, H_B * V_B), lambda b, i: (b, i, 0)),
        out_shape=jax.ShapeDtypeStruct((bsz, seq, H_B * V_B), BF16),
        scratch_shapes=[pltpu.VMEM((H_B, 1, tq), F32), pltpu.VMEM((H_B, 1, tq), F32),
                        pltpu.VMEM((H_B, KV_RANK, tq), F32), pltpu.VMEM((H_B, 1, tq), F32)],
        compiler_params=_cparams(("parallel", "arbitrary")),
        name="attn",
    )(q, kc, vt, wuv_t)


def _attn_dec_body(q_ref, ckv_ref, krt_ref, kn_ref, wuv_ref, o_ref, m_ref, l_ref, acc_ref,
                   *, tq, past, n_past):
    j = pl.program_id(1)
    rows = H_B * tq
    nt_dims = (((1,), (1,)), ((), ()))
    q2 = q_ref[...].reshape(rows, KC_W)

    @pl.when(j == 0)
    def _():
        m_ref[...] = jnp.full(m_ref.shape, -jnp.inf, F32)
        l_ref[...] = jnp.zeros(l_ref.shape, F32)
        acc_ref[...] = jnp.zeros(acc_ref.shape, F32)

    def accumulate(s, v_nat):
        m_prev = m_ref[...]
        m_new = jnp.maximum(m_prev, jnp.max(s, -1, keepdims=True))
        p = jnp.exp2(s - m_new)
        alpha = jnp.exp2(m_prev - m_new)
        l_ref[...] = alpha * l_ref[...] + jnp.sum(p, -1, keepdims=True)
        acc_ref[...] = alpha * acc_ref[...] + jnp.dot(p.astype(BF16), v_nat, preferred_element_type=F32)
        m_ref[...] = m_new

    @pl.when(j < n_past)
    def _():
        k_lat = ckv_ref[0].astype(BF16)
        kr_t = krt_ref[0].astype(BF16)
        kr2_t = jnp.concatenate([kr_t, kr_t], axis=0)
        s = (lax.dot_general(q2[:, 0:KV_RANK], k_lat, nt_dims, preferred_element_type=F32)
             + jnp.dot(q2[:, KV_RANK:KC_W], kr2_t, preferred_element_type=F32))
        accumulate(s, k_lat)

    @pl.when(j == n_past)
    def _():
        kn = kn_ref[...]
        s = lax.dot_general(q2, kn, nt_dims, preferred_element_type=F32)
        shift = CHUNK.bit_length() - 1
        qpos = past + (lax.broadcasted_iota(jnp.int32, s.shape, 0) & (tq - 1))
        kpos = past + lax.broadcasted_iota(jnp.int32, s.shape, 1)
        s = jnp.where((kpos >> shift) <= (qpos >> shift), s, -jnp.inf)
        accumulate(s, kn[:, 0:KV_RANK])
        o = acc_ref[...] * (1.0 / l_ref[...])
        for h in range(H_B):
            oh = o[h * tq:(h + 1) * tq, :].astype(BF16)
            o_ref[:, h * V_B:(h + 1) * V_B] = jnp.dot(
                oh, wuv_ref[h], preferred_element_type=F32).astype(o_ref.dtype)


def _attn_dec(q, cache_ckv, cache_kr_t, layer, kc_new, wuv, bsz, seq, tk):
    past = cache_ckv.shape[2]
    assert past % tk == 0 and past % CHUNK == 0 and seq & (seq - 1) == 0
    n_past = past // tk
    body = functools.partial(_attn_dec_body, tq=seq, past=past, n_past=n_past)
    return pl.pallas_call(
        body,
        grid=(bsz, n_past + 1),
        in_specs=[pl.BlockSpec((H_B, seq, KC_W), lambda b, j: (0, b, 0)),
                  pl.BlockSpec((None, 1, tk, KV_RANK), lambda b, j: (layer, b, jnp.minimum(j, n_past - 1), 0)),
                  pl.BlockSpec((None, 1, ROPE, tk), lambda b, j: (layer, b, 0, jnp.minimum(j, n_past - 1))),
                  pl.BlockSpec((seq, KC_W), lambda b, j: (b, 0)),
                  pl.BlockSpec(wuv.shape, lambda b, j: (0, 0, 0))],
        out_specs=pl.BlockSpec((seq, H_B * V_B), lambda b, j: (b, 0)),
        out_shape=jax.ShapeDtypeStruct((bsz * seq, H_B * V_B), BF16),
        scratch_shapes=[pltpu.VMEM((H_B * seq, 1), F32), pltpu.VMEM((H_B * seq, 1), F32),
                        pltpu.VMEM((H_B * seq, KV_RANK), F32)],
        compiler_params=_cparams(("parallel", "arbitrary")),
        name="attn_dec",
    )(q, cache_ckv, cache_kr_t, kc_new, wuv)


def _layer_norm(r, g, b):
    mu = jnp.mean(r, -1, keepdims=True)
    d = r - mu
    var = jnp.mean(d * d, -1, keepdims=True)
    return d * lax.rsqrt(var + EPS) * g + b


def _merge_body(oa_ref, ob_ref, ga_ref, gb_ref, x_ref, woa_ref, wob_ref, wout_ref, g_ref, b_ref, o_ref):
    tm = x_ref.shape[0]
    for r0 in range(0, tm, tm // MERGE_SPLIT):
        rs = slice(r0, r0 + tm // MERGE_SPLIT)
        ya = jnp.dot(oa_ref[rs, :], woa_ref[...], preferred_element_type=F32)
        yb = jnp.dot(ob_ref[rs, :], wob_ref[...], preferred_element_type=F32)
        m = _sigmoid(ga_ref[rs, :]) * ya + _sigmoid(gb_ref[rs, :]) * yb
        r = ALPHA * x_ref[rs, :] + jnp.dot(m.astype(BF16), wout_ref[...], preferred_element_type=F32)
        o_ref[rs, :] = _layer_norm(r, g_ref[...], b_ref[...])


def _merge(oa, ob, h1, x, woa, wob, wout, g, b, tm):
    t = x.shape[0]
    row = lambda i: (i, 0)
    const = lambda i: (0, 0)
    wspec = pl.BlockSpec((D_MODEL, D_MODEL), const, pipeline_mode=pl.Buffered(1))
    return pl.pallas_call(
        _merge_body,
        grid=(t // tm,),
        in_specs=[pl.BlockSpec((tm, V_A), row), pl.BlockSpec((tm, H_B * V_B), row),
                  pl.BlockSpec((tm, D_MODEL), lambda i: (i, COL_GA // D_MODEL)),
                  pl.BlockSpec((tm, D_MODEL), lambda i: (i, COL_GB // D_MODEL)),
                  pl.BlockSpec((tm, D_MODEL), row), wspec, wspec, wspec,
                  pl.BlockSpec((1, D_MODEL), const), pl.BlockSpec((1, D_MODEL), const)],
        out_specs=pl.BlockSpec((tm, D_MODEL), row),
        out_shape=jax.ShapeDtypeStruct((t, D_MODEL), F32),
        compiler_params=_cparams(("parallel",)),
        name="merge",
    )(oa, ob, h1, h1, x, woa, wob, wout, g, b)


def _ffn_body(x_ref, wg_ref, wu_ref, wd_ref, g_ref, b_ref, o_ref, *, chunks):
    tm = x_ref.shape[0]
    for r0 in range(0, tm, tm // FFN_SPLIT):
        rs = slice(r0, r0 + tm // FFN_SPLIT)
        x = x_ref[rs, :]
        xb = x.astype(BF16)
        y = ALPHA * x
        off = 0
        for width in chunks:
            sl = slice(off, off + width)
            f1 = jnp.dot(xb, wg_ref[:, sl], preferred_element_type=F32)
            f3 = jnp.dot(xb, wu_ref[:, sl], preferred_element_type=F32)
            hc = (_silu(f1) * f3).astype(BF16)
            y = y + jnp.dot(hc, wd_ref[sl, :], preferred_element_type=F32)
            off += width
        o_ref[rs, :] = _layer_norm(y, g_ref[...], b_ref[...])


def _ffn(x, wg, wu, wd, g, b, tm, chunks):
    assert sum(chunks) == D_FF
    t = x.shape[0]
    row = lambda i: (i, 0)
    const = lambda i: (0, 0)
    single = pl.Buffered(1)
    return pl.pallas_call(
        functools.partial(_ffn_body, chunks=chunks),
        grid=(t // tm,),
        in_specs=[pl.BlockSpec((tm, D_MODEL), row),
                  pl.BlockSpec((D_MODEL, D_FF), const, pipeline_mode=single),
                  pl.BlockSpec((D_MODEL, D_FF), const, pipeline_mode=single),
                  pl.BlockSpec((D_FF, D_MODEL), const, pipeline_mode=single),
                  pl.BlockSpec((1, D_MODEL), const), pl.BlockSpec((1, D_MODEL), const)],
        out_specs=pl.BlockSpec((tm, D_MODEL), row),
        out_shape=jax.ShapeDtypeStruct((t, D_MODEL), F32),
        compiler_params=_cparams(("parallel",)),
        name="ffn",
    )(x, wg, wu, wd, g, b)


def _prep_layer_weights(w_in_t, conv_w, a_log, dt_bias, gdn_norm_g, w_oa, q_norm_g, w_uq, kv_norm_g,
                        w_ukv, w_ob, w_out, ln1_g, ln1_b, w_gu, w_down, ln2_g, ln2_b):
    seg = lambda i: w_in_t[_OFFS[i]:_OFFS[i + 1], :]
    qkv, z, a, b, c_q, c_kv, k_r, g_a, g_b = (seg(i) for i in range(9))
    half = ROPE // 2
    k_r_rot = jnp.concatenate([-k_r[half:], k_r[:half]], axis=0)
    pad = jnp.zeros((LANES - 2 * H_A, D_MODEL), w_in_t.dtype)
    w_proj = jnp.concatenate([qkv, z, g_a, g_b, c_kv, k_r, k_r, c_q, k_r_rot, k_r_rot, a, b, pad],
                             axis=0).astype(BF16)
    lane_pad = jnp.zeros((LANES - H_A,), F32)
    al_lane = jnp.concatenate([a_log.astype(F32), lane_pad]).reshape(1, LANES)
    dt_lane = jnp.concatenate([dt_bias.astype(F32), lane_pad]).reshape(1, LANES)
    uq = w_uq.reshape(Q_RANK, H_B, NOPE + ROPE)
    uq_nope = uq[:, :, :NOPE].reshape(Q_RANK, H_B * NOPE)
    uq_rope = uq[:, :, NOPE:]
    uq_rot = jnp.concatenate([-uq_rope[:, :, half:], uq_rope[:, :, :half]], axis=2)
    w_uq_ext = jnp.concatenate([uq_nope, uq_rope.reshape(Q_RANK, H_B * ROPE),
                                uq_rot.reshape(Q_RANK, H_B * ROPE)], axis=1).astype(BF16)
    ukv = w_ukv.reshape(KV_RANK, H_B, NOPE + V_B)
    w_uk_t = jnp.transpose(ukv[:, :, :NOPE], (1, 2, 0)).astype(BF16)
    w_uv = jnp.transpose(ukv[:, :, NOPE:], (1, 0, 2)).astype(BF16)
    w_uv_t = jnp.transpose(ukv[:, :, NOPE:], (1, 2, 0)).astype(BF16)
    return dict(
        w_proj=w_proj, conv_w=conv_w.astype(F32), al_lane=al_lane, dt_lane=dt_lane,
        gn=gdn_norm_g.reshape(1, DV_A).astype(F32), w_oa=w_oa.astype(BF16),
        qg=q_norm_g.reshape(1, Q_RANK).astype(F32), w_uq=w_uq_ext, w_uk_t=w_uk_t, w_uv=w_uv, w_uv_t=w_uv_t,
        kvg=kv_norm_g.reshape(1, KV_RANK).astype(F32), w_ob=w_ob.astype(BF16),
        w_out=w_out.astype(BF16), ln1_g=ln1_g.reshape(1, D_MODEL), ln1_b=ln1_b.reshape(1, D_MODEL),
        w_g=w_gu[:, :D_FF].astype(BF16), w_u=w_gu[:, D_FF:].astype(BF16), w_down=w_down.astype(BF16),
        ln2_g=ln2_g.reshape(1, D_MODEL), ln2_b=ln2_b.reshape(1, D_MODEL))


def _rope_tables(past, seq, reps):
    half = ROPE // 2
    inv = ROPE_THETA ** (-jnp.arange(half, dtype=F32) / half)
    ang = (past + jnp.arange(seq)).astype(F32)[:, None] * inv[None, :]
    cos = jnp.tile(jnp.cos(ang), (reps, LANES // half))
    sin = jnp.tile(jnp.sin(ang), (reps, LANES // half))
    return cos, sin


TM_PROJ, TN_PROJ, TM_GATES, TM_MLA, TM_MERGE, TM_FFN = 512, N_PROJ // 4, 4096, 1024, 1024, 1024
FFN_CHUNKS = (768, 768, 768, D_FF - 3 * 768)
MERGE_SPLIT = 4
FFN_SPLIT = 4
TQ_ATTN, TK_ATTN, CW_ATTN, TK_DEC = 512, 512, 512, 2048
WIDE_ATTN = 2
TB_GDN, HB_GDN = 2 * GDN_CHUNK, H_A


def _trunk_layer(x, conv_state, s0, caches, wl, bsz, seq, layer, depth, carried):
    decode = caches is not None
    t_rows = bsz * seq
    h1 = _proj_in(x, wl["w_proj"], min(TM_PROJ, t_rows), TN_PROJ)
    seq_pad = -(-seq // GDN_CHUNK) * GDN_CHUNK
    if seq_pad == seq:
        gates = _gates(h1, COL_AB // LANES, wl["al_lane"], wl["dt_lane"], min(TM_GATES, t_rows), seq, seq)
    else:
        ab = h1[:, COL_AB:COL_AB + LANES].reshape(bsz, seq, LANES)
        ab = jnp.pad(ab, ((0, 0), (0, seq_pad - seq), (0, 0))).reshape(bsz * seq_pad, LANES)
        gates = _gates(ab, 0, wl["al_lane"], wl["dt_lane"], min(TM_GATES, bsz * seq_pad), seq_pad, seq)
    o_a, s_new = _gdn(h1, gates.reshape(2 * H_A, 1, bsz * seq_pad), wl["conv_w"], conv_state, s0, wl["gn"],
                      bsz, seq, min(TB_GDN, seq_pad), HB_GDN)
    if decode:
        cache_ckv, cache_kr_t = caches
        tm_mla = t_rows if t_rows <= TM_MLA else seq
        cos_t, sin_t = _rope_tables(cache_ckv.shape[2], seq, tm_mla // seq)
        q, kc, ckv_all, kr_all = _mla_pre(h1, cos_t, sin_t, wl["qg"], wl["w_uq"], wl["w_uk_t"], wl["kvg"],
                                          bsz, seq, tm_mla, False, layer, depth, carried)
        tk_dec = TK_DEC
        while cache_ckv.shape[2] % tk_dec:
            tk_dec //= 2
        o_b = _attn_dec(q, cache_ckv, cache_kr_t, layer, kc, wl["w_uv"], bsz, seq, tk_dec)
    else:
        cos_t, sin_t = _rope_tables(0, seq, 1)
        q, kc, ckv_all, kr_all, vt = _mla_pre(h1, cos_t, sin_t, wl["qg"], wl["w_uq"], wl["w_uk_t"],
                                              wl["kvg"], bsz, seq, TM_MLA, True, layer, depth, carried)
        o_b = _attn(q, kc.reshape(bsz, seq, KC_W), vt, wl["w_uv_t"], bsz, seq, TQ_ATTN, TK_ATTN, CW_ATTN)
        o_b = o_b.reshape(t_rows, H_B * V_B)
    x1 = _merge(o_a, o_b, h1, x, wl["w_oa"], wl["w_ob"], wl["w_out"], wl["ln1_g"], wl["ln1_b"],
                min(TM_MERGE, t_rows))
    x2 = _ffn(x1, wl["w_g"], wl["w_u"], wl["w_down"], wl["ln2_g"], wl["ln2_b"], min(TM_FFN, t_rows), FFN_CHUNKS)
    conv_new = h1.reshape(bsz, seq, N_PROJ)[:, seq - (CONV_W - 1):, COL_QKV:COL_QKV + C_QKV]
    return x2, conv_new, s_new, (ckv_all, kr_all)


def kernel(x_prompt, x_sample, state_conv, state_gdn, cache_ckv, cache_krope, w_in, conv_w, a_log, dt_bias, gdn_norm_g, w_oa, q_norm_g, w_uq, kv_norm_g, w_ukv, w_ob, w_out, ln1_g, ln1_b, w_gu, w_down, ln2_g, ln2_b):
    bp, lp, _ = x_prompt.shape
    bs, ls, _ = x_sample.shape
    yp = x_prompt.reshape(bp * lp, D_MODEL)
    ys = x_sample.reshape(bs * ls, D_MODEL)
    zero_conv = jnp.zeros((bp, CONV_W - 1, C_QKV), F32)
    zero_s = jnp.zeros((bp, H_A, DK_A, DV_A), F32)
    cache_kr_t = jnp.swapaxes(cache_krope, 2, 3)
    w_in_t = jnp.swapaxes(w_in, 1, 2)
    depth = w_in.shape[0]
    conv_p, gdn_p, conv_s, gdn_s = [], [], [], []
    kv_p = kv_s = ()
    for l in range(depth):
        wl = _prep_layer_weights(w_in_t[l], conv_w[l], a_log[l], dt_bias[l], gdn_norm_g[l], w_oa[l],
                                 q_norm_g[l], w_uq[l], kv_norm_g[l], w_ukv[l], w_ob[l], w_out[l],
                                 ln1_g[l], ln1_b[l], w_gu[l], w_down[l], ln2_g[l], ln2_b[l])
        yp, c_new, g_new, kv_p = _trunk_layer(yp, zero_conv, zero_s, None, wl, bp, lp, l, depth, kv_p)
        conv_p.append(c_new), gdn_p.append(g_new)
        ys, c_new, g_new, kv_s = _trunk_layer(ys, state_conv[l], state_gdn[l], (cache_ckv, cache_kr_t), wl,
                                              bs, ls, l, depth, kv_s)
        conv_s.append(c_new), gdn_s.append(g_new)
    return (yp.reshape(bp, lp, D_MODEL), ys.reshape(bs, ls, D_MODEL),
            jnp.stack(conv_p), jnp.stack(gdn_p),
            kv_p[0].reshape(depth, bp, lp, KV_RANK), kv_p[1].reshape(depth, bp, lp, ROPE),
            jnp.stack(conv_s), jnp.stack(gdn_s),
            kv_s[0].reshape(depth, bs, ls, KV_RANK), kv_s[1].reshape(depth, bs, ls, ROPE))
```

```python
import functools

import numpy as np
import jax
import jax.numpy as jnp
from jax import lax
from jax.experimental import pallas as pl
from jax.experimental.pallas import tpu as pltpu

F32 = jnp.float32
BF16 = jnp.bfloat16

D_MODEL = 1024
DEPTH = 2
CHUNK = 64
H_A = 8
DK_A = 128
DV_A = 128
QK_A = H_A * DK_A
V_A = H_A * DV_A
C_QKV = 2 * QK_A + V_A
CONV_W = 4
H_B = 8
NOPE = 128
ROPE = 64
V_B = 128
Q_RANK = 384
KV_RANK = 256
ROPE_THETA = 10000.0
ATTN_SCALE = (NOPE + ROPE) ** -0.5
LOG2_E = float(np.log2(np.e))
Q_SCALE = ATTN_SCALE * LOG2_E
D_FF = -(-8 * D_MODEL // (3 * 256)) * 256
ALPHA = (2 * DEPTH) ** 0.25
EPS = 1e-6
_SIZES = (C_QKV, V_A, H_A, H_A, Q_RANK, KV_RANK, ROPE, D_MODEL, D_MODEL)
_OFFS = tuple(int(v) for v in np.cumsum((0,) + _SIZES))

LANES = 128
VMEM_LIMIT = 56 * 1024 * 1024

COL_QKV = 0
COL_Z = COL_QKV + C_QKV
COL_GA = COL_Z + V_A
COL_GB = COL_GA + D_MODEL
COL_CKV = COL_GB + D_MODEL
COL_KRA = COL_CKV + KV_RANK
COL_CQ = COL_KRA + LANES
COL_KRB = COL_CQ + Q_RANK
COL_AB = COL_KRB + LANES
N_PROJ = COL_AB + LANES
KC_W = KV_RANK + LANES

GDN_CHUNK = 128
MAX_STALE_EXCESS = 64.0


def _cparams(sem):
    return pltpu.CompilerParams(dimension_semantics=sem, vmem_limit_bytes=VMEM_LIMIT)


def _sigmoid(x):
    return jax.nn.sigmoid(x)


def _silu(x):
    return x * jax.nn.sigmoid(x)


def _mm(a, b):
    return jnp.dot(a.astype(BF16), b.astype(BF16), preferred_element_type=F32)


def _mm_nt(a, b):
    return lax.dot_general(a.astype(BF16), b.astype(BF16), (((1,), (1,)), ((), ())),
                           preferred_element_type=F32)


def _mm_tn(a, b):
    return lax.dot_general(a.astype(BF16), b.astype(BF16), (((0,), (0,)), ((), ())),
                           preferred_element_type=F32)


def _proj_body(x_ref, wt_ref, o_ref, *, tn):
    xb = x_ref[...].astype(BF16)
    for c in range(0, wt_ref.shape[0], tn):
        o_ref[:, c:c + tn] = lax.dot_general(xb, wt_ref[c:c + tn, :], (((1,), (1,)), ((), ())),
                                             preferred_element_type=F32)


def _proj_in(x, w_t, tm, tn):
    t, k = x.shape
    n = w_t.shape[0]
    return pl.pallas_call(
        functools.partial(_proj_body, tn=tn),
        grid=(t // tm,),
        in_specs=[pl.BlockSpec((tm, k), lambda i: (i, 0)),
                  pl.BlockSpec((n, k), lambda i: (0, 0), pipeline_mode=pl.Buffered(1))],
        out_specs=pl.BlockSpec((tm, n), lambda i: (i, 0)),
        out_shape=jax.ShapeDtypeStruct((t, n), F32),
        compiler_params=_cparams(("parallel",)),
        name="proj_in",
    )(x, w_t)


def _gates_body(ab_ref, al_ref, dt_ref, o_ref, *, tm, l_pad, l_valid):
    x = ab_ref[...]
    lane = lax.broadcasted_iota(jnp.int32, x.shape, 1)
    xa = x + dt_ref[...]
    sp = jnp.maximum(xa, 0.0) + jnp.log1p(jnp.exp(-jnp.abs(xa)))
    g = -jnp.exp(al_ref[...]) * sp * LOG2_E
    y = jnp.where(lane < H_A, g, _sigmoid(x))
    yt = y.T[0:2 * H_A, :]
    if l_valid < l_pad:
        col = lax.broadcasted_iota(jnp.int32, yt.shape, 1) + pl.program_id(0) * tm
        yt = jnp.where(col % l_pad < l_valid, yt, 0.0)
    r = lax.broadcasted_iota(jnp.int32, (GDN_CHUNK, GDN_CHUNK), 0)
    c = lax.broadcasted_iota(jnp.int32, (GDN_CHUNK, GDN_CHUNK), 1)
    tri = jnp.where(r <= c, 1.0, 0.0).astype(F32)
    for s in range(tm // GDN_CHUNK):
        sl = slice(s * GDN_CHUNK, (s + 1) * GDN_CHUNK)
        o_ref[0:H_A, sl] = jnp.dot(yt[0:H_A, sl], tri, precision=lax.Precision.HIGHEST,
                                   preferred_element_type=F32)
    o_ref[H_A:2 * H_A, :] = yt[H_A:2 * H_A, :]


def _gates(src, col_block, al_lane, dt_lane, tm, l_pad, l_valid):
    t = src.shape[0]
    body = functools.partial(_gates_body, tm=tm, l_pad=l_pad, l_valid=l_valid)
    return pl.pallas_call(
        body,
        grid=(t // tm,),
        in_specs=[pl.BlockSpec((tm, LANES), lambda i: (i, col_block)),
                  pl.BlockSpec((1, LANES), lambda i: (0, 0)),
                  pl.BlockSpec((1, LANES), lambda i: (0, 0))],
        out_specs=pl.BlockSpec((2 * H_A, tm), lambda i: (0, i)),
        out_shape=jax.ShapeDtypeStruct((2 * H_A, t), F32),
        compiler_params=_cparams(("parallel",)),
        name="gates",
    )(src, al_lane, dt_lane)


def _gdn_body(q_ref, k_ref, v_ref, z_ref, gc_ref, bt_ref, cwq_ref, cwk_ref, cwv_ref,
              csq_ref, csk_ref, csv_ref, s0_ref, gn_ref, o_ref, so_ref,
              xq_ref, xk_ref, xv_ref, s_ref, *, tb, tr, nt, hb):
    t = pl.program_id(2)
    hist = CONV_W - 1
    base = 8

    @pl.when(t == 0)
    def _():
        s_ref[...] = s0_ref[0]
        xq_ref[base - hist:base, :] = csq_ref[0]
        xk_ref[base - hist:base, :] = csk_ref[0]
        xv_ref[base - hist:base, :] = csv_ref[0]

    for xs_ref, raw_ref in ((xq_ref, q_ref), (xk_ref, k_ref), (xv_ref, v_ref)):
        xs_ref[base:base + tr, :] = raw_ref[...]
        if tr < tb:
            xs_ref[base + tr:base + tb, :] = jnp.zeros((tb - tr, xs_ref.shape[1]), F32)

    def conv(xs_ref, cw_ref, ln):
        slab = xs_ref[0:base + tr, ln]
        y = pltpu.roll(slab, hist, 0)[base:base + tr] * cw_ref[0:1, ln]
        for j in range(1, CONV_W):
            tap = slab if j == hist else pltpu.roll(slab, hist - j, 0)
            y = y + tap[base:base + tr] * cw_ref[j:j + 1, ln]
        y = _silu(y)
        if tr < tb:
            y = jnp.concatenate([y, jnp.zeros((tb - tr, LANES), F32)], axis=0)
        return y

    cc = GDN_CHUNK
    ii = lax.broadcasted_iota(jnp.int32, (cc, cc), 0)
    jj = lax.broadcasted_iota(jnp.int32, (cc, cc), 1)
    strict = ii > jj
    eye = jnp.where(ii == jj, 1.0, 0.0).astype(F32)
    heads = range(hb)
    lanes = [slice(hd * LANES, (hd + 1) * LANES) for hd in heads]
    units = [(hd, c) for hd in heads for c in range(tb // cc)]
    rows = {u: slice(u[1] * cc, (u[1] + 1) * cc) for u in units}
    q = [conv(xq_ref, cwq_ref, ln) for ln in lanes]
    k = [conv(xk_ref, cwk_ref, ln) for ln in lanes]
    v = [conv(xv_ref, cwv_ref, ln) for ln in lanes]
    q = [x * (lax.rsqrt(jnp.sum(x * x, -1, keepdims=True) + EPS) * (DK_A ** -0.5)) for x in q]
    k = [x * lax.rsqrt(jnp.sum(x * x, -1, keepdims=True) + EPS) for x in k]
    qc = {u: q[u[0]][rows[u]] for u in units}
    kc = {u: k[u[0]][rows[u]] for u in units}
    vc = {u: v[u[0]][rows[u]] for u in units}
    m_row = {u: jnp.broadcast_to(gc_ref[u[0], :, rows[u]], (cc, cc)) for u in units}
    m_col = {u: m_row[u].T for u in units}
    b_col = {u: jnp.broadcast_to(bt_ref[u[0], :, rows[u]], (cc, cc)).T for u in units}
    kq = {u: _mm_nt(jnp.concatenate([kc[u], qc[u]], axis=0), kc[u]) for u in units}
    diff = {u: m_col[u] - m_row[u] for u in units}
    dec = {u: jnp.exp2(jnp.where(strict, diff[u], -jnp.inf)) for u in units}
    a_mat = {u: b_col[u] * kq[u][0:cc] * dec[u] for u in units}
    qk = {u: kq[u][cc:2 * cc] * (dec[u] + eye) for u in units}
    e_g = {u: jnp.exp2(m_col[u]) for u in units}
    x_inv = {u: eye - jnp.where((ii >> 1) == (jj >> 1), a_mat[u], 0.0) for u in units}
    sft = 1
    while (1 << sft) < min(cc, tr):
        off = ((ii >> (sft + 1)) == (jj >> (sft + 1))) & ((ii >> sft) != (jj >> sft))
        lx = {u: _mm(jnp.where(off, a_mat[u], 0.0), x_inv[u]) for u in units}
        x_inv = {u: x_inv[u] - _mm(x_inv[u], lx[u]) for u in units}
        sft += 1
    uw = {u: _mm(x_inv[u], jnp.concatenate([vc[u] * b_col[u], kc[u] * (b_col[u] * e_g[u])], axis=1))
          for u in units}
    wq = {u: jnp.concatenate([uw[u][:, DV_A:], qc[u] * e_g[u]], axis=0) for u in units}
    g_last = {u: m_col[u][cc - 1:cc, :] for u in units}
    kd = {u: kc[u] * jnp.exp2(g_last[u] - m_col[u]) for u in units}
    s_state = [s_ref[hd] for hd in heads]
    for c in range(tb // cc):
        ws = [_mm(wq[(hd, c)], s_state[hd]) for hd in heads]
        v_new = [uw[(hd, c)][:, 0:DV_A] - ws[hd][0:cc] for hd in heads]
        s_state = [s_state[hd] * jnp.exp2(g_last[(hd, c)]) + _mm_tn(kd[(hd, c)], v_new[hd]) for hd in heads]
        o = [ws[hd][cc:2 * cc] + _mm(qk[(hd, c)], v_new[hd]) for hd in heads]
        o = [x * lax.rsqrt(jnp.mean(x * x, -1, keepdims=True) + EPS) * gn_ref[...] for x in o]
        nr = min(cc, tr - c * cc)
        r = slice(c * cc, c * cc + nr)
        for hd in heads:
            o_ref[r, lanes[hd]] = (o[hd][0:nr] * _silu(z_ref[r, lanes[hd]])).astype(o_ref.dtype)
    for hd in heads:
        s_ref[hd] = s_state[hd]

    xq_ref[base - hist:base, :] = xq_ref[base + tb - hist:base + tb, :]
    xk_ref[base - hist:base, :] = xk_ref[base + tb - hist:base + tb, :]
    xv_ref[base - hist:base, :] = xv_ref[base + tb - hist:base + tb, :]

    @pl.when(t == nt - 1)
    def _():
        so_ref[0] = s_ref[...]


def _gdn(h1, gates3, conv_w, conv_state, s0, gn, bsz, seq, tb, hb):
    nt = -(-seq // tb)
    tr = min(tb, seq)
    assert seq % tr == 0 and (tr == tb or (nt == 1 and tb == GDN_CHUNK))
    t_rows = bsz * seq
    body = functools.partial(_gdn_body, tb=tb, tr=tr, nt=nt, hb=hb)
    width = hb * LANES
    ng = H_A // hb

    def rows(seg):
        return pl.BlockSpec((tr, width), lambda b, g, t: (b * nt + t, seg * ng + g))

    def cw(seg):
        return pl.BlockSpec((CONV_W, width), lambda b, g, t: (0, seg * ng + g))

    def cs(seg):
        return pl.BlockSpec((1, CONV_W - 1, width), lambda b, g, t: (b, 0, seg * ng + g))

    return pl.pallas_call(
        body,
        grid=(bsz, ng, nt),
        in_specs=[rows(0), rows(1), rows(2), rows(COL_Z // QK_A),
                  pl.BlockSpec((hb, 1, tb), lambda b, g, t: (g, 0, b * nt + t)),
                  pl.BlockSpec((hb, 1, tb), lambda b, g, t: (ng + g, 0, b * nt + t)),
                  cw(0), cw(1), cw(2), cs(0), cs(1), cs(2),
                  pl.BlockSpec((1, hb, DK_A, DV_A), lambda b, g, t: (b, g, 0, 0)),
                  pl.BlockSpec((1, DV_A), lambda b, g, t: (0, 0))],
        out_specs=[pl.BlockSpec((tr, width), lambda b, g, t: (b * nt + t, g)),
                   pl.BlockSpec((1, hb, DK_A, DV_A), lambda b, g, t: (b, g, 0, 0))],
        out_shape=[jax.ShapeDtypeStruct((t_rows, V_A), BF16),
                   jax.ShapeDtypeStruct((bsz, H_A, DK_A, DV_A), F32)],
        scratch_shapes=[pltpu.VMEM((tb + 8, width), F32)] * 3 + [pltpu.VMEM((hb, DK_A, DV_A), F32)],
        compiler_params=_cparams(("parallel", "parallel", "arbitrary")),
        name="gdn",
    )(h1, h1, h1, h1, gates3, gates3, conv_w, conv_w, conv_w,
      conv_state, conv_state, conv_state, s0, gn)


def _mla_pre_body(cq_ref, ckv_ref, kra_ref, krb_ref, cos_ref, sin_ref, qg_ref, wuq_ref, wuk_ref,
                  kvg_ref, *refs, n_carried):
    q_ref, kc_ref, ckvo_ref, kro_ref, *maybe_vt_ref = refs[n_carried:]
    cq = cq_ref[...]
    cqn = cq * lax.rsqrt(jnp.mean(cq * cq, -1, keepdims=True) + EPS) * qg_ref[...]
    qf = jnp.dot(cqn.astype(BF16), wuq_ref[...], preferred_element_type=F32)
    cos_k = cos_ref[...]
    sin_k = sin_ref[...]
    reps = H_B * ROPE // LANES
    cos_t = jnp.concatenate([cos_k] * reps, axis=1)
    sin_t = jnp.concatenate([sin_k] * reps, axis=1)
    n_nope = H_B * NOPE
    n_rope = H_B * ROPE
    qr = (qf[:, n_nope:n_nope + n_rope] * cos_t + qf[:, n_nope + n_rope:] * sin_t) * Q_SCALE
    lane = lax.broadcasted_iota(jnp.int32, (cq.shape[0], LANES), 1)
    for h in range(H_B):
        ql = jnp.dot(qf[:, h * NOPE:(h + 1) * NOPE].astype(BF16), wuk_ref[h],
                     preferred_element_type=F32) * Q_SCALE
        blk = qr[:, (h // 2) * LANES:(h // 2 + 1) * LANES]
        keep = (lane < ROPE) if h % 2 == 0 else (lane >= ROPE)
        q_ref[h, :, 0:KV_RANK] = ql.astype(BF16)
        q_ref[h, :, KV_RANK:KC_W] = jnp.where(keep, blk, 0.0).astype(BF16)
    ckv = ckv_ref[...]
    ckvn = ckv * lax.rsqrt(jnp.mean(ckv * ckv, -1, keepdims=True) + EPS) * kvg_ref[...]
    kr2 = kra_ref[...] * cos_k + krb_ref[...] * sin_k
    ckvo_ref[...] = ckvn
    if maybe_vt_ref:
        kro_ref[0] = kr2.T[0:ROPE, :]
    else:
        kro_ref[...] = kr2[:, 0:ROPE]
    kc_ref[:, 0:KV_RANK] = ckvn.astype(BF16)
    kc_ref[:, KV_RANK:KC_W] = kr2.astype(BF16)
    if maybe_vt_ref:
        maybe_vt_ref[0][0] = ckvn.T.astype(BF16)


def _mla_pre(h1, cos_t, sin_t, qg, wuq, wuk, kvg, bsz, seq, tm, with_vt, layer, depth, carried):
    t_rows = bsz * seq
    ntab = cos_t.shape[0] // tm
    npb = max(seq // tm, 1)
    const2 = lambda i: (0, 0)
    out_specs = [pl.BlockSpec((H_B, tm, KC_W), lambda i: (0, i, 0)),
                 pl.BlockSpec((tm, KC_W), lambda i: (i, 0)),
                 pl.BlockSpec((None, tm, KV_RANK), lambda i: (layer, i, 0)),
                 pl.BlockSpec((None, tm, ROPE), lambda i: (layer, i, 0))]
    out_shape = [jax.ShapeDtypeStruct((H_B, t_rows, KC_W), BF16),
                 jax.ShapeDtypeStruct((t_rows, KC_W), BF16),
                 jax.ShapeDtypeStruct((depth, t_rows, KV_RANK), F32),
                 jax.ShapeDtypeStruct((depth, t_rows, ROPE), F32)]
    if with_vt:
        out_specs[3] = pl.BlockSpec((None, 1, ROPE, tm), lambda i: (layer, i // npb, 0, i % npb))
        out_shape[3] = jax.ShapeDtypeStruct((depth, bsz, ROPE, seq), F32)
        out_specs.append(pl.BlockSpec((1, KV_RANK, tm), lambda i: (i // npb, 0, i % npb)))
        out_shape.append(jax.ShapeDtypeStruct((bsz, KV_RANK, seq), BF16))
    n_in = 10
    return pl.pallas_call(
        functools.partial(_mla_pre_body, n_carried=len(carried)),
        input_output_aliases={n_in + k: 2 + k for k in range(len(carried))},
        grid=(t_rows // tm,),
        in_specs=[pl.BlockSpec((tm, Q_RANK), lambda i: (i, COL_CQ // Q_RANK)),
                  pl.BlockSpec((tm, KV_RANK), lambda i: (i, COL_CKV // KV_RANK)),
                  pl.BlockSpec((tm, LANES), lambda i: (i, COL_KRA // LANES)),
                  pl.BlockSpec((tm, LANES), lambda i: (i, COL_KRB // LANES)),
                  pl.BlockSpec((tm, LANES), lambda i: (i % ntab, 0)),
                  pl.BlockSpec((tm, LANES), lambda i: (i % ntab, 0)),
                  pl.BlockSpec((1, Q_RANK), const2),
                  pl.BlockSpec(wuq.shape, const2),
                  pl.BlockSpec(wuk.shape, lambda i: (0, 0, 0)),
                  pl.BlockSpec((1, KV_RANK), const2)] + [pl.BlockSpec(memory_space=pl.ANY)] * len(carried),
        out_specs=out_specs,
        out_shape=out_shape,
        compiler_params=_cparams(("parallel",)),
        name="mla_pre",
    )(h1, h1, h1, h1, cos_t, sin_t, qg, wuq, wuk, kvg, *carried)


def _attn_body(q_ref, k_ref, vt_ref, wuv_ref, o_ref, m_ref, l_ref, acc_ref, ex_ref, *, tq, tk, cw, lk):
    i = pl.program_id(1)
    shift = CHUNK.bit_length() - 1

    def update(j, masked, lazy, opening=False, wide=1):
        tkw = wide * tk
        k0 = pl.multiple_of(j * tk, tk)
        kt = k_ref[0, pl.ds(k0, tkw), :]
        vt = vt_ref[0, :, pl.ds(k0, tkw)]
        bias = None
        if masked:
            kpos = k0 + lax.broadcasted_iota(jnp.int32, (tkw, tq), 0)
            qpos = i * tq + lax.broadcasted_iota(jnp.int32, (tkw, tq), 1)
            bias = jnp.where((kpos >> shift) <= (qpos >> shift), 0.0, -jnp.inf).astype(F32)

        units = [(h, slice(c, c + cw)) for h in range(H_B) for c in range(0, tq, cw)]

        def scores(u):
            h, cs = u
            s = lax.dot_general(kt, q_ref[h, cs, :], (((1,), (1,)), ((), ())),
                                preferred_element_type=F32)
            return s if bias is None else s + bias[:, cs]

        def softmax(u, s):
            h, cs = u
            c_max = jnp.max(s, 0, keepdims=True)
            if opening:
                m_prev = jnp.zeros_like(c_max)
                m_new = c_max
                gap = jnp.abs(c_max)
            else:
                m_prev = m_ref[h, :, cs]
                m_new = jnp.maximum(m_prev, c_max)
                gap = c_max - m_prev
            alpha = jnp.exp2(m_prev - m_new)
            if lazy:
                p = jnp.exp2(s - m_prev)
                ex_ref[h, :, cs] = jnp.maximum(ex_ref[h, :, cs], gap)
                l_ref[h, :, cs] = alpha * (l_ref[h, :, cs] + jnp.sum(p, 0, keepdims=True))
            else:
                p = jnp.exp2(s - m_new)
                l_ref[h, :, cs] = alpha * l_ref[h, :, cs] + jnp.sum(p, 0, keepdims=True)
            m_ref[h, :, cs] = m_new
            return p.astype(BF16), alpha

        def accumulate(u, alpha, pv):
            h, cs = u
            if lazy:
                acc_ref[h, :, cs] = alpha * (acc_ref[h, :, cs] + pv)
            else:
                acc_ref[h, :, cs] = alpha * acc_ref[h, :, cs] + pv

        n_units = len(units)
        s_q = {0: scores(units[0])}
        if n_units > 1:
            s_q[1] = scores(units[1])
        p0, alpha = softmax(units[0], s_q.pop(0))
        alphas = {0: alpha}
        pvs = {0: jnp.dot(vt, p0, preferred_element_type=F32)}
        for n in range(n_units):
            if n + 2 < n_units:
                s_q[n + 2] = scores(units[n + 2])
            if n + 1 < n_units:
                p_next, alphas[n + 1] = softmax(units[n + 1], s_q.pop(n + 1))
            accumulate(units[n], alphas.pop(n), pvs.pop(n))
            if n + 1 < n_units:
                pvs[n + 1] = jnp.dot(vt, p_next, preferred_element_type=F32)

    n_full = ((((i * tq) >> shift) + 1) << shift) // tk
    n_all = jnp.minimum(((((i * tq + tq - 1) >> shift) + 1) << shift) + tk - 1, lk + tk - 1) // tk

    def tile_loop(lo, hi, masked, lazy, opening=False, wide=1, first_tile=0):
        def step(p, carry):
            update(first_tile + p * wide, masked, lazy, opening, wide)
            return carry
        lax.fori_loop(lo, hi, step, 0)

    def attempt(a, redo):
        first = a == 0
        run = jnp.logical_or(first, redo > 0)

        @pl.when(run)
        def _():
            m_ref[...] = jnp.full(m_ref.shape, -jnp.inf, F32)
            l_ref[...] = jnp.zeros(l_ref.shape, F32)
            acc_ref[...] = jnp.zeros(acc_ref.shape, F32)
            ex_ref[...] = jnp.full(ex_ref.shape, -jnp.inf, F32)

        tile_loop(0, jnp.where(first, 0, n_all) * run.astype(jnp.int32), True, False)
        tile_loop(0, jnp.where(first, 1, 0), True, True, True)
        n_wide = jnp.maximum(n_full - 1, 0) // WIDE_ATTN
        tile_loop(0, jnp.where(first, n_wide, 0), False, True, wide=WIDE_ATTN, first_tile=1)
        tile_loop(1 + n_wide * WIDE_ATTN, jnp.where(first, n_full, 0), False, True)
        tile_loop(jnp.maximum(n_full, 1), jnp.where(first, n_all, 0), True, True)
        excess = jnp.max(ex_ref[...])
        return jnp.where(first, (excess > MAX_STALE_EXCESS).astype(jnp.int32), 0)

    lax.fori_loop(0, 2, attempt, jnp.int32(0))

    for h in range(H_B):
        o_t = (acc_ref[h] * (1.0 / l_ref[h])).astype(BF16)
        ob_t = jnp.dot(wuv_ref[h], o_t, preferred_element_type=F32)
        o_ref[0, :, h * V_B:(h + 1) * V_B] = ob_t.T.astype(o_ref.dtype)


def _attn(q, kc, vt, wuv_t, bsz, seq, tq, tk, cw):
    assert seq % tq == 0 and seq % tk == 0
    nq = seq // tq
    body = functools.partial(_attn_body, tq=tq, tk=tk, cw=cw, lk=seq)
    return pl.pallas_call(
        body,
        grid=(bsz, nq),
        in_specs=[pl.BlockSpec((H_B, tq, KC_W), lambda b, i: (0, b * nq + i, 0)),
                  pl.BlockSpec((1, seq, KC_W), lambda b, i: (b, 0, 0)),
                  pl.BlockSpec((1, KV_RANK, seq), lambda b, i: (b, 0, 0)),
                  pl.BlockSpec(wuv_t.shape, lambda b, i: (0, 0, 0))],
        out_specs=pl.BlockSpec((1, tq, H_B * V_B), lambda b, i: (b, i, 0)),
        out_shape=jax.ShapeDtypeStruct((bsz, seq, H_B * V_B), BF16),
        scratch_shapes=[pltpu.VMEM((H_B, 1, tq), F32), pltpu.VMEM((H_B, 1, tq), F32),
                        pltpu.VMEM((H_B, KV_RANK, tq), F32), pltpu.VMEM((H_B, 1, tq), F32)],
        compiler_params=_cparams(("parallel", "arbitrary")),
        name="attn",
    )(q, kc, vt, wuv_t)


def _attn_dec_body(q_ref, ckv_ref, krt_ref, kn_ref, wuv_ref, o_ref, m_ref, l_ref, acc_ref,
                   *, tq, past, n_past):
    j = pl.program_id(1)
    rows = H_B * tq
    nt_dims = (((1,), (1,)), ((), ()))
    q2 = q_ref[...].reshape(rows, KC_W)

    @pl.when(j == 0)
    def _():
        m_ref[...] = jnp.full(m_ref.shape, -jnp.inf, F32)
        l_ref[...] = jnp.zeros(l_ref.shape, F32)
        acc_ref[...] = jnp.zeros(acc_ref.shape, F32)

    def accumulate(s, v_nat):
        m_prev = m_ref[...]
        m_new = jnp.maximum(m_prev, jnp.max(s, -1, keepdims=True))
        p = jnp.exp2(s - m_new)
        alpha = jnp.exp2(m_prev - m_new)
        l_ref[...] = alpha * l_ref[...] + jnp.sum(p, -1, keepdims=True)
        acc_ref[...] = alpha * acc_ref[...] + jnp.dot(p.astype(BF16), v_nat, preferred_element_type=F32)
        m_ref[...] = m_new

    @pl.when(j < n_past)
    def _():
        k_lat = ckv_ref[0].astype(BF16)
        kr_t = krt_ref[0].astype(BF16)
        kr2_t = jnp.concatenate([kr_t, kr_t], axis=0)
        s = (lax.dot_general(q2[:, 0:KV_RANK], k_lat, nt_dims, preferred_element_type=F32)
             + jnp.dot(q2[:, KV_RANK:KC_W], kr2_t, preferred_element_type=F32))
        accumulate(s, k_lat)

    @pl.when(j == n_past)
    def _():
        kn = kn_ref[...]
        s = lax.dot_general(q2, kn, nt_dims, preferred_element_type=F32)
        shift = CHUNK.bit_length() - 1
        qpos = past + (lax.broadcasted_iota(jnp.int32, s.shape, 0) & (tq - 1))
        kpos = past + lax.broadcasted_iota(jnp.int32, s.shape, 1)
        s = jnp.where((kpos >> shift) <= (qpos >> shift), s, -jnp.inf)
        accumulate(s, kn[:, 0:KV_RANK])
        o = acc_ref[...] * (1.0 / l_ref[...])
        for h in range(H_B):
            oh = o[h * tq:(h + 1) * tq, :].astype(BF16)
            o_ref[:, h * V_B:(h + 1) * V_B] = jnp.dot(
                oh, wuv_ref[h], preferred_element_type=F32).astype(o_ref.dtype)


def _attn_dec(q, cache_ckv, cache_kr_t, layer, kc_new, wuv, bsz, seq, tk):
    past = cache_ckv.shape[2]
    assert past % tk == 0 and past % CHUNK == 0 and seq & (seq - 1) == 0
    n_past = past // tk
    body = functools.partial(_attn_dec_body, tq=seq, past=past, n_past=n_past)
    return pl.pallas_call(
        body,
        grid=(bsz, n_past + 1),
        in_specs=[pl.BlockSpec((H_B, seq, KC_W), lambda b, j: (0, b, 0)),
                  pl.BlockSpec((None, 1, tk, KV_RANK), lambda b, j: (layer, b, jnp.minimum(j, n_past - 1), 0)),
                  pl.BlockSpec((None, 1, ROPE, tk), lambda b, j: (layer, b, 0, jnp.minimum(j, n_past - 1))),
                  pl.BlockSpec((seq, KC_W), lambda b, j: (b, 0)),
                  pl.BlockSpec(wuv.shape, lambda b, j: (0, 0, 0))],
        out_specs=pl.BlockSpec((seq, H_B * V_B), lambda b, j: (b, 0)),
        out_shape=jax.ShapeDtypeStruct((bsz * seq, H_B * V_B), BF16),
        scratch_shapes=[pltpu.VMEM((H_B * seq, 1), F32), pltpu.VMEM((H_B * seq, 1), F32),
                        pltpu.VMEM((H_B * seq, KV_RANK), F32)],
        compiler_params=_cparams(("parallel", "arbitrary")),
        name="attn_dec",
    )(q, cache_ckv, cache_kr_t, kc_new, wuv)


def _layer_norm(r, g, b):
    mu = jnp.mean(r, -1, keepdims=True)
    d = r - mu
    var = jnp.mean(d * d, -1, keepdims=True)
    return d * lax.rsqrt(var + EPS) * g + b


def _merge_body(oa_ref, ob_ref, ga_ref, gb_ref, x_ref, woa_ref, wob_ref, wout_ref, g_ref, b_ref, o_ref):
    tm = x_ref.shape[0]
    for r0 in range(0, tm, tm // MERGE_SPLIT):
        rs = slice(r0, r0 + tm // MERGE_SPLIT)
        ya = jnp.dot(oa_ref[rs, :], woa_ref[...], preferred_element_type=F32)
        yb = jnp.dot(ob_ref[rs, :], wob_ref[...], preferred_element_type=F32)
        m = _sigmoid(ga_ref[rs, :]) * ya + _sigmoid(gb_ref[rs, :]) * yb
        r = ALPHA * x_ref[rs, :] + jnp.dot(m.astype(BF16), wout_ref[...], preferred_element_type=F32)
        o_ref[rs, :] = _layer_norm(r, g_ref[...], b_ref[...])


def _merge(oa, ob, h1, x, woa, wob, wout, g, b, tm):
    t = x.shape[0]
    row = lambda i: (i, 0)
    const = lambda i: (0, 0)
    wspec = pl.BlockSpec((D_MODEL, D_MODEL), const, pipeline_mode=pl.Buffered(1))
    return pl.pallas_call(
        _merge_body,
        grid=(t // tm,),
        in_specs=[pl.BlockSpec((tm, V_A), row), pl.BlockSpec((tm, H_B * V_B), row),
                  pl.BlockSpec((tm, D_MODEL), lambda i: (i, COL_GA // D_MODEL)),
                  pl.BlockSpec((tm, D_MODEL), lambda i: (i, COL_GB // D_MODEL)),
                  pl.BlockSpec((tm, D_MODEL), row), wspec, wspec, wspec,
                  pl.BlockSpec((1, D_MODEL), const), pl.BlockSpec((1, D_MODEL), const)],
        out_specs=pl.BlockSpec((tm, D_MODEL), row),
        out_shape=jax.ShapeDtypeStruct((t, D_MODEL), F32),
        compiler_params=_cparams(("parallel",)),
        name="merge",
    )(oa, ob, h1, h1, x, woa, wob, wout, g, b)


def _ffn_body(x_ref, wg_ref, wu_ref, wd_ref, g_ref, b_ref, o_ref, *, chunks):
    tm = x_ref.shape[0]
    for r0 in range(0, tm, tm // FFN_SPLIT):
        rs = slice(r0, r0 + tm // FFN_SPLIT)
        x = x_ref[rs, :]
        xb = x.astype(BF16)
        y = ALPHA * x
        off = 0
        for width in chunks:
            sl = slice(off, off + width)
            f1 = jnp.dot(xb, wg_ref[:, sl], preferred_element_type=F32)
            f3 = jnp.dot(xb, wu_ref[:, sl], preferred_element_type=F32)
            hc = (_silu(f1) * f3).astype(BF16)
            y = y + jnp.dot(hc, wd_ref[sl, :], preferred_element_type=F32)
            off += width
        o_ref[rs, :] = _layer_norm(y, g_ref[...], b_ref[...])


def _ffn(x, wg, wu, wd, g, b, tm, chunks):
    assert sum(chunks) == D_FF
    t = x.shape[0]
    row = lambda i: (i, 0)
    const = lambda i: (0, 0)
    single = pl.Buffered(1)
    return pl.pallas_call(
        functools.partial(_ffn_body, chunks=chunks),
        grid=(t // tm,),
        in_specs=[pl.BlockSpec((tm, D_MODEL), row),
                  pl.BlockSpec((D_MODEL, D_FF), const, pipeline_mode=single),
                  pl.BlockSpec((D_MODEL, D_FF), const, pipeline_mode=single),
                  pl.BlockSpec((D_FF, D_MODEL), const, pipeline_mode=single),
                  pl.BlockSpec((1, D_MODEL), const), pl.BlockSpec((1, D_MODEL), const)],
        out_specs=pl.BlockSpec((tm, D_MODEL), row),
        out_shape=jax.ShapeDtypeStruct((t, D_MODEL), F32),
        compiler_params=_cparams(("parallel",)),
        name="ffn",
    )(x, wg, wu, wd, g, b)


def _prep_layer_weights(w_in_t, conv_w, a_log, dt_bias, gdn_norm_g, w_oa, q_norm_g, w_uq, kv_norm_g,
                        w_ukv, w_ob, w_out, ln1_g, ln1_b, w_gu, w_down, ln2_g, ln2_b):
    seg = lambda i: w_in_t[_OFFS[i]:_OFFS[i + 1], :]
    qkv, z, a, b, c_q, c_kv, k_r, g_a, g_b = (seg(i) for i in range(9))
    half = ROPE // 2
    k_r_rot = jnp.concatenate([-k_r[half:], k_r[:half]], axis=0)
    pad = jnp.zeros((LANES - 2 * H_A, D_MODEL), w_in_t.dtype)
    w_proj = jnp.concatenate([qkv, z, g_a, g_b, c_kv, k_r, k_r, c_q, k_r_rot, k_r_rot, a, b, pad],
                             axis=0).astype(BF16)
    lane_pad = jnp.zeros((LANES - H_A,), F32)
    al_lane = jnp.concatenate([a_log.astype(F32), lane_pad]).reshape(1, LANES)
    dt_lane = jnp.concatenate([dt_bias.astype(F32), lane_pad]).reshape(1, LANES)
    uq = w_uq.reshape(Q_RANK, H_B, NOPE + ROPE)
    uq_nope = uq[:, :, :NOPE].reshape(Q_RANK, H_B * NOPE)
    uq_rope = uq[:, :, NOPE:]
    uq_rot = jnp.concatenate([-uq_rope[:, :, half:], uq_rope[:, :, :half]], axis=2)
    w_uq_ext = jnp.concatenate([uq_nope, uq_rope.reshape(Q_RANK, H_B * ROPE),
                                uq_rot.reshape(Q_RANK, H_B * ROPE)], axis=1).astype(BF16)
    ukv = w_ukv.reshape(KV_RANK, H_B, NOPE + V_B)
    w_uk_t = jnp.transpose(ukv[:, :, :NOPE], (1, 2, 0)).astype(BF16)
    w_uv = jnp.transpose(ukv[:, :, NOPE:], (1, 0, 2)).astype(BF16)
    w_uv_t = jnp.transpose(ukv[:, :, NOPE:], (1, 2, 0)).astype(BF16)
    return dict(
        w_proj=w_proj, conv_w=conv_w.astype(F32), al_lane=al_lane, dt_lane=dt_lane,
        gn=gdn_norm_g.reshape(1, DV_A).astype(F32), w_oa=w_oa.astype(BF16),
        qg=q_norm_g.reshape(1, Q_RANK).astype(F32), w_uq=w_uq_ext, w_uk_t=w_uk_t, w_uv=w_uv, w_uv_t=w_uv_t,
        kvg=kv_norm_g.reshape(1, KV_RANK).astype(F32), w_ob=w_ob.astype(BF16),
        w_out=w_out.astype(BF16), ln1_g=ln1_g.reshape(1, D_MODEL), ln1_b=ln1_b.reshape(1, D_MODEL),
        w_g=w_gu[:, :D_FF].astype(BF16), w_u=w_gu[:, D_FF:].astype(BF16), w_down=w_down.astype(BF16),
        ln2_g=ln2_g.reshape(1, D_MODEL), ln2_b=ln2_b.reshape(1, D_MODEL))


def _rope_tables(past, seq, reps):
    half = ROPE // 2
    inv = ROPE_THETA ** (-jnp.arange(half, dtype=F32) / half)
    ang = (past + jnp.arange(seq)).astype(F32)[:, None] * inv[None, :]
    cos = jnp.tile(jnp.cos(ang), (reps, LANES // half))
    sin = jnp.tile(jnp.sin(ang), (reps, LANES // half))
    return cos, sin


TM_PROJ, TN_PROJ, TM_GATES, TM_MLA, TM_MERGE, TM_FFN = 512, N_PROJ // 4, 4096, 1024, 1024, 1024
FFN_CHUNKS = (768, 768, 768, D_FF - 3 * 768)
MERGE_SPLIT = 4
FFN_SPLIT = 4
TQ_ATTN, TK_ATTN, CW_ATTN, TK_DEC = 512, 512, 512, 2048
WIDE_ATTN = 2
TB_GDN, HB_GDN = 2 * GDN_CHUNK, H_A


def _trunk_layer(x, conv_state, s0, caches, wl, bsz, seq, layer, depth, carried):
    decode = caches is not None
    t_rows = bsz * seq
    h1 = _proj_in(x, wl["w_proj"], min(TM_PROJ, t_rows), TN_PROJ)
    seq_pad = -(-seq // GDN_CHUNK) * GDN_CHUNK
    if seq_pad == seq:
        gates = _gates(h1, COL_AB // LANES, wl["al_lane"], wl["dt_lane"], min(TM_GATES, t_rows), seq, seq)
    else:
        ab = h1[:, COL_AB:COL_AB + LANES].reshape(bsz, seq, LANES)
        ab = jnp.pad(ab, ((0, 0), (0, seq_pad - seq), (0, 0))).reshape(bsz * seq_pad, LANES)
        gates = _gates(ab, 0, wl["al_lane"], wl["dt_lane"], min(TM_GATES, bsz * seq_pad), seq_pad, seq)
    o_a, s_new = _gdn(h1, gates.reshape(2 * H_A, 1, bsz * seq_pad), wl["conv_w"], conv_state, s0, wl["gn"],
                      bsz, seq, min(TB_GDN, seq_pad), HB_GDN)
    if decode:
        cache_ckv, cache_kr_t = caches
        tm_mla = t_rows if t_rows <= TM_MLA else seq
        cos_t, sin_t = _rope_tables(cache_ckv.shape[2], seq, tm_mla // seq)
        q, kc, ckv_all, kr_all = _mla_pre(h1, cos_t, sin_t, wl["qg"], wl["w_uq"], wl["w_uk_t"], wl["kvg"],
                                          bsz, seq, tm_mla, False, layer, depth, carried)
        tk_dec = TK_DEC
        while cache_ckv.shape[2] % tk_dec:
            tk_dec //= 2
        o_b = _attn_dec(q, cache_ckv, cache_kr_t, layer, kc, wl["w_uv"], bsz, seq, tk_dec)
    else:
        cos_t, sin_t = _rope_tables(0, seq, 1)
        q, kc, ckv_all, kr_all, vt = _mla_pre(h1, cos_t, sin_t, wl["qg"], wl["w_uq"], wl["w_uk_t"],
                                              wl["kvg"], bsz, seq, TM_MLA, True, layer, depth, carried)
        o_b = _attn(q, kc.reshape(bsz, seq, KC_W), vt, wl["w_uv_t"], bsz, seq, TQ_ATTN, TK_ATTN, CW_ATTN)
        o_b = o_b.reshape(t_rows, H_B * V_B)
    x1 = _merge(o_a, o_b, h1, x, wl["w_oa"], wl["w_ob"], wl["w_out"], wl["ln1_g"], wl["ln1_b"],
                min(TM_MERGE, t_rows))
    x2 = _ffn(x1, wl["w_g"], wl["w_u"], wl["w_down"], wl["ln2_g"], wl["ln2_b"], min(TM_FFN, t_rows), FFN_CHUNKS)
    conv_new = h1.reshape(bsz, seq, N_PROJ)[:, seq - (CONV_W - 1):, COL_QKV:COL_QKV + C_QKV]
    return x2, conv_new, s_new, (ckv_all, kr_all)


def kernel(x_prompt, x_sample, state_conv, state_gdn, cache_ckv, cache_krope, w_in, conv_w, a_log, dt_bias, gdn_norm_g, w_oa, q_norm_g, w_uq, kv_norm_g, w_ukv, w_ob, w_out, ln1_g, ln1_b, w_gu, w_down, ln2_g, ln2_b):
    bp, lp, _ = x_prompt.shape
    bs, ls, _ = x_sample.shape
    yp = x_prompt.reshape(bp * lp, D_MODEL)
    ys = x_sample.reshape(bs * ls, D_MODEL)
    zero_conv = jnp.zeros((bp, CONV_W - 1, C_QKV), F32)
    zero_s = jnp.zeros((bp, H_A, DK_A, DV_A), F32)
    cache_kr_t = jnp.swapaxes(cache_krope, 2, 3)
    w_in_t = jnp.swapaxes(w_in, 1, 2).astype(BF16)
    depth = w_in.shape[0]
    conv_p, gdn_p, conv_s, gdn_s = [], [], [], []
    kv_p = kv_s = ()
    for l in range(depth):
        wl = _prep_layer_weights(w_in_t[l], conv_w[l], a_log[l], dt_bias[l], gdn_norm_g[l], w_oa[l],
                                 q_norm_g[l], w_uq[l], kv_norm_g[l], w_ukv[l], w_ob[l], w_out[l],
                                 ln1_g[l], ln1_b[l], w_gu[l], w_down[l], ln2_g[l], ln2_b[l])
        yp, c_new, g_new, kv_p = _trunk_layer(yp, zero_conv, zero_s, None, wl, bp, lp, l, depth, kv_p)
        conv_p.append(c_new), gdn_p.append(g_new)
        ys, c_new, g_new, kv_s = _trunk_layer(ys, state_conv[l], state_gdn[l], (cache_ckv, cache_kr_t), wl,
                                              bs, ls, l, depth, kv_s)
        conv_s.append(c_new), gdn_s.append(g_new)
    return (yp.reshape(bp, lp, D_MODEL), ys.reshape(bs, ls, D_MODEL),
            jnp.stack(conv_p), jnp.stack(gdn_p),
            kv_p[0].reshape(depth, bp, lp, KV_RANK), jnp.swapaxes(kv_p[1], 2, 3),
            jnp.stack(conv_s), jnp.stack(gdn_s),
            kv_s[0].reshape(depth, bs, ls, KV_RANK), kv_s[1].reshape(depth, bs, ls, ROPE))
```

```python
import functools

import numpy as np
import jax
import jax.numpy as jnp
from jax import lax
from jax.experimental import pallas as pl
from jax.experimental.pallas import tpu as pltpu

F32 = jnp.float32
BF16 = jnp.bfloat16

D_MODEL = 1024
DEPTH = 2
CHUNK = 64
H_A = 8
DK_A = 128
DV_A = 128
QK_A = H_A * DK_A
V_A = H_A * DV_A
C_QKV = 2 * QK_A + V_A
CONV_W = 4
H_B = 8
NOPE = 128
ROPE = 64
V_B = 128
Q_RANK = 384
KV_RANK = 256
ROPE_THETA = 10000.0
ATTN_SCALE = (NOPE + ROPE) ** -0.5
LOG2_E = float(np.log2(np.e))
Q_SCALE = ATTN_SCALE * LOG2_E
D_FF = -(-8 * D_MODEL // (3 * 256)) * 256
ALPHA = (2 * DEPTH) ** 0.25
EPS = 1e-6
_SIZES = (C_QKV, V_A, H_A, H_A, Q_RANK, KV_RANK, ROPE, D_MODEL, D_MODEL)
_OFFS = tuple(int(v) for v in np.cumsum((0,) + _SIZES))

LANES = 128
VMEM_LIMIT = 56 * 1024 * 1024

COL_QKV = 0
COL_Z = COL_QKV + C_QKV
COL_GA = COL_Z + V_A
COL_GB = COL_GA + D_MODEL
COL_CKV = COL_GB + D_MODEL
COL_KRA = COL_CKV + KV_RANK
COL_CQ = COL_KRA + LANES
COL_KRB = COL_CQ + Q_RANK
COL_AB = COL_KRB + LANES
N_PROJ = COL_AB + LANES
KC_W = KV_RANK + LANES

GDN_CHUNK = 128
MAX_STALE_EXCESS = 64.0


def _cparams(sem):
    return pltpu.CompilerParams(dimension_semantics=sem, vmem_limit_bytes=VMEM_LIMIT)


def _sigmoid(x):
    return jax.nn.sigmoid(x)


def _silu(x):
    return x * jax.nn.sigmoid(x)


def _mm(a, b):
    return jnp.dot(a.astype(BF16), b.astype(BF16), preferred_element_type=F32)


def _mm_nt(a, b):
    return lax.dot_general(a.astype(BF16), b.astype(BF16), (((1,), (1,)), ((), ())),
                           preferred_element_type=F32)


def _mm_tn(a, b):
    return lax.dot_general(a.astype(BF16), b.astype(BF16), (((0,), (0,)), ((), ())),
                           preferred_element_type=F32)


def _proj_body(x_ref, wt_ref, o_ref, *, tn):
    xb = x_ref[...].astype(BF16)
    for c in range(0, wt_ref.shape[0], tn):
        o_ref[:, c:c + tn] = lax.dot_general(xb, wt_ref[c:c + tn, :], (((1,), (1,)), ((), ())),
                                             preferred_element_type=F32)


def _proj_in(x, w_t, tm, tn):
    t, k = x.shape
    n = w_t.shape[0]
    return pl.pallas_call(
        functools.partial(_proj_body, tn=tn),
        grid=(t // tm,),
        in_specs=[pl.BlockSpec((tm, k), lambda i: (i, 0)),
                  pl.BlockSpec((n, k), lambda i: (0, 0), pipeline_mode=pl.Buffered(1))],
        out_specs=pl.BlockSpec((tm, n), lambda i: (i, 0)),
        out_shape=jax.ShapeDtypeStruct((t, n), F32),
        compiler_params=_cparams(("parallel",)),
        name="proj_in",
    )(x, w_t)


def _gates_body(ab_ref, al_ref, dt_ref, o_ref, *, tm, l_pad, l_valid):
    x = ab_ref[...]
    lane = lax.broadcasted_iota(jnp.int32, x.shape, 1)
    xa = x + dt_ref[...]
    sp = jnp.maximum(xa, 0.0) + jnp.log1p(jnp.exp(-jnp.abs(xa)))
    g = -jnp.exp(al_ref[...]) * sp * LOG2_E
    y = jnp.where(lane < H_A, g, _sigmoid(x))
    yt = y.T[0:2 * H_A, :]
    if l_valid < l_pad:
        col = lax.broadcasted_iota(jnp.int32, yt.shape, 1) + pl.program_id(0) * tm
        yt = jnp.where(col % l_pad < l_valid, yt, 0.0)
    r = lax.broadcasted_iota(jnp.int32, (GDN_CHUNK, GDN_CHUNK), 0)
    c = lax.broadcasted_iota(jnp.int32, (GDN_CHUNK, GDN_CHUNK), 1)
    tri = jnp.where(r <= c, 1.0, 0.0).astype(F32)
    for s in range(tm // GDN_CHUNK):
        sl = slice(s * GDN_CHUNK, (s + 1) * GDN_CHUNK)
        o_ref[0:H_A, sl] = jnp.dot(yt[0:H_A, sl], tri, precision=lax.Precision.HIGHEST,
                                   preferred_element_type=F32)
    o_ref[H_A:2 * H_A, :] = yt[H_A:2 * H_A, :]


def _gates(src, col_block, al_lane, dt_lane, tm, l_pad, l_valid):
    t = src.shape[0]
    body = functools.partial(_gates_body, tm=tm, l_pad=l_pad, l_valid=l_valid)
    return pl.pallas_call(
        body,
        grid=(t // tm,),
        in_specs=[pl.BlockSpec((tm, LANES), lambda i: (i, col_block)),
                  pl.BlockSpec((1, LANES), lambda i: (0, 0)),
                  pl.BlockSpec((1, LANES), lambda i: (0, 0))],
        out_specs=pl.BlockSpec((2 * H_A, tm), lambda i: (0, i)),
        out_shape=jax.ShapeDtypeStruct((2 * H_A, t), F32),
        compiler_params=_cparams(("parallel",)),
        name="gates",
    )(src, al_lane, dt_lane)


def _gdn_body(q_ref, k_ref, v_ref, z_ref, gc_ref, bt_ref, cwq_ref, cwk_ref, cwv_ref,
              csq_ref, csk_ref, csv_ref, s0_ref, gn_ref, o_ref, so_ref,
              xq_ref, xk_ref, xv_ref, s_ref, *, tb, tr, nt, hb):
    t = pl.program_id(2)
    hist = CONV_W - 1
    base = 8

    @pl.when(t == 0)
    def _():
        s_ref[...] = s0_ref[0]
        xq_ref[base - hist:base, :] = csq_ref[0]
        xk_ref[base - hist:base, :] = csk_ref[0]
        xv_ref[base - hist:base, :] = csv_ref[0]

    for xs_ref, raw_ref in ((xq_ref, q_ref), (xk_ref, k_ref), (xv_ref, v_ref)):
        xs_ref[base:base + tr, :] = raw_ref[...]
        if tr < tb:
            xs_ref[base + tr:base + tb, :] = jnp.zeros((tb - tr, xs_ref.shape[1]), F32)

    def conv(xs_ref, cw_ref, ln):
        slab = xs_ref[0:base + tr, ln]
        y = pltpu.roll(slab, hist, 0)[base:base + tr] * cw_ref[0:1, ln]
        for j in range(1, CONV_W):
            tap = slab if j == hist else pltpu.roll(slab, hist - j, 0)
            y = y + tap[base:base + tr] * cw_ref[j:j + 1, ln]
        y = _silu(y)
        if tr < tb:
            y = jnp.concatenate([y, jnp.zeros((tb - tr, LANES), F32)], axis=0)
        return y

    cc = GDN_CHUNK
    ii = lax.broadcasted_iota(jnp.int32, (cc, cc), 0)
    jj = lax.broadcasted_iota(jnp.int32, (cc, cc), 1)
    strict = ii > jj
    eye = jnp.where(ii == jj, 1.0, 0.0).astype(F32)
    heads = range(hb)
    lanes = [slice(hd * LANES, (hd + 1) * LANES) for hd in heads]
    units = [(hd, c) for hd in heads for c in range(tb // cc)]
    rows = {u: slice(u[1] * cc, (u[1] + 1) * cc) for u in units}
    q = [conv(xq_ref, cwq_ref, ln) for ln in lanes]
    k = [conv(xk_ref, cwk_ref, ln) for ln in lanes]
    v = [conv(xv_ref, cwv_ref, ln) for ln in lanes]
    q = [x * (lax.rsqrt(jnp.sum(x * x, -1, keepdims=True) + EPS) * (DK_A ** -0.5)) for x in q]
    k = [x * lax.rsqrt(jnp.sum(x * x, -1, keepdims=True) + EPS) for x in k]
    qc = {u: q[u[0]][rows[u]] for u in units}
    kc = {u: k[u[0]][rows[u]] for u in units}
    vc = {u: v[u[0]][rows[u]] for u in units}
    m_row = {u: jnp.broadcast_to(gc_ref[u[0], :, rows[u]], (cc, cc)) for u in units}
    m_col = {u: m_row[u].T for u in units}
    b_col = {u: jnp.broadcast_to(bt_ref[u[0], :, rows[u]], (cc, cc)).T for u in units}
    kq = {u: _mm_nt(jnp.concatenate([kc[u], qc[u]], axis=0), kc[u]) for u in units}
    diff = {u: m_col[u] - m_row[u] for u in units}
    dec = {u: jnp.exp2(jnp.where(strict, diff[u], -jnp.inf)) for u in units}
    a_mat = {u: b_col[u] * kq[u][0:cc] * dec[u] for u in units}
    qk = {u: kq[u][cc:2 * cc] * (dec[u] + eye) for u in units}
    e_g = {u: jnp.exp2(m_col[u]) for u in units}
    x_inv = {u: eye - jnp.where((ii >> 1) == (jj >> 1), a_mat[u], 0.0) for u in units}
    sft = 1
    while (1 << sft) < min(cc, tr):
        off = ((ii >> (sft + 1)) == (jj >> (sft + 1))) & ((ii >> sft) != (jj >> sft))
        lx = {u: _mm(jnp.where(off, a_mat[u], 0.0), x_inv[u]) for u in units}
        x_inv = {u: x_inv[u] - _mm(x_inv[u], lx[u]) for u in units}
        sft += 1
    uw = {u: _mm(x_inv[u], jnp.concatenate([vc[u] * b_col[u], kc[u] * (b_col[u] * e_g[u])], axis=1))
          for u in units}
    wq = {u: jnp.concatenate([uw[u][:, DV_A:], qc[u] * e_g[u]], axis=0) for u in units}
    g_last = {u: m_col[u][cc - 1:cc, :] for u in units}
    kd = {u: kc[u] * jnp.exp2(g_last[u] - m_col[u]) for u in units}
    s_state = [s_ref[hd] for hd in heads]
    for c in range(tb // cc):
        ws = [_mm(wq[(hd, c)], s_state[hd]) for hd in heads]
        v_new = [uw[(hd, c)][:, 0:DV_A] - ws[hd][0:cc] for hd in heads]
        s_state = [s_state[hd] * jnp.exp2(g_last[(hd, c)]) + _mm_tn(kd[(hd, c)], v_new[hd]) for hd in heads]
        o = [ws[hd][cc:2 * cc] + _mm(qk[(hd, c)], v_new[hd]) for hd in heads]
        o = [x * lax.rsqrt(jnp.mean(x * x, -1, keepdims=True) + EPS) * gn_ref[...] for x in o]
        nr = min(cc, tr - c * cc)
        r = slice(c * cc, c * cc + nr)
        for hd in heads:
            o_ref[r, lanes[hd]] = (o[hd][0:nr] * _silu(z_ref[r, lanes[hd]])).astype(o_ref.dtype)
    for hd in heads:
        s_ref[hd] = s_state[hd]

    xq_ref[base - hist:base, :] = xq_ref[base + tb - hist:base + tb, :]
    xk_ref[base - hist:base, :] = xk_ref[base + tb - hist:base + tb, :]
    xv_ref[base - hist:base, :] = xv_ref[base + tb - hist:base + tb, :]

    @pl.when(t == nt - 1)
    def _():
        so_ref[0] = s_ref[...]


def _gdn(h1, gates3, conv_w, conv_state, s0, s0_layer, gn, bsz, seq, tb, hb):
    nt = -(-seq // tb)
    tr = min(tb, seq)
    assert seq % tr == 0 and (tr == tb or (nt == 1 and tb == GDN_CHUNK))
    t_rows = bsz * seq
    body = functools.partial(_gdn_body, tb=tb, tr=tr, nt=nt, hb=hb)
    width = hb * LANES
    ng = H_A // hb

    def rows(seg):
        return pl.BlockSpec((tr, width), lambda b, g, t: (b * nt + t, seg * ng + g))

    def cw(seg):
        return pl.BlockSpec((CONV_W, width), lambda b, g, t: (0, seg * ng + g))

    def cs(seg):
        return pl.BlockSpec((1, CONV_W - 1, width), lambda b, g, t: (b, 0, seg * ng + g))

    return pl.pallas_call(
        body,
        grid=(bsz, ng, nt),
        in_specs=[rows(0), rows(1), rows(2), rows(COL_Z // QK_A),
                  pl.BlockSpec((hb, 1, tb), lambda b, g, t: (g, 0, b * nt + t)),
                  pl.BlockSpec((hb, 1, tb), lambda b, g, t: (ng + g, 0, b * nt + t)),
                  cw(0), cw(1), cw(2), cs(0), cs(1), cs(2),
                  pl.BlockSpec((None, 1, hb, DK_A, DV_A), lambda b, g, t: (s0_layer, b, g, 0, 0)),
                  pl.BlockSpec((1, DV_A), lambda b, g, t: (0, 0))],
        out_specs=[pl.BlockSpec((tr, width), lambda b, g, t: (b * nt + t, g)),
                   pl.BlockSpec((1, hb, DK_A, DV_A), lambda b, g, t: (b, g, 0, 0))],
        out_shape=[jax.ShapeDtypeStruct((t_rows, V_A), BF16),
                   jax.ShapeDtypeStruct((bsz, H_A, DK_A, DV_A), F32)],
        scratch_shapes=[pltpu.VMEM((tb + 8, width), F32)] * 3 + [pltpu.VMEM((hb, DK_A, DV_A), F32)],
        compiler_params=_cparams(("parallel", "parallel", "arbitrary")),
        name="gdn",
    )(h1, h1, h1, h1, gates3, gates3, conv_w, conv_w, conv_w,
      conv_state, conv_state, conv_state, s0, gn)


def _mla_pre_body(cq_ref, ckv_ref, kra_ref, krb_ref, cos_ref, sin_ref, qg_ref, wuq_ref, wuk_ref,
                  kvg_ref, *refs, n_carried):
    q_ref, kc_ref, ckvo_ref, kro_ref, *maybe_vt_ref = refs[n_carried:]
    cq = cq_ref[...]
    cqn = cq * lax.rsqrt(jnp.mean(cq * cq, -1, keepdims=True) + EPS) * qg_ref[...]
    qf = jnp.dot(cqn.astype(BF16), wuq_ref[...], preferred_element_type=F32)
    cos_k = cos_ref[...]
    sin_k = sin_ref[...]
    reps = H_B * ROPE // LANES
    cos_t = jnp.concatenate([cos_k] * reps, axis=1)
    sin_t = jnp.concatenate([sin_k] * reps, axis=1)
    n_nope = H_B * NOPE
    n_rope = H_B * ROPE
    qr = (qf[:, n_nope:n_nope + n_rope] * cos_t + qf[:, n_nope + n_rope:] * sin_t) * Q_SCALE
    lane = lax.broadcasted_iota(jnp.int32, (cq.shape[0], LANES), 1)
    for h in range(H_B):
        ql = jnp.dot(qf[:, h * NOPE:(h + 1) * NOPE].astype(BF16), wuk_ref[h],
                     preferred_element_type=F32) * Q_SCALE
        blk = qr[:, (h // 2) * LANES:(h // 2 + 1) * LANES]
        keep = (lane < ROPE) if h % 2 == 0 else (lane >= ROPE)
        q_ref[h, :, 0:KV_RANK] = ql.astype(BF16)
        q_ref[h, :, KV_RANK:KC_W] = jnp.where(keep, blk, 0.0).astype(BF16)
    ckv = ckv_ref[...]
    ckvn = ckv * lax.rsqrt(jnp.mean(ckv * ckv, -1, keepdims=True) + EPS) * kvg_ref[...]
    kr2 = kra_ref[...] * cos_k + krb_ref[...] * sin_k
    ckvo_ref[...] = ckvn
    if maybe_vt_ref:
        kro_ref[0] = kr2.T[0:ROPE, :]
    else:
        kro_ref[...] = kr2[:, 0:ROPE]
    kc_ref[:, 0:KV_RANK] = ckvn.astype(BF16)
    kc_ref[:, KV_RANK:KC_W] = kr2.astype(BF16)
    if maybe_vt_ref:
        maybe_vt_ref[0][0] = ckvn.T.astype(BF16)


def _mla_pre(h1, cos_t, sin_t, qg, wuq, wuk, kvg, bsz, seq, tm, with_vt, layer, depth, carried):
    t_rows = bsz * seq
    ntab = cos_t.shape[0] // tm
    npb = max(seq // tm, 1)
    const2 = lambda i: (0, 0)
    out_specs = [pl.BlockSpec((H_B, tm, KC_W), lambda i: (0, i, 0)),
                 pl.BlockSpec((tm, KC_W), lambda i: (i, 0)),
                 pl.BlockSpec((None, tm, KV_RANK), lambda i: (layer, i, 0)),
                 pl.BlockSpec((None, tm, ROPE), lambda i: (layer, i, 0))]
    out_shape = [jax.ShapeDtypeStruct((H_B, t_rows, KC_W), BF16),
                 jax.ShapeDtypeStruct((t_rows, KC_W), BF16),
                 jax.ShapeDtypeStruct((depth, t_rows, KV_RANK), F32),
                 jax.ShapeDtypeStruct((depth, t_rows, ROPE), F32)]
    if with_vt:
        out_specs[3] = pl.BlockSpec((None, 1, ROPE, tm), lambda i: (layer, i // npb, 0, i % npb))
        out_shape[3] = jax.ShapeDtypeStruct((depth, bsz, ROPE, seq), F32)
        out_specs.append(pl.BlockSpec((1, KV_RANK, tm), lambda i: (i // npb, 0, i % npb)))
        out_shape.append(jax.ShapeDtypeStruct((bsz, KV_RANK, seq), BF16))
    n_in = 10
    return pl.pallas_call(
        functools.partial(_mla_pre_body, n_carried=len(carried)),
        input_output_aliases={n_in + k: 2 + k for k in range(len(carried))},
        grid=(t_rows // tm,),
        in_specs=[pl.BlockSpec((tm, Q_RANK), lambda i: (i, COL_CQ // Q_RANK)),
                  pl.BlockSpec((tm, KV_RANK), lambda i: (i, COL_CKV // KV_RANK)),
                  pl.BlockSpec((tm, LANES), lambda i: (i, COL_KRA // LANES)),
                  pl.BlockSpec((tm, LANES), lambda i: (i, COL_KRB // LANES)),
                  pl.BlockSpec((tm, LANES), lambda i: (i % ntab, 0)),
                  pl.BlockSpec((tm, LANES), lambda i: (i % ntab, 0)),
                  pl.BlockSpec((1, Q_RANK), const2),
                  pl.BlockSpec(wuq.shape, const2),
                  pl.BlockSpec(wuk.shape, lambda i: (0, 0, 0)),
                  pl.BlockSpec((1, KV_RANK), const2)] + [pl.BlockSpec(memory_space=pl.ANY)] * len(carried),
        out_specs=out_specs,
        out_shape=out_shape,
        compiler_params=_cparams(("parallel",)),
        name="mla_pre",
    )(h1, h1, h1, h1, cos_t, sin_t, qg, wuq, wuk, kvg, *carried)


def _attn_body(q_ref, k_ref, vt_ref, wuv_ref, o_ref, m_ref, l_ref, acc_ref, ex_ref, *, tq, tk, cw, lk):
    i = pl.program_id(1)
    shift = CHUNK.bit_length() - 1

    def update(j, masked, lazy, opening=False, wide=1):
        tkw = wide * tk
        k0 = pl.multiple_of(j * tk, tk)
        kt = k_ref[0, pl.ds(k0, tkw), :]
        vt = vt_ref[0, :, pl.ds(k0, tkw)]
        bias = None
        if masked:
            kpos = k0 + lax.broadcasted_iota(jnp.int32, (tkw, tq), 0)
            qpos = i * tq + lax.broadcasted_iota(jnp.int32, (tkw, tq), 1)
            bias = jnp.where((kpos >> shift) <= (qpos >> shift), 0.0, -jnp.inf).astype(F32)

        units = [(h, slice(c, c + cw)) for h in range(H_B) for c in range(0, tq, cw)]

        def scores(u):
            h, cs = u
            s = lax.dot_general(kt, q_ref[h, cs, :], (((1,), (1,)), ((), ())),
                                preferred_element_type=F32)
            return s if bias is None else s + bias[:, cs]

        def softmax(u, s):
            h, cs = u
            c_max = jnp.max(s, 0, keepdims=True)
            if opening:
                m_prev = jnp.zeros_like(c_max)
                m_new = c_max
                gap = jnp.abs(c_max)
            else:
                m_prev = m_ref[h, :, cs]
                m_new = jnp.maximum(m_prev, c_max)
                gap = c_max - m_prev
            alpha = jnp.exp2(m_prev - m_new)
            if lazy:
                p = jnp.exp2(s - m_prev)
                ex_ref[h, :, cs] = jnp.maximum(ex_ref[h, :, cs], gap)
                l_ref[h, :, cs] = alpha * (l_ref[h, :, cs] + jnp.sum(p, 0, keepdims=True))
            else:
                p = jnp.exp2(s - m_new)
                l_ref[h, :, cs] = alpha * l_ref[h, :, cs] + jnp.sum(p, 0, keepdims=True)
            m_ref[h, :, cs] = m_new
            return p.astype(BF16), alpha

        def accumulate(u, alpha, pv):
            h, cs = u
            if lazy:
                acc_ref[h, :, cs] = alpha * (acc_ref[h, :, cs] + pv)
            else:
                acc_ref[h, :, cs] = alpha * acc_ref[h, :, cs] + pv

        n_units = len(units)
        s_q = {0: scores(units[0])}
        if n_units > 1:
            s_q[1] = scores(units[1])
        p0, alpha = softmax(units[0], s_q.pop(0))
        alphas = {0: alpha}
        pvs = {0: jnp.dot(vt, p0, preferred_element_type=F32)}
        for n in range(n_units):
            if n + 2 < n_units:
                s_q[n + 2] = scores(units[n + 2])
            if n + 1 < n_units:
                p_next, alphas[n + 1] = softmax(units[n + 1], s_q.pop(n + 1))
            accumulate(units[n], alphas.pop(n), pvs.pop(n))
            if n + 1 < n_units:
                pvs[n + 1] = jnp.dot(vt, p_next, preferred_element_type=F32)

    n_full = ((((i * tq) >> shift) + 1) << shift) // tk
    n_all = jnp.minimum(((((i * tq + tq - 1) >> shift) + 1) << shift) + tk - 1, lk + tk - 1) // tk

    def tile_loop(lo, hi, masked, lazy, opening=False, wide=1, first_tile=0):
        def step(p, carry):
            update(first_tile + p * wide, masked, lazy, opening, wide)
            return carry
        lax.fori_loop(lo, hi, step, 0)

    def attempt(a, redo):
        first = a == 0
        run = jnp.logical_or(first, redo > 0)

        @pl.when(run)
        def _():
            m_ref[...] = jnp.full(m_ref.shape, -jnp.inf, F32)
            l_ref[...] = jnp.zeros(l_ref.shape, F32)
            acc_ref[...] = jnp.zeros(acc_ref.shape, F32)
            ex_ref[...] = jnp.full(ex_ref.shape, -jnp.inf, F32)

        tile_loop(0, jnp.where(first, 0, n_all) * run.astype(jnp.int32), True, False)
        tile_loop(0, jnp.where(first, 1, 0), True, True, True)
        n_wide = jnp.maximum(n_full - 1, 0) // WIDE_ATTN
        tile_loop(0, jnp.where(first, n_wide, 0), False, True, wide=WIDE_ATTN, first_tile=1)
        tile_loop(1 + n_wide * WIDE_ATTN, jnp.where(first, n_full, 0), False, True)
        tile_loop(jnp.maximum(n_full, 1), jnp.where(first, n_all, 0), True, True)
        excess = jnp.max(ex_ref[...])
        return jnp.where(first, (excess > MAX_STALE_EXCESS).astype(jnp.int32), 0)

    lax.fori_loop(0, 2, attempt, jnp.int32(0))

    for h in range(H_B):
        o_t = (acc_ref[h] * (1.0 / l_ref[h])).astype(BF16)
        ob_t = jnp.dot(wuv_ref[h], o_t, preferred_element_type=F32)
        o_ref[0, :, h * V_B:(h + 1) * V_B] = ob_t.T.astype(o_ref.dtype)


def _attn(q, kc, vt, wuv_t, bsz, seq, tq, tk, cw):
    assert seq % tq == 0 and seq % tk == 0
    nq = seq // tq
    body = functools.partial(_attn_body, tq=tq, tk=tk, cw=cw, lk=seq)
    return pl.pallas_call(
        body,
        grid=(bsz, nq),
        in_specs=[pl.BlockSpec((H_B, tq, KC_W), lambda b, i: (0, b * nq + i, 0)),
                  pl.BlockSpec((1, seq, KC_W), lambda b, i: (b, 0, 0)),
                  pl.BlockSpec((1, KV_RANK, seq), lambda b, i: (b, 0, 0)),
                  pl.BlockSpec(wuv_t.shape, lambda b, i: (0, 0, 0))],
        out_specs=pl.BlockSpec((1, tq, H_B * V_B), lambda b, i: (b, i, 0)),
        out_shape=jax.ShapeDtypeStruct((bsz, seq, H_B * V_B), BF16),
        scratch_shapes=[pltpu.VMEM((H_B, 1, tq), F32), pltpu.VMEM((H_B, 1, tq), F32),
                        pltpu.VMEM((H_B, KV_RANK, tq), F32), pltpu.VMEM((H_B, 1, tq), F32)],
        compiler_params=_cparams(("parallel", "arbitrary")),
        name="attn",
    )(q, kc, vt, wuv_t)


def _attn_dec_body(q_ref, ckv_ref, krt_ref, kn_ref, wuv_ref, o_ref, m_ref, l_ref, acc_ref,
                   *, tq, past, n_past):
    j = pl.program_id(1)
    rows = H_B * tq
    nt_dims = (((1,), (1,)), ((), ()))
    q2 = q_ref[...].reshape(rows, KC_W)

    @pl.when(j == 0)
    def _():
        m_ref[...] = jnp.full(m_ref.shape, -jnp.inf, F32)
        l_ref[...] = jnp.zeros(l_ref.shape, F32)
        acc_ref[...] = jnp.zeros(acc_ref.shape, F32)

    def accumulate(s, v_nat):
        m_prev = m_ref[...]
        m_new = jnp.maximum(m_prev, jnp.max(s, -1, keepdims=True))
        p = jnp.exp2(s - m_new)
        alpha = jnp.exp2(m_prev - m_new)
        l_ref[...] = alpha * l_ref[...] + jnp.sum(p, -1, keepdims=True)
        acc_ref[...] = alpha * acc_ref[...] + jnp.dot(p.astype(BF16), v_nat, preferred_element_type=F32)
        m_ref[...] = m_new

    @pl.when(j < n_past)
    def _():
        tk = ckv_ref.shape[1]
        for c in range(0, tk, min(tk, DEC_SUB)):
            cs = slice(c, c + min(tk, DEC_SUB))
            k_lat = ckv_ref[0, cs, :].astype(BF16)
            kr_t = krt_ref[0, :, cs].astype(BF16)
            kr2_t = jnp.concatenate([kr_t, kr_t], axis=0)
            s = (lax.dot_general(q2[:, 0:KV_RANK], k_lat, nt_dims, preferred_element_type=F32)
                 + jnp.dot(q2[:, KV_RANK:KC_W], kr2_t, preferred_element_type=F32))
            accumulate(s, k_lat)

    @pl.when(j == n_past)
    def _():
        kn = kn_ref[...]
        s = lax.dot_general(q2, kn, nt_dims, preferred_element_type=F32)
        shift = CHUNK.bit_length() - 1
        qpos = past + (lax.broadcasted_iota(jnp.int32, s.shape, 0) & (tq - 1))
        kpos = past + lax.broadcasted_iota(jnp.int32, s.shape, 1)
        s = jnp.where((kpos >> shift) <= (qpos >> shift), s, -jnp.inf)
        accumulate(s, kn[:, 0:KV_RANK])
        o = acc_ref[...] * (1.0 / l_ref[...])
        for h in range(H_B):
            oh = o[h * tq:(h + 1) * tq, :].astype(BF16)
            o_ref[:, h * V_B:(h + 1) * V_B] = jnp.dot(
                oh, wuv_ref[h], preferred_element_type=F32).astype(o_ref.dtype)


def _attn_dec(q, cache_ckv, cache_kr_t, layer, kc_new, wuv, bsz, seq, tk):
    past = cache_ckv.shape[2]
    assert past % tk == 0 and past % CHUNK == 0 and seq & (seq - 1) == 0
    n_past = past // tk
    body = functools.partial(_attn_dec_body, tq=seq, past=past, n_past=n_past)
    return pl.pallas_call(
        body,
        grid=(bsz, n_past + 1),
        in_specs=[pl.BlockSpec((H_B, seq, KC_W), lambda b, j: (0, b, 0)),
                  pl.BlockSpec((None, 1, tk, KV_RANK), lambda b, j: (layer, b, jnp.minimum(j, n_past - 1), 0)),
                  pl.BlockSpec((None, 1, ROPE, tk), lambda b, j: (layer, b, 0, jnp.minimum(j, n_past - 1))),
                  pl.BlockSpec((seq, KC_W), lambda b, j: (b, 0)),
                  pl.BlockSpec(wuv.shape, lambda b, j: (0, 0, 0))],
        out_specs=pl.BlockSpec((seq, H_B * V_B), lambda b, j: (b, 0)),
        out_shape=jax.ShapeDtypeStruct((bsz * seq, H_B * V_B), BF16),
        scratch_shapes=[pltpu.VMEM((H_B * seq, 1), F32), pltpu.VMEM((H_B * seq, 1), F32),
                        pltpu.VMEM((H_B * seq, KV_RANK), F32)],
        compiler_params=_cparams(("parallel", "arbitrary")),
        name="attn_dec",
    )(q, cache_ckv, cache_kr_t, kc_new, wuv)


def _layer_norm(r, g, b):
    mu = jnp.mean(r, -1, keepdims=True)
    d = r - mu
    var = jnp.mean(d * d, -1, keepdims=True)
    return d * lax.rsqrt(var + EPS) * g + b


def _merge_body(oa_ref, ob_ref, ga_ref, gb_ref, x_ref, woa_ref, wob_ref, wout_ref, g_ref, b_ref, o_ref):
    tm = x_ref.shape[0]
    part = max(tm // ROW_PARTS, min(tm, MIN_PART_ROWS))
    for r0 in range(0, tm, part):
        rs = slice(r0, r0 + part)
        ya = jnp.dot(oa_ref[rs, :], woa_ref[...], preferred_element_type=F32)
        yb = jnp.dot(ob_ref[rs, :], wob_ref[...], preferred_element_type=F32)
        m = _sigmoid(ga_ref[rs, :]) * ya + _sigmoid(gb_ref[rs, :]) * yb
        r = ALPHA * x_ref[rs, :] + jnp.dot(m.astype(BF16), wout_ref[...], preferred_element_type=F32)
        o_ref[rs, :] = _layer_norm(r, g_ref[...], b_ref[...])


def _merge(oa, ob, h1, x, woa, wob, wout, g, b, tm):
    t = x.shape[0]
    row = lambda i: (i, 0)
    const = lambda i: (0, 0)
    wspec = pl.BlockSpec((D_MODEL, D_MODEL), const, pipeline_mode=pl.Buffered(1))
    return pl.pallas_call(
        _merge_body,
        grid=(t // tm,),
        in_specs=[pl.BlockSpec((tm, V_A), row), pl.BlockSpec((tm, H_B * V_B), row),
                  pl.BlockSpec((tm, D_MODEL), lambda i: (i, COL_GA // D_MODEL)),
                  pl.BlockSpec((tm, D_MODEL), lambda i: (i, COL_GB // D_MODEL)),
                  pl.BlockSpec((tm, D_MODEL), row), wspec, wspec, wspec,
                  pl.BlockSpec((1, D_MODEL), const), pl.BlockSpec((1, D_MODEL), const)],
        out_specs=pl.BlockSpec((tm, D_MODEL), row),
        out_shape=jax.ShapeDtypeStruct((t, D_MODEL), F32),
        compiler_params=_cparams(("parallel",)),
        name="merge",
    )(oa, ob, h1, h1, x, woa, wob, wout, g, b)


def _ffn_body(x_ref, wg_ref, wu_ref, wd_ref, g_ref, b_ref, o_ref, *, chunks):
    tm = x_ref.shape[0]
    part = max(tm // ROW_PARTS, min(tm, MIN_PART_ROWS))
    for r0 in range(0, tm, part):
        rs = slice(r0, r0 + part)
        x = x_ref[rs, :]
        xb = x.astype(BF16)
        y = ALPHA * x
        off = 0
        for width in chunks:
            sl = slice(off, off + width)
            f1 = jnp.dot(xb, wg_ref[:, sl], preferred_element_type=F32)
            f3 = jnp.dot(xb, wu_ref[:, sl], preferred_element_type=F32)
            hc = (_silu(f1) * f3).astype(BF16)
            y = y + jnp.dot(hc, wd_ref[sl, :], preferred_element_type=F32)
            off += width
        o_ref[rs, :] = _layer_norm(y, g_ref[...], b_ref[...])


def _ffn(x, wg, wu, wd, g, b, tm, chunks):
    assert sum(chunks) == D_FF
    t = x.shape[0]
    row = lambda i: (i, 0)
    const = lambda i: (0, 0)
    single = pl.Buffered(1)
    return pl.pallas_call(
        functools.partial(_ffn_body, chunks=chunks),
        grid=(t // tm,),
        in_specs=[pl.BlockSpec((tm, D_MODEL), row),
                  pl.BlockSpec((D_MODEL, D_FF), const, pipeline_mode=single),
                  pl.BlockSpec((D_MODEL, D_FF), const, pipeline_mode=single),
                  pl.BlockSpec((D_FF, D_MODEL), const, pipeline_mode=single),
                  pl.BlockSpec((1, D_MODEL), const), pl.BlockSpec((1, D_MODEL), const)],
        out_specs=pl.BlockSpec((tm, D_MODEL), row),
        out_shape=jax.ShapeDtypeStruct((t, D_MODEL), F32),
        compiler_params=_cparams(("parallel",)),
        name="ffn",
    )(x, wg, wu, wd, g, b)


def _prep_layer_weights(w_in_t, conv_w, a_log, dt_bias, gdn_norm_g, w_oa, q_norm_g, w_uq, kv_norm_g,
                        w_ukv, w_ob, w_out, ln1_g, ln1_b, w_gu, w_down, ln2_g, ln2_b):
    seg = lambda i: w_in_t[_OFFS[i]:_OFFS[i + 1], :]
    qkv, z, a, b, c_q, c_kv, k_r, g_a, g_b = (seg(i) for i in range(9))
    half = ROPE // 2
    k_r_rot = jnp.concatenate([-k_r[half:], k_r[:half]], axis=0)
    pad = jnp.zeros((LANES - 2 * H_A, D_MODEL), w_in_t.dtype)
    w_proj = jnp.concatenate([qkv, z, g_a, g_b, c_kv, k_r, k_r, c_q, k_r_rot, k_r_rot, a, b, pad],
                             axis=0).astype(BF16)
    lane_pad = jnp.zeros((LANES - H_A,), F32)
    al_lane = jnp.concatenate([a_log.astype(F32), lane_pad]).reshape(1, LANES)
    dt_lane = jnp.concatenate([dt_bias.astype(F32), lane_pad]).reshape(1, LANES)
    uq = w_uq.reshape(Q_RANK, H_B, NOPE + ROPE)
    uq_nope = uq[:, :, :NOPE].reshape(Q_RANK, H_B * NOPE)
    uq_rope = uq[:, :, NOPE:]
    uq_rot = jnp.concatenate([-uq_rope[:, :, half:], uq_rope[:, :, :half]], axis=2)
    w_uq_ext = jnp.concatenate([uq_nope, uq_rope.reshape(Q_RANK, H_B * ROPE),
                                uq_rot.reshape(Q_RANK, H_B * ROPE)], axis=1).astype(BF16)
    ukv = w_ukv.reshape(KV_RANK, H_B, NOPE + V_B)
    w_uk_t = jnp.transpose(ukv[:, :, :NOPE], (1, 2, 0)).astype(BF16)
    w_uv = jnp.transpose(ukv[:, :, NOPE:], (1, 0, 2)).astype(BF16)
    w_uv_t = jnp.transpose(ukv[:, :, NOPE:], (1, 2, 0)).astype(BF16)
    return dict(
        w_proj=w_proj, conv_w=conv_w.astype(F32), al_lane=al_lane, dt_lane=dt_lane,
        gn=gdn_norm_g.reshape(1, DV_A).astype(F32), w_oa=w_oa.astype(BF16),
        qg=q_norm_g.reshape(1, Q_RANK).astype(F32), w_uq=w_uq_ext, w_uk_t=w_uk_t, w_uv=w_uv, w_uv_t=w_uv_t,
        kvg=kv_norm_g.reshape(1, KV_RANK).astype(F32), w_ob=w_ob.astype(BF16),
        w_out=w_out.astype(BF16), ln1_g=ln1_g.reshape(1, D_MODEL), ln1_b=ln1_b.reshape(1, D_MODEL),
        w_g=w_gu[:, :D_FF].astype(BF16), w_u=w_gu[:, D_FF:].astype(BF16), w_down=w_down.astype(BF16),
        ln2_g=ln2_g.reshape(1, D_MODEL), ln2_b=ln2_b.reshape(1, D_MODEL))


def _rope_tables(past, seq, reps):
    half = ROPE // 2
    inv = ROPE_THETA ** (-jnp.arange(half, dtype=F32) / half)
    ang = (past + jnp.arange(seq)).astype(F32)[:, None] * inv[None, :]
    cos = jnp.tile(jnp.cos(ang), (reps, LANES // half))
    sin = jnp.tile(jnp.sin(ang), (reps, LANES // half))
    return cos, sin


TM_PROJ, TN_PROJ, TM_GATES, TM_MLA, TM_MERGE, TM_FFN = 512, N_PROJ // 4, 4096, 1024, 1024, 1024
FFN_CHUNKS = (768, 768, 768, D_FF - 3 * 768)
ROW_PARTS, MIN_PART_ROWS = 4, 256
TQ_ATTN, TK_ATTN, CW_ATTN, TK_DEC = 512, 512, 512, 4096
WIDE_ATTN = 2
DEC_SUB = 1024
TB_GDN, HB_GDN = 2 * GDN_CHUNK, H_A


def _trunk_layer(x, conv_state, s0, caches, wl, bsz, seq, layer, depth, carried):
    decode = caches is not None
    t_rows = bsz * seq
    h1 = _proj_in(x, wl["w_proj"], min(TM_PROJ, t_rows), TN_PROJ)
    seq_pad = -(-seq // GDN_CHUNK) * GDN_CHUNK
    if seq_pad == seq:
        gates = _gates(h1, COL_AB // LANES, wl["al_lane"], wl["dt_lane"], min(TM_GATES, t_rows), seq, seq)
    else:
        ab = h1[:, COL_AB:COL_AB + LANES].reshape(bsz, seq, LANES)
        ab = jnp.pad(ab, ((0, 0), (0, seq_pad - seq), (0, 0))).reshape(bsz * seq_pad, LANES)
        gates = _gates(ab, 0, wl["al_lane"], wl["dt_lane"], min(TM_GATES, bsz * seq_pad), seq_pad, seq)
    o_a, s_new = _gdn(h1, gates.reshape(2 * H_A, 1, bsz * seq_pad), wl["conv_w"], conv_state, s0[0], s0[1],
                      wl["gn"], bsz, seq, min(TB_GDN, seq_pad), HB_GDN)
    if decode:
        cache_ckv, cache_kr_t = caches
        tm_mla = t_rows if t_rows <= TM_MLA else seq
        cos_t, sin_t = _rope_tables(cache_ckv.shape[2], seq, tm_mla // seq)
        q, kc, ckv_all, kr_all = _mla_pre(h1, cos_t, sin_t, wl["qg"], wl["w_uq"], wl["w_uk_t"], wl["kvg"],
                                          bsz, seq, tm_mla, False, layer, depth, carried)
        tk_dec = TK_DEC
        while cache_ckv.shape[2] % tk_dec:
            tk_dec //= 2
        o_b = _attn_dec(q, cache_ckv, cache_kr_t, layer, kc, wl["w_uv"], bsz, seq, tk_dec)
    else:
        cos_t, sin_t = _rope_tables(0, seq, 1)
        q, kc, ckv_all, kr_all, vt = _mla_pre(h1, cos_t, sin_t, wl["qg"], wl["w_uq"], wl["w_uk_t"],
                                              wl["kvg"], bsz, seq, TM_MLA, True, layer, depth, carried)
        o_b = _attn(q, kc.reshape(bsz, seq, KC_W), vt, wl["w_uv_t"], bsz, seq, TQ_ATTN, TK_ATTN, CW_ATTN)
        o_b = o_b.reshape(t_rows, H_B * V_B)
    x1 = _merge(o_a, o_b, h1, x, wl["w_oa"], wl["w_ob"], wl["w_out"], wl["ln1_g"], wl["ln1_b"],
                min(TM_MERGE, t_rows))
    x2 = _ffn(x1, wl["w_g"], wl["w_u"], wl["w_down"], wl["ln2_g"], wl["ln2_b"], min(TM_FFN, t_rows), FFN_CHUNKS)
    conv_new = h1.reshape(bsz, seq, N_PROJ)[:, seq - (CONV_W - 1):, COL_QKV:COL_QKV + C_QKV]
    return x2, conv_new, s_new, (ckv_all, kr_all)


def kernel(x_prompt, x_sample, state_conv, state_gdn, cache_ckv, cache_krope, w_in, conv_w, a_log, dt_bias, gdn_norm_g, w_oa, q_norm_g, w_uq, kv_norm_g, w_ukv, w_ob, w_out, ln1_g, ln1_b, w_gu, w_down, ln2_g, ln2_b):
    bp, lp, _ = x_prompt.shape
    bs, ls, _ = x_sample.shape
    yp = x_prompt.reshape(bp * lp, D_MODEL)
    ys = x_sample.reshape(bs * ls, D_MODEL)
    zero_conv = jnp.zeros((bp, CONV_W - 1, C_QKV), F32)
    zero_s = jnp.zeros((1, bp, H_A, DK_A, DV_A), F32)
    cache_kr_t = jnp.swapaxes(cache_krope, 2, 3)
    w_in_t = jnp.swapaxes(w_in, 1, 2).astype(BF16)
    depth = w_in.shape[0]
    conv_p, gdn_p, conv_s, gdn_s = [], [], [], []
    kv_p = kv_s = ()
    for l in range(depth):
        wl = _prep_layer_weights(w_in_t[l], conv_w[l], a_log[l], dt_bias[l], gdn_norm_g[l], w_oa[l],
                                 q_norm_g[l], w_uq[l], kv_norm_g[l], w_ukv[l], w_ob[l], w_out[l],
                                 ln1_g[l], ln1_b[l], w_gu[l], w_down[l], ln2_g[l], ln2_b[l])
        yp, c_new, g_new, kv_p = _trunk_layer(yp, zero_conv, (zero_s, 0), None, wl, bp, lp, l, depth, kv_p)
        conv_p.append(c_new), gdn_p.append(g_new)
        ys, c_new, g_new, kv_s = _trunk_layer(ys, state_conv[l], (state_gdn, l), (cache_ckv, cache_kr_t), wl,
                                              bs, ls, l, depth, kv_s)
        conv_s.append(c_new), gdn_s.append(g_new)
    return (yp.reshape(bp, lp, D_MODEL), ys.reshape(bs, ls, D_MODEL),
            jnp.stack(conv_p), jnp.stack(gdn_p),
            kv_p[0].reshape(depth, bp, lp, KV_RANK), jnp.swapaxes(kv_p[1], 2, 3),
            jnp.stack(conv_s), jnp.stack(gdn_s),
            kv_s[0].reshape(depth, bs, ls, KV_RANK), kv_s[1].reshape(depth, bs, ls, ROPE))
```

```python
import functools

import numpy as np
import jax
import jax.numpy as jnp
from jax import lax
from jax.experimental import pallas as pl
from jax.experimental.pallas import tpu as pltpu

F32 = jnp.float32
BF16 = jnp.bfloat16

D_MODEL = 1024
DEPTH = 2
CHUNK = 64
H_A = 8
DK_A = 128
DV_A = 128
QK_A = H_A * DK_A
V_A = H_A * DV_A
C_QKV = 2 * QK_A + V_A
CONV_W = 4
H_B = 8
NOPE = 128
ROPE = 64
V_B = 128
Q_RANK = 384
KV_RANK = 256
ROPE_THETA = 10000.0
ATTN_SCALE = (NOPE + ROPE) ** -0.5
LOG2_E = float(np.log2(np.e))
Q_SCALE = ATTN_SCALE * LOG2_E
D_FF = -(-8 * D_MODEL // (3 * 256)) * 256
ALPHA = (2 * DEPTH) ** 0.25
EPS = 1e-6
_SIZES = (C_QKV, V_A, H_A, H_A, Q_RANK, KV_RANK, ROPE, D_MODEL, D_MODEL)
_OFFS = tuple(int(v) for v in np.cumsum((0,) + _SIZES))

LANES = 128
VMEM_LIMIT = 56 * 1024 * 1024

COL_QKV = 0
COL_Z = COL_QKV + C_QKV
COL_GA = COL_Z + V_A
COL_GB = COL_GA + D_MODEL
COL_CKV = COL_GB + D_MODEL
COL_KRA = COL_CKV + KV_RANK
COL_CQ = COL_KRA + LANES
COL_KRB = COL_CQ + Q_RANK
COL_AB = COL_KRB + LANES
N_PROJ = COL_AB + LANES
KC_W = KV_RANK + LANES

GDN_CHUNK = 128
MAX_STALE_EXCESS = 64.0


def _cparams(sem):
    return pltpu.CompilerParams(dimension_semantics=sem, vmem_limit_bytes=VMEM_LIMIT)


def _sigmoid(x):
    return jax.nn.sigmoid(x)


def _silu(x):
    return x * jax.nn.sigmoid(x)


def _mm(a, b):
    return jnp.dot(a.astype(BF16), b.astype(BF16), preferred_element_type=F32)


def _mm_nt(a, b):
    return lax.dot_general(a.astype(BF16), b.astype(BF16), (((1,), (1,)), ((), ())),
                           preferred_element_type=F32)


def _mm_tn(a, b):
    return lax.dot_general(a.astype(BF16), b.astype(BF16), (((0,), (0,)), ((), ())),
                           preferred_element_type=F32)


def _proj_body(x_ref, wt_ref, o_ref, *, tn):
    xb = x_ref[...].astype(BF16)
    for c in range(0, wt_ref.shape[0], tn):
        o_ref[:, c:c + tn] = lax.dot_general(xb, wt_ref[c:c + tn, :], (((1,), (1,)), ((), ())),
                                             preferred_element_type=F32)


def _proj_in(x, w_t, tm, tn):
    t, k = x.shape
    n = w_t.shape[0]
    return pl.pallas_call(
        functools.partial(_proj_body, tn=tn),
        grid=(t // tm,),
        in_specs=[pl.BlockSpec((tm, k), lambda i: (i, 0)),
                  pl.BlockSpec((n, k), lambda i: (0, 0), pipeline_mode=pl.Buffered(1))],
        out_specs=pl.BlockSpec((tm, n), lambda i: (i, 0)),
        out_shape=jax.ShapeDtypeStruct((t, n), F32),
        compiler_params=_cparams(("parallel",)),
        name="proj_in",
    )(x, w_t)


def _gates_body(ab_ref, al_ref, dt_ref, o_ref, *, tm, l_pad, l_valid):
    x = ab_ref[...]
    lane = lax.broadcasted_iota(jnp.int32, x.shape, 1)
    xa = x + dt_ref[...]
    sp = jnp.maximum(xa, 0.0) + jnp.log1p(jnp.exp(-jnp.abs(xa)))
    g = -jnp.exp(al_ref[...]) * sp * LOG2_E
    y = jnp.where(lane < H_A, g, _sigmoid(x))
    yt = y.T[0:2 * H_A, :]
    if l_valid < l_pad:
        col = lax.broadcasted_iota(jnp.int32, yt.shape, 1) + pl.program_id(0) * tm
        yt = jnp.where(col % l_pad < l_valid, yt, 0.0)
    r = lax.broadcasted_iota(jnp.int32, (GDN_CHUNK, GDN_CHUNK), 0)
    c = lax.broadcasted_iota(jnp.int32, (GDN_CHUNK, GDN_CHUNK), 1)
    tri = jnp.where(r <= c, 1.0, 0.0).astype(F32)
    for s in range(tm // GDN_CHUNK):
        sl = slice(s * GDN_CHUNK, (s + 1) * GDN_CHUNK)
        o_ref[0:H_A, sl] = jnp.dot(yt[0:H_A, sl], tri, precision=lax.Precision.HIGHEST,
                                   preferred_element_type=F32)
    o_ref[H_A:2 * H_A, :] = yt[H_A:2 * H_A, :]


def _gates(src, col_block, al_lane, dt_lane, tm, l_pad, l_valid):
    t = src.shape[0]
    body = functools.partial(_gates_body, tm=tm, l_pad=l_pad, l_valid=l_valid)
    return pl.pallas_call(
        body,
        grid=(t // tm,),
        in_specs=[pl.BlockSpec((tm, LANES), lambda i: (i, col_block)),
                  pl.BlockSpec((1, LANES), lambda i: (0, 0)),
                  pl.BlockSpec((1, LANES), lambda i: (0, 0))],
        out_specs=pl.BlockSpec((2 * H_A, tm), lambda i: (0, i)),
        out_shape=jax.ShapeDtypeStruct((2 * H_A, t), F32),
        compiler_params=_cparams(("parallel",)),
        name="gates",
    )(src, al_lane, dt_lane)


def _gdn_body(q_ref, k_ref, v_ref, z_ref, gc_ref, bt_ref, cwq_ref, cwk_ref, cwv_ref,
              csq_ref, csk_ref, csv_ref, s0_ref, gn_ref, o_ref, so_ref,
              xq_ref, xk_ref, xv_ref, s_ref, *, tb, tr, nt, hb):
    t = pl.program_id(2)
    hist = CONV_W - 1
    base = 8

    @pl.when(t == 0)
    def _():
        s_ref[...] = s0_ref[0]
        xq_ref[base - hist:base, :] = csq_ref[0]
        xk_ref[base - hist:base, :] = csk_ref[0]
        xv_ref[base - hist:base, :] = csv_ref[0]

    for xs_ref, raw_ref in ((xq_ref, q_ref), (xk_ref, k_ref), (xv_ref, v_ref)):
        xs_ref[base:base + tr, :] = raw_ref[...]
        if tr < tb:
            xs_ref[base + tr:base + tb, :] = jnp.zeros((tb - tr, xs_ref.shape[1]), F32)

    def conv(xs_ref, cw_ref, ln):
        slab = xs_ref[0:base + tr, ln]
        y = pltpu.roll(slab, hist, 0)[base:base + tr] * cw_ref[0:1, ln]
        for j in range(1, CONV_W):
            tap = slab if j == hist else pltpu.roll(slab, hist - j, 0)
            y = y + tap[base:base + tr] * cw_ref[j:j + 1, ln]
        y = _silu(y)
        if tr < tb:
            y = jnp.concatenate([y, jnp.zeros((tb - tr, LANES), F32)], axis=0)
        return y

    cc = GDN_CHUNK
    ii = lax.broadcasted_iota(jnp.int32, (cc, cc), 0)
    jj = lax.broadcasted_iota(jnp.int32, (cc, cc), 1)
    strict = ii > jj
    eye = jnp.where(ii == jj, 1.0, 0.0).astype(F32)
    heads = range(hb)
    lanes = [slice(hd * LANES, (hd + 1) * LANES) for hd in heads]
    units = [(hd, c) for hd in heads for c in range(tb // cc)]
    rows = {u: slice(u[1] * cc, (u[1] + 1) * cc) for u in units}
    q = [conv(xq_ref, cwq_ref, ln) for ln in lanes]
    k = [conv(xk_ref, cwk_ref, ln) for ln in lanes]
    v = [conv(xv_ref, cwv_ref, ln) for ln in lanes]
    q = [x * (lax.rsqrt(jnp.sum(x * x, -1, keepdims=True) + EPS) * (DK_A ** -0.5)) for x in q]
    k = [x * lax.rsqrt(jnp.sum(x * x, -1, keepdims=True) + EPS) for x in k]
    qc = {u: q[u[0]][rows[u]] for u in units}
    kc = {u: k[u[0]][rows[u]] for u in units}
    vc = {u: v[u[0]][rows[u]] for u in units}
    m_row = {u: jnp.broadcast_to(gc_ref[u[0], :, rows[u]], (cc, cc)) for u in units}
    m_col = {u: m_row[u].T for u in units}
    b_col = {u: jnp.broadcast_to(bt_ref[u[0], :, rows[u]], (cc, cc)).T for u in units}
    kq = {u: _mm_nt(jnp.concatenate([kc[u], qc[u]], axis=0), kc[u]) for u in units}
    diff = {u: m_col[u] - m_row[u] for u in units}
    dec = {u: jnp.exp2(jnp.where(strict, diff[u], -jnp.inf)) for u in units}
    a_mat = {u: b_col[u] * kq[u][0:cc] * dec[u] for u in units}
    qk = {u: kq[u][cc:2 * cc] * (dec[u] + eye) for u in units}
    e_g = {u: jnp.exp2(m_col[u]) for u in units}
    x_inv = {u: eye - jnp.where((ii >> 1) == (jj >> 1), a_mat[u], 0.0) for u in units}
    sft = 1
    while (1 << sft) < min(cc, tr):
        off = ((ii >> (sft + 1)) == (jj >> (sft + 1))) & ((ii >> sft) != (jj >> sft))
        lx = {u: _mm(jnp.where(off, a_mat[u], 0.0), x_inv[u]) for u in units}
        x_inv = {u: x_inv[u] - _mm(x_inv[u], lx[u]) for u in units}
        sft += 1
    uw = {u: _mm(x_inv[u], jnp.concatenate([vc[u] * b_col[u], kc[u] * (b_col[u] * e_g[u])], axis=1))
          for u in units}
    wq = {u: jnp.concatenate([uw[u][:, DV_A:], qc[u] * e_g[u]], axis=0) for u in units}
    g_last = {u: m_col[u][cc - 1:cc, :] for u in units}
    kd = {u: kc[u] * jnp.exp2(g_last[u] - m_col[u]) for u in units}
    s_state = [s_ref[hd] for hd in heads]
    for c in range(tb // cc):
        ws = [_mm(wq[(hd, c)], s_state[hd]) for hd in heads]
        v_new = [uw[(hd, c)][:, 0:DV_A] - ws[hd][0:cc] for hd in heads]
        s_state = [s_state[hd] * jnp.exp2(g_last[(hd, c)]) + _mm_tn(kd[(hd, c)], v_new[hd]) for hd in heads]
        o = [ws[hd][cc:2 * cc] + _mm(qk[(hd, c)], v_new[hd]) for hd in heads]
        o = [x * lax.rsqrt(jnp.mean(x * x, -1, keepdims=True) + EPS) * gn_ref[...] for x in o]
        nr = min(cc, tr - c * cc)
        r = slice(c * cc, c * cc + nr)
        for hd in heads:
            o_ref[r, lanes[hd]] = (o[hd][0:nr] * _silu(z_ref[r, lanes[hd]])).astype(o_ref.dtype)
    for hd in heads:
        s_ref[hd] = s_state[hd]

    xq_ref[base - hist:base, :] = xq_ref[base + tb - hist:base + tb, :]
    xk_ref[base - hist:base, :] = xk_ref[base + tb - hist:base + tb, :]
    xv_ref[base - hist:base, :] = xv_ref[base + tb - hist:base + tb, :]

    @pl.when(t == nt - 1)
    def _():
        so_ref[0] = s_ref[...]


def _gdn(h1, gates3, conv_w, conv_state, s0, s0_layer, gn, bsz, seq, tb, hb):
    nt = -(-seq // tb)
    tr = min(tb, seq)
    assert seq % tr == 0 and (tr == tb or (nt == 1 and tb == GDN_CHUNK))
    t_rows = bsz * seq
    body = functools.partial(_gdn_body, tb=tb, tr=tr, nt=nt, hb=hb)
    width = hb * LANES
    ng = H_A // hb

    def rows(seg):
        return pl.BlockSpec((tr, width), lambda b, g, t: (b * nt + t, seg * ng + g))

    def cw(seg):
        return pl.BlockSpec((CONV_W, width), lambda b, g, t: (0, seg * ng + g))

    def cs(seg):
        return pl.BlockSpec((1, CONV_W - 1, width), lambda b, g, t: (b, 0, seg * ng + g))

    return pl.pallas_call(
        body,
        grid=(bsz, ng, nt),
        in_specs=[rows(0), rows(1), rows(2), rows(COL_Z // QK_A),
                  pl.BlockSpec((hb, 1, tb), lambda b, g, t: (g, 0, b * nt + t)),
                  pl.BlockSpec((hb, 1, tb), lambda b, g, t: (ng + g, 0, b * nt + t)),
                  cw(0), cw(1), cw(2), cs(0), cs(1), cs(2),
                  pl.BlockSpec((None, 1, hb, DK_A, DV_A), lambda b, g, t: (s0_layer, b, g, 0, 0)),
                  pl.BlockSpec((1, DV_A), lambda b, g, t: (0, 0))],
        out_specs=[pl.BlockSpec((tr, width), lambda b, g, t: (b * nt + t, g)),
                   pl.BlockSpec((1, hb, DK_A, DV_A), lambda b, g, t: (b, g, 0, 0))],
        out_shape=[jax.ShapeDtypeStruct((t_rows, V_A), BF16),
                   jax.ShapeDtypeStruct((bsz, H_A, DK_A, DV_A), F32)],
        scratch_shapes=[pltpu.VMEM((tb + 8, width), F32)] * 3 + [pltpu.VMEM((hb, DK_A, DV_A), F32)],
        compiler_params=_cparams(("parallel", "parallel", "arbitrary")),
        name="gdn",
    )(h1, h1, h1, h1, gates3, gates3, conv_w, conv_w, conv_w,
      conv_state, conv_state, conv_state, s0, gn)


def _mla_pre_body(cq_ref, ckv_ref, kra_ref, krb_ref, cos_ref, sin_ref, qg_ref, wuq_ref, wuk_ref,
                  kvg_ref, *refs, n_carried):
    q_ref, kc_ref, ckvo_ref, kro_ref, *maybe_vt_ref = refs[n_carried:]
    cq = cq_ref[...]
    cqn = cq * lax.rsqrt(jnp.mean(cq * cq, -1, keepdims=True) + EPS) * qg_ref[...]
    qf = jnp.dot(cqn.astype(BF16), wuq_ref[...], preferred_element_type=F32)
    cos_k = cos_ref[...]
    sin_k = sin_ref[...]
    reps = H_B * ROPE // LANES
    cos_t = jnp.concatenate([cos_k] * reps, axis=1)
    sin_t = jnp.concatenate([sin_k] * reps, axis=1)
    n_nope = H_B * NOPE
    n_rope = H_B * ROPE
    qr = (qf[:, n_nope:n_nope + n_rope] * cos_t + qf[:, n_nope + n_rope:] * sin_t) * Q_SCALE
    lane = lax.broadcasted_iota(jnp.int32, (cq.shape[0], LANES), 1)
    for h in range(H_B):
        ql = jnp.dot(qf[:, h * NOPE:(h + 1) * NOPE].astype(BF16), wuk_ref[h],
                     preferred_element_type=F32) * Q_SCALE
        blk = qr[:, (h // 2) * LANES:(h // 2 + 1) * LANES]
        keep = (lane < ROPE) if h % 2 == 0 else (lane >= ROPE)
        q_ref[h, :, 0:KV_RANK] = ql.astype(BF16)
        q_ref[h, :, KV_RANK:KC_W] = jnp.where(keep, blk, 0.0).astype(BF16)
    ckv = ckv_ref[...]
    ckvn = ckv * lax.rsqrt(jnp.mean(ckv * ckv, -1, keepdims=True) + EPS) * kvg_ref[...]
    kr2 = kra_ref[...] * cos_k + krb_ref[...] * sin_k
    ckvo_ref[...] = ckvn
    if maybe_vt_ref:
        kro_ref[0] = kr2.T[0:ROPE, :]
    else:
        kro_ref[...] = kr2[:, 0:ROPE]
    kc_ref[:, 0:KV_RANK] = ckvn.astype(BF16)
    kc_ref[:, KV_RANK:KC_W] = kr2.astype(BF16)
    if maybe_vt_ref:
        maybe_vt_ref[0][0] = ckvn.T.astype(BF16)


def _mla_pre(h1, cos_t, sin_t, qg, wuq, wuk, kvg, bsz, seq, tm, with_vt, layer, depth, carried):
    t_rows = bsz * seq
    ntab = cos_t.shape[0] // tm
    npb = max(seq // tm, 1)
    const2 = lambda i: (0, 0)
    out_specs = [pl.BlockSpec((H_B, tm, KC_W), lambda i: (0, i, 0)),
                 pl.BlockSpec((tm, KC_W), lambda i: (i, 0)),
                 pl.BlockSpec((None, tm, KV_RANK), lambda i: (layer, i, 0)),
                 pl.BlockSpec((None, tm, ROPE), lambda i: (layer, i, 0))]
    out_shape = [jax.ShapeDtypeStruct((H_B, t_rows, KC_W), BF16),
                 jax.ShapeDtypeStruct((t_rows, KC_W), BF16),
                 jax.ShapeDtypeStruct((depth, t_rows, KV_RANK), F32),
                 jax.ShapeDtypeStruct((depth, t_rows, ROPE), F32)]
    if with_vt:
        out_specs[3] = pl.BlockSpec((None, 1, ROPE, tm), lambda i: (layer, i // npb, 0, i % npb))
        out_shape[3] = jax.ShapeDtypeStruct((depth, bsz, ROPE, seq), F32)
        out_specs.append(pl.BlockSpec((1, KV_RANK, tm), lambda i: (i // npb, 0, i % npb)))
        out_shape.append(jax.ShapeDtypeStruct((bsz, KV_RANK, seq), BF16))
    n_in = 10
    return pl.pallas_call(
        functools.partial(_mla_pre_body, n_carried=len(carried)),
        input_output_aliases={n_in + k: 2 + k for k in range(len(carried))},
        grid=(t_rows // tm,),
        in_specs=[pl.BlockSpec((tm, Q_RANK), lambda i: (i, COL_CQ // Q_RANK)),
                  pl.BlockSpec((tm, KV_RANK), lambda i: (i, COL_CKV // KV_RANK)),
                  pl.BlockSpec((tm, LANES), lambda i: (i, COL_KRA // LANES)),
                  pl.BlockSpec((tm, LANES), lambda i: (i, COL_KRB // LANES)),
                  pl.BlockSpec((tm, LANES), lambda i: (i % ntab, 0)),
                  pl.BlockSpec((tm, LANES), lambda i: (i % ntab, 0)),
                  pl.BlockSpec((1, Q_RANK), const2),
                  pl.BlockSpec(wuq.shape, const2),
                  pl.BlockSpec(wuk.shape, lambda i: (0, 0, 0)),
                  pl.BlockSpec((1, KV_RANK), const2)] + [pl.BlockSpec(memory_space=pl.ANY)] * len(carried),
        out_specs=out_specs,
        out_shape=out_shape,
        compiler_params=_cparams(("parallel",)),
        name="mla_pre",
    )(h1, h1, h1, h1, cos_t, sin_t, qg, wuq, wuk, kvg, *carried)


def _attn_body(q_ref, k_ref, vt_ref, wuv_ref, o_ref, m_ref, l_ref, acc_ref, ex_ref, *, tq, tk, cw, lk):
    i = pl.program_id(1)
    shift = CHUNK.bit_length() - 1

    def update(j, masked, lazy, opening=False, wide=1):
        tkw = wide * tk
        k0 = pl.multiple_of(j * tk, tk)
        kt = k_ref[0, pl.ds(k0, tkw), :]
        vt = vt_ref[0, :, pl.ds(k0, tkw)]
        bias = None
        if masked:
            kpos = k0 + lax.broadcasted_iota(jnp.int32, (tkw, tq), 0)
            qpos = i * tq + lax.broadcasted_iota(jnp.int32, (tkw, tq), 1)
            bias = jnp.where((kpos >> shift) <= (qpos >> shift), 0.0, -jnp.inf).astype(F32)

        units = [(h, slice(c, c + cw)) for h in range(H_B) for c in range(0, tq, cw)]

        def scores(u):
            h, cs = u
            s = lax.dot_general(kt, q_ref[h, cs, :], (((1,), (1,)), ((), ())),
                                preferred_element_type=F32)
            return s if bias is None else s + bias[:, cs]

        def softmax(u, s):
            h, cs = u
            c_max = jnp.max(s, 0, keepdims=True)
            if opening:
                m_prev = jnp.zeros_like(c_max)
                m_new = c_max
                gap = jnp.abs(c_max)
            else:
                m_prev = m_ref[h, :, cs]
                m_new = jnp.maximum(m_prev, c_max)
                gap = c_max - m_prev
            alpha = jnp.exp2(m_prev - m_new)
            if lazy:
                p = jnp.exp2(s - m_prev)
                ex_ref[h, :, cs] = jnp.maximum(ex_ref[h, :, cs], gap)
                l_ref[h, :, cs] = alpha * (l_ref[h, :, cs] + jnp.sum(p, 0, keepdims=True))
            else:
                p = jnp.exp2(s - m_new)
                l_ref[h, :, cs] = alpha * l_ref[h, :, cs] + jnp.sum(p, 0, keepdims=True)
            m_ref[h, :, cs] = m_new
            return p.astype(BF16), alpha

        def accumulate(u, alpha, pv):
            h, cs = u
            if lazy:
                acc_ref[h, :, cs] = alpha * (acc_ref[h, :, cs] + pv)
            else:
                acc_ref[h, :, cs] = alpha * acc_ref[h, :, cs] + pv

        n_units = len(units)
        s_q = {0: scores(units[0])}
        if n_units > 1:
            s_q[1] = scores(units[1])
        p0, alpha = softmax(units[0], s_q.pop(0))
        alphas = {0: alpha}
        pvs = {0: jnp.dot(vt, p0, preferred_element_type=F32)}
        for n in range(n_units):
            if n + 2 < n_units:
                s_q[n + 2] = scores(units[n + 2])
            if n + 1 < n_units:
                p_next, alphas[n + 1] = softmax(units[n + 1], s_q.pop(n + 1))
            accumulate(units[n], alphas.pop(n), pvs.pop(n))
            if n + 1 < n_units:
                pvs[n + 1] = jnp.dot(vt, p_next, preferred_element_type=F32)

    n_full = ((((i * tq) >> shift) + 1) << shift) // tk
    n_all = jnp.minimum(((((i * tq + tq - 1) >> shift) + 1) << shift) + tk - 1, lk + tk - 1) // tk

    def tile_loop(lo, hi, masked, lazy, opening=False, wide=1, first_tile=0):
        def step(p, carry):
            update(first_tile + p * wide, masked, lazy, opening, wide)
            return carry
        lax.fori_loop(lo, hi, step, 0)

    def attempt(a, redo):
        first = a == 0
        run = jnp.logical_or(first, redo > 0)

        @pl.when(run)
        def _():
            m_ref[...] = jnp.full(m_ref.shape, -jnp.inf, F32)
            l_ref[...] = jnp.zeros(l_ref.shape, F32)
            acc_ref[...] = jnp.zeros(acc_ref.shape, F32)
            ex_ref[...] = jnp.full(ex_ref.shape, -jnp.inf, F32)

        tile_loop(0, jnp.where(first, 0, n_all) * run.astype(jnp.int32), True, False)
        tile_loop(0, jnp.where(first, 1, 0), True, True, True)
        n_wide = jnp.maximum(n_full - 1, 0) // WIDE_ATTN
        tile_loop(0, jnp.where(first, n_wide, 0), False, True, wide=WIDE_ATTN, first_tile=1)
        tile_loop(1 + n_wide * WIDE_ATTN, jnp.where(first, n_full, 0), False, True)
        tile_loop(jnp.maximum(n_full, 1), jnp.where(first, n_all, 0), True, True)
        excess = jnp.max(ex_ref[...])
        return jnp.where(first, (excess > MAX_STALE_EXCESS).astype(jnp.int32), 0)

    lax.fori_loop(0, 2, attempt, jnp.int32(0))

    for h in range(H_B):
        o_t = (acc_ref[h] * (1.0 / l_ref[h])).astype(BF16)
        ob_t = jnp.dot(wuv_ref[h], o_t, preferred_element_type=F32)
        o_ref[0, :, h * V_B:(h + 1) * V_B] = ob_t.T.astype(o_ref.dtype)


def _attn(q, kc, vt, wuv_t, bsz, seq, tq, tk, cw):
    assert seq % tq == 0 and seq % tk == 0
    nq = seq // tq
    body = functools.partial(_attn_body, tq=tq, tk=tk, cw=cw, lk=seq)
    return pl.pallas_call(
        body,
        grid=(bsz, nq),
        in_specs=[pl.BlockSpec((H_B, tq, KC_W), lambda b, i: (0, b * nq + i, 0)),
                  pl.BlockSpec((1, seq, KC_W), lambda b, i: (b, 0, 0)),
                  pl.BlockSpec((1, KV_RANK, seq), lambda b, i: (b, 0, 0)),
                  pl.BlockSpec(wuv_t.shape, lambda b, i: (0, 0, 0))],
        out_specs=pl.BlockSpec((1, tq, H_B * V_B), lambda b, i: (b, i, 0)),
        out_shape=jax.ShapeDtypeStruct((bsz, seq, H_B * V_B), BF16),
        scratch_shapes=[pltpu.VMEM((H_B, 1, tq), F32), pltpu.VMEM((H_B, 1, tq), F32),
                        pltpu.VMEM((H_B, KV_RANK, tq), F32), pltpu.VMEM((H_B, 1, tq), F32)],
        compiler_params=_cparams(("parallel", "arbitrary")),
        name="attn",
    )(q, kc, vt, wuv_t)


def _attn_dec_body(q_ref, ckv_ref, krt_ref, kn_ref, wuv_ref, o_ref, m_ref, l_ref, acc_ref,
                   *, tq, past, n_past):
    j = pl.program_id(1)
    rows = H_B * tq
    nt_dims = (((1,), (1,)), ((), ()))
    q2 = q_ref[...].reshape(rows, KC_W)

    @pl.when(j == 0)
    def _():
        m_ref[...] = jnp.full(m_ref.shape, -jnp.inf, F32)
        l_ref[...] = jnp.zeros(l_ref.shape, F32)
        acc_ref[...] = jnp.zeros(acc_ref.shape, F32)

    def accumulate(s, v_nat):
        m_prev = m_ref[...]
        m_new = jnp.maximum(m_prev, jnp.max(s, -1, keepdims=True))
        p = jnp.exp2(s - m_new)
        alpha = jnp.exp2(m_prev - m_new)
        l_ref[...] = alpha * l_ref[...] + jnp.sum(p, -1, keepdims=True)
        acc_ref[...] = alpha * acc_ref[...] + jnp.dot(p.astype(BF16), v_nat, preferred_element_type=F32)
        m_ref[...] = m_new

    @pl.when(j < n_past)
    def _():
        tk = ckv_ref.shape[1]
        for c in range(0, tk, min(tk, DEC_SUB)):
            cs = slice(c, c + min(tk, DEC_SUB))
            k_lat = ckv_ref[0, cs, :].astype(BF16)
            kr_t = krt_ref[0, :, cs].astype(BF16)
            kr2_t = jnp.concatenate([kr_t, kr_t], axis=0)
            s = (lax.dot_general(q2[:, 0:KV_RANK], k_lat, nt_dims, preferred_element_type=F32)
                 + jnp.dot(q2[:, KV_RANK:KC_W], kr2_t, preferred_element_type=F32))
            accumulate(s, k_lat)

    @pl.when(j == n_past)
    def _():
        kn = kn_ref[...]
        s = lax.dot_general(q2, kn, nt_dims, preferred_element_type=F32)
        shift = CHUNK.bit_length() - 1
        qpos = past + (lax.broadcasted_iota(jnp.int32, s.shape, 0) & (tq - 1))
        kpos = past + lax.broadcasted_iota(jnp.int32, s.shape, 1)
        s = jnp.where((kpos >> shift) <= (qpos >> shift), s, -jnp.inf)
        accumulate(s, kn[:, 0:KV_RANK])
        o = acc_ref[...] * (1.0 / l_ref[...])
        for h in range(H_B):
            oh = o[h * tq:(h + 1) * tq, :].astype(BF16)
            o_ref[:, h * V_B:(h + 1) * V_B] = jnp.dot(
                oh, wuv_ref[h], preferred_element_type=F32).astype(o_ref.dtype)


def _attn_dec(q, cache_ckv, cache_kr_t, layer, kc_new, wuv, bsz, seq, tk):
    past = cache_ckv.shape[2]
    assert past % tk == 0 and past % CHUNK == 0 and seq & (seq - 1) == 0
    n_past = past // tk
    body = functools.partial(_attn_dec_body, tq=seq, past=past, n_past=n_past)
    return pl.pallas_call(
        body,
        grid=(bsz, n_past + 1),
        in_specs=[pl.BlockSpec((H_B, seq, KC_W), lambda b, j: (0, b, 0)),
                  pl.BlockSpec((None, 1, tk, KV_RANK), lambda b, j: (layer, b, jnp.minimum(j, n_past - 1), 0)),
                  pl.BlockSpec((None, 1, ROPE, tk), lambda b, j: (layer, b, 0, jnp.minimum(j, n_past - 1))),
                  pl.BlockSpec((seq, KC_W), lambda b, j: (b, 0)),
                  pl.BlockSpec(wuv.shape, lambda b, j: (0, 0, 0))],
        out_specs=pl.BlockSpec((seq, H_B * V_B), lambda b, j: (b, 0)),
        out_shape=jax.ShapeDtypeStruct((bsz * seq, H_B * V_B), BF16),
        scratch_shapes=[pltpu.VMEM((H_B * seq, 1), F32), pltpu.VMEM((H_B * seq, 1), F32),
                        pltpu.VMEM((H_B * seq, KV_RANK), F32)],
        compiler_params=_cparams(("parallel", "arbitrary")),
        name="attn_dec",
    )(q, cache_ckv, cache_kr_t, kc_new, wuv)


def _layer_norm(r, g, b):
    mu = jnp.mean(r, -1, keepdims=True)
    d = r - mu
    var = jnp.mean(d * d, -1, keepdims=True)
    return d * lax.rsqrt(var + EPS) * g + b


def _merge_body(oa_ref, ob_ref, ga_ref, gb_ref, x_ref, woa_ref, wob_ref, wout_ref, g_ref, b_ref, o_ref):
    tm = x_ref.shape[0]
    part = max(tm // ROW_PARTS, min(tm, MIN_PART_ROWS))
    for r0 in range(0, tm, part):
        rs = slice(r0, r0 + part)
        ya = jnp.dot(oa_ref[rs, :], woa_ref[...], preferred_element_type=F32)
        yb = jnp.dot(ob_ref[rs, :], wob_ref[...], preferred_element_type=F32)
        m = _sigmoid(ga_ref[rs, :]) * ya + _sigmoid(gb_ref[rs, :]) * yb
        r = ALPHA * x_ref[rs, :] + jnp.dot(m.astype(BF16), wout_ref[...], preferred_element_type=F32)
        o_ref[rs, :] = _layer_norm(r, g_ref[...], b_ref[...])


def _merge(oa, ob, h1, x, woa, wob, wout, g, b, tm):
    t = x.shape[0]
    row = lambda i: (i, 0)
    const = lambda i: (0, 0)
    wspec = pl.BlockSpec((D_MODEL, D_MODEL), const, pipeline_mode=pl.Buffered(1))
    return pl.pallas_call(
        _merge_body,
        grid=(t // tm,),
        in_specs=[pl.BlockSpec((tm, V_A), row), pl.BlockSpec((tm, H_B * V_B), row),
                  pl.BlockSpec((tm, D_MODEL), lambda i: (i, COL_GA // D_MODEL)),
                  pl.BlockSpec((tm, D_MODEL), lambda i: (i, COL_GB // D_MODEL)),
                  pl.BlockSpec((tm, D_MODEL), row), wspec, wspec, wspec,
                  pl.BlockSpec((1, D_MODEL), const), pl.BlockSpec((1, D_MODEL), const)],
        out_specs=pl.BlockSpec((tm, D_MODEL), row),
        out_shape=jax.ShapeDtypeStruct((t, D_MODEL), F32),
        compiler_params=_cparams(("parallel",)),
        name="merge",
    )(oa, ob, h1, h1, x, woa, wob, wout, g, b)


def _ffn_body(x_ref, wg_ref, wu_ref, wd_ref, g_ref, b_ref, o_ref, *, chunks):
    tm = x_ref.shape[0]
    part = max(tm // ROW_PARTS, min(tm, MIN_PART_ROWS))
    for r0 in range(0, tm, part):
        rs = slice(r0, r0 + part)
        x = x_ref[rs, :]
        xb = x.astype(BF16)
        y = ALPHA * x
        off = 0
        for width in chunks:
            sl = slice(off, off + width)
            f1 = jnp.dot(xb, wg_ref[:, sl], preferred_element_type=F32)
            f3 = jnp.dot(xb, wu_ref[:, sl], preferred_element_type=F32)
            hc = (_silu(f1) * f3).astype(BF16)
            y = y + jnp.dot(hc, wd_ref[sl, :], preferred_element_type=F32)
            off += width
        o_ref[rs, :] = _layer_norm(y, g_ref[...], b_ref[...])


def _ffn(x, wg, wu, wd, g, b, tm, chunks):
    assert sum(chunks) == D_FF
    t = x.shape[0]
    row = lambda i: (i, 0)
    const = lambda i: (0, 0)
    single = pl.Buffered(1)
    return pl.pallas_call(
        functools.partial(_ffn_body, chunks=chunks),
        grid=(t // tm,),
        in_specs=[pl.BlockSpec((tm, D_MODEL), row),
                  pl.BlockSpec((D_MODEL, D_FF), const, pipeline_mode=single),
                  pl.BlockSpec((D_MODEL, D_FF), const, pipeline_mode=single),
                  pl.BlockSpec((D_FF, D_MODEL), const, pipeline_mode=single),
                  pl.BlockSpec((1, D_MODEL), const), pl.BlockSpec((1, D_MODEL), const)],
        out_specs=pl.BlockSpec((tm, D_MODEL), row),
        out_shape=jax.ShapeDtypeStruct((t, D_MODEL), F32),
        compiler_params=_cparams(("parallel",)),
        name="ffn",
    )(x, wg, wu, wd, g, b)


def _prep_layer_weights(w_in_t, conv_w, a_log, dt_bias, gdn_norm_g, w_oa, q_norm_g, w_uq, kv_norm_g,
                        w_ukv, w_ob, w_out, ln1_g, ln1_b, w_gu, w_down, ln2_g, ln2_b):
    seg = lambda i: w_in_t[_OFFS[i]:_OFFS[i + 1], :]
    qkv, z, a, b, c_q, c_kv, k_r, g_a, g_b = (seg(i) for i in range(9))
    half = ROPE // 2
    k_r_rot = jnp.concatenate([-k_r[half:], k_r[:half]], axis=0)
    pad = jnp.zeros((LANES - 2 * H_A, D_MODEL), w_in_t.dtype)
    w_proj = jnp.concatenate([qkv, z, g_a, g_b, c_kv, k_r, k_r, c_q, k_r_rot, k_r_rot, a, b, pad],
                             axis=0).astype(BF16)
    lane_pad = jnp.zeros((LANES - H_A,), F32)
    al_lane = jnp.concatenate([a_log.astype(F32), lane_pad]).reshape(1, LANES)
    dt_lane = jnp.concatenate([dt_bias.astype(F32), lane_pad]).reshape(1, LANES)
    uq = w_uq.reshape(Q_RANK, H_B, NOPE + ROPE)
    uq_nope = uq[:, :, :NOPE].reshape(Q_RANK, H_B * NOPE)
    uq_rope = uq[:, :, NOPE:]
    uq_rot = jnp.concatenate([-uq_rope[:, :, half:], uq_rope[:, :, :half]], axis=2)
    w_uq_ext = jnp.concatenate([uq_nope, uq_rope.reshape(Q_RANK, H_B * ROPE),
                                uq_rot.reshape(Q_RANK, H_B * ROPE)], axis=1).astype(BF16)
    ukv = w_ukv.reshape(KV_RANK, H_B, NOPE + V_B)
    w_uk_t = jnp.transpose(ukv[:, :, :NOPE], (1, 2, 0)).astype(BF16)
    w_uv = jnp.transpose(ukv[:, :, NOPE:], (1, 0, 2)).astype(BF16)
    w_uv_t = jnp.transpose(ukv[:, :, NOPE:], (1, 2, 0)).astype(BF16)
    return dict(
        w_proj=w_proj, conv_w=conv_w.astype(F32), al_lane=al_lane, dt_lane=dt_lane,
        gn=gdn_norm_g.reshape(1, DV_A).astype(F32), w_oa=w_oa.astype(BF16),
        qg=q_norm_g.reshape(1, Q_RANK).astype(F32), w_uq=w_uq_ext, w_uk_t=w_uk_t, w_uv=w_uv, w_uv_t=w_uv_t,
        kvg=kv_norm_g.reshape(1, KV_RANK).astype(F32), w_ob=w_ob.astype(BF16),
        w_out=w_out.astype(BF16), ln1_g=ln1_g.reshape(1, D_MODEL), ln1_b=ln1_b.reshape(1, D_MODEL),
        w_g=w_gu[:, :D_FF].astype(BF16), w_u=w_gu[:, D_FF:].astype(BF16), w_down=w_down.astype(BF16),
        ln2_g=ln2_g.reshape(1, D_MODEL), ln2_b=ln2_b.reshape(1, D_MODEL))


def _rope_tables(past, seq, reps):
    half = ROPE // 2
    inv = ROPE_THETA ** (-jnp.arange(half, dtype=F32) / half)
    ang = (past + jnp.arange(seq)).astype(F32)[:, None] * inv[None, :]
    cos = jnp.tile(jnp.cos(ang), (reps, LANES // half))
    sin = jnp.tile(jnp.sin(ang), (reps, LANES // half))
    return cos, sin


TM_PROJ, TN_PROJ, TM_GATES, TM_MLA, TM_MERGE, TM_FFN = 512, N_PROJ // 4, 4096, 1024, 1024, 1024
FFN_CHUNKS = (1024, 1024, D_FF - 2 * 1024)
ROW_PARTS, MIN_PART_ROWS = 4, 256
TQ_ATTN, TK_ATTN, CW_ATTN, TK_DEC = 512, 512, 512, 4096
WIDE_ATTN = 2
DEC_SUB = 1024
TB_GDN, HB_GDN = 4 * GDN_CHUNK, H_A


def _trunk_layer(x, conv_state, s0, caches, wl, bsz, seq, layer, depth, carried):
    decode = caches is not None
    t_rows = bsz * seq
    h1 = _proj_in(x, wl["w_proj"], min(TM_PROJ, t_rows), TN_PROJ)
    seq_pad = -(-seq // GDN_CHUNK) * GDN_CHUNK
    if seq_pad == seq:
        gates = _gates(h1, COL_AB // LANES, wl["al_lane"], wl["dt_lane"], min(TM_GATES, t_rows), seq, seq)
    else:
        ab = h1[:, COL_AB:COL_AB + LANES].reshape(bsz, seq, LANES)
        ab = jnp.pad(ab, ((0, 0), (0, seq_pad - seq), (0, 0))).reshape(bsz * seq_pad, LANES)
        gates = _gates(ab, 0, wl["al_lane"], wl["dt_lane"], min(TM_GATES, bsz * seq_pad), seq_pad, seq)
    o_a, s_new = _gdn(h1, gates.reshape(2 * H_A, 1, bsz * seq_pad), wl["conv_w"], conv_state, s0[0], s0[1],
                      wl["gn"], bsz, seq, min(TB_GDN, seq_pad), HB_GDN)
    if decode:
        cache_ckv, cache_kr_t = caches
        tm_mla = t_rows if t_rows <= TM_MLA else seq
        cos_t, sin_t = _rope_tables(cache_ckv.shape[2], seq, tm_mla // seq)
        q, kc, ckv_all, kr_all = _mla_pre(h1, cos_t, sin_t, wl["qg"], wl["w_uq"], wl["w_uk_t"], wl["kvg"],
                                          bsz, seq, tm_mla, False, layer, depth, carried)
        tk_dec = TK_DEC
        while cache_ckv.shape[2] % tk_dec:
            tk_dec //= 2
        o_b = _attn_dec(q, cache_ckv, cache_kr_t, layer, kc, wl["w_uv"], bsz, seq, tk_dec)
    else:
        cos_t, sin_t = _rope_tables(0, seq, 1)
        q, kc, ckv_all, kr_all, vt = _mla_pre(h1, cos_t, sin_t, wl["qg"], wl["w_uq"], wl["w_uk_t"],
                                              wl["kvg"], bsz, seq, TM_MLA, True, layer, depth, carried)
        o_b = _attn(q, kc.reshape(bsz, seq, KC_W), vt, wl["w_uv_t"], bsz, seq, TQ_ATTN, TK_ATTN, CW_ATTN)
        o_b = o_b.reshape(t_rows, H_B * V_B)
    x1 = _merge(o_a, o_b, h1, x, wl["w_oa"], wl["w_ob"], wl["w_out"], wl["ln1_g"], wl["ln1_b"],
                min(TM_MERGE, t_rows))
    x2 = _ffn(x1, wl["w_g"], wl["w_u"], wl["w_down"], wl["ln2_g"], wl["ln2_b"], min(TM_FFN, t_rows), FFN_CHUNKS)
    conv_new = h1.reshape(bsz, seq, N_PROJ)[:, seq - (CONV_W - 1):, COL_QKV:COL_QKV + C_QKV]
    return x2, conv_new, s_new, (ckv_all, kr_all)


def kernel(x_prompt, x_sample, state_conv, state_gdn, cache_ckv, cache_krope, w_in, conv_w, a_log, dt_bias, gdn_norm_g, w_oa, q_norm_g, w_uq, kv_norm_g, w_ukv, w_ob, w_out, ln1_g, ln1_b, w_gu, w_down, ln2_g, ln2_b):
    bp, lp, _ = x_prompt.shape
    bs, ls, _ = x_sample.shape
    yp = x_prompt.reshape(bp * lp, D_MODEL)
    ys = x_sample.reshape(bs * ls, D_MODEL)
    zero_conv = jnp.zeros((bp, CONV_W - 1, C_QKV), F32)
    zero_s = jnp.zeros((1, bp, H_A, DK_A, DV_A), F32)
    cache_kr_t = jnp.swapaxes(cache_krope, 2, 3)
    w_in_t = jnp.swapaxes(w_in, 1, 2).astype(BF16)
    depth = w_in.shape[0]
    conv_p, gdn_p, conv_s, gdn_s = [], [], [], []
    kv_p = kv_s = ()
    for l in range(depth):
        wl = _prep_layer_weights(w_in_t[l], conv_w[l], a_log[l], dt_bias[l], gdn_norm_g[l], w_oa[l],
                                 q_norm_g[l], w_uq[l], kv_norm_g[l], w_ukv[l], w_ob[l], w_out[l],
                                 ln1_g[l], ln1_b[l], w_gu[l], w_down[l], ln2_g[l], ln2_b[l])
        yp, c_new, g_new, kv_p = _trunk_layer(yp, zero_conv, (zero_s, 0), None, wl, bp, lp, l, depth, kv_p)
        conv_p.append(c_new), gdn_p.append(g_new)
        ys, c_new, g_new, kv_s = _trunk_layer(ys, state_conv[l], (state_gdn, l), (cache_ckv, cache_kr_t), wl,
                                              bs, ls, l, depth, kv_s)
        conv_s.append(c_new), gdn_s.append(g_new)
    return (yp.reshape(bp, lp, D_MODEL), ys.reshape(bs, ls, D_MODEL),
            jnp.stack(conv_p), jnp.stack(gdn_p),
            kv_p[0].reshape(depth, bp, lp, KV_RANK), jnp.swapaxes(kv_p[1], 2, 3),
            jnp.stack(conv_s), jnp.stack(gdn_s),
            kv_s[0].reshape(depth, bs, ls, KV_RANK), kv_s[1].reshape(depth, bs, ls, ROPE))
```

```python
import functools

import numpy as np
import jax
import jax.numpy as jnp
from jax import lax
from jax.experimental import pallas as pl
from jax.experimental.pallas import tpu as pltpu

F32 = jnp.float32
BF16 = jnp.bfloat16

D_MODEL = 1024
DEPTH = 2
CHUNK = 64
H_A = 8
DK_A = 128
DV_A = 128
QK_A = H_A * DK_A
V_A = H_A * DV_A
C_QKV = 2 * QK_A + V_A
CONV_W = 4
H_B = 8
NOPE = 128
ROPE = 64
V_B = 128
Q_RANK = 384
KV_RANK = 256
ROPE_THETA = 10000.0
ATTN_SCALE = (NOPE + ROPE) ** -0.5
LOG2_E = float(np.log2(np.e))
Q_SCALE = ATTN_SCALE * LOG2_E
D_FF = -(-8 * D_MODEL // (3 * 256)) * 256
ALPHA = (2 * DEPTH) ** 0.25
EPS = 1e-6
_SIZES = (C_QKV, V_A, H_A, H_A, Q_RANK, KV_RANK, ROPE, D_MODEL, D_MODEL)
_OFFS = tuple(int(v) for v in np.cumsum((0,) + _SIZES))

LANES = 128
VMEM_LIMIT = 56 * 1024 * 1024

COL_QKV = 0
COL_Z = COL_QKV + C_QKV
COL_GA = COL_Z + V_A
COL_GB = COL_GA + D_MODEL
COL_CKV = COL_GB + D_MODEL
COL_KRA = COL_CKV + KV_RANK
COL_CQ = COL_KRA + LANES
COL_KRB = COL_CQ + Q_RANK
COL_AB = COL_KRB + LANES
N_PROJ = COL_AB + LANES
KC_W = KV_RANK + LANES

GDN_CHUNK = 128
MAX_STALE_EXCESS = 64.0


def _cparams(sem):
    return pltpu.CompilerParams(dimension_semantics=sem, vmem_limit_bytes=VMEM_LIMIT)


def _sigmoid(x):
    return jax.nn.sigmoid(x)


def _silu(x):
    return x * jax.nn.sigmoid(x)


def _mm(a, b):
    return jnp.dot(a.astype(BF16), b.astype(BF16), preferred_element_type=F32)


def _mm_nt(a, b):
    return lax.dot_general(a.astype(BF16), b.astype(BF16), (((1,), (1,)), ((), ())),
                           preferred_element_type=F32)


def _mm_tn(a, b):
    return lax.dot_general(a.astype(BF16), b.astype(BF16), (((0,), (0,)), ((), ())),
                           preferred_element_type=F32)


def _proj_body(x_ref, wt_ref, o_ref, *, tn):
    xb = x_ref[...].astype(BF16)
    for c in range(0, wt_ref.shape[0], tn):
        o_ref[:, c:c + tn] = lax.dot_general(xb, wt_ref[c:c + tn, :], (((1,), (1,)), ((), ())),
                                             preferred_element_type=F32)
        lo, hi = max(c, COL_Z), min(c + tn, COL_Z + V_A)
        if lo < hi:
            o_ref[:, lo:hi] = _silu(o_ref[:, lo:hi])


def _proj_in(x, w_t, tm, tn):
    t, k = x.shape
    n = w_t.shape[0]
    return pl.pallas_call(
        functools.partial(_proj_body, tn=tn),
        grid=(t // tm,),
        in_specs=[pl.BlockSpec((tm, k), lambda i: (i, 0)),
                  pl.BlockSpec((n, k), lambda i: (0, 0), pipeline_mode=pl.Buffered(1))],
        out_specs=pl.BlockSpec((tm, n), lambda i: (i, 0)),
        out_shape=jax.ShapeDtypeStruct((t, n), F32),
        compiler_params=_cparams(("parallel",)),
        name="proj_in",
    )(x, w_t)


def _gates_body(ab_ref, al_ref, dt_ref, o_ref, *, tm, l_pad, l_valid):
    x = ab_ref[...]
    lane = lax.broadcasted_iota(jnp.int32, x.shape, 1)
    xa = x + dt_ref[...]
    sp = jnp.maximum(xa, 0.0) + jnp.log1p(jnp.exp(-jnp.abs(xa)))
    g = -jnp.exp(al_ref[...]) * sp * LOG2_E
    y = jnp.where(lane < H_A, g, _sigmoid(x))
    yt = y.T[0:2 * H_A, :]
    if l_valid < l_pad:
        col = lax.broadcasted_iota(jnp.int32, yt.shape, 1) + pl.program_id(0) * tm
        yt = jnp.where(col % l_pad < l_valid, yt, 0.0)
    r = lax.broadcasted_iota(jnp.int32, (GDN_CHUNK, GDN_CHUNK), 0)
    c = lax.broadcasted_iota(jnp.int32, (GDN_CHUNK, GDN_CHUNK), 1)
    tri = jnp.where(r <= c, 1.0, 0.0).astype(F32)
    for s in range(tm // GDN_CHUNK):
        sl = slice(s * GDN_CHUNK, (s + 1) * GDN_CHUNK)
        o_ref[0:H_A, sl] = jnp.dot(yt[0:H_A, sl], tri, precision=lax.Precision.HIGHEST,
                                   preferred_element_type=F32)
    o_ref[H_A:2 * H_A, :] = yt[H_A:2 * H_A, :]


def _gates(src, col_block, al_lane, dt_lane, tm, l_pad, l_valid):
    t = src.shape[0]
    body = functools.partial(_gates_body, tm=tm, l_pad=l_pad, l_valid=l_valid)
    return pl.pallas_call(
        body,
        grid=(t // tm,),
        in_specs=[pl.BlockSpec((tm, LANES), lambda i: (i, col_block)),
                  pl.BlockSpec((1, LANES), lambda i: (0, 0)),
                  pl.BlockSpec((1, LANES), lambda i: (0, 0))],
        out_specs=pl.BlockSpec((2 * H_A, tm), lambda i: (0, i)),
        out_shape=jax.ShapeDtypeStruct((2 * H_A, t), F32),
        compiler_params=_cparams(("parallel",)),
        name="gates",
    )(src, al_lane, dt_lane)


def _gdn_body(q_ref, k_ref, v_ref, z_ref, gc_ref, bt_ref, cwq_ref, cwk_ref, cwv_ref,
              csq_ref, csk_ref, csv_ref, s0_ref, gn_ref, o_ref, so_ref,
              xq_ref, xk_ref, xv_ref, s_ref, *, tb, tr, nt, hb):
    t = pl.program_id(2)
    hist = CONV_W - 1
    base = 8

    @pl.when(t == 0)
    def _():
        s_ref[...] = s0_ref[0]
        xq_ref[base - hist:base, :] = csq_ref[0]
        xk_ref[base - hist:base, :] = csk_ref[0]
        xv_ref[base - hist:base, :] = csv_ref[0]

    for xs_ref, raw_ref in ((xq_ref, q_ref), (xk_ref, k_ref), (xv_ref, v_ref)):
        xs_ref[base:base + tr, :] = raw_ref[...]
        if tr < tb:
            xs_ref[base + tr:base + tb, :] = jnp.zeros((tb - tr, xs_ref.shape[1]), F32)

    def conv(xs_ref, cw_ref, ln):
        slab = xs_ref[0:base + tr, ln]
        y = pltpu.roll(slab, hist, 0)[base:base + tr] * cw_ref[0:1, ln]
        for j in range(1, CONV_W):
            tap = slab if j == hist else pltpu.roll(slab, hist - j, 0)
            y = y + tap[base:base + tr] * cw_ref[j:j + 1, ln]
        y = _silu(y)
        if tr < tb:
            y = jnp.concatenate([y, jnp.zeros((tb - tr, LANES), F32)], axis=0)
        return y

    cc = GDN_CHUNK
    ii = lax.broadcasted_iota(jnp.int32, (cc, cc), 0)
    jj = lax.broadcasted_iota(jnp.int32, (cc, cc), 1)
    strict = ii > jj
    eye = jnp.where(ii == jj, 1.0, 0.0).astype(F32)
    heads = range(hb)
    lanes = [slice(hd * LANES, (hd + 1) * LANES) for hd in heads]
    units = [(hd, c) for hd in heads for c in range(tb // cc)]
    rows = {u: slice(u[1] * cc, (u[1] + 1) * cc) for u in units}
    q = [conv(xq_ref, cwq_ref, ln) for ln in lanes]
    k = [conv(xk_ref, cwk_ref, ln) for ln in lanes]
    v = [conv(xv_ref, cwv_ref, ln) for ln in lanes]
    q = [x * (lax.rsqrt(jnp.sum(x * x, -1, keepdims=True) + EPS) * (DK_A ** -0.5)) for x in q]
    k = [x * lax.rsqrt(jnp.sum(x * x, -1, keepdims=True) + EPS) for x in k]
    qc = {u: q[u[0]][rows[u]] for u in units}
    kc = {u: k[u[0]][rows[u]] for u in units}
    vc = {u: v[u[0]][rows[u]] for u in units}
    m_row = {u: jnp.broadcast_to(gc_ref[u[0], :, rows[u]], (cc, cc)) for u in units}
    m_col = {u: m_row[u].T for u in units}
    b_col = {u: jnp.broadcast_to(bt_ref[u[0], :, rows[u]], (cc, cc)).T for u in units}
    kq = {u: _mm_nt(jnp.concatenate([kc[u], qc[u]], axis=0), kc[u]) for u in units}
    diff = {u: m_col[u] - m_row[u] for u in units}
    dec = {u: jnp.exp2(jnp.where(strict, diff[u], -jnp.inf)) for u in units}
    a_mat = {u: b_col[u] * kq[u][0:cc] * dec[u] for u in units}
    qk = {u: kq[u][cc:2 * cc] * (dec[u] + eye) for u in units}
    e_g = {u: jnp.exp2(m_col[u]) for u in units}
    x_inv = {u: eye - jnp.where((ii >> 1) == (jj >> 1), a_mat[u], 0.0) for u in units}
    sft = 1
    while (1 << sft) < min(cc, tr):
        off = ((ii >> (sft + 1)) == (jj >> (sft + 1))) & ((ii >> sft) != (jj >> sft))
        lx = {u: _mm(jnp.where(off, a_mat[u], 0.0), x_inv[u]) for u in units}
        x_inv = {u: x_inv[u] - _mm(x_inv[u], lx[u]) for u in units}
        sft += 1
    uw = {u: _mm(x_inv[u], jnp.concatenate([vc[u] * b_col[u], kc[u] * (b_col[u] * e_g[u])], axis=1))
          for u in units}
    wq = {u: jnp.concatenate([uw[u][:, DV_A:], qc[u] * e_g[u]], axis=0) for u in units}
    g_last = {u: m_col[u][cc - 1:cc, :] for u in units}
    kd = {u: kc[u] * jnp.exp2(g_last[u] - m_col[u]) for u in units}
    s_state = [s_ref[hd] for hd in heads]
    for c in range(tb // cc):
        ws = [_mm(wq[(hd, c)], s_state[hd]) for hd in heads]
        v_new = [uw[(hd, c)][:, 0:DV_A] - ws[hd][0:cc] for hd in heads]
        s_state = [s_state[hd] * jnp.exp2(g_last[(hd, c)]) + _mm_tn(kd[(hd, c)], v_new[hd]) for hd in heads]
        o = [ws[hd][cc:2 * cc] + _mm(qk[(hd, c)], v_new[hd]) for hd in heads]
        o = [x * lax.rsqrt(jnp.mean(x * x, -1, keepdims=True) + EPS) * gn_ref[...] for x in o]
        nr = min(cc, tr - c * cc)
        r = slice(c * cc, c * cc + nr)
        for hd in heads:
            o_ref[r, lanes[hd]] = (o[hd][0:nr] * z_ref[r, lanes[hd]]).astype(o_ref.dtype)
    for hd in heads:
        s_ref[hd] = s_state[hd]

    xq_ref[base - hist:base, :] = xq_ref[base + tb - hist:base + tb, :]
    xk_ref[base - hist:base, :] = xk_ref[base + tb - hist:base + tb, :]
    xv_ref[base - hist:base, :] = xv_ref[base + tb - hist:base + tb, :]

    @pl.when(t == nt - 1)
    def _():
        so_ref[0] = s_ref[...]


def _gdn(h1, gates3, conv_w, conv_state, s0, s0_layer, gn, bsz, seq, tb, hb):
    nt = -(-seq // tb)
    tr = min(tb, seq)
    assert seq % tr == 0 and (tr == tb or (nt == 1 and tb == GDN_CHUNK))
    t_rows = bsz * seq
    body = functools.partial(_gdn_body, tb=tb, tr=tr, nt=nt, hb=hb)
    width = hb * LANES
    ng = H_A // hb

    def rows(seg):
        return pl.BlockSpec((tr, width), lambda b, g, t: (b * nt + t, seg * ng + g))

    def cw(seg):
        return pl.BlockSpec((CONV_W, width), lambda b, g, t: (0, seg * ng + g))

    def cs(seg):
        return pl.BlockSpec((1, CONV_W - 1, width), lambda b, g, t: (b, 0, seg * ng + g))

    return pl.pallas_call(
        body,
        grid=(bsz, ng, nt),
        in_specs=[rows(0), rows(1), rows(2), rows(COL_Z // QK_A),
                  pl.BlockSpec((hb, 1, tb), lambda b, g, t: (g, 0, b * nt + t)),
                  pl.BlockSpec((hb, 1, tb), lambda b, g, t: (ng + g, 0, b * nt + t)),
                  cw(0), cw(1), cw(2), cs(0), cs(1), cs(2),
                  pl.BlockSpec((None, 1, hb, DK_A, DV_A), lambda b, g, t: (s0_layer, b, g, 0, 0)),
                  pl.BlockSpec((1, DV_A), lambda b, g, t: (0, 0))],
        out_specs=[pl.BlockSpec((tr, width), lambda b, g, t: (b * nt + t, g)),
                   pl.BlockSpec((1, hb, DK_A, DV_A), lambda b, g, t: (b, g, 0, 0))],
        out_shape=[jax.ShapeDtypeStruct((t_rows, V_A), BF16),
                   jax.ShapeDtypeStruct((bsz, H_A, DK_A, DV_A), F32)],
        scratch_shapes=[pltpu.VMEM((tb + 8, width), F32)] * 3 + [pltpu.VMEM((hb, DK_A, DV_A), F32)],
        compiler_params=_cparams(("parallel", "parallel", "arbitrary")),
        name="gdn",
    )(h1, h1, h1, h1, gates3, gates3, conv_w, conv_w, conv_w,
      conv_state, conv_state, conv_state, s0, gn)


def _mla_pre_body(cq_ref, ckv_ref, kra_ref, krb_ref, cos_ref, sin_ref, qg_ref, wuq_ref, wuk_ref,
                  kvg_ref, *refs, n_carried):
    q_ref, kc_ref, ckvo_ref, kro_ref, *maybe_vt_ref = refs[n_carried:]
    cq = cq_ref[...]
    cqn = cq * lax.rsqrt(jnp.mean(cq * cq, -1, keepdims=True) + EPS) * qg_ref[...]
    qf = jnp.dot(cqn.astype(BF16), wuq_ref[...], preferred_element_type=F32)
    cos_k = cos_ref[...]
    sin_k = sin_ref[...]
    reps = H_B * ROPE // LANES
    cos_t = jnp.concatenate([cos_k] * reps, axis=1)
    sin_t = jnp.concatenate([sin_k] * reps, axis=1)
    n_nope = H_B * NOPE
    n_rope = H_B * ROPE
    qr = (qf[:, n_nope:n_nope + n_rope] * cos_t + qf[:, n_nope + n_rope:] * sin_t) * Q_SCALE
    lane = lax.broadcasted_iota(jnp.int32, (cq.shape[0], LANES), 1)
    for h in range(H_B):
        ql = jnp.dot(qf[:, h * NOPE:(h + 1) * NOPE].astype(BF16), wuk_ref[h],
                     preferred_element_type=F32) * Q_SCALE
        blk = qr[:, (h // 2) * LANES:(h // 2 + 1) * LANES]
        keep = (lane < ROPE) if h % 2 == 0 else (lane >= ROPE)
        q_ref[h, :, 0:KV_RANK] = ql.astype(BF16)
        q_ref[h, :, KV_RANK:KC_W] = jnp.where(keep, blk, 0.0).astype(BF16)
    ckv = ckv_ref[...]
    ckvn = ckv * lax.rsqrt(jnp.mean(ckv * ckv, -1, keepdims=True) + EPS) * kvg_ref[...]
    kr2 = kra_ref[...] * cos_k + krb_ref[...] * sin_k
    ckvo_ref[...] = ckvn
    if maybe_vt_ref:
        kro_ref[0] = kr2.T[0:ROPE, :]
    else:
        kro_ref[...] = kr2[:, 0:ROPE]
    kc_ref[:, 0:KV_RANK] = ckvn.astype(BF16)
    kc_ref[:, KV_RANK:KC_W] = kr2.astype(BF16)
    if maybe_vt_ref:
        maybe_vt_ref[0][0] = ckvn.T.astype(BF16)


def _mla_pre(h1, cos_t, sin_t, qg, wuq, wuk, kvg, bsz, seq, tm, with_vt, layer, depth, carried):
    t_rows = bsz * seq
    ntab = cos_t.shape[0] // tm
    npb = max(seq // tm, 1)
    const2 = lambda i: (0, 0)
    out_specs = [pl.BlockSpec((H_B, tm, KC_W), lambda i: (0, i, 0)),
                 pl.BlockSpec((tm, KC_W), lambda i: (i, 0)),
                 pl.BlockSpec((None, tm, KV_RANK), lambda i: (layer, i, 0)),
                 pl.BlockSpec((None, tm, ROPE), lambda i: (layer, i, 0))]
    out_shape = [jax.ShapeDtypeStruct((H_B, t_rows, KC_W), BF16),
                 jax.ShapeDtypeStruct((t_rows, KC_W), BF16),
                 jax.ShapeDtypeStruct((depth, t_rows, KV_RANK), F32),
                 jax.ShapeDtypeStruct((depth, t_rows, ROPE), F32)]
    if with_vt:
        out_specs[3] = pl.BlockSpec((None, 1, ROPE, tm), lambda i: (layer, i // npb, 0, i % npb))
        out_shape[3] = jax.ShapeDtypeStruct((depth, bsz, ROPE, seq), F32)
        out_specs.append(pl.BlockSpec((1, KV_RANK, tm), lambda i: (i // npb, 0, i % npb)))
        out_shape.append(jax.ShapeDtypeStruct((bsz, KV_RANK, seq), BF16))
    n_in = 10
    return pl.pallas_call(
        functools.partial(_mla_pre_body, n_carried=len(carried)),
        input_output_aliases={n_in + k: 2 + k for k in range(len(carried))},
        grid=(t_rows // tm,),
        in_specs=[pl.BlockSpec((tm, Q_RANK), lambda i: (i, COL_CQ // Q_RANK)),
                  pl.BlockSpec((tm, KV_RANK), lambda i: (i, COL_CKV // KV_RANK)),
                  pl.BlockSpec((tm, LANES), lambda i: (i, COL_KRA // LANES)),
                  pl.BlockSpec((tm, LANES), lambda i: (i, COL_KRB // LANES)),
                  pl.BlockSpec((tm, LANES), lambda i: (i % ntab, 0)),
                  pl.BlockSpec((tm, LANES), lambda i: (i % ntab, 0)),
                  pl.BlockSpec((1, Q_RANK), const2),
                  pl.BlockSpec(wuq.shape, const2),
                  pl.BlockSpec(wuk.shape, lambda i: (0, 0, 0)),
                  pl.BlockSpec((1, KV_RANK), const2)] + [pl.BlockSpec(memory_space=pl.ANY)] * len(carried),
        out_specs=out_specs,
        out_shape=out_shape,
        compiler_params=_cparams(("parallel",)),
        name="mla_pre",
    )(h1, h1, h1, h1, cos_t, sin_t, qg, wuq, wuk, kvg, *carried)


def _attn_body(q_ref, k_ref, vt_ref, wuv_ref, o_ref, m_ref, l_ref, acc_ref, ex_ref, *, tq, tk, cw, lk):
    i = pl.program_id(1)
    shift = CHUNK.bit_length() - 1

    def update(j, masked, lazy, opening=False, wide=1):
        tkw = wide * tk
        k0 = pl.multiple_of(j * tk, tk)
        kt = k_ref[0, pl.ds(k0, tkw), :]
        vt = vt_ref[0, :, pl.ds(k0, tkw)]
        bias = None
        if masked:
            kpos = k0 + lax.broadcasted_iota(jnp.int32, (tkw, tq), 0)
            qpos = i * tq + lax.broadcasted_iota(jnp.int32, (tkw, tq), 1)
            bias = jnp.where((kpos >> shift) <= (qpos >> shift), 0.0, -jnp.inf).astype(F32)

        units = [(h, slice(c, c + cw)) for h in range(H_B) for c in range(0, tq, cw)]

        def scores(u):
            h, cs = u
            s = lax.dot_general(kt, q_ref[h, cs, :], (((1,), (1,)), ((), ())),
                                preferred_element_type=F32)
            return s if bias is None else s + bias[:, cs]

        def softmax(u, s):
            h, cs = u
            c_max = jnp.max(s, 0, keepdims=True)
            if opening:
                m_prev = jnp.zeros_like(c_max)
                m_new = c_max
                gap = jnp.abs(c_max)
            else:
                m_prev = m_ref[h, :, cs]
                m_new = jnp.maximum(m_prev, c_max)
                gap = c_max - m_prev
            alpha = jnp.exp2(m_prev - m_new)
            if lazy:
                p = jnp.exp2(s - m_prev)
                ex_ref[h, :, cs] = jnp.maximum(ex_ref[h, :, cs], gap)
                l_ref[h, :, cs] = alpha * (l_ref[h, :, cs] + jnp.sum(p, 0, keepdims=True))
            else:
                p = jnp.exp2(s - m_new)
                l_ref[h, :, cs] = alpha * l_ref[h, :, cs] + jnp.sum(p, 0, keepdims=True)
            m_ref[h, :, cs] = m_new
            return p.astype(BF16), alpha

        def accumulate(u, alpha, pv):
            h, cs = u
            if lazy:
                acc_ref[h, :, cs] = alpha * (acc_ref[h, :, cs] + pv)
            else:
                acc_ref[h, :, cs] = alpha * acc_ref[h, :, cs] + pv

        n_units = len(units)
        s_q = {0: scores(units[0])}
        if n_units > 1:
            s_q[1] = scores(units[1])
        p0, alpha = softmax(units[0], s_q.pop(0))
        alphas = {0: alpha}
        pvs = {0: jnp.dot(vt, p0, preferred_element_type=F32)}
        for n in range(n_units):
            if n + 2 < n_units:
                s_q[n + 2] = scores(units[n + 2])
            if n + 1 < n_units:
                p_next, alphas[n + 1] = softmax(units[n + 1], s_q.pop(n + 1))
            accumulate(units[n], alphas.pop(n), pvs.pop(n))
            if n + 1 < n_units:
                pvs[n + 1] = jnp.dot(vt, p_next, preferred_element_type=F32)

    n_full = ((((i * tq) >> shift) + 1) << shift) // tk
    n_all = jnp.minimum(((((i * tq + tq - 1) >> shift) + 1) << shift) + tk - 1, lk + tk - 1) // tk

    def tile_loop(lo, hi, masked, lazy, opening=False, wide=1, first_tile=0):
        def step(p, carry):
            update(first_tile + p * wide, masked, lazy, opening, wide)
            return carry
        lax.fori_loop(lo, hi, step, 0)

    def attempt(a, redo):
        first = a == 0
        run = jnp.logical_or(first, redo > 0)

        @pl.when(run)
        def _():
            m_ref[...] = jnp.full(m_ref.shape, -jnp.inf, F32)
            l_ref[...] = jnp.zeros(l_ref.shape, F32)
            acc_ref[...] = jnp.zeros(acc_ref.shape, F32)
            ex_ref[...] = jnp.full(ex_ref.shape, -jnp.inf, F32)

        tile_loop(0, jnp.where(first, 0, n_all) * run.astype(jnp.int32), True, False)
        tile_loop(0, jnp.where(first, 1, 0), True, True, True)
        n_wide = jnp.maximum(n_full - 1, 0) // WIDE_ATTN
        tile_loop(0, jnp.where(first, n_wide, 0), False, True, wide=WIDE_ATTN, first_tile=1)
        tile_loop(1 + n_wide * WIDE_ATTN, jnp.where(first, n_full, 0), False, True)
        tile_loop(jnp.maximum(n_full, 1), jnp.where(first, n_all, 0), True, True)
        excess = jnp.max(ex_ref[...])
        return jnp.where(first, (excess > MAX_STALE_EXCESS).astype(jnp.int32), 0)

    lax.fori_loop(0, 2, attempt, jnp.int32(0))

    for h in range(H_B):
        o_t = (acc_ref[h] * (1.0 / l_ref[h])).astype(BF16)
        ob_t = jnp.dot(wuv_ref[h], o_t, preferred_element_type=F32)
        o_ref[0, :, h * V_B:(h + 1) * V_B] = ob_t.T.astype(o_ref.dtype)


def _attn(q, kc, vt, wuv_t, bsz, seq, tq, tk, cw):
    assert seq % tq == 0 and seq % tk == 0
    nq = seq // tq
    body = functools.partial(_attn_body, tq=tq, tk=tk, cw=cw, lk=seq)
    return pl.pallas_call(
        body,
        grid=(bsz, nq),
        in_specs=[pl.BlockSpec((H_B, tq, KC_W), lambda b, i: (0, b * nq + i, 0)),
                  pl.BlockSpec((1, seq, KC_W), lambda b, i: (b, 0, 0)),
                  pl.BlockSpec((1, KV_RANK, seq), lambda b, i: (b, 0, 0)),
                  pl.BlockSpec(wuv_t.shape, lambda b, i: (0, 0, 0))],
        out_specs=pl.BlockSpec((1, tq, H_B * V_B), lambda b, i: (b, i, 0)),
        out_shape=jax.ShapeDtypeStruct((bsz, seq, H_B * V_B), BF16),
        scratch_shapes=[pltpu.VMEM((H_B, 1, tq), F32), pltpu.VMEM((H_B, 1, tq), F32),
                        pltpu.VMEM((H_B, KV_RANK, tq), F32), pltpu.VMEM((H_B, 1, tq), F32)],
        compiler_params=_cparams(("parallel", "arbitrary")),
        name="attn",
    )(q, kc, vt, wuv_t)


def _attn_dec_body(q_ref, ckv_ref, krt_ref, kn_ref, wuv_ref, o_ref, m_ref, l_ref, acc_ref,
                   *, tq, past, n_past):
    j = pl.program_id(1)
    rows = H_B * tq
    nt_dims = (((1,), (1,)), ((), ()))
    q2 = q_ref[...].reshape(rows, KC_W)

    @pl.when(j == 0)
    def _():
        m_ref[...] = jnp.full(m_ref.shape, -jnp.inf, F32)
        l_ref[...] = jnp.zeros(l_ref.shape, F32)
        acc_ref[...] = jnp.zeros(acc_ref.shape, F32)

    def accumulate(s, v_nat):
        m_prev = m_ref[...]
        m_new = jnp.maximum(m_prev, jnp.max(s, -1, keepdims=True))
        p = jnp.exp2(s - m_new)
        alpha = jnp.exp2(m_prev - m_new)
        l_ref[...] = alpha * l_ref[...] + jnp.sum(p, -1, keepdims=True)
        acc_ref[...] = alpha * acc_ref[...] + jnp.dot(p.astype(BF16), v_nat, preferred_element_type=F32)
        m_ref[...] = m_new

    @pl.when(j < n_past)
    def _():
        tk = ckv_ref.shape[1]
        for c in range(0, tk, min(tk, DEC_SUB)):
            cs = slice(c, c + min(tk, DEC_SUB))
            k_lat = ckv_ref[0, cs, :].astype(BF16)
            kr_t = krt_ref[0, :, cs].astype(BF16)
            kr2_t = jnp.concatenate([kr_t, kr_t], axis=0)
            s = (lax.dot_general(q2[:, 0:KV_RANK], k_lat, nt_dims, preferred_element_type=F32)
                 + jnp.dot(q2[:, KV_RANK:KC_W], kr2_t, preferred_element_type=F32))
            accumulate(s, k_lat)

    @pl.when(j == n_past)
    def _():
        kn = kn_ref[...]
        s = lax.dot_general(q2, kn, nt_dims, preferred_element_type=F32)
        shift = CHUNK.bit_length() - 1
        qpos = past + (lax.broadcasted_iota(jnp.int32, s.shape, 0) & (tq - 1))
        kpos = past + lax.broadcasted_iota(jnp.int32, s.shape, 1)
        s = jnp.where((kpos >> shift) <= (qpos >> shift), s, -jnp.inf)
        accumulate(s, kn[:, 0:KV_RANK])
        o = acc_ref[...] * (1.0 / l_ref[...])
        for h in range(H_B):
            oh = o[h * tq:(h + 1) * tq, :].astype(BF16)
            o_ref[:, h * V_B:(h + 1) * V_B] = jnp.dot(
                oh, wuv_ref[h], preferred_element_type=F32).astype(o_ref.dtype)


def _attn_dec(q, cache_ckv, cache_kr_t, layer, kc_new, wuv, bsz, seq, tk):
    past = cache_ckv.shape[2]
    assert past % tk == 0 and past % CHUNK == 0 and seq & (seq - 1) == 0
    n_past = past // tk
    body = functools.partial(_attn_dec_body, tq=seq, past=past, n_past=n_past)
    return pl.pallas_call(
        body,
        grid=(bsz, n_past + 1),
        in_specs=[pl.BlockSpec((H_B, seq, KC_W), lambda b, j: (0, b, 0)),
                  pl.BlockSpec((None, 1, tk, KV_RANK), lambda b, j: (layer, b, jnp.minimum(j, n_past - 1), 0)),
                  pl.BlockSpec((None, 1, ROPE, tk), lambda b, j: (layer, b, 0, jnp.minimum(j, n_past - 1))),
                  pl.BlockSpec((seq, KC_W), lambda b, j: (b, 0)),
                  pl.BlockSpec(wuv.shape, lambda b, j: (0, 0, 0))],
        out_specs=pl.BlockSpec((seq, H_B * V_B), lambda b, j: (b, 0)),
        out_shape=jax.ShapeDtypeStruct((bsz * seq, H_B * V_B), BF16),
        scratch_shapes=[pltpu.VMEM((H_B * seq, 1), F32), pltpu.VMEM((H_B * seq, 1), F32),
                        pltpu.VMEM((H_B * seq, KV_RANK), F32)],
        compiler_params=_cparams(("parallel", "arbitrary")),
        name="attn_dec",
    )(q, cache_ckv, cache_kr_t, kc_new, wuv)


def _layer_norm(r, g, b):
    mu = jnp.mean(r, -1, keepdims=True)
    d = r - mu
    var = jnp.mean(d * d, -1, keepdims=True)
    return d * lax.rsqrt(var + EPS) * g + b


def _merge_body(oa_ref, ob_ref, ga_ref, gb_ref, x_ref, woa_ref, wob_ref, wout_ref, g_ref, b_ref, o_ref):
    tm = x_ref.shape[0]
    part = max(tm // ROW_PARTS, min(tm, MIN_PART_ROWS))
    for r0 in range(0, tm, part):
        rs = slice(r0, r0 + part)
        ya = jnp.dot(oa_ref[rs, :], woa_ref[...], preferred_element_type=F32)
        yb = jnp.dot(ob_ref[rs, :], wob_ref[...], preferred_element_type=F32)
        m = _sigmoid(ga_ref[rs, :]) * ya + _sigmoid(gb_ref[rs, :]) * yb
        r = ALPHA * x_ref[rs, :] + jnp.dot(m.astype(BF16), wout_ref[...], preferred_element_type=F32)
        o_ref[rs, :] = _layer_norm(r, g_ref[...], b_ref[...])


def _merge(oa, ob, h1, x, woa, wob, wout, g, b, tm):
    t = x.shape[0]
    row = lambda i: (i, 0)
    const = lambda i: (0, 0)
    wspec = pl.BlockSpec((D_MODEL, D_MODEL), const, pipeline_mode=pl.Buffered(1))
    return pl.pallas_call(
        _merge_body,
        grid=(t // tm,),
        in_specs=[pl.BlockSpec((tm, V_A), row), pl.BlockSpec((tm, H_B * V_B), row),
                  pl.BlockSpec((tm, D_MODEL), lambda i: (i, COL_GA // D_MODEL)),
                  pl.BlockSpec((tm, D_MODEL), lambda i: (i, COL_GB // D_MODEL)),
                  pl.BlockSpec((tm, D_MODEL), row), wspec, wspec, wspec,
                  pl.BlockSpec((1, D_MODEL), const), pl.BlockSpec((1, D_MODEL), const)],
        out_specs=pl.BlockSpec((tm, D_MODEL), row),
        out_shape=jax.ShapeDtypeStruct((t, D_MODEL), F32),
        compiler_params=_cparams(("parallel",)),
        name="merge",
    )(oa, ob, h1, h1, x, woa, wob, wout, g, b)


def _ffn_body(x_ref, wg_ref, wu_ref, wd_ref, g_ref, b_ref, o_ref, *, chunks):
    tm = x_ref.shape[0]
    part = max(tm // ROW_PARTS, min(tm, MIN_PART_ROWS))
    for r0 in range(0, tm, part):
        rs = slice(r0, r0 + part)
        x = x_ref[rs, :]
        xb = x.astype(BF16)
        y = ALPHA * x
        off = 0
        for width in chunks:
            sl = slice(off, off + width)
            f1 = jnp.dot(xb, wg_ref[:, sl], preferred_element_type=F32)
            f3 = jnp.dot(xb, wu_ref[:, sl], preferred_element_type=F32)
            hc = (_silu(f1) * f3).astype(BF16)
            y = y + jnp.dot(hc, wd_ref[sl, :], preferred_element_type=F32)
            off += width
        o_ref[rs, :] = _layer_norm(y, g_ref[...], b_ref[...])


def _ffn(x, wg, wu, wd, g, b, tm, chunks):
    assert sum(chunks) == D_FF
    t = x.shape[0]
    row = lambda i: (i, 0)
    const = lambda i: (0, 0)
    single = pl.Buffered(1)
    return pl.pallas_call(
        functools.partial(_ffn_body, chunks=chunks),
        grid=(t // tm,),
        in_specs=[pl.BlockSpec((tm, D_MODEL), row),
                  pl.BlockSpec((D_MODEL, D_FF), const, pipeline_mode=single),
                  pl.BlockSpec((D_MODEL, D_FF), const, pipeline_mode=single),
                  pl.BlockSpec((D_FF, D_MODEL), const, pipeline_mode=single),
                  pl.BlockSpec((1, D_MODEL), const), pl.BlockSpec((1, D_MODEL), const)],
        out_specs=pl.BlockSpec((tm, D_MODEL), row),
        out_shape=jax.ShapeDtypeStruct((t, D_MODEL), F32),
        compiler_params=_cparams(("parallel",)),
        name="ffn",
    )(x, wg, wu, wd, g, b)


def _prep_layer_weights(w_in_t, conv_w, a_log, dt_bias, gdn_norm_g, w_oa, q_norm_g, w_uq, kv_norm_g,
                        w_ukv, w_ob, w_out, ln1_g, ln1_b, w_gu, w_down, ln2_g, ln2_b):
    seg = lambda i: w_in_t[_OFFS[i]:_OFFS[i + 1], :]
    qkv, z, a, b, c_q, c_kv, k_r, g_a, g_b = (seg(i) for i in range(9))
    half = ROPE // 2
    k_r_rot = jnp.concatenate([-k_r[half:], k_r[:half]], axis=0)
    pad = jnp.zeros((LANES - 2 * H_A, D_MODEL), w_in_t.dtype)
    w_proj = jnp.concatenate([qkv, z, g_a, g_b, c_kv, k_r, k_r, c_q, k_r_rot, k_r_rot, a, b, pad],
                             axis=0).astype(BF16)
    lane_pad = jnp.zeros((LANES - H_A,), F32)
    al_lane = jnp.concatenate([a_log.astype(F32), lane_pad]).reshape(1, LANES)
    dt_lane = jnp.concatenate([dt_bias.astype(F32), lane_pad]).reshape(1, LANES)
    uq = w_uq.reshape(Q_RANK, H_B, NOPE + ROPE)
    uq_nope = uq[:, :, :NOPE].reshape(Q_RANK, H_B * NOPE)
    uq_rope = uq[:, :, NOPE:]
    uq_rot = jnp.concatenate([-uq_rope[:, :, half:], uq_rope[:, :, :half]], axis=2)
    w_uq_ext = jnp.concatenate([uq_nope, uq_rope.reshape(Q_RANK, H_B * ROPE),
                                uq_rot.reshape(Q_RANK, H_B * ROPE)], axis=1).astype(BF16)
    ukv = w_ukv.reshape(KV_RANK, H_B, NOPE + V_B)
    w_uk_t = jnp.transpose(ukv[:, :, :NOPE], (1, 2, 0)).astype(BF16)
    w_uv = jnp.transpose(ukv[:, :, NOPE:], (1, 0, 2)).astype(BF16)
    w_uv_t = jnp.transpose(ukv[:, :, NOPE:], (1, 2, 0)).astype(BF16)
    return dict(
        w_proj=w_proj, conv_w=conv_w.astype(F32), al_lane=al_lane, dt_lane=dt_lane,
        gn=gdn_norm_g.reshape(1, DV_A).astype(F32), w_oa=w_oa.astype(BF16),
        qg=q_norm_g.reshape(1, Q_RANK).astype(F32), w_uq=w_uq_ext, w_uk_t=w_uk_t, w_uv=w_uv, w_uv_t=w_uv_t,
        kvg=kv_norm_g.reshape(1, KV_RANK).astype(F32), w_ob=w_ob.astype(BF16),
        w_out=w_out.astype(BF16), ln1_g=ln1_g.reshape(1, D_MODEL), ln1_b=ln1_b.reshape(1, D_MODEL),
        w_g=w_gu[:, :D_FF].astype(BF16), w_u=w_gu[:, D_FF:].astype(BF16), w_down=w_down.astype(BF16),
        ln2_g=ln2_g.reshape(1, D_MODEL), ln2_b=ln2_b.reshape(1, D_MODEL))


def _rope_tables(past, seq, reps):
    half = ROPE // 2
    inv = ROPE_THETA ** (-jnp.arange(half, dtype=F32) / half)
    ang = (past + jnp.arange(seq)).astype(F32)[:, None] * inv[None, :]
    cos = jnp.tile(jnp.cos(ang), (reps, LANES // half))
    sin = jnp.tile(jnp.sin(ang), (reps, LANES // half))
    return cos, sin


TM_PROJ, TN_PROJ, TM_GATES, TM_MLA, TM_MERGE, TM_FFN = 512, N_PROJ // 4, 4096, 1024, 1024, 1024
FFN_CHUNKS = (1024, 1024, D_FF - 2 * 1024)
ROW_PARTS, MIN_PART_ROWS = 4, 256
TQ_ATTN, TK_ATTN, CW_ATTN, TK_DEC = 512, 512, 512, 4096
WIDE_ATTN = 2
DEC_SUB = 1024
TB_GDN, HB_GDN = 4 * GDN_CHUNK, H_A


def _trunk_layer(x, conv_state, s0, caches, wl, bsz, seq, layer, depth, carried):
    decode = caches is not None
    t_rows = bsz * seq
    h1 = _proj_in(x, wl["w_proj"], min(TM_PROJ, t_rows), TN_PROJ)
    seq_pad = -(-seq // GDN_CHUNK) * GDN_CHUNK
    if seq_pad == seq:
        gates = _gates(h1, COL_AB // LANES, wl["al_lane"], wl["dt_lane"], min(TM_GATES, t_rows), seq, seq)
    else:
        ab = h1[:, COL_AB:COL_AB + LANES].reshape(bsz, seq, LANES)
        ab = jnp.pad(ab, ((0, 0), (0, seq_pad - seq), (0, 0))).reshape(bsz * seq_pad, LANES)
        gates = _gates(ab, 0, wl["al_lane"], wl["dt_lane"], min(TM_GATES, bsz * seq_pad), seq_pad, seq)
    o_a, s_new = _gdn(h1, gates.reshape(2 * H_A, 1, bsz * seq_pad), wl["conv_w"], conv_state, s0[0], s0[1],
                      wl["gn"], bsz, seq, min(TB_GDN, seq_pad), HB_GDN)
    if decode:
        cache_ckv, cache_kr_t = caches
        tm_mla = t_rows if t_rows <= TM_MLA else seq
        cos_t, sin_t = _rope_tables(cache_ckv.shape[2], seq, tm_mla // seq)
        q, kc, ckv_all, kr_all = _mla_pre(h1, cos_t, sin_t, wl["qg"], wl["w_uq"], wl["w_uk_t"], wl["kvg"],
                                          bsz, seq, tm_mla, False, layer, depth, carried)
        tk_dec = TK_DEC
        while cache_ckv.shape[2] % tk_dec:
            tk_dec //= 2
        o_b = _attn_dec(q, cache_ckv, cache_kr_t, layer, kc, wl["w_uv"], bsz, seq, tk_dec)
    else:
        cos_t, sin_t = _rope_tables(0, seq, 1)
        q, kc, ckv_all, kr_all, vt = _mla_pre(h1, cos_t, sin_t, wl["qg"], wl["w_uq"], wl["w_uk_t"],
                                              wl["kvg"], bsz, seq, TM_MLA, True, layer, depth, carried)
        o_b = _attn(q, kc.reshape(bsz, seq, KC_W), vt, wl["w_uv_t"], bsz, seq, TQ_ATTN, TK_ATTN, CW_ATTN)
        o_b = o_b.reshape(t_rows, H_B * V_B)
    x1 = _merge(o_a, o_b, h1, x, wl["w_oa"], wl["w_ob"], wl["w_out"], wl["ln1_g"], wl["ln1_b"],
                min(TM_MERGE, t_rows))
    x2 = _ffn(x1, wl["w_g"], wl["w_u"], wl["w_down"], wl["ln2_g"], wl["ln2_b"], min(TM_FFN, t_rows), FFN_CHUNKS)
    conv_new = h1.reshape(bsz, seq, N_PROJ)[:, seq - (CONV_W - 1):, COL_QKV:COL_QKV + C_QKV]
    return x2, conv_new, s_new, (ckv_all, kr_all)


def kernel(x_prompt, x_sample, state_conv, state_gdn, cache_ckv, cache_krope, w_in, conv_w, a_log, dt_bias, gdn_norm_g, w_oa, q_norm_g, w_uq, kv_norm_g, w_ukv, w_ob, w_out, ln1_g, ln1_b, w_gu, w_down, ln2_g, ln2_b):
    bp, lp, _ = x_prompt.shape
    bs, ls, _ = x_sample.shape
    yp = x_prompt.reshape(bp * lp, D_MODEL)
    ys = x_sample.reshape(bs * ls, D_MODEL)
    zero_conv = jnp.zeros((bp, CONV_W - 1, C_QKV), F32)
    zero_s = jnp.zeros((1, bp, H_A, DK_A, DV_A), F32)
    cache_kr_t = jnp.swapaxes(cache_krope, 2, 3)
    w_in_t = jnp.swapaxes(w_in, 1, 2).astype(BF16)
    depth = w_in.shape[0]
    conv_p, gdn_p, conv_s, gdn_s = [], [], [], []
    kv_p = kv_s = ()
    for l in range(depth):
        wl = _prep_layer_weights(w_in_t[l], conv_w[l], a_log[l], dt_bias[l], gdn_norm_g[l], w_oa[l],
                                 q_norm_g[l], w_uq[l], kv_norm_g[l], w_ukv[l], w_ob[l], w_out[l],
                                 ln1_g[l], ln1_b[l], w_gu[l], w_down[l], ln2_g[l], ln2_b[l])
        yp, c_new, g_new, kv_p = _trunk_layer(yp, zero_conv, (zero_s, 0), None, wl, bp, lp, l, depth, kv_p)
        conv_p.append(c_new), gdn_p.append(g_new)
        ys, c_new, g_new, kv_s = _trunk_layer(ys, state_conv[l], (state_gdn, l), (cache_ckv, cache_kr_t), wl,
                                              bs, ls, l, depth, kv_s)
        conv_s.append(c_new), gdn_s.append(g_new)
    return (yp.reshape(bp, lp, D_MODEL), ys.reshape(bs, ls, D_MODEL),
            jnp.stack(conv_p), jnp.stack(gdn_p),
            kv_p[0].reshape(depth, bp, lp, KV_RANK), jnp.swapaxes(kv_p[1], 2, 3),
            jnp.stack(conv_s), jnp.stack(gdn_s),
            kv_s[0].reshape(depth, bs, ls, KV_RANK), kv_s[1].reshape(depth, bs, ls, ROPE))
```
